```python
import math, functools
import jax, jax.numpy as jnp
from jax import lax
import numpy as np

D_MODEL = 2048
BATCH = 4
SEQ = 2048
DEPTH = 1
DEC_BATCH = 128
DEC_SEQ = 1
PAST_LEN = 16384
PAGE_SIZE = 128

N_META = 16
CHUNK = 128
D_MIX = D_MODEL
D_SSM = D_MIX // 2
D_RET = D_MIX - D_SSM
SSM_HEADDIM = 64
SSM_HEADS = D_SSM // SSM_HEADDIM
SSM_GROUPS = 2
HEADS_PER_GROUP = SSM_HEADS // SSM_GROUPS
D_STATE = 128
CONV_W = 4
CONV_DIM = D_SSM + 2 * SSM_GROUPS * D_STATE
RET_HEADS = 4
RET_HEADDIM = D_RET // RET_HEADS
ROPE_BASE = 10000.0
D_FF = -(-(8 * D_MODEL) // (3 * 256)) * 256
PROJ_SPLITS = (D_SSM, D_SSM + CONV_DIM, D_SSM + CONV_DIM + SSM_HEADS,
               D_SSM + CONV_DIM + SSM_HEADS + D_RET,
               D_SSM + CONV_DIM + SSM_HEADS + 2 * D_RET,
               D_SSM + CONV_DIM + SSM_HEADS + 3 * D_RET)
PROJ_DIM = D_SSM + CONV_DIM + SSM_HEADS + 4 * D_RET
EPS = 1e-6

kernel_name = 'hymba_ssd_retention_step'


def normalize(x):
    xf = x.astype(jnp.float32)
    return xf * lax.rsqrt(jnp.mean(xf * xf, axis=-1, keepdims=True) + EPS)


def rmsnorm(x, g):
    return (normalize(x) * g.astype(jnp.float32)).astype(x.dtype)


def rotary(x, pos):
    half = x.shape[-1] // 2
    inv_freq = ROPE_BASE ** (-jnp.arange(half, dtype=jnp.float32) / half)
    ang = (pos[:, None] * inv_freq[None, :])[:, None, :]
    cos, sin = jnp.cos(ang), jnp.sin(ang)
    x1, x2 = x[..., :half], x[..., half:]
    return jnp.concatenate([x1 * cos - x2 * sin, x1 * sin + x2 * cos], axis=-1)


def causal_conv(xbc, buf, w, b):
    l = xbc.shape[1]
    full = jnp.concatenate([buf, xbc], axis=1)
    out = b
    for i in range(CONV_W):
        out = out + full[:, i:i + l] * w[i]
    return jax.nn.silu(out), full[:, l:]


def scan_chunks(step, state, xs, chunk):
    b, L = xs[0].shape[:2]
    n = L // chunk
    def split(a):
        return jnp.moveaxis(a.reshape(b, n, chunk, *a.shape[2:]), 1, 0)
    state, ys = lax.scan(step, state, tuple(split(a) for a in xs))
    ys = jnp.moveaxis(ys, 0, 1).reshape(b, L, *ys.shape[3:])
    return state, ys


def ssd_chunk(a_neg, h, inp):
    x, dt, bm, cm = inp
    b, c = x.shape[:2]
    x = x.reshape(b, c, SSM_GROUPS, HEADS_PER_GROUP, SSM_HEADDIM)
    dtg = dt.reshape(b, c, SSM_GROUPS, HEADS_PER_GROUP)
    hg = h.reshape(b, SSM_GROUPS, HEADS_PER_GROUP, SSM_HEADDIM, D_STATE)
    lcum = jnp.cumsum(dtg * a_neg.reshape(SSM_GROUPS, HEADS_PER_GROUP), axis=1)
    causal = jnp.tril(jnp.ones((c, c), dtype=bool))
    seg = lcum[:, :, None] - lcum[:, None, :]
    decay = jnp.exp(jnp.where(causal[None, :, :, None, None], seg, -jnp.inf))
    cb = jnp.einsum('bign,bjgn->bijg', cm, bm)
    w = cb[..., None] * decay * dtg[:, None]
    y = jnp.einsum('bijgh,bjghp->bighp', w, x)
    y = y + jnp.einsum('bign,bghpn->bighp', cm, hg) * jnp.exp(lcum)[..., None]
    last = lcum[:, -1]
    wts = jnp.exp(last[:, None] - lcum) * dtg
    hg = jnp.exp(last)[..., None, None] * hg + jnp.einsum('bjgh,bjgn,bjghp->bghpn', wts, bm, x)
    return hg.reshape(b, SSM_HEADS, SSM_HEADDIM, D_STATE), y.reshape(b, c, SSM_HEADS, SSM_HEADDIM)


def retention_chunk(s, inp):
    q, k, v = inp
    c = q.shape[1]
    lg = jnp.log1p(-(2.0 ** (-5.0 - jnp.arange(RET_HEADS, dtype=jnp.float32))))
    idx = jnp.arange(c, dtype=jnp.float32)
    diff = idx[:, None] - idx[None, :]
    decay = jnp.where((diff >= 0)[..., None], jnp.exp(jnp.maximum(diff, 0.0)[..., None] * lg), 0.0)
    scores = jnp.einsum('bihd,bjhd->bijh', q, k) * decay
    y = jnp.einsum('bijh,bjhv->bihv', scores, v)
    y = y + jnp.einsum('bihd,bhdv->bihv', q, s) * jnp.exp((idx + 1.0)[:, None] * lg)[..., None]
    wk = jnp.exp((c - 1.0 - idx)[:, None] * lg)
    s = jnp.exp(c * lg)[:, None, None] * s + jnp.einsum('bjhd,bjhv->bhdv', k * wk[..., None], v)
    return s, y


def mixers(u, pos, conv_buf, ssm_h, ret_s, segments, w_in, conv_w, conv_b, dt_bias, a_log, d_skip, ssm_norm_g, ret_norm_g):
    b, l, _ = u.shape
    f32 = jnp.float32
    proj = (u @ w_in).astype(f32)
    z, xbc, dt_raw, q, k, v, g = jnp.split(proj, PROJ_SPLITS, axis=-1)
    xbc, new_conv = causal_conv(xbc, conv_buf.astype(f32), conv_w.astype(f32), conv_b.astype(f32))
    xs, bm, cm = jnp.split(xbc, (D_SSM, D_SSM + SSM_GROUPS * D_STATE), axis=-1)
    xs = xs.reshape(b, l, SSM_HEADS, SSM_HEADDIM)
    bm = bm.reshape(b, l, SSM_GROUPS, D_STATE)
    cm = cm.reshape(b, l, SSM_GROUPS, D_STATE)
    dt = jax.nn.softplus(dt_raw + dt_bias.astype(f32))
    a_neg = -jnp.exp(a_log.astype(f32))
    q = rotary(q.reshape(b, l, RET_HEADS, RET_HEADDIM), pos)
    k = rotary(k.reshape(b, l, RET_HEADS, RET_HEADDIM), pos) * (RET_HEADDIM ** -0.5)
    v = v.reshape(b, l, RET_HEADS, RET_HEADDIM)
    ssm_h = ssm_h.astype(f32)
    ret_s = ret_s.astype(f32)
    ssd_step = functools.partial(ssd_chunk, a_neg)
    y_ssm, y_ret = [], []
    start = 0
    for length, chunk in segments:
        sl = slice(start, start + length)
        ssm_h, ys = scan_chunks(ssd_step, ssm_h, (xs[:, sl], dt[:, sl], bm[:, sl], cm[:, sl]), chunk)
        ret_s, yr = scan_chunks(retention_chunk, ret_s, (q[:, sl], k[:, sl], v[:, sl]), chunk)
        y_ssm.append(ys)
        y_ret.append(yr)
        start += length
    y1 = jnp.concatenate(y_ssm, axis=1) + d_skip.astype(f32)[:, None] * xs
    y1 = (y1.reshape(b, l, D_SSM) * jax.nn.silu(z)).reshape(b, l, SSM_GROUPS, D_SSM // SSM_GROUPS)
    y1 = normalize(y1).reshape(b, l, D_SSM) * ssm_norm_g.astype(f32)
    y2 = normalize(jnp.concatenate(y_ret, axis=1)).reshape(b, l, D_RET) * ret_norm_g.astype(f32) * jax.nn.silu(g)
    mix = jnp.concatenate([y1, y2], axis=-1).astype(u.dtype)
    return mix, new_conv, ssm_h, ret_s


def layer(h, pos, conv_buf, ssm_h, ret_s, segments, pre_mix_g, post_mix_g, pre_ffn_g, post_ffn_g,
          w_in, conv_w, conv_b, dt_bias, a_log, d_skip, ssm_norm_g, ret_norm_g, w_out, w_gate, w_up, w_down):
    u = rmsnorm(h, pre_mix_g)
    mix, conv_buf, ssm_h, ret_s = mixers(u, pos, conv_buf, ssm_h, ret_s, segments, w_in, conv_w, conv_b,
                                         dt_bias, a_log, d_skip, ssm_norm_g, ret_norm_g)
    h = h + rmsnorm(mix @ w_out, post_mix_g)
    f = rmsnorm(h, pre_ffn_g)
    f = (jax.nn.silu(f @ w_gate) * (f @ w_up)) @ w_down
    h = h + rmsnorm(f, post_ffn_g)
    return h, conv_buf, ssm_h, ret_s


def setup_inputs(seed: int = 0) -> dict:
    key = jax.random.key(seed)
    ks = jax.random.split(key, 24)
    f32 = jnp.float32
    def nrm(k, shape, scale):
        return jax.random.normal(k, shape, f32) * scale
    def gain(k, shape):
        return 1.0 + 0.05 * jax.random.normal(k, shape, f32)
    dt0 = jnp.exp(jax.random.uniform(ks[14], (DEPTH, SSM_HEADS), f32, math.log(1e-3), math.log(1e-1)))
    dt_bias = dt0 + jnp.log(-jnp.expm1(-dt0))
    return dict(
        x_prompt=nrm(ks[0], (BATCH, SEQ, D_MODEL), 1.0),
        x_sample=nrm(ks[1], (DEC_BATCH, DEC_SEQ, D_MODEL), 1.0),
        state_conv=nrm(ks[2], (DEPTH, DEC_BATCH, CONV_W - 1, CONV_DIM), 1.0),
        state_ssm=nrm(ks[3], (DEPTH, DEC_BATCH, SSM_HEADS, SSM_HEADDIM, D_STATE), 0.5),
        state_ret=nrm(ks[4], (DEPTH, DEC_BATCH, RET_HEADS, RET_HEADDIM, RET_HEADDIM), 0.5),
        meta_tokens=nrm(ks[5], (N_META, D_MODEL), 1.0),
        pre_mix_g=gain(ks[6], (DEPTH, D_MODEL)),
        post_mix_g=gain(ks[7], (DEPTH, D_MODEL)),
        pre_ffn_g=gain(ks[8], (DEPTH, D_MODEL)),
        post_ffn_g=gain(ks[9], (DEPTH, D_MODEL)),
        w_in=nrm(ks[10], (DEPTH, D_MODEL, PROJ_DIM), D_MODEL ** -0.5),
        conv_w=nrm(ks[11], (DEPTH, CONV_W, CONV_DIM), CONV_W ** -0.5),
        conv_b=nrm(ks[12], (DEPTH, CONV_DIM), 0.02),
        dt_bias=dt_bias,
        a_log=jnp.log(jax.random.uniform(ks[13], (DEPTH, SSM_HEADS), f32, 1.0, 16.0)),
        d_skip=gain(ks[15], (DEPTH, SSM_HEADS)),
        ssm_norm_g=gain(ks[16], (DEPTH, D_SSM)),
        ret_norm_g=gain(ks[17], (DEPTH, D_RET)),
        w_out=nrm(ks[18], (DEPTH, D_MIX, D_MODEL), D_MIX ** -0.5),
        w_gate=nrm(ks[19], (DEPTH, D_MODEL, D_FF), D_MODEL ** -0.5),
        w_up=nrm(ks[20], (DEPTH, D_MODEL, D_FF), D_MODEL ** -0.5),
        w_down=nrm(ks[21], (DEPTH, D_FF, D_MODEL), D_FF ** -0.5),
    )


def reference(x_prompt, x_sample, state_conv, state_ssm, state_ret, meta_tokens, pre_mix_g, post_mix_g,
              pre_ffn_g, post_ffn_g, w_in, conv_w, conv_b, dt_bias, a_log, d_skip, ssm_norm_g, ret_norm_g,
              w_out, w_gate, w_up, w_down):
    f32 = jnp.float32
    bp, seq = x_prompt.shape[:2]
    bs, dec_seq = x_sample.shape[:2]
    hp = jnp.concatenate([jnp.broadcast_to(meta_tokens[None].astype(x_prompt.dtype), (bp, N_META, D_MODEL)),
                          x_prompt], axis=1)
    pos_p = jnp.arange(N_META + seq, dtype=f32)
    seg_p = ((N_META, N_META), (seq, CHUNK))
    hs = x_sample
    pos_s = PAST_LEN + jnp.arange(dec_seq, dtype=f32)
    dec_chunk = CHUNK if dec_seq % CHUNK == 0 else dec_seq
    seg_s = ((dec_seq, dec_chunk),)
    p_conv, p_ssm, p_ret, s_conv, s_ssm, s_ret = [], [], [], [], [], []
    for i in range(DEPTH):
        lp = (pre_mix_g[i], post_mix_g[i], pre_ffn_g[i], post_ffn_g[i], w_in[i], conv_w[i], conv_b[i],
              dt_bias[i], a_log[i], d_skip[i], ssm_norm_g[i], ret_norm_g[i], w_out[i], w_gate[i], w_up[i], w_down[i])
        hp, c, s, r = layer(hp, pos_p,
                            jnp.zeros((bp, CONV_W - 1, CONV_DIM), f32),
                            jnp.zeros((bp, SSM_HEADS, SSM_HEADDIM, D_STATE), f32),
                            jnp.zeros((bp, RET_HEADS, RET_HEADDIM, RET_HEADDIM), f32),
                            seg_p, *lp)
        p_conv.append(c); p_ssm.append(s); p_ret.append(r)
        hs, c, s, r = layer(hs, pos_s, state_conv[i], state_ssm[i], state_ret[i], seg_s, *lp)
        s_conv.append(c); s_ssm.append(s); s_ret.append(r)
    y_prompt = hp[:, N_META:]
    y_sample = hs
    prompt_conv = jnp.stack(p_conv)
    prompt_ssm = jnp.stack(p_ssm)
    prompt_ret = jnp.stack(p_ret)
    sample_conv = jnp.stack(s_conv)
    sample_ssm = jnp.stack(s_ssm)
    sample_ret = jnp.stack(s_ret)
    return (y_prompt, y_sample, prompt_conv, prompt_ssm, prompt_ret, sample_conv, sample_ssm, sample_ret)
```

```python
import functools

import numpy as np
import jax
import jax.numpy as jnp
from jax import lax
from jax.experimental import pallas as pl
from jax.experimental.pallas import tpu as pltpu

F32 = jnp.float32
BF16 = jnp.bfloat16

D_MODEL = 2048
N_META = 16
CHUNK = 128
D_SSM = 1024
D_RET = 1024
SSM_HEADDIM = 64
SSM_HEADS = 16
SSM_GROUPS = 2
GROUP_DIM = D_SSM // SSM_GROUPS
D_STATE = 128
CONV_W = 4
CONV_DIM = D_SSM + 2 * SSM_GROUPS * D_STATE
RET_HEADS = 4
RET_HEADDIM = 256
ROPE_BASE = 10000.0
D_FF = 5632
EPS = 1e-6
PAST_LEN = 16384

LANES = 128
STEP_ROWS = 8

OFF_Z = 0
OFF_XBC = D_SSM
OFF_Q = OFF_XBC + CONV_DIM
OFF_K = OFF_Q + D_RET
OFF_V = OFF_K + D_RET
OFF_G = OFF_V + D_RET
PROJ_MAIN = OFF_G + D_RET

VMEM_LIMIT = 56 * 1024 * 1024

RET_LOG_GAMMA = [float(np.log1p(-np.float32(2.0) ** np.float32(-5.0 - h)).astype(np.float32))
                 for h in range(RET_HEADS)]


def _silu(x):
    return x / (1.0 + jnp.exp(-x))


def _softplus(x):
    return jnp.maximum(x, 0.0) + jnp.log1p(jnp.exp(-jnp.abs(x)))


def _rms(x):
    return x * lax.rsqrt(jnp.mean(x * x, axis=-1, keepdims=True) + EPS)


def _split3(x):
    hi = x.astype(BF16)
    r = x - hi.astype(F32)
    mid = r.astype(BF16)
    lo = (r - mid.astype(F32)).astype(BF16)
    return hi, mid, lo


def _dot(a, b):
    return jnp.dot(a, b, preferred_element_type=F32)


def _dot_nt(a, b):
    return lax.dot_general(a, b, (((1,), (1,)), ((), ())), preferred_element_type=F32)


def _dot_tn(a, b):
    return lax.dot_general(a, b, (((0,), (0,)), ((), ())), preferred_element_type=F32)


def _exact_right(x, sel):
    hi, mid, lo = _split3(x)
    return _dot(hi, sel) + _dot(mid, sel) + _dot(lo, sel)


def _exact_left(sel, x):
    hi, mid, lo = _split3(x)
    return _dot(sel, hi) + _dot(sel, mid) + _dot(sel, lo)


def _exact_tn(x, sel):
    hi, mid, lo = _split3(x)
    return _dot_tn(hi, sel) + _dot_tn(mid, sel) + _dot_tn(lo, sel)


def _head_expand():
    r = lax.broadcasted_iota(jnp.int32, (LANES, D_SSM), 0)
    c = lax.broadcasted_iota(jnp.int32, (LANES, D_SSM), 1)
    return (c // SSM_HEADDIM == r).astype(BF16)


def _inproj_kernel(x_ref, g_ref, w_ref, wdt_ref, o_ref, odt_ref, u_ref):
    @pl.when(pl.program_id(1) == 0)
    def _():
        u = (_rms(x_ref[...]) * g_ref[...]).astype(BF16)
        u_ref[...] = u
        odt_ref[...] = _dot(u, wdt_ref[...])

    o_ref[...] = _dot(u_ref[...], w_ref[...])


def _inproj(x, g, w_main, w_dt, *, bm, bn, name):
    m = x.shape[0]
    return pl.pallas_call(
        _inproj_kernel,
        out_shape=(jax.ShapeDtypeStruct((m, PROJ_MAIN), F32),
                   jax.ShapeDtypeStruct((m, LANES), F32)),
        grid=(m // bm, PROJ_MAIN // bn),
        in_specs=[pl.BlockSpec((bm, D_MODEL), lambda i, j: (i, 0)),
                  pl.BlockSpec((1, D_MODEL), lambda i, j: (0, 0)),
                  pl.BlockSpec((D_MODEL, bn), lambda i, j: (0, j)),
                  pl.BlockSpec((D_MODEL, LANES), lambda i, j: (0, 0))],
        out_specs=(pl.BlockSpec((bm, bn), lambda i, j: (i, j)),
                   pl.BlockSpec((bm, LANES), lambda i, j: (i, 0))),
        scratch_shapes=[pltpu.VMEM((bm, D_MODEL), BF16)],
        compiler_params=pltpu.CompilerParams(
            dimension_semantics=("arbitrary", "arbitrary"), vmem_limit_bytes=VMEM_LIMIT),
        name=name,
    )(x, g, w_main, w_dt)


def _mixer_seq_kernel(proj_ref, dtr_ref, conv0_ref, ssm0_ref, ret0_ref,
                      convw_ref, convb_ref, dtb_ref, alog_ref, dskip_ref, sg_ref, rg_ref, invf_ref,
                      mix_ref, convo_ref, ssmo_ref, reto_ref,
                      cbuf_ref, rdec_ref, *, valid, pos_base):
    C = CHUNK
    b = pl.program_id(0)
    c = pl.program_id(1)
    rowi = lax.broadcasted_iota(jnp.int32, (C, 1), 0)
    rowf = rowi.astype(F32)
    ri = lax.broadcasted_iota(jnp.int32, (C, C), 0)
    ci = lax.broadcasted_iota(jnp.int32, (C, C), 1)
    causal = ri >= ci

    @pl.when((b == 0) & (c == 0))
    def _():
        diff = (ri - ci).astype(F32)
        for h in range(RET_HEADS):
            rdec_ref[h] = jnp.where(causal, jnp.exp(jnp.maximum(diff, 0.0) * RET_LOG_GAMMA[h]), 0.0)

    @pl.when(c == 0)
    def _():
        cbuf_ref[5:8, :] = conv0_ref[0]
        ssmo_ref[0] = ssm0_ref[0]
        reto_ref[0] = ret0_ref[0]

    xbc_raw = proj_ref[0, :, OFF_XBC:OFF_Q]
    cbuf_ref[8:8 + C, :] = xbc_raw
    acc = convb_ref[...] + xbc_raw * convw_ref[3:4, :]
    for i in range(CONV_W - 1):
        acc = acc + cbuf_ref[5 + i:5 + i + C, :] * convw_ref[i:i + 1, :]
    xbc = _silu(acc)
    new_prev = cbuf_ref[5 + valid:8 + valid, :]
    cbuf_ref[5:8, :] = new_prev
    convo_ref[0] = new_prev

    xs = xbc[:, :D_SSM]
    bmat = xbc[:, D_SSM:D_SSM + SSM_GROUPS * D_STATE].astype(BF16)
    cmat = xbc[:, D_SSM + SSM_GROUPS * D_STATE:].astype(BF16)

    dt = _softplus(dtr_ref[0] + dtb_ref[...])
    if valid < C:
        dt = jnp.where(rowi < valid, dt, 0.0)
    la = dt * (-jnp.exp(alog_ref[...]))
    tril = causal.astype(BF16)
    triu = (ri <= ci).astype(BF16)
    eye = (ri == ci).astype(BF16)
    lcum = _exact_left(tril, la)
    lcum_t = _exact_tn(la, triu)
    dt_t = _exact_tn(dt, eye)
    expand = _head_expand()
    lcum_x = _exact_right(lcum, expand)
    dt_x = _exact_right(dt, expand)
    la_x = _exact_right(la, expand)
    last_x = lcum_x[C - 1:C, :]

    cbs = [_dot_nt(cmat[:, g * D_STATE:(g + 1) * D_STATE], bmat[:, g * D_STATE:(g + 1) * D_STATE])
           for g in range(SSM_GROUPS)]
    lane = lax.broadcasted_iota(jnp.int32, (C, LANES), 1)
    left = lane < SSM_HEADDIM
    y_intra = []
    for m in range(SSM_HEADS // 2):
        ws = []
        for h in (2 * m, 2 * m + 1):
            seg = lcum[:, h:h + 1] - lcum_t[h:h + 1, :]
            decay = jnp.exp(jnp.where(causal, seg, -jnp.inf))
            ws.append((cbs[h // (SSM_HEADS // SSM_GROUPS)] * decay * dt_t[h:h + 1, :]).astype(BF16))
        xm = xs[:, m * LANES:(m + 1) * LANES]
        xst = jnp.concatenate([jnp.where(left, xm, 0.0), jnp.where(left, 0.0, xm)], axis=0).astype(BF16)
        y_intra.append(_dot(jnp.concatenate(ws, axis=1), xst))
    y = jnp.concatenate(y_intra, axis=1)

    hstate = ssmo_ref[0]
    hb = hstate.astype(BF16)
    y_inter = jnp.concatenate(
        [_dot_nt(cmat[:, g * D_STATE:(g + 1) * D_STATE], hb[g * GROUP_DIM:(g + 1) * GROUP_DIM, :])
         for g in range(SSM_GROUPS)], axis=1)
    y = y + y_inter * jnp.exp(lcum_x) + dskip_ref[...] * xs

    xw = (xs * (jnp.exp(last_x - lcum_x) * dt_x)).astype(BF16)
    upd = jnp.concatenate(
        [_dot_tn(xw[:, g * GROUP_DIM:(g + 1) * GROUP_DIM], bmat[:, g * D_STATE:(g + 1) * D_STATE])
         for g in range(SSM_GROUPS)], axis=0)
    ones_cl = jnp.ones((C, LANES), BF16)
    chunk_decay = jnp.exp(_exact_tn(la_x, ones_cl))
    ssmo_ref[0] = chunk_decay * hstate + upd

    z = proj_ref[0, :, OFF_Z:OFF_XBC]
    y = y * _silu(z)
    y1 = jnp.concatenate([_rms(y[:, g * GROUP_DIM:(g + 1) * GROUP_DIM]) for g in range(SSM_GROUPS)],
                         axis=1) * sg_ref[...]

    pos = (pos_base + c * C).astype(F32) + rowf
    ang = pos * invf_ref[...]
    cos = jnp.cos(ang)
    sin = jnp.sin(ang)
    half = RET_HEADDIM // 2
    y2 = []
    for h in range(RET_HEADS):
        lg = RET_LOG_GAMMA[h]
        q1 = proj_ref[0, :, OFF_Q + h * RET_HEADDIM:OFF_Q + h * RET_HEADDIM + half]
        q2 = proj_ref[0, :, OFF_Q + h * RET_HEADDIM + half:OFF_Q + (h + 1) * RET_HEADDIM]
        k1 = proj_ref[0, :, OFF_K + h * RET_HEADDIM:OFF_K + h * RET_HEADDIM + half]
        k2 = proj_ref[0, :, OFF_K + h * RET_HEADDIM + half:OFF_K + (h + 1) * RET_HEADDIM]
        vh = proj_ref[0, :, OFF_V + h * RET_HEADDIM:OFF_V + (h + 1) * RET_HEADDIM].astype(BF16)
        qr = jnp.concatenate([q1 * cos - q2 * sin, q1 * sin + q2 * cos], axis=1)
        kr = jnp.concatenate([k1 * cos - k2 * sin, k1 * sin + k2 * cos], axis=1) * (RET_HEADDIM ** -0.5)
        if valid < C:
            kr = jnp.where(rowi < valid, kr, 0.0)
        qb = qr.astype(BF16)
        scores = _dot_nt(qb, kr.astype(BF16)) * rdec_ref[h]
        s_old = reto_ref[0, h * RET_HEADDIM:(h + 1) * RET_HEADDIM, :]
        yr = _dot(scores.astype(BF16), vh) + _dot(qb, s_old.astype(BF16)) * jnp.exp((rowf + 1.0) * lg)
        kw = (kr * jnp.exp((valid - 1.0 - rowf) * lg)).astype(BF16)
        reto_ref[0, h * RET_HEADDIM:(h + 1) * RET_HEADDIM, :] = (
            float(np.exp(np.float32(valid * lg))) * s_old + _dot_tn(kw, vh))
        y2.append(_rms(yr))
    gate = proj_ref[0, :, OFF_G:PROJ_MAIN]
    y2 = jnp.concatenate(y2, axis=1) * rg_ref[...] * _silu(gate)

    mix_ref[0, :, :D_SSM] = y1.astype(BF16)
    mix_ref[0, :, D_SSM:] = y2.astype(BF16)


def _mixer_seq(proj, dtr, conv0, ssm0, ret0, params, *, nchunks, chunk_offset, valid, pos_base, name):
    nb = proj.shape[0]
    row = lambda b, c: (b, c + chunk_offset, 0)
    const3 = lambda b, c: (0, 0, 0)
    const2 = lambda b, c: (0, 0)
    per_b = lambda b, c: (b, 0, 0)
    pspecs = [pl.BlockSpec(p.shape, const2) for p in params]
    kern = functools.partial(_mixer_seq_kernel, valid=valid, pos_base=pos_base)
    return pl.pallas_call(
        kern,
        out_shape=(jax.ShapeDtypeStruct((nb, nchunks * CHUNK, D_MODEL), BF16),
                   jax.ShapeDtypeStruct((nb, CONV_W - 1, CONV_DIM), F32),
                   jax.ShapeDtypeStruct((nb, D_SSM, D_STATE), F32),
                   jax.ShapeDtypeStruct((nb, D_RET, RET_HEADDIM), F32)),
        grid=(nb, nchunks),
        in_specs=[pl.BlockSpec((1, CHUNK, PROJ_MAIN), row),
                  pl.BlockSpec((1, CHUNK, LANES), row),
                  pl.BlockSpec((1, CONV_W - 1, CONV_DIM), const3),
                  pl.BlockSpec((1, D_SSM, D_STATE), const3),
                  pl.BlockSpec((1, D_RET, RET_HEADDIM), const3)] + pspecs,
        out_specs=(pl.BlockSpec((1, CHUNK, D_MODEL), lambda b, c: (b, c, 0)),
                   pl.BlockSpec((1, CONV_W - 1, CONV_DIM), per_b),
                   pl.BlockSpec((1, D_SSM, D_STATE), per_b),
                   pl.BlockSpec((1, D_RET, RET_HEADDIM), per_b)),
        scratch_shapes=[pltpu.VMEM((8 + CHUNK, CONV_DIM), F32),
                        pltpu.VMEM((RET_HEADS, CHUNK, CHUNK), F32)],
        compiler_params=pltpu.CompilerParams(
            dimension_semantics=("arbitrary", "arbitrary"), vmem_limit_bytes=VMEM_LIMIT),
        name=name,
    )(proj, dtr, conv0, ssm0, ret0, *params)


def _mixer_step_kernel(proj_ref, dtr_ref, conv_ref, ssm_ref, ret_ref,
                       convw_ref, convb_ref, dtb_ref, alog_ref, dskip_ref, sg_ref, rg_ref, invf_ref,
                       mix_ref, convo_ref, ssmo_ref, reto_ref, cols_ref):
    R = STEP_ROWS
    xbc_raw = proj_ref[:, OFF_XBC:OFF_Q]
    acc = convb_ref[...] + xbc_raw * convw_ref[3:4, :]
    for i in range(CONV_W - 1):
        acc = acc + conv_ref[i] * convw_ref[i:i + 1, :]
    xbc = _silu(acc)
    convo_ref[0] = conv_ref[1]
    convo_ref[1] = conv_ref[2]
    convo_ref[2] = xbc_raw

    xs = xbc[:, :D_SSM]
    bmat = xbc[:, D_SSM:D_SSM + SSM_GROUPS * D_STATE]
    cmat = xbc[:, D_SSM + SSM_GROUPS * D_STATE:]
    dt = _softplus(dtr_ref[...] + dtb_ref[...])
    la = dt * (-jnp.exp(alog_ref[...]))
    expand = _head_expand()
    dt_x = _exact_right(dt, expand)
    decay_x = jnp.exp(_exact_right(la, expand))
    xdt = xs * dt_x

    ang = jnp.float32(PAST_LEN) * invf_ref[...]
    cos = jnp.cos(ang)
    sin = jnp.sin(ang)
    half = RET_HEADDIM // 2
    qs, ks = [], []
    for h in range(RET_HEADS):
        q1 = proj_ref[:, OFF_Q + h * RET_HEADDIM:OFF_Q + h * RET_HEADDIM + half]
        q2 = proj_ref[:, OFF_Q + h * RET_HEADDIM + half:OFF_Q + (h + 1) * RET_HEADDIM]
        k1 = proj_ref[:, OFF_K + h * RET_HEADDIM:OFF_K + h * RET_HEADDIM + half]
        k2 = proj_ref[:, OFF_K + h * RET_HEADDIM + half:OFF_K + (h + 1) * RET_HEADDIM]
        qs += [q1 * cos - q2 * sin, q1 * sin + q2 * cos]
        ks += [(k1 * cos - k2 * sin) * (RET_HEADDIM ** -0.5), (k1 * sin + k2 * cos) * (RET_HEADDIM ** -0.5)]
    qr = jnp.concatenate(qs, axis=1)
    kr = jnp.concatenate(ks, axis=1)
    vv = proj_ref[:, OFF_V:OFF_G]

    allq = jnp.concatenate([decay_x, xdt, kr, qr], axis=1)
    hi = allq.astype(BF16).astype(F32)
    r1 = allq - hi
    mid = r1.astype(BF16).astype(F32)
    lo = (r1 - mid).astype(BF16).astype(F32)
    stack = jnp.concatenate([hi, mid, lo, jnp.zeros_like(hi)], axis=0).astype(BF16)
    krow = lax.broadcasted_iota(jnp.int32, (4 * R, LANES), 0)
    row8 = lax.broadcasted_iota(jnp.int32, (R, 1), 0)
    lane = lax.broadcasted_iota(jnp.int32, (1, LANES), 1)

    y_cols = jnp.zeros((D_SSM, LANES), F32)
    y_ret = jnp.zeros((R, D_RET), F32)
    for r in range(R):
        sel = ((krow % R == r) & (krow < 3 * R)).astype(BF16)
        cols_ref[...] = _dot_tn(stack, sel)
        ycol = []
        for g in range(SSM_GROUPS):
            rows = slice(g * GROUP_DIM, (g + 1) * GROUP_DIM)
            h_old = ssm_ref[r, rows, :]
            h_new = (h_old * cols_ref[g * GROUP_DIM:(g + 1) * GROUP_DIM, :]
                     + cols_ref[D_SSM + g * GROUP_DIM:D_SSM + (g + 1) * GROUP_DIM, :]
                     * bmat[r:r + 1, g * D_STATE:(g + 1) * D_STATE])
            ssmo_ref[r, rows, :] = h_new
            ycol.append(jnp.sum(h_new * cmat[r:r + 1, g * D_STATE:(g + 1) * D_STATE], axis=1, keepdims=True))
        y_cols = jnp.where(lane == r, jnp.concatenate(ycol, axis=0), y_cols)
        yrow = []
        for h in range(RET_HEADS):
            rows = slice(h * RET_HEADDIM, (h + 1) * RET_HEADDIM)
            kcol = cols_ref[2 * D_SSM + h * RET_HEADDIM:2 * D_SSM + (h + 1) * RET_HEADDIM, :]
            qcol = cols_ref[3 * D_SSM + h * RET_HEADDIM:3 * D_SSM + (h + 1) * RET_HEADDIM, :]
            gamma = float(np.exp(np.float32(RET_LOG_GAMMA[h])))
            s_new = (gamma * ret_ref[r, rows, :]
                     + jnp.concatenate([kcol, kcol], axis=1) * vv[r:r + 1, h * RET_HEADDIM:(h + 1) * RET_HEADDIM])
            reto_ref[r, rows, :] = s_new
            yrow.append(jnp.sum(jnp.concatenate([qcol, qcol], axis=1) * s_new, axis=0, keepdims=True))
        y_ret = jnp.where(row8 == r, jnp.concatenate(yrow, axis=1), y_ret)

    y_ssd = y_cols.T[:R, :]
    y = (y_ssd + dskip_ref[...] * xs) * _silu(proj_ref[:, OFF_Z:OFF_XBC])
    y1 = jnp.concatenate([_rms(y[:, g * GROUP_DIM:(g + 1) * GROUP_DIM]) for g in range(SSM_GROUPS)],
                         axis=1) * sg_ref[...]
    y2 = jnp.concatenate([_rms(y_ret[:, h * RET_HEADDIM:(h + 1) * RET_HEADDIM]) for h in range(RET_HEADS)],
                         axis=1) * rg_ref[...] * _silu(proj_ref[:, OFF_G:PROJ_MAIN])
    mix_ref[:, :D_SSM] = y1
    mix_ref[:, D_SSM:] = y2


def _mixer_step(proj, dtr, conv_t, ssm, ret, params, *, nb):
    R = STEP_ROWS
    rows2 = lambda i: (i, 0)
    rows3 = lambda i: (i, 0, 0)
    mid3 = lambda i: (0, i, 0)
    const2 = lambda i: (0, 0)
    pspecs = [pl.BlockSpec(p.shape, const2) for p in params]
    return pl.pallas_call(
        _mixer_step_kernel,
        out_shape=(jax.ShapeDtypeStruct((nb, D_MODEL), F32),
                   jax.ShapeDtypeStruct((CONV_W - 1, nb, CONV_DIM), F32),
                   jax.ShapeDtypeStruct((nb, D_SSM, D_STATE), F32),
                   jax.ShapeDtypeStruct((nb, D_RET, RET_HEADDIM), F32)),
        grid=(nb // R,),
        in_specs=[pl.BlockSpec((R, PROJ_MAIN), rows2),
                  pl.BlockSpec((R, LANES), rows2),
                  pl.BlockSpec((CONV_W - 1, R, CONV_DIM), mid3),
                  pl.BlockSpec((R, D_SSM, D_STATE), rows3),
                  pl.BlockSpec((R, D_RET, RET_HEADDIM), rows3)] + pspecs,
        out_specs=(pl.BlockSpec((R, D_MODEL), rows2),
                   pl.BlockSpec((CONV_W - 1, R, CONV_DIM), mid3),
                   pl.BlockSpec((R, D_SSM, D_STATE), rows3),
                   pl.BlockSpec((R, D_RET, RET_HEADDIM), rows3)),
        scratch_shapes=[pltpu.VMEM((4 * D_SSM, LANES), F32)],
        compiler_params=pltpu.CompilerParams(
            dimension_semantics=("arbitrary",), vmem_limit_bytes=VMEM_LIMIT),
        name="mixer_step",
    )(proj, dtr, conv_t, ssm, ret, *params)


def _outproj_kernel(mix_ref, w_ref, h_ref, g1_ref, g2_ref, hout_ref, f_ref):
    y = _dot(mix_ref[...].astype(BF16), w_ref[...])
    h = h_ref[...] + _rms(y) * g1_ref[...]
    hout_ref[...] = h
    f_ref[...] = (_rms(h) * g2_ref[...]).astype(BF16)


def _outproj(mix, w, h, g1, g2, *, bm, name):
    m = mix.shape[0]
    return pl.pallas_call(
        _outproj_kernel,
        out_shape=(jax.ShapeDtypeStruct((m, D_MODEL), F32),
                   jax.ShapeDtypeStruct((m, D_MODEL), BF16)),
        grid=(m // bm,),
        in_specs=[pl.BlockSpec((bm, D_MODEL), lambda i: (i, 0)),
                  pl.BlockSpec((D_MODEL, D_MODEL), lambda i: (0, 0)),
                  pl.BlockSpec((bm, D_MODEL), lambda i: (i, 0)),
                  pl.BlockSpec((1, D_MODEL), lambda i: (0, 0)),
                  pl.BlockSpec((1, D_MODEL), lambda i: (0, 0))],
        out_specs=(pl.BlockSpec((bm, D_MODEL), lambda i: (i, 0)),
                   pl.BlockSpec((bm, D_MODEL), lambda i: (i, 0))),
        compiler_params=pltpu.CompilerParams(
            dimension_semantics=("arbitrary",), vmem_limit_bytes=VMEM_LIMIT),
        name=name,
    )(mix, w, h, g1, g2)


def _ffn_up_kernel(f_ref, wg_ref, wu_ref, o_ref):
    f = f_ref[...]
    o_ref[...] = (_silu(_dot(f, wg_ref[...])) * _dot(f, wu_ref[...])).astype(BF16)


def _ffn_up(f, wg, wu, *, bm, bn, name):
    m = f.shape[0]
    return pl.pallas_call(
        _ffn_up_kernel,
        out_shape=jax.ShapeDtypeStruct((m, D_FF), BF16),
        grid=(m // bm, D_FF // bn),
        in_specs=[pl.BlockSpec((bm, D_MODEL), lambda i, j: (i, 0)),
                  pl.BlockSpec((D_MODEL, bn), lambda i, j: (0, j)),
                  pl.BlockSpec((D_MODEL, bn), lambda i, j: (0, j))],
        out_specs=pl.BlockSpec((bm, bn), lambda i, j: (i, j)),
        compiler_params=pltpu.CompilerParams(
            dimension_semantics=("arbitrary", "arbitrary"), vmem_limit_bytes=VMEM_LIMIT),
        name=name,
    )(f, wg, wu)


def _ffn_down_kernel(a_ref, w_ref, h_ref, g_ref, o_ref, acc_ref):
    k = pl.program_id(1)

    @pl.when(k == 0)
    def _():
        acc_ref[...] = jnp.zeros_like(acc_ref)

    acc_ref[...] += _dot(a_ref[...], w_ref[...])

    @pl.when(k == pl.num_programs(1) - 1)
    def _():
        o_ref[...] = h_ref[...] + _rms(acc_ref[...]) * g_ref[...]


def _ffn_down(a, w, h, g, *, bm, bk, name):
    m = a.shape[0]
    return pl.pallas_call(
        _ffn_down_kernel,
        out_shape=jax.ShapeDtypeStruct((m, D_MODEL), F32),
        grid=(m // bm, D_FF // bk),
        in_specs=[pl.BlockSpec((bm, bk), lambda i, k: (i, k)),
                  pl.BlockSpec((bk, D_MODEL), lambda i, k: (k, 0)),
                  pl.BlockSpec((bm, D_MODEL), lambda i, k: (i, 0)),
                  pl.BlockSpec((1, D_MODEL), lambda i, k: (0, 0))],
        out_specs=pl.BlockSpec((bm, D_MODEL), lambda i, k: (i, 0)),
        scratch_shapes=[pltpu.VMEM((bm, D_MODEL), F32)],
        compiler_params=pltpu.CompilerParams(
            dimension_semantics=("arbitrary", "arbitrary"), vmem_limit_bytes=VMEM_LIMIT),
        name=name,
    )(a, w, h, g)


def kernel(x_prompt, x_sample, state_conv, state_ssm, state_ret, meta_tokens, pre_mix_g, post_mix_g,
           pre_ffn_g, post_ffn_g, w_in, conv_w, conv_b, dt_bias, a_log, d_skip, ssm_norm_g, ret_norm_g,
           w_out, w_gate, w_up, w_down):
    bp, seq = x_prompt.shape[:2]
    bs = x_sample.shape[0]
    assert w_in.shape[0] == 1 and x_sample.shape[1] == 1 and seq % CHUNK == 0 and bs == CHUNK

    w_in0 = w_in[0]
    dt_lo = D_SSM + CONV_DIM
    w_main = jnp.concatenate([w_in0[:, :dt_lo], w_in0[:, dt_lo + SSM_HEADS:]], axis=1).astype(BF16)
    w_dt = jnp.pad(w_in0[:, dt_lo:dt_lo + SSM_HEADS], ((0, 0), (0, LANES - SSM_HEADS))).astype(BF16)
    w_out_b = w_out[0].astype(BF16)
    w_gate_b = w_gate[0].astype(BF16)
    w_up_b = w_up[0].astype(BF16)
    w_down_b = w_down[0].astype(BF16)
    pad16 = lambda v: jnp.pad(v, ((0, 0), (0, LANES - SSM_HEADS)))
    inv_freq = (ROPE_BASE ** (-jnp.arange(RET_HEADDIM // 2, dtype=F32) / (RET_HEADDIM // 2)))[None, :]
    params = (conv_w[0], conv_b, pad16(dt_bias), pad16(a_log),
              jnp.repeat(d_skip, SSM_HEADDIM, axis=1), ssm_norm_g, ret_norm_g, inv_freq)

    xp = x_prompt.reshape(bp * seq, D_MODEL)
    proj_p, dtr_p = _inproj(xp, pre_mix_g, w_main, w_dt, bm=1024, bn=512, name="inproj_prompt")
    xs_rows = x_sample.reshape(bs, D_MODEL)
    x_small = jnp.concatenate(
        [xs_rows, meta_tokens.astype(F32), jnp.zeros((CHUNK - N_META, D_MODEL), F32)], axis=0)
    proj_s, dtr_s = _inproj(x_small, pre_mix_g, w_main, w_dt, bm=2 * CHUNK, bn=512, name="inproj_small")

    zc = jnp.zeros((1, CONV_W - 1, CONV_DIM), F32)
    zs = jnp.zeros((1, D_SSM, D_STATE), F32)
    zr = jnp.zeros((1, D_RET, RET_HEADDIM), F32)
    _, m_conv, m_ssm, m_ret = _mixer_seq(
        proj_s.reshape(1, 2 * CHUNK, PROJ_MAIN), dtr_s.reshape(1, 2 * CHUNK, LANES), zc, zs, zr, params,
        nchunks=1, chunk_offset=1, valid=N_META, pos_base=0, name="mixer_meta")

    mix_p, p_conv, p_ssm, p_ret = _mixer_seq(
        proj_p.reshape(bp, seq, PROJ_MAIN), dtr_p.reshape(bp, seq, LANES), m_conv, m_ssm, m_ret, params,
        nchunks=seq // CHUNK, chunk_offset=0, valid=CHUNK, pos_base=N_META, name="mixer_prompt")

    conv_t = jnp.transpose(state_conv[0], (1, 0, 2))
    mix_s, s_conv_t, s_ssm, s_ret = _mixer_step(
        proj_s, dtr_s, conv_t, state_ssm[0].reshape(bs, D_SSM, D_STATE),
        state_ret[0].reshape(bs, D_RET, RET_HEADDIM), params, nb=bs)

    def tail(mix, h, bm_o, bm_u, bm_d, tag):
        h1, f = _outproj(mix, w_out_b, h, post_mix_g, pre_ffn_g, bm=bm_o, name="outproj_" + tag)
        act = _ffn_up(f, w_gate_b, w_up_b, bm=bm_u, bn=512, name="ffn_up_" + tag)
        return _ffn_down(act, w_down_b, h1, post_ffn_g, bm=bm_d, bk=1408, name="ffn_down_" + tag)

    y_p = tail(mix_p.reshape(bp * seq, D_MODEL), xp, 256, 1024, 512, "prompt")
    y_s = tail(mix_s, xs_rows, bs, bs, bs, "sample")

    return (y_p.reshape(bp, seq, D_MODEL),
            y_s.reshape(bs, 1, D_MODEL),
            p_conv[None],
            p_ssm.reshape(1, bp, SSM_HEADS, SSM_HEADDIM, D_STATE),
            p_ret.reshape(1, bp, RET_HEADS, RET_HEADDIM, RET_HEADDIM),
            jnp.transpose(s_conv_t, (1, 0, 2))[None],
            s_ssm.reshape(1, bs, SSM_HEADS, SSM_HEADDIM, D_STATE),
            s_ret.reshape(1, bs, RET_HEADS, RET_HEADDIM, RET_HEADDIM))
```

```python
import functools

import numpy as np
import jax
import jax.numpy as jnp
from jax import lax
from jax.experimental import pallas as pl
from jax.experimental.pallas import tpu as pltpu

F32 = jnp.float32
BF16 = jnp.bfloat16

D_MODEL = 2048
N_META = 16
CHUNK = 128
D_SSM = 1024
D_RET = 1024
SSM_HEADDIM = 64
SSM_HEADS = 16
SSM_GROUPS = 2
GROUP_DIM = D_SSM // SSM_GROUPS
D_STATE = 128
CONV_W = 4
CONV_DIM = D_SSM + 2 * SSM_GROUPS * D_STATE
RET_HEADS = 4
RET_HEADDIM = 256
ROPE_BASE = 10000.0
D_FF = 5632
EPS = 1e-6
PAST_LEN = 16384

LANES = 128
SUBLANES = 8
STEP_ROWS = SUBLANES
CONV_PAD = SUBLANES

OFF_Z = 0
OFF_XBC = D_SSM
OFF_Q = OFF_XBC + CONV_DIM
OFF_K = OFF_Q + D_RET
OFF_V = OFF_K + D_RET
OFF_G = OFF_V + D_RET
PROJ_MAIN = OFF_G + D_RET

VMEM_LIMIT = 56 * 1024 * 1024

RET_LOG_GAMMA = [float(np.log1p(-np.float32(2.0) ** np.float32(-5.0 - h)).astype(np.float32))
                 for h in range(RET_HEADS)]


def _silu(x):
    return x / (1.0 + jnp.exp(-x))


def _softplus(x):
    return jnp.maximum(x, 0.0) + jnp.log1p(jnp.exp(-jnp.abs(x)))


def _rms(x):
    return x * lax.rsqrt(jnp.mean(x * x, axis=-1, keepdims=True) + EPS)


def _split3(x):
    hi = x.astype(BF16)
    r = x - hi.astype(F32)
    mid = r.astype(BF16)
    lo = (r - mid.astype(F32)).astype(BF16)
    return hi, mid, lo


def _dot(a, b):
    return jnp.dot(a, b, preferred_element_type=F32)


def _dot_nt(a, b):
    return lax.dot_general(a, b, (((1,), (1,)), ((), ())), preferred_element_type=F32)


def _dot_tn(a, b):
    return lax.dot_general(a, b, (((0,), (0,)), ((), ())), preferred_element_type=F32)


def _exact_right(x, sel):
    hi, mid, lo = _split3(x)
    return _dot(hi, sel) + _dot(mid, sel) + _dot(lo, sel)


def _exact_left(sel, x):
    hi, mid, lo = _split3(x)
    return _dot(sel, hi) + _dot(sel, mid) + _dot(sel, lo)


def _exact_tn(x, sel):
    hi, mid, lo = _split3(x)
    return _dot_tn(hi, sel) + _dot_tn(mid, sel) + _dot_tn(lo, sel)


def _head_expand():
    r = lax.broadcasted_iota(jnp.int32, (LANES, D_SSM), 0)
    c = lax.broadcasted_iota(jnp.int32, (LANES, D_SSM), 1)
    return (c // SSM_HEADDIM == r).astype(BF16)


def _wprep_kernel(w_ref, o_ref, odt_ref):
    dt_lo = D_SSM + CONV_DIM
    o_ref[:, :dt_lo] = w_ref[:, :dt_lo].astype(BF16)
    o_ref[:, dt_lo:] = w_ref[:, dt_lo + SSM_HEADS:].astype(BF16)
    lane = lax.broadcasted_iota(jnp.int32, odt_ref.shape, 1)
    odt_ref[...] = jnp.where(lane < SSM_HEADS, w_ref[:, dt_lo:dt_lo + LANES], 0.0).astype(BF16)


def _wprep(w_in0, *, br):
    k, n = w_in0.shape
    return pl.pallas_call(
        _wprep_kernel,
        out_shape=(jax.ShapeDtypeStruct((k, PROJ_MAIN), BF16), jax.ShapeDtypeStruct((k, LANES), BF16)),
        grid=(k // br,),
        in_specs=[pl.BlockSpec((br, n), lambda i: (i, 0))],
        out_specs=(pl.BlockSpec((br, PROJ_MAIN), lambda i: (i, 0)), pl.BlockSpec((br, LANES), lambda i: (i, 0))),
        compiler_params=pltpu.CompilerParams(
            dimension_semantics=("arbitrary",), vmem_limit_bytes=VMEM_LIMIT),
        name="wprep",
    )(w_in0)


NORM_ROWS = 256


def _inproj_kernel(x_ref, g_ref, w_ref, wdt_ref, o_ref, odt_ref, u_ref, *, npro):
    j = pl.program_id(1)
    xr = x_ref.shape[0]

    @pl.when(j < npro)
    def _():
        def body(t, carry):
            src = pl.ds(pl.multiple_of(t * NORM_ROWS, NORM_ROWS), NORM_ROWS)
            dst = pl.ds(pl.multiple_of(j * xr + t * NORM_ROWS, NORM_ROWS), NORM_ROWS)
            u = (_rms(x_ref[src, :]) * g_ref[...]).astype(BF16)
            u_ref[dst, :] = u
            odt_ref[dst, :] = _dot(u, wdt_ref[...])
            return carry
        lax.fori_loop(0, xr // NORM_ROWS, body, 0)

    @pl.when(j >= npro)
    def _():
        o_ref[...] = _dot(u_ref[...], w_ref[...])


def _inproj(x, g, w_main, w_dt, *, bm, xr, bn, name):
    m = x.shape[0]
    npro = bm // xr
    col = lambda j: jnp.maximum(j - npro, 0)
    return pl.pallas_call(
        functools.partial(_inproj_kernel, npro=npro),
        out_shape=(jax.ShapeDtypeStruct((m, PROJ_MAIN), F32),
                   jax.ShapeDtypeStruct((m, LANES), F32)),
        grid=(m // bm, npro + PROJ_MAIN // bn),
        in_specs=[pl.BlockSpec((xr, D_MODEL), lambda i, j: (i * npro + jnp.minimum(j, npro - 1), 0)),
                  pl.BlockSpec((1, D_MODEL), lambda i, j: (0, 0)),
                  pl.BlockSpec((D_MODEL, bn), lambda i, j: (0, col(j))),
                  pl.BlockSpec((D_MODEL, LANES), lambda i, j: (0, 0))],
        out_specs=(pl.BlockSpec((bm, bn), lambda i, j: (i, col(j))),
                   pl.BlockSpec((bm, LANES), lambda i, j: (i, 0))),
        scratch_shapes=[pltpu.VMEM((bm, D_MODEL), BF16)],
        compiler_params=pltpu.CompilerParams(
            dimension_semantics=("arbitrary", "arbitrary"), vmem_limit_bytes=VMEM_LIMIT),
        name=name,
    )(x, g, w_main, w_dt)


def _mixer_seq_kernel(proj_ref, dtr_ref, conv0_ref, ssm0_ref, ret0_ref,
                      convw_ref, convb_ref, dtb_ref, alog_ref, dskip_ref, sg_ref, rg_ref, invf_ref,
                      mix_ref, convo_ref, ssmo_ref, reto_ref,
                      cbuf_ref, rdec_ref, *, valid, pos_base):
    C = CHUNK
    b = pl.program_id(0)
    c = pl.program_id(1)
    rowi = lax.broadcasted_iota(jnp.int32, (C, 1), 0)
    rowf = rowi.astype(F32)
    ri = lax.broadcasted_iota(jnp.int32, (C, C), 0)
    ci = lax.broadcasted_iota(jnp.int32, (C, C), 1)
    causal = ri >= ci

    @pl.when((b == 0) & (c == 0))
    def _():
        diff = (ri - ci).astype(F32)
        for h in range(RET_HEADS):
            rdec_ref[h] = jnp.where(causal, jnp.exp(jnp.maximum(diff, 0.0) * RET_LOG_GAMMA[h]), 0.0)

    hist = CONV_PAD - (CONV_W - 1)

    @pl.when(c == 0)
    def _():
        cbuf_ref[hist:CONV_PAD, :] = conv0_ref[0]
        ssmo_ref[0] = ssm0_ref[0]
        reto_ref[0] = ret0_ref[0]

    xbc_raw = proj_ref[0, :, OFF_XBC:OFF_Q]
    cbuf_ref[CONV_PAD:CONV_PAD + C, :] = xbc_raw
    acc = convb_ref[...] + xbc_raw * convw_ref[CONV_W - 1:CONV_W, :]
    for i in range(CONV_W - 1):
        acc = acc + cbuf_ref[hist + i:hist + i + C, :] * convw_ref[i:i + 1, :]
    xbc = _silu(acc)
    new_prev = cbuf_ref[hist + valid:CONV_PAD + valid, :]
    cbuf_ref[hist:CONV_PAD, :] = new_prev
    convo_ref[0] = new_prev

    xs = xbc[:, :D_SSM]
    bmat = xbc[:, D_SSM:D_SSM + SSM_GROUPS * D_STATE].astype(BF16)
    cmat = xbc[:, D_SSM + SSM_GROUPS * D_STATE:].astype(BF16)

    dt = _softplus(dtr_ref[0] + dtb_ref[...])
    if valid < C:
        dt = jnp.where(rowi < valid, dt, 0.0)
    la = dt * (-jnp.exp(alog_ref[...]))
    tril = causal.astype(BF16)
    triu = (ri <= ci).astype(BF16)
    eye = (ri == ci).astype(BF16)
    lcum = _exact_left(tril, la)
    lcum_t = _exact_tn(la, triu)
    dt_t = _exact_tn(dt, eye)
    expand = _head_expand()
    lcum_x = _exact_right(lcum, expand)
    dt_x = _exact_right(dt, expand)
    la_x = _exact_right(la, expand)
    last_x = lcum_x[C - 1:C, :]

    cbs = [_dot_nt(cmat[:, g * D_STATE:(g + 1) * D_STATE], bmat[:, g * D_STATE:(g + 1) * D_STATE])
           for g in range(SSM_GROUPS)]
    lane = lax.broadcasted_iota(jnp.int32, (C, LANES), 1)
    left = lane < SSM_HEADDIM
    y_intra = []
    for m in range(SSM_HEADS // 2):
        ws = []
        for h in (2 * m, 2 * m + 1):
            seg = lcum[:, h:h + 1] - lcum_t[h:h + 1, :]
            decay = jnp.exp(jnp.where(causal, seg, -jnp.inf))
            ws.append((cbs[h // (SSM_HEADS // SSM_GROUPS)] * decay * dt_t[h:h + 1, :]).astype(BF16))
        xm = xs[:, m * LANES:(m + 1) * LANES]
        xst = jnp.concatenate([jnp.where(left, xm, 0.0), jnp.where(left, 0.0, xm)], axis=0).astype(BF16)
        y_intra.append(_dot(jnp.concatenate(ws, axis=1), xst))
    y = jnp.concatenate(y_intra, axis=1)

    hstate = ssmo_ref[0]
    hb = hstate.astype(BF16)
    y_inter = jnp.concatenate(
        [_dot_nt(cmat[:, g * D_STATE:(g + 1) * D_STATE], hb[g * GROUP_DIM:(g + 1) * GROUP_DIM, :])
         for g in range(SSM_GROUPS)], axis=1)
    y = y + y_inter * jnp.exp(lcum_x) + dskip_ref[...] * xs

    xw = (xs * (jnp.exp(last_x - lcum_x) * dt_x)).astype(BF16)
    upd = jnp.concatenate(
        [_dot_tn(xw[:, g * GROUP_DIM:(g + 1) * GROUP_DIM], bmat[:, g * D_STATE:(g + 1) * D_STATE])
         for g in range(SSM_GROUPS)], axis=0)
    ones_cl = jnp.ones((C, LANES), BF16)
    chunk_decay = jnp.exp(_exact_tn(la_x, ones_cl))
    ssmo_ref[0] = chunk_decay * hstate + upd

    z = proj_ref[0, :, OFF_Z:OFF_XBC]
    y = y * _silu(z)
    y1 = jnp.concatenate([_rms(y[:, g * GROUP_DIM:(g + 1) * GROUP_DIM]) for g in range(SSM_GROUPS)],
                         axis=1) * sg_ref[...]

    pos = (pos_base + c * C).astype(F32) + rowf
    ang = pos * invf_ref[...]
    cos = jnp.cos(ang)
    sin = jnp.sin(ang)
    half = RET_HEADDIM // 2
    y2 = []
    for h in range(RET_HEADS):
        lg = RET_LOG_GAMMA[h]
        q1 = proj_ref[0, :, OFF_Q + h * RET_HEADDIM:OFF_Q + h * RET_HEADDIM + half]
        q2 = proj_ref[0, :, OFF_Q + h * RET_HEADDIM + half:OFF_Q + (h + 1) * RET_HEADDIM]
        k1 = proj_ref[0, :, OFF_K + h * RET_HEADDIM:OFF_K + h * RET_HEADDIM + half]
        k2 = proj_ref[0, :, OFF_K + h * RET_HEADDIM + half:OFF_K + (h + 1) * RET_HEADDIM]
        vh = proj_ref[0, :, OFF_V + h * RET_HEADDIM:OFF_V + (h + 1) * RET_HEADDIM].astype(BF16)
        qr = jnp.concatenate([q1 * cos - q2 * sin, q1 * sin + q2 * cos], axis=1)
        kr = jnp.concatenate([k1 * cos - k2 * sin, k1 * sin + k2 * cos], axis=1) * (RET_HEADDIM ** -0.5)
        if valid < C:
            kr = jnp.where(rowi < valid, kr, 0.0)
        qb = qr.astype(BF16)
        scores = _dot_nt(qb, kr.astype(BF16)) * rdec_ref[h]
        s_old = reto_ref[0, h * RET_HEADDIM:(h + 1) * RET_HEADDIM, :]
        yr = _dot(scores.astype(BF16), vh) + _dot(qb, s_old.astype(BF16)) * jnp.exp((rowf + 1.0) * lg)
        kw = (kr * jnp.exp((valid - 1.0 - rowf) * lg)).astype(BF16)
        reto_ref[0, h * RET_HEADDIM:(h + 1) * RET_HEADDIM, :] = (
            float(np.exp(np.float32(valid * lg))) * s_old + _dot_tn(kw, vh))
        y2.append(_rms(yr))
    gate = proj_ref[0, :, OFF_G:PROJ_MAIN]
    y2 = jnp.concatenate(y2, axis=1) * rg_ref[...] * _silu(gate)

    mix_ref[0, :, :D_SSM] = y1.astype(BF16)
    mix_ref[0, :, D_SSM:] = y2.astype(BF16)


def _mixer_seq(proj, dtr, conv0, ssm0, ret0, params, *, nchunks, chunk_offset, valid, pos_base, name):
    nb = proj.shape[0]
    row = lambda b, c: (b, c + chunk_offset, 0)
    const3 = lambda b, c: (0, 0, 0)
    const2 = lambda b, c: (0, 0)
    per_b = lambda b, c: (b, 0, 0)
    pspecs = [pl.BlockSpec(p.shape, const2) for p in params]
    kern = functools.partial(_mixer_seq_kernel, valid=valid, pos_base=pos_base)
    return pl.pallas_call(
        kern,
        out_shape=(jax.ShapeDtypeStruct((nb, nchunks * CHUNK, D_MODEL), BF16),
                   jax.ShapeDtypeStruct((nb, CONV_W - 1, CONV_DIM), F32),
                   jax.ShapeDtypeStruct((nb, D_SSM, D_STATE), F32),
                   jax.ShapeDtypeStruct((nb, D_RET, RET_HEADDIM), F32)),
        grid=(nb, nchunks),
        in_specs=[pl.BlockSpec((1, CHUNK, PROJ_MAIN), row),
                  pl.BlockSpec((1, CHUNK, LANES), row),
                  pl.BlockSpec((1, CONV_W - 1, CONV_DIM), const3),
                  pl.BlockSpec((1, D_SSM, D_STATE), const3),
                  pl.BlockSpec((1, D_RET, RET_HEADDIM), const3)] + pspecs,
        out_specs=(pl.BlockSpec((1, CHUNK, D_MODEL), lambda b, c: (b, c, 0)),
                   pl.BlockSpec((1, CONV_W - 1, CONV_DIM), per_b),
                   pl.BlockSpec((1, D_SSM, D_STATE), per_b),
                   pl.BlockSpec((1, D_RET, RET_HEADDIM), per_b)),
        scratch_shapes=[pltpu.VMEM((CONV_PAD + CHUNK, CONV_DIM), F32),
                        pltpu.VMEM((RET_HEADS, CHUNK, CHUNK), F32)],
        compiler_params=pltpu.CompilerParams(
            dimension_semantics=("arbitrary", "arbitrary"), vmem_limit_bytes=VMEM_LIMIT),
        name=name,
    )(proj, dtr, conv0, ssm0, ret0, *params)


def _mixer_step_kernel(proj_ref, dtr_ref, conv_ref, ssm_ref, ret_ref,
                       convw_ref, convb_ref, dtb_ref, alog_ref, dskip_ref, sg_ref, rg_ref, invf_ref,
                       mix_ref, convo_ref, ssmo_ref, reto_ref, cols_ref):
    R = STEP_ROWS
    xbc_raw = proj_ref[:, OFF_XBC:OFF_Q]
    acc = convb_ref[...] + xbc_raw * convw_ref[3:4, :]
    for i in range(CONV_W - 1):
        acc = acc + conv_ref[i] * convw_ref[i:i + 1, :]
    xbc = _silu(acc)
    convo_ref[0] = conv_ref[1]
    convo_ref[1] = conv_ref[2]
    convo_ref[2] = xbc_raw

    xs = xbc[:, :D_SSM]
    bmat = xbc[:, D_SSM:D_SSM + SSM_GROUPS * D_STATE]
    cmat = xbc[:, D_SSM + SSM_GROUPS * D_STATE:]
    dt = _softplus(dtr_ref[...] + dtb_ref[...])
    la = dt * (-jnp.exp(alog_ref[...]))
    expand = _head_expand()
    dt_x = _exact_right(dt, expand)
    decay_x = jnp.exp(_exact_right(la, expand))
    xdt = xs * dt_x

    ang = jnp.float32(PAST_LEN) * invf_ref[...]
    cos = jnp.cos(ang)
    sin = jnp.sin(ang)
    half = RET_HEADDIM // 2
    qs, ks = [], []
    for h in range(RET_HEADS):
        q1 = proj_ref[:, OFF_Q + h * RET_HEADDIM:OFF_Q + h * RET_HEADDIM + half]
        q2 = proj_ref[:, OFF_Q + h * RET_HEADDIM + half:OFF_Q + (h + 1) * RET_HEADDIM]
        k1 = proj_ref[:, OFF_K + h * RET_HEADDIM:OFF_K + h * RET_HEADDIM + half]
        k2 = proj_ref[:, OFF_K + h * RET_HEADDIM + half:OFF_K + (h + 1) * RET_HEADDIM]
        qs += [q1 * cos - q2 * sin, q1 * sin + q2 * cos]
        ks += [(k1 * cos - k2 * sin) * (RET_HEADDIM ** -0.5), (k1 * sin + k2 * cos) * (RET_HEADDIM ** -0.5)]
    qr = jnp.concatenate(qs, axis=1)
    kr = jnp.concatenate(ks, axis=1)
    vv = proj_ref[:, OFF_V:OFF_G]

    allq = jnp.concatenate([decay_x, xdt, kr, qr], axis=1)
    hi = allq.astype(BF16).astype(F32)
    r1 = allq - hi
    mid = r1.astype(BF16).astype(F32)
    lo = (r1 - mid).astype(BF16).astype(F32)
    stack = jnp.concatenate([hi, mid, lo, jnp.zeros_like(hi)], axis=0).astype(BF16)
    krow = lax.broadcasted_iota(jnp.int32, (4 * R, LANES), 0)
    row8 = lax.broadcasted_iota(jnp.int32, (R, 1), 0)
    lane = lax.broadcasted_iota(jnp.int32, (1, LANES), 1)

    y_cols = jnp.zeros((D_SSM, LANES), F32)
    y_ret = jnp.zeros((R, D_RET), F32)
    for r in range(R):
        sel = ((krow % R == r) & (krow < 3 * R)).astype(BF16)
        cols_ref[...] = _dot_tn(stack, sel)
        ycol = []
        for g in range(SSM_GROUPS):
            rows = slice(g * GROUP_DIM, (g + 1) * GROUP_DIM)
            h_old = ssm_ref[r, rows, :]
            h_new = (h_old * cols_ref[g * GROUP_DIM:(g + 1) * GROUP_DIM, :]
                     + cols_ref[D_SSM + g * GROUP_DIM:D_SSM + (g + 1) * GROUP_DIM, :]
                     * bmat[r:r + 1, g * D_STATE:(g + 1) * D_STATE])
            ssmo_ref[r, rows, :] = h_new
            ycol.append(jnp.sum(h_new * cmat[r:r + 1, g * D_STATE:(g + 1) * D_STATE], axis=1, keepdims=True))
        y_cols = jnp.where(lane == r, jnp.concatenate(ycol, axis=0), y_cols)
        yrow = []
        for h in range(RET_HEADS):
            rows = slice(h * RET_HEADDIM, (h + 1) * RET_HEADDIM)
            kcol = cols_ref[2 * D_SSM + h * RET_HEADDIM:2 * D_SSM + (h + 1) * RET_HEADDIM, :]
            qcol = cols_ref[3 * D_SSM + h * RET_HEADDIM:3 * D_SSM + (h + 1) * RET_HEADDIM, :]
            gamma = float(np.exp(np.float32(RET_LOG_GAMMA[h])))
            s_new = (gamma * ret_ref[r, rows, :]
                     + jnp.concatenate([kcol, kcol], axis=1) * vv[r:r + 1, h * RET_HEADDIM:(h + 1) * RET_HEADDIM])
            reto_ref[r, rows, :] = s_new
            yrow.append(jnp.sum(jnp.concatenate([qcol, qcol], axis=1) * s_new, axis=0, keepdims=True))
        y_ret = jnp.where(row8 == r, jnp.concatenate(yrow, axis=1), y_ret)

    y_ssd = y_cols.T[:R, :]
    y = (y_ssd + dskip_ref[...] * xs) * _silu(proj_ref[:, OFF_Z:OFF_XBC])
    y1 = jnp.concatenate([_rms(y[:, g * GROUP_DIM:(g + 1) * GROUP_DIM]) for g in range(SSM_GROUPS)],
                         axis=1) * sg_ref[...]
    y2 = jnp.concatenate([_rms(y_ret[:, h * RET_HEADDIM:(h + 1) * RET_HEADDIM]) for h in range(RET_HEADS)],
                         axis=1) * rg_ref[...] * _silu(proj_ref[:, OFF_G:PROJ_MAIN])
    mix_ref[:, :D_SSM] = y1
    mix_ref[:, D_SSM:] = y2


def _mixer_step(proj, dtr, conv_t, ssm, ret, params, *, nb):
    R = STEP_ROWS
    rows2 = lambda i: (i, 0)
    rows3 = lambda i: (i, 0, 0)
    mid3 = lambda i: (0, i, 0)
    const2 = lambda i: (0, 0)
    pspecs = [pl.BlockSpec(p.shape, const2) for p in params]
    return pl.pallas_call(
        _mixer_step_kernel,
        out_shape=(jax.ShapeDtypeStruct((nb, D_MODEL), F32),
                   jax.ShapeDtypeStruct((CONV_W - 1, nb, CONV_DIM), F32),
                   jax.ShapeDtypeStruct((nb, D_SSM, D_STATE), F32),
                   jax.ShapeDtypeStruct((nb, D_RET, RET_HEADDIM), F32)),
        grid=(nb // R,),
        in_specs=[pl.BlockSpec((R, PROJ_MAIN), rows2),
                  pl.BlockSpec((R, LANES), rows2),
                  pl.BlockSpec((CONV_W - 1, R, CONV_DIM), mid3),
                  pl.BlockSpec((R, D_SSM, D_STATE), rows3),
                  pl.BlockSpec((R, D_RET, RET_HEADDIM), rows3)] + pspecs,
        out_specs=(pl.BlockSpec((R, D_MODEL), rows2),
                   pl.BlockSpec((CONV_W - 1, R, CONV_DIM), mid3),
                   pl.BlockSpec((R, D_SSM, D_STATE), rows3),
                   pl.BlockSpec((R, D_RET, RET_HEADDIM), rows3)),
        scratch_shapes=[pltpu.VMEM((4 * D_SSM, LANES), F32)],
        compiler_params=pltpu.CompilerParams(
            dimension_semantics=("arbitrary",), vmem_limit_bytes=VMEM_LIMIT),
        name="mixer_step",
    )(proj, dtr, conv_t, ssm, ret, *params)


def _outproj_kernel(mix_ref, w_ref, h_ref, g1_ref, g2_ref, hout_ref, f_ref):
    y = _dot(mix_ref[...].astype(BF16), w_ref[...])
    h = h_ref[...] + _rms(y) * g1_ref[...]
    hout_ref[...] = h
    f_ref[...] = (_rms(h) * g2_ref[...]).astype(BF16)


def _outproj(mix, w, h, g1, g2, *, bm, name):
    m = mix.shape[0]
    return pl.pallas_call(
        _outproj_kernel,
        out_shape=(jax.ShapeDtypeStruct((m, D_MODEL), F32),
                   jax.ShapeDtypeStruct((m, D_MODEL), BF16)),
        grid=(m // bm,),
        in_specs=[pl.BlockSpec((bm, D_MODEL), lambda i: (i, 0)),
                  pl.BlockSpec((D_MODEL, D_MODEL), lambda i: (0, 0)),
                  pl.BlockSpec((bm, D_MODEL), lambda i: (i, 0)),
                  pl.BlockSpec((1, D_MODEL), lambda i: (0, 0)),
                  pl.BlockSpec((1, D_MODEL), lambda i: (0, 0))],
        out_specs=(pl.BlockSpec((bm, D_MODEL), lambda i: (i, 0)),
                   pl.BlockSpec((bm, D_MODEL), lambda i: (i, 0))),
        compiler_params=pltpu.CompilerParams(
            dimension_semantics=("arbitrary",), vmem_limit_bytes=VMEM_LIMIT),
        name=name,
    )(mix, w, h, g1, g2)


def _ffn_up_kernel(f_ref, wg_ref, wu_ref, o_ref):
    f = f_ref[...]
    gate = _dot(f, wg_ref[...].astype(BF16))
    up = _dot(f, wu_ref[...].astype(BF16))
    o_ref[...] = (_silu(gate) * up).astype(BF16)


def _ffn_up(f, wg, wu, *, bm, bn, name):
    m = f.shape[0]
    return pl.pallas_call(
        _ffn_up_kernel,
        out_shape=jax.ShapeDtypeStruct((m, D_FF), BF16),
        grid=(m // bm, D_FF // bn),
        in_specs=[pl.BlockSpec((bm, D_MODEL), lambda i, j: (i, 0)),
                  pl.BlockSpec((D_MODEL, bn), lambda i, j: (0, j)),
                  pl.BlockSpec((D_MODEL, bn), lambda i, j: (0, j))],
        out_specs=pl.BlockSpec((bm, bn), lambda i, j: (i, j)),
        compiler_params=pltpu.CompilerParams(
            dimension_semantics=("arbitrary", "arbitrary"), vmem_limit_bytes=VMEM_LIMIT),
        name=name,
    )(f, wg, wu)


def _ffn_down_kernel(a_ref, w_ref, h_ref, g_ref, o_ref, y_ref, *, bn):
    j = pl.program_id(1)
    y_ref[j] = _dot(a_ref[...], w_ref[...])

    @pl.when(j == pl.num_programs(1) - 1)
    def _():
        nt = y_ref.shape[0]
        ssq = sum(jnp.sum(y_ref[t] * y_ref[t], axis=-1, keepdims=True) for t in range(nt))
        scale = lax.rsqrt(ssq / (nt * bn) + EPS)
        for t in range(nt):
            cols = slice(t * bn, (t + 1) * bn)
            o_ref[:, cols] = h_ref[:, cols] + y_ref[t] * scale * g_ref[:, cols]


def _ffn_down(a, w, h, g, *, bm, bn, name):
    m = a.shape[0]
    return pl.pallas_call(
        functools.partial(_ffn_down_kernel, bn=bn),
        out_shape=jax.ShapeDtypeStruct((m, D_MODEL), F32),
        grid=(m // bm, D_MODEL // bn),
        in_specs=[pl.BlockSpec((bm, D_FF), lambda i, j: (i, 0)),
                  pl.BlockSpec((D_FF, bn), lambda i, j: (0, j)),
                  pl.BlockSpec((bm, D_MODEL), lambda i, j: (i, 0)),
                  pl.BlockSpec((1, D_MODEL), lambda i, j: (0, 0))],
        out_specs=pl.BlockSpec((bm, D_MODEL), lambda i, j: (i, 0)),
        scratch_shapes=[pltpu.VMEM((D_MODEL // bn, bm, bn), F32)],
        compiler_params=pltpu.CompilerParams(
            dimension_semantics=("arbitrary", "arbitrary"), vmem_limit_bytes=VMEM_LIMIT),
        name=name,
    )(a, w, h, g)


def kernel(x_prompt, x_sample, state_conv, state_ssm, state_ret, meta_tokens, pre_mix_g, post_mix_g,
           pre_ffn_g, post_ffn_g, w_in, conv_w, conv_b, dt_bias, a_log, d_skip, ssm_norm_g, ret_norm_g,
           w_out, w_gate, w_up, w_down):
    bp, seq = x_prompt.shape[:2]
    bs = x_sample.shape[0]
    assert w_in.shape[0] == 1 and x_sample.shape[1] == 1 and seq % CHUNK == 0 and bs == CHUNK

    w_main, w_dt = _wprep(w_in[0], br=256)
    w_out_b = w_out[0].astype(BF16)
    w_down_b = w_down[0].astype(BF16)
    pad16 = lambda v: jnp.pad(v, ((0, 0), (0, LANES - SSM_HEADS)))
    inv_freq = (ROPE_BASE ** (-jnp.arange(RET_HEADDIM // 2, dtype=F32) / (RET_HEADDIM // 2)))[None, :]
    params = (conv_w[0], conv_b, pad16(dt_bias), pad16(a_log),
              jnp.repeat(d_skip, SSM_HEADDIM, axis=1), ssm_norm_g, ret_norm_g, inv_freq)

    xp = x_prompt.reshape(bp * seq, D_MODEL)
    proj_p, dtr_p = _inproj(xp, pre_mix_g, w_main, w_dt, bm=2048, xr=1024, bn=512, name="inproj_prompt")
    xs_rows = x_sample.reshape(bs, D_MODEL)
    x_small = jnp.concatenate(
        [xs_rows, meta_tokens.astype(F32), jnp.zeros((CHUNK - N_META, D_MODEL), F32)], axis=0)
    proj_s, dtr_s = _inproj(x_small, pre_mix_g, w_main, w_dt, bm=2 * CHUNK, xr=2 * CHUNK, bn=512,
                           name="inproj_small")

    zc = jnp.zeros((1, CONV_W - 1, CONV_DIM), F32)
    zs = jnp.zeros((1, D_SSM, D_STATE), F32)
    zr = jnp.zeros((1, D_RET, RET_HEADDIM), F32)
    _, m_conv, m_ssm, m_ret = _mixer_seq(
        proj_s.reshape(1, 2 * CHUNK, PROJ_MAIN), dtr_s.reshape(1, 2 * CHUNK, LANES), zc, zs, zr, params,
        nchunks=1, chunk_offset=1, valid=N_META, pos_base=0, name="mixer_meta")

    mix_p, p_conv, p_ssm, p_ret = _mixer_seq(
        proj_p.reshape(bp, seq, PROJ_MAIN), dtr_p.reshape(bp, seq, LANES), m_conv, m_ssm, m_ret, params,
        nchunks=seq // CHUNK, chunk_offset=0, valid=CHUNK, pos_base=N_META, name="mixer_prompt")

    conv_t = jnp.transpose(state_conv[0], (1, 0, 2))
    mix_s, s_conv_t, s_ssm, s_ret = _mixer_step(
        proj_s, dtr_s, conv_t, state_ssm[0].reshape(bs, D_SSM, D_STATE),
        state_ret[0].reshape(bs, D_RET, RET_HEADDIM), params, nb=bs)

    def tail(mix, h, bm_o, bm_u, bm_d, tag):
        h1, f = _outproj(mix, w_out_b, h, post_mix_g, pre_ffn_g, bm=bm_o, name="outproj_" + tag)
        act = _ffn_up(f, w_gate[0], w_up[0], bm=bm_u, bn=512, name="ffn_up_" + tag)
        return _ffn_down(act, w_down_b, h1, post_ffn_g, bm=bm_d, bn=512, name="ffn_down_" + tag)

    y_p = tail(mix_p.reshape(bp * seq, D_MODEL), xp, 512, 2048, 512, "prompt")
    y_s = tail(mix_s, xs_rows, bs, bs, bs, "sample")

    return (y_p.reshape(bp, seq, D_MODEL),
            y_s.reshape(bs, 1, D_MODEL),
            p_conv[None],
            p_ssm.reshape(1, bp, SSM_HEADS, SSM_HEADDIM, D_STATE),
            p_ret.reshape(1, bp, RET_HEADS, RET_HEADDIM, RET_HEADDIM),
            jnp.transpose(s_conv_t, (1, 0, 2))[None],
            s_ssm.reshape(1, bs, SSM_HEADS, SSM_HEADDIM, D_STATE),
            s_ret.reshape(1, bs, RET_HEADS, RET_HEADDIM, RET_HEADDIM))
```

```python
import functools

import numpy as np
import jax
import jax.numpy as jnp
from jax import lax
from jax.experimental import pallas as pl
from jax.experimental.pallas import tpu as pltpu

F32 = jnp.float32
BF16 = jnp.bfloat16

D_MODEL = 2048
N_META = 16
CHUNK = 128
D_SSM = 1024
D_RET = 1024
SSM_HEADDIM = 64
SSM_HEADS = 16
SSM_GROUPS = 2
GROUP_DIM = D_SSM // SSM_GROUPS
D_STATE = 128
CONV_W = 4
CONV_DIM = D_SSM + 2 * SSM_GROUPS * D_STATE
RET_HEADS = 4
RET_HEADDIM = 256
ROPE_BASE = 10000.0
D_FF = 5632
EPS = 1e-6
PAST_LEN = 16384

LANES = 128
SUBLANES = 8
STEP_ROWS = SUBLANES
CONV_PAD = SUBLANES

OFF_Z = 0
OFF_XBC = D_SSM
OFF_Q = OFF_XBC + CONV_DIM
OFF_K = OFF_Q + D_RET
OFF_V = OFF_K + D_RET
OFF_G = OFF_V + D_RET
PROJ_MAIN = OFF_G + D_RET

VMEM_LIMIT = 56 * 1024 * 1024

RET_LOG_GAMMA = [float(np.log1p(-np.float32(2.0) ** np.float32(-5.0 - h)).astype(np.float32))
                 for h in range(RET_HEADS)]


def _silu(x):
    return x / (1.0 + jnp.exp(-x))


def _softplus(x):
    return jnp.maximum(x, 0.0) + jnp.log1p(jnp.exp(-jnp.abs(x)))


def _rms(x):
    return x * lax.rsqrt(jnp.mean(x * x, axis=-1, keepdims=True) + EPS)


def _split3(x):
    hi = x.astype(BF16)
    r = x - hi.astype(F32)
    mid = r.astype(BF16)
    lo = (r - mid.astype(F32)).astype(BF16)
    return hi, mid, lo


def _dot(a, b):
    return jnp.dot(a, b, preferred_element_type=F32)


def _dot_nt(a, b):
    return lax.dot_general(a, b, (((1,), (1,)), ((), ())), preferred_element_type=F32)


def _dot_tn(a, b):
    return lax.dot_general(a, b, (((0,), (0,)), ((), ())), preferred_element_type=F32)


def _exact_right(x, sel):
    hi, mid, lo = _split3(x)
    return _dot(hi, sel) + _dot(mid, sel) + _dot(lo, sel)


def _exact_left(sel, x):
    hi, mid, lo = _split3(x)
    return _dot(sel, hi) + _dot(sel, mid) + _dot(sel, lo)


def _exact_tn(x, sel):
    hi, mid, lo = _split3(x)
    return _dot_tn(hi, sel) + _dot_tn(mid, sel) + _dot_tn(lo, sel)


def _head_expand():
    r = lax.broadcasted_iota(jnp.int32, (LANES, D_SSM), 0)
    c = lax.broadcasted_iota(jnp.int32, (LANES, D_SSM), 1)
    return (c // SSM_HEADDIM == r).astype(BF16)


NORM_ROWS = 256
DT_ROW = D_SSM + CONV_DIM


def _inproj_kernel(x_ref, g_ref, wt_ref, wdt_ref, o_ref, odt_ref, u_ref, *, npro, nsplit):
    j = pl.program_id(1)
    xr = x_ref.shape[0]

    @pl.when(j < npro)
    def _():
        wdt = wdt_ref[...].astype(BF16)
        lane = lax.broadcasted_iota(jnp.int32, (NORM_ROWS, LANES), 1)

        def body(t, carry):
            src = pl.ds(pl.multiple_of(t * NORM_ROWS, NORM_ROWS), NORM_ROWS)
            dst = pl.ds(pl.multiple_of(j * xr + t * NORM_ROWS, NORM_ROWS), NORM_ROWS)
            u = (_rms(x_ref[src, :]) * g_ref[...]).astype(BF16)
            u_ref[dst, :] = u
            odt_ref[dst, :] = jnp.where(lane < SSM_HEADS, _dot_nt(u, wdt), 0.0)
            return carry
        lax.fori_loop(0, xr // NORM_ROWS, body, 0)

    @pl.when(j >= npro)
    def _():
        sub = wt_ref.shape[0] // nsplit
        for s in range(nsplit):
            w = wt_ref[s * sub:(s + 1) * sub, :].astype(BF16)
            o_ref[:, s * sub:(s + 1) * sub] = _dot_nt(u_ref[...], w)


def _inproj(x, g, wt, *, bm, xr, bn, name):
    m = x.shape[0]
    npro = bm // xr
    assert DT_ROW % bn == 0

    def wrow(i, j):
        t = jnp.maximum(j - npro, 0)
        skip = jnp.where(t * bn >= DT_ROW, SSM_HEADS // SUBLANES, 0)
        return ((t * (bn // SUBLANES) + skip) * SUBLANES, 0)

    col = lambda j: jnp.maximum(j - npro, 0)
    return pl.pallas_call(
        functools.partial(_inproj_kernel, npro=npro, nsplit=2),
        out_shape=(jax.ShapeDtypeStruct((m, PROJ_MAIN), F32),
                   jax.ShapeDtypeStruct((m, LANES), F32)),
        grid=(m // bm, npro + PROJ_MAIN // bn),
        in_specs=[pl.BlockSpec((xr, D_MODEL), lambda i, j: (i * npro + jnp.minimum(j, npro - 1), 0)),
                  pl.BlockSpec((1, D_MODEL), lambda i, j: (0, 0)),
                  pl.BlockSpec((pl.Element(bn), pl.Element(D_MODEL)), wrow),
                  pl.BlockSpec((pl.Element(LANES), pl.Element(D_MODEL)), lambda i, j: (DT_ROW, 0))],
        out_specs=(pl.BlockSpec((bm, bn), lambda i, j: (i, col(j))),
                   pl.BlockSpec((bm, LANES), lambda i, j: (i, 0))),
        scratch_shapes=[pltpu.VMEM((bm, D_MODEL), BF16)],
        compiler_params=pltpu.CompilerParams(
            dimension_semantics=("arbitrary", "arbitrary"), vmem_limit_bytes=VMEM_LIMIT),
        name=name,
    )(x, g, wt, wt)


def _mixer_seq_kernel(proj_ref, dtr_ref, conv0_ref, ssm0_ref, ret0_ref,
                      convw_ref, convb_ref, dtb_ref, alog_ref, dskip_ref, sg_ref, rg_ref, invf_ref,
                      mix_ref, convo_ref, ssmo_ref, reto_ref,
                      cbuf_ref, rdec_ref, *, valid, pos_base):
    C = CHUNK
    b = pl.program_id(0)
    c = pl.program_id(1)
    rowi = lax.broadcasted_iota(jnp.int32, (C, 1), 0)
    rowf = rowi.astype(F32)
    ri = lax.broadcasted_iota(jnp.int32, (C, C), 0)
    ci = lax.broadcasted_iota(jnp.int32, (C, C), 1)
    causal = ri >= ci

    @pl.when((b == 0) & (c == 0))
    def _():
        diff = (ri - ci).astype(F32)
        for h in range(RET_HEADS):
            rdec_ref[h] = jnp.where(causal, jnp.exp(jnp.maximum(diff, 0.0) * RET_LOG_GAMMA[h]), 0.0)

    hist = CONV_PAD - (CONV_W - 1)

    @pl.when(c == 0)
    def _():
        cbuf_ref[hist:CONV_PAD, :] = conv0_ref[0]
        ssmo_ref[0] = ssm0_ref[0]
        reto_ref[0] = ret0_ref[0]

    xbc_raw = proj_ref[0, :, OFF_XBC:OFF_Q]
    cbuf_ref[CONV_PAD:CONV_PAD + C, :] = xbc_raw
    acc = convb_ref[...] + xbc_raw * convw_ref[CONV_W - 1:CONV_W, :]
    for i in range(CONV_W - 1):
        acc = acc + cbuf_ref[hist + i:hist + i + C, :] * convw_ref[i:i + 1, :]
    xbc = _silu(acc)
    new_prev = cbuf_ref[hist + valid:CONV_PAD + valid, :]
    cbuf_ref[hist:CONV_PAD, :] = new_prev
    convo_ref[0] = new_prev

    xs = xbc[:, :D_SSM]
    bmat = xbc[:, D_SSM:D_SSM + SSM_GROUPS * D_STATE].astype(BF16)
    cmat = xbc[:, D_SSM + SSM_GROUPS * D_STATE:].astype(BF16)

    dt = _softplus(dtr_ref[0] + dtb_ref[...])
    if valid < C:
        dt = jnp.where(rowi < valid, dt, 0.0)
    la = dt * (-jnp.exp(alog_ref[...]))
    tril = causal.astype(BF16)
    triu = (ri <= ci).astype(BF16)
    eye = (ri == ci).astype(BF16)
    lcum = _exact_left(tril, la)
    lcum_t = _exact_tn(la, triu)
    dt_t = _exact_tn(dt, eye)
    expand = _head_expand()
    lcum_x = _exact_right(lcum, expand)
    dt_x = _exact_right(dt, expand)
    last_x = lcum_x[C - 1:C, :]

    cbs = [_dot_nt(cmat[:, g * D_STATE:(g + 1) * D_STATE], bmat[:, g * D_STATE:(g + 1) * D_STATE])
           for g in range(SSM_GROUPS)]
    lane = lax.broadcasted_iota(jnp.int32, (C, LANES), 1)
    left = lane < SSM_HEADDIM
    y_intra = []
    for m in range(SSM_HEADS // 2):
        ws = []
        for h in (2 * m, 2 * m + 1):
            seg = lcum[:, h:h + 1] - lcum_t[h:h + 1, :]
            decay = jnp.exp(jnp.where(causal, seg, -jnp.inf))
            ws.append((cbs[h // (SSM_HEADS // SSM_GROUPS)] * decay * dt_t[h:h + 1, :]).astype(BF16))
        xm = xs[:, m * LANES:(m + 1) * LANES]
        xst = jnp.concatenate([jnp.where(left, xm, 0.0), jnp.where(left, 0.0, xm)], axis=0).astype(BF16)
        y_intra.append(_dot(jnp.concatenate(ws, axis=1), xst))
    y = jnp.concatenate(y_intra, axis=1)

    hstate = ssmo_ref[0]
    hb = hstate.astype(BF16)
    y_inter = jnp.concatenate(
        [_dot_nt(cmat[:, g * D_STATE:(g + 1) * D_STATE], hb[g * GROUP_DIM:(g + 1) * GROUP_DIM, :])
         for g in range(SSM_GROUPS)], axis=1)
    y = y + y_inter * jnp.exp(lcum_x) + dskip_ref[...] * xs

    xw = (xs * (jnp.exp(last_x - lcum_x) * dt_x)).astype(BF16)
    upd = jnp.concatenate(
        [_dot_tn(xw[:, g * GROUP_DIM:(g + 1) * GROUP_DIM], bmat[:, g * D_STATE:(g + 1) * D_STATE])
         for g in range(SSM_GROUPS)], axis=0)
    la_tot = _exact_tn(la, jnp.ones((C, LANES), BF16))
    er = lax.broadcasted_iota(jnp.int32, (D_SSM, LANES), 0)
    ec = lax.broadcasted_iota(jnp.int32, (D_SSM, LANES), 1)
    expand_t = (er // SSM_HEADDIM == ec).astype(BF16)
    chunk_decay = jnp.exp(_exact_left(expand_t, la_tot))
    ssmo_ref[0] = chunk_decay * hstate + upd

    z = proj_ref[0, :, OFF_Z:OFF_XBC]
    y = y * _silu(z)
    y1 = jnp.concatenate([_rms(y[:, g * GROUP_DIM:(g + 1) * GROUP_DIM]) for g in range(SSM_GROUPS)],
                         axis=1) * sg_ref[...]

    pos = (pos_base + c * C).astype(F32) + rowf
    ang = pos * invf_ref[...]
    cos = jnp.cos(ang)
    sin = jnp.sin(ang)
    half = RET_HEADDIM // 2
    y2 = []
    for h in range(RET_HEADS):
        lg = RET_LOG_GAMMA[h]
        q1 = proj_ref[0, :, OFF_Q + h * RET_HEADDIM:OFF_Q + h * RET_HEADDIM + half]
        q2 = proj_ref[0, :, OFF_Q + h * RET_HEADDIM + half:OFF_Q + (h + 1) * RET_HEADDIM]
        k1 = proj_ref[0, :, OFF_K + h * RET_HEADDIM:OFF_K + h * RET_HEADDIM + half]
        k2 = proj_ref[0, :, OFF_K + h * RET_HEADDIM + half:OFF_K + (h + 1) * RET_HEADDIM]
        vh = proj_ref[0, :, OFF_V + h * RET_HEADDIM:OFF_V + (h + 1) * RET_HEADDIM].astype(BF16)
        qr = jnp.concatenate([q1 * cos - q2 * sin, q1 * sin + q2 * cos], axis=1)
        kr = jnp.concatenate([k1 * cos - k2 * sin, k1 * sin + k2 * cos], axis=1) * (RET_HEADDIM ** -0.5)
        if valid < C:
            kr = jnp.where(rowi < valid, kr, 0.0)
        qb = qr.astype(BF16)
        scores = _dot_nt(qb, kr.astype(BF16)) * rdec_ref[h]
        s_old = reto_ref[0, h * RET_HEADDIM:(h + 1) * RET_HEADDIM, :]
        yr = _dot(scores.astype(BF16), vh) + _dot(qb, s_old.astype(BF16)) * jnp.exp((rowf + 1.0) * lg)
        kw = (kr * jnp.exp((valid - 1.0 - rowf) * lg)).astype(BF16)
        reto_ref[0, h * RET_HEADDIM:(h + 1) * RET_HEADDIM, :] = (
            float(np.exp(np.float32(valid * lg))) * s_old + _dot_tn(kw, vh))
        y2.append(_rms(yr))
    gate = proj_ref[0, :, OFF_G:PROJ_MAIN]
    y2 = jnp.concatenate(y2, axis=1) * rg_ref[...] * _silu(gate)

    mix_ref[0, :, :D_SSM] = y1.astype(BF16)
    mix_ref[0, :, D_SSM:] = y2.astype(BF16)


def _mixer_seq(proj, dtr, conv0, ssm0, ret0, params, *, nchunks, chunk_offset, valid, pos_base, name):
    nb = proj.shape[0]
    row = lambda b, c: (b, c + chunk_offset, 0)
    const3 = lambda b, c: (0, 0, 0)
    const2 = lambda b, c: (0, 0)
    per_b = lambda b, c: (b, 0, 0)
    pspecs = [pl.BlockSpec(p.shape, const2) for p in params]
    kern = functools.partial(_mixer_seq_kernel, valid=valid, pos_base=pos_base)
    return pl.pallas_call(
        kern,
        out_shape=(jax.ShapeDtypeStruct((nb, nchunks * CHUNK, D_MODEL), BF16),
                   jax.ShapeDtypeStruct((nb, CONV_W - 1, CONV_DIM), F32),
                   jax.ShapeDtypeStruct((nb, D_SSM, D_STATE), F32),
                   jax.ShapeDtypeStruct((nb, D_RET, RET_HEADDIM), F32)),
        grid=(nb, nchunks),
        in_specs=[pl.BlockSpec((1, CHUNK, PROJ_MAIN), row),
                  pl.BlockSpec((1, CHUNK, LANES), row),
                  pl.BlockSpec((1, CONV_W - 1, CONV_DIM), const3),
                  pl.BlockSpec((1, D_SSM, D_STATE), const3),
                  pl.BlockSpec((1, D_RET, RET_HEADDIM), const3)] + pspecs,
        out_specs=(pl.BlockSpec((1, CHUNK, D_MODEL), lambda b, c: (b, c, 0)),
                   pl.BlockSpec((1, CONV_W - 1, CONV_DIM), per_b),
                   pl.BlockSpec((1, D_SSM, D_STATE), per_b),
                   pl.BlockSpec((1, D_RET, RET_HEADDIM), per_b)),
        scratch_shapes=[pltpu.VMEM((CONV_PAD + CHUNK, CONV_DIM), F32),
                        pltpu.VMEM((RET_HEADS, CHUNK, CHUNK), F32)],
        compiler_params=pltpu.CompilerParams(
            dimension_semantics=("arbitrary", "arbitrary"), vmem_limit_bytes=VMEM_LIMIT),
        name=name,
    )(proj, dtr, conv0, ssm0, ret0, *params)


def _mixer_step_kernel(proj_ref, dtr_ref, conv_ref, ssm_ref, ret_ref,
                       convw_ref, convb_ref, dtb_ref, alog_ref, dskip_ref, sg_ref, rg_ref, invf_ref,
                       mix_ref, convo_ref, ssmo_ref, reto_ref, cols_ref):
    R = STEP_ROWS
    xbc_raw = proj_ref[:, OFF_XBC:OFF_Q]
    acc = convb_ref[...] + xbc_raw * convw_ref[3:4, :]
    for i in range(CONV_W - 1):
        acc = acc + conv_ref[i] * convw_ref[i:i + 1, :]
    xbc = _silu(acc)
    convo_ref[0] = conv_ref[1]
    convo_ref[1] = conv_ref[2]
    convo_ref[2] = xbc_raw

    xs = xbc[:, :D_SSM]
    bmat = xbc[:, D_SSM:D_SSM + SSM_GROUPS * D_STATE]
    cmat = xbc[:, D_SSM + SSM_GROUPS * D_STATE:]
    dt = _softplus(dtr_ref[...] + dtb_ref[...])
    la = dt * (-jnp.exp(alog_ref[...]))
    expand = _head_expand()
    dt_x = _exact_right(dt, expand)
    decay_x = jnp.exp(_exact_right(la, expand))
    xdt = xs * dt_x

    ang = jnp.float32(PAST_LEN) * invf_ref[...]
    cos = jnp.cos(ang)
    sin = jnp.sin(ang)
    half = RET_HEADDIM // 2
    qs, ks = [], []
    for h in range(RET_HEADS):
        q1 = proj_ref[:, OFF_Q + h * RET_HEADDIM:OFF_Q + h * RET_HEADDIM + half]
        q2 = proj_ref[:, OFF_Q + h * RET_HEADDIM + half:OFF_Q + (h + 1) * RET_HEADDIM]
        k1 = proj_ref[:, OFF_K + h * RET_HEADDIM:OFF_K + h * RET_HEADDIM + half]
        k2 = proj_ref[:, OFF_K + h * RET_HEADDIM + half:OFF_K + (h + 1) * RET_HEADDIM]
        qs += [q1 * cos - q2 * sin, q1 * sin + q2 * cos]
        ks += [(k1 * cos - k2 * sin) * (RET_HEADDIM ** -0.5), (k1 * sin + k2 * cos) * (RET_HEADDIM ** -0.5)]
    qr = jnp.concatenate(qs, axis=1)
    kr = jnp.concatenate(ks, axis=1)
    vv = proj_ref[:, OFF_V:OFF_G]

    allq = jnp.concatenate([decay_x, xdt, kr, qr], axis=1)
    hi = allq.astype(BF16).astype(F32)
    r1 = allq - hi
    mid = r1.astype(BF16).astype(F32)
    lo = (r1 - mid).astype(BF16).astype(F32)
    stack = jnp.concatenate([hi, mid, lo, jnp.zeros_like(hi)], axis=0).astype(BF16)
    krow = lax.broadcasted_iota(jnp.int32, (4 * R, LANES), 0)
    row8 = lax.broadcasted_iota(jnp.int32, (R, 1), 0)
    lane = lax.broadcasted_iota(jnp.int32, (1, LANES), 1)

    y_cols = jnp.zeros((D_SSM, LANES), F32)
    y_ret = jnp.zeros((R, D_RET), F32)
    for r in range(R):
        sel = ((krow % R == r) & (krow < 3 * R)).astype(BF16)
        cols_ref[...] = _dot_tn(stack, sel)
        ycol = []
        for g in range(SSM_GROUPS):
            rows = slice(g * GROUP_DIM, (g + 1) * GROUP_DIM)
            h_old = ssm_ref[r, rows, :]
            h_new = (h_old * cols_ref[g * GROUP_DIM:(g + 1) * GROUP_DIM, :]
                     + cols_ref[D_SSM + g * GROUP_DIM:D_SSM + (g + 1) * GROUP_DIM, :]
                     * bmat[r:r + 1, g * D_STATE:(g + 1) * D_STATE])
            ssmo_ref[r, rows, :] = h_new
            ycol.append(jnp.sum(h_new * cmat[r:r + 1, g * D_STATE:(g + 1) * D_STATE], axis=1, keepdims=True))
        y_cols = jnp.where(lane == r, jnp.concatenate(ycol, axis=0), y_cols)
        yrow = []
        for h in range(RET_HEADS):
            rows = slice(h * RET_HEADDIM, (h + 1) * RET_HEADDIM)
            kcol = cols_ref[2 * D_SSM + h * RET_HEADDIM:2 * D_SSM + (h + 1) * RET_HEADDIM, :]
            qcol = cols_ref[3 * D_SSM + h * RET_HEADDIM:3 * D_SSM + (h + 1) * RET_HEADDIM, :]
            gamma = float(np.exp(np.float32(RET_LOG_GAMMA[h])))
            s_new = (gamma * ret_ref[r, rows, :]
                     + jnp.concatenate([kcol, kcol], axis=1) * vv[r:r + 1, h * RET_HEADDIM:(h + 1) * RET_HEADDIM])
            reto_ref[r, rows, :] = s_new
            yrow.append(jnp.sum(jnp.concatenate([qcol, qcol], axis=1) * s_new, axis=0, keepdims=True))
        y_ret = jnp.where(row8 == r, jnp.concatenate(yrow, axis=1), y_ret)

    y_ssd = y_cols.T[:R, :]
    y = (y_ssd + dskip_ref[...] * xs) * _silu(proj_ref[:, OFF_Z:OFF_XBC])
    y1 = jnp.concatenate([_rms(y[:, g * GROUP_DIM:(g + 1) * GROUP_DIM]) for g in range(SSM_GROUPS)],
                         axis=1) * sg_ref[...]
    y2 = jnp.concatenate([_rms(y_ret[:, h * RET_HEADDIM:(h + 1) * RET_HEADDIM]) for h in range(RET_HEADS)],
                         axis=1) * rg_ref[...] * _silu(proj_ref[:, OFF_G:PROJ_MAIN])
    mix_ref[:, :D_SSM] = y1
    mix_ref[:, D_SSM:] = y2


def _mixer_step(proj, dtr, conv_t, ssm, ret, params, *, nb):
    R = STEP_ROWS
    rows2 = lambda i: (i, 0)
    rows3 = lambda i: (i, 0, 0)
    mid3 = lambda i: (0, i, 0)
    const2 = lambda i: (0, 0)
    pspecs = [pl.BlockSpec(p.shape, const2) for p in params]
    return pl.pallas_call(
        _mixer_step_kernel,
        out_shape=(jax.ShapeDtypeStruct((nb, D_MODEL), F32),
                   jax.ShapeDtypeStruct((CONV_W - 1, nb, CONV_DIM), F32),
                   jax.ShapeDtypeStruct((nb, D_SSM, D_STATE), F32),
                   jax.ShapeDtypeStruct((nb, D_RET, RET_HEADDIM), F32)),
        grid=(nb // R,),
        in_specs=[pl.BlockSpec((R, PROJ_MAIN), rows2),
                  pl.BlockSpec((R, LANES), rows2),
                  pl.BlockSpec((CONV_W - 1, R, CONV_DIM), mid3),
                  pl.BlockSpec((R, D_SSM, D_STATE), rows3),
                  pl.BlockSpec((R, D_RET, RET_HEADDIM), rows3)] + pspecs,
        out_specs=(pl.BlockSpec((R, D_MODEL), rows2),
                   pl.BlockSpec((CONV_W - 1, R, CONV_DIM), mid3),
                   pl.BlockSpec((R, D_SSM, D_STATE), rows3),
                   pl.BlockSpec((R, D_RET, RET_HEADDIM), rows3)),
        scratch_shapes=[pltpu.VMEM((4 * D_SSM, LANES), F32)],
        compiler_params=pltpu.CompilerParams(
            dimension_semantics=("arbitrary",), vmem_limit_bytes=VMEM_LIMIT),
        name="mixer_step",
    )(proj, dtr, conv_t, ssm, ret, *params)


def _outproj_kernel(mix_ref, w_ref, h_ref, g1_ref, g2_ref, hout_ref, f_ref):
    y = _dot(mix_ref[...].astype(BF16), w_ref[...])
    h = h_ref[...] + _rms(y) * g1_ref[...]
    hout_ref[...] = h
    f_ref[...] = (_rms(h) * g2_ref[...]).astype(BF16)


def _outproj(mix, w, h, g1, g2, *, bm, name):
    m = mix.shape[0]
    return pl.pallas_call(
        _outproj_kernel,
        out_shape=(jax.ShapeDtypeStruct((m, D_MODEL), F32),
                   jax.ShapeDtypeStruct((m, D_MODEL), BF16)),
        grid=(m // bm,),
        in_specs=[pl.BlockSpec((bm, D_MODEL), lambda i: (i, 0)),
                  pl.BlockSpec((D_MODEL, D_MODEL), lambda i: (0, 0)),
                  pl.BlockSpec((bm, D_MODEL), lambda i: (i, 0)),
                  pl.BlockSpec((1, D_MODEL), lambda i: (0, 0)),
                  pl.BlockSpec((1, D_MODEL), lambda i: (0, 0))],
        out_specs=(pl.BlockSpec((bm, D_MODEL), lambda i: (i, 0)),
                   pl.BlockSpec((bm, D_MODEL), lambda i: (i, 0))),
        compiler_params=pltpu.CompilerParams(
            dimension_semantics=("arbitrary",), vmem_limit_bytes=VMEM_LIMIT),
        name=name,
    )(mix, w, h, g1, g2)


def _ffn_up_kernel(f_ref, wg_ref, wu_ref, o_ref):
    f = f_ref[...]
    sub = o_ref.shape[1] // 2
    for s in range(2):
        cols = slice(s * sub, (s + 1) * sub)
        gate = _dot(f, wg_ref[:, cols].astype(BF16))
        up = _dot(f, wu_ref[:, cols].astype(BF16))
        o_ref[:, cols] = (_silu(gate) * up).astype(BF16)


def _ffn_up(f, wg, wu, *, bm, bn, name):
    m = f.shape[0]
    return pl.pallas_call(
        _ffn_up_kernel,
        out_shape=jax.ShapeDtypeStruct((m, D_FF), BF16),
        grid=(m // bm, D_FF // bn),
        in_specs=[pl.BlockSpec((bm, D_MODEL), lambda i, j: (i, 0)),
                  pl.BlockSpec((D_MODEL, bn), lambda i, j: (0, j)),
                  pl.BlockSpec((D_MODEL, bn), lambda i, j: (0, j))],
        out_specs=pl.BlockSpec((bm, bn), lambda i, j: (i, j)),
        compiler_params=pltpu.CompilerParams(
            dimension_semantics=("arbitrary", "arbitrary"), vmem_limit_bytes=VMEM_LIMIT),
        name=name,
    )(f, wg, wu)


def _ffn_down_kernel(a_ref, w_ref, h_ref, g_ref, o_ref, y_ref, *, bn):
    j = pl.program_id(1)
    y_ref[j] = _dot(a_ref[...], w_ref[...])

    @pl.when(j == pl.num_programs(1) - 1)
    def _():
        nt = y_ref.shape[0]
        ssq = sum(jnp.sum(y_ref[t] * y_ref[t], axis=-1, keepdims=True) for t in range(nt))
        scale = lax.rsqrt(ssq / (nt * bn) + EPS)
        for t in range(nt):
            cols = slice(t * bn, (t + 1) * bn)
            o_ref[:, cols] = h_ref[:, cols] + y_ref[t] * scale * g_ref[:, cols]


def _ffn_down(a, w, h, g, *, bm, bn, name):
    m = a.shape[0]
    return pl.pallas_call(
        functools.partial(_ffn_down_kernel, bn=bn),
        out_shape=jax.ShapeDtypeStruct((m, D_MODEL), F32),
        grid=(m // bm, D_MODEL // bn),
        in_specs=[pl.BlockSpec((bm, D_FF), lambda i, j: (i, 0)),
                  pl.BlockSpec((D_FF, bn), lambda i, j: (0, j)),
                  pl.BlockSpec((bm, D_MODEL), lambda i, j: (i, 0)),
                  pl.BlockSpec((1, D_MODEL), lambda i, j: (0, 0))],
        out_specs=pl.BlockSpec((bm, D_MODEL), lambda i, j: (i, 0)),
        scratch_shapes=[pltpu.VMEM((D_MODEL // bn, bm, bn), F32)],
        compiler_params=pltpu.CompilerParams(
            dimension_semantics=("arbitrary", "arbitrary"), vmem_limit_bytes=VMEM_LIMIT),
        name=name,
    )(a, w, h, g)


def kernel(x_prompt, x_sample, state_conv, state_ssm, state_ret, meta_tokens, pre_mix_g, post_mix_g,
           pre_ffn_g, post_ffn_g, w_in, conv_w, conv_b, dt_bias, a_log, d_skip, ssm_norm_g, ret_norm_g,
           w_out, w_gate, w_up, w_down):
    bp, seq = x_prompt.shape[:2]
    bs = x_sample.shape[0]
    assert w_in.shape[0] == 1 and x_sample.shape[1] == 1 and seq % CHUNK == 0 and bs == CHUNK

    w_in_t = jnp.swapaxes(w_in[0], 0, 1)
    w_out_b = w_out[0].astype(BF16)
    w_down_b = w_down[0].astype(BF16)
    pad16 = lambda v: jnp.pad(v, ((0, 0), (0, LANES - SSM_HEADS)))
    inv_freq = (ROPE_BASE ** (-jnp.arange(RET_HEADDIM // 2, dtype=F32) / (RET_HEADDIM // 2)))[None, :]
    params = (conv_w[0], conv_b, pad16(dt_bias), pad16(a_log),
              jnp.repeat(d_skip, SSM_HEADDIM, axis=1), ssm_norm_g, ret_norm_g, inv_freq)

    xp = x_prompt.reshape(bp * seq, D_MODEL)
    proj_p, dtr_p = _inproj(xp, pre_mix_g, w_in_t, bm=2048, xr=1024, bn=512, name="inproj_prompt")
    xs_rows = x_sample.reshape(bs, D_MODEL)
    x_small = jnp.concatenate(
        [xs_rows, meta_tokens.astype(F32), jnp.zeros((CHUNK - N_META, D_MODEL), F32)], axis=0)
    proj_s, dtr_s = _inproj(x_small, pre_mix_g, w_in_t, bm=2 * CHUNK, xr=2 * CHUNK, bn=512,
                           name="inproj_small")

    zc = jnp.zeros((1, CONV_W - 1, CONV_DIM), F32)
    zs = jnp.zeros((1, D_SSM, D_STATE), F32)
    zr = jnp.zeros((1, D_RET, RET_HEADDIM), F32)
    _, m_conv, m_ssm, m_ret = _mixer_seq(
        proj_s.reshape(1, 2 * CHUNK, PROJ_MAIN), dtr_s.reshape(1, 2 * CHUNK, LANES), zc, zs, zr, params,
        nchunks=1, chunk_offset=1, valid=N_META, pos_base=0, name="mixer_meta")

    mix_p, p_conv, p_ssm, p_ret = _mixer_seq(
        proj_p.reshape(bp, seq, PROJ_MAIN), dtr_p.reshape(bp, seq, LANES), m_conv, m_ssm, m_ret, params,
        nchunks=seq // CHUNK, chunk_offset=0, valid=CHUNK, pos_base=N_META, name="mixer_prompt")

    conv_t = jnp.transpose(state_conv[0], (1, 0, 2))
    mix_s, s_conv_t, s_ssm, s_ret = _mixer_step(
        proj_s, dtr_s, conv_t, state_ssm[0].reshape(bs, D_SSM, D_STATE),
        state_ret[0].reshape(bs, D_RET, RET_HEADDIM), params, nb=bs)

    def tail(mix, h, bm_o, bm_u, bm_d, tag):
        h1, f = _outproj(mix, w_out_b, h, post_mix_g, pre_ffn_g, bm=bm_o, name="outproj_" + tag)
        act = _ffn_up(f, w_gate[0], w_up[0], bm=bm_u, bn=512, name="ffn_up_" + tag)
        return _ffn_down(act, w_down_b, h1, post_ffn_g, bm=bm_d, bn=512, name="ffn_down_" + tag)

    y_p = tail(mix_p.reshape(bp * seq, D_MODEL), xp, 512, 2048, 512, "prompt")
    y_s = tail(mix_s, xs_rows, bs, bs, bs, "sample")

    return (y_p.reshape(bp, seq, D_MODEL),
            y_s.reshape(bs, 1, D_MODEL),
            p_conv[None],
            p_ssm.reshape(1, bp, SSM_HEADS, SSM_HEADDIM, D_STATE),
            p_ret.reshape(1, bp, RET_HEADS, RET_HEADDIM, RET_HEADDIM),
            jnp.transpose(s_conv_t, (1, 0, 2))[None],
            s_ssm.reshape(1, bs, SSM_HEADS, SSM_HEADDIM, D_STATE),
            s_ret.reshape(1, bs, RET_HEADS, RET_HEADDIM, RET_HEADDIM))
```

```python
import functools

import numpy as np
import jax
import jax.numpy as jnp
from jax import lax
from jax.experimental import pallas as pl
from jax.experimental.pallas import tpu as pltpu

F32 = jnp.float32
BF16 = jnp.bfloat16

D_MODEL = 2048
N_META = 16
CHUNK = 128
D_SSM = 1024
D_RET = 1024
SSM_HEADDIM = 64
SSM_HEADS = 16
SSM_GROUPS = 2
GROUP_DIM = D_SSM // SSM_GROUPS
D_STATE = 128
CONV_W = 4
CONV_DIM = D_SSM + 2 * SSM_GROUPS * D_STATE
RET_HEADS = 4
RET_HEADDIM = 256
ROPE_BASE = 10000.0
D_FF = 5632
EPS = 1e-6
PAST_LEN = 16384

LANES = 128
SUBLANES = 8
STEP_ROWS = SUBLANES
CONV_PAD = SUBLANES

OFF_Z = 0
OFF_XBC = D_SSM
OFF_Q = OFF_XBC + CONV_DIM
OFF_K = OFF_Q + D_RET
OFF_V = OFF_K + D_RET
OFF_G = OFF_V + D_RET
PROJ_MAIN = OFF_G + D_RET

VMEM_LIMIT = 56 * 1024 * 1024

RET_LOG_GAMMA = [float(np.log1p(-np.float32(2.0) ** np.float32(-5.0 - h)).astype(np.float32))
                 for h in range(RET_HEADS)]


def _silu(x):
    return x / (1.0 + jnp.exp(-x))


def _softplus(x):
    return jnp.maximum(x, 0.0) + jnp.log1p(jnp.exp(-jnp.abs(x)))


def _rms(x):
    return x * lax.rsqrt(jnp.mean(x * x, axis=-1, keepdims=True) + EPS)


def _split3(x):
    hi = x.astype(BF16)
    r = x - hi.astype(F32)
    mid = r.astype(BF16)
    lo = (r - mid.astype(F32)).astype(BF16)
    return hi, mid, lo


def _dot(a, b):
    return jnp.dot(a, b, preferred_element_type=F32)


def _dot_nt(a, b):
    return lax.dot_general(a, b, (((1,), (1,)), ((), ())), preferred_element_type=F32)


def _dot_tn(a, b):
    return lax.dot_general(a, b, (((0,), (0,)), ((), ())), preferred_element_type=F32)


def _exact_right(x, sel):
    hi, mid, lo = _split3(x)
    return _dot(hi, sel) + _dot(mid, sel) + _dot(lo, sel)


def _exact_left(sel, x):
    hi, mid, lo = _split3(x)
    return _dot(sel, hi) + _dot(sel, mid) + _dot(sel, lo)


def _exact_tn(x, sel):
    hi, mid, lo = _split3(x)
    return _dot_tn(hi, sel) + _dot_tn(mid, sel) + _dot_tn(lo, sel)


def _head_expand():
    r = lax.broadcasted_iota(jnp.int32, (LANES, D_SSM), 0)
    c = lax.broadcasted_iota(jnp.int32, (LANES, D_SSM), 1)
    return (c // SSM_HEADDIM == r).astype(BF16)


NORM_ROWS = 256
DT_ROW = D_SSM + CONV_DIM


def _inproj_kernel(x_ref, xs_ref, g_ref, wt_ref, wdt_ref, o_ref, odt_ref, os_ref, odts_ref, u_ref, us_ref,
                   *, npro, nsplit):
    i = pl.program_id(0)
    j = pl.program_id(1)
    on_last = i == pl.num_programs(0) - 1
    xr = x_ref.shape[0]

    @pl.when(j < npro)
    def _():
        wdt = wdt_ref[...].astype(BF16)
        lane = lax.broadcasted_iota(jnp.int32, (NORM_ROWS, LANES), 1)

        def norm_rows(src_ref, src, dst_ref, dt_ref, dst):
            u = (_rms(src_ref[src, :]) * g_ref[...]).astype(BF16)
            dst_ref[dst, :] = u
            dt_ref[dst, :] = jnp.where(lane < SSM_HEADS, _dot_nt(u, wdt), 0.0)

        def body(t, carry):
            src = pl.ds(pl.multiple_of(t * NORM_ROWS, NORM_ROWS), NORM_ROWS)
            dst = pl.ds(pl.multiple_of(j * xr + t * NORM_ROWS, NORM_ROWS), NORM_ROWS)
            norm_rows(x_ref, src, u_ref, odt_ref, dst)
            return carry
        lax.fori_loop(0, xr // NORM_ROWS, body, 0)

        @pl.when(on_last & (j == 0))
        def _():
            for t in range(xs_ref.shape[0] // NORM_ROWS):
                rows = pl.ds(t * NORM_ROWS, NORM_ROWS)
                norm_rows(xs_ref, rows, us_ref, odts_ref, rows)

    @pl.when(j >= npro)
    def _():
        sub = wt_ref.shape[0] // nsplit
        for s in range(nsplit):
            cols = slice(s * sub, (s + 1) * sub)
            w = wt_ref[cols, :].astype(BF16)
            o_ref[:, cols] = _dot_nt(u_ref[...], w)

            @pl.when(on_last)
            def _():
                os_ref[:, cols] = _dot_nt(us_ref[...], w)


def _inproj(x, xs, g, wt, *, bm, xr, bn):
    m = x.shape[0]
    ms = xs.shape[0]
    nm = m // bm
    npro = bm // xr
    assert DT_ROW % bn == 0 and ms % NORM_ROWS == 0

    def wrow(i, j):
        t = jnp.maximum(j - npro, 0)
        skip = jnp.where(t * bn >= DT_ROW, SSM_HEADS // SUBLANES, 0)
        return ((t * (bn // SUBLANES) + skip) * SUBLANES, 0)

    col = lambda j: jnp.maximum(j - npro, 0)
    const = lambda i, j: (0, 0)
    return pl.pallas_call(
        functools.partial(_inproj_kernel, npro=npro, nsplit=2),
        out_shape=(jax.ShapeDtypeStruct((m, PROJ_MAIN), F32), jax.ShapeDtypeStruct((m, LANES), F32),
                   jax.ShapeDtypeStruct((ms, PROJ_MAIN), F32), jax.ShapeDtypeStruct((ms, LANES), F32)),
        grid=(nm, npro + PROJ_MAIN // bn),
        in_specs=[pl.BlockSpec((xr, D_MODEL), lambda i, j: (i * npro + jnp.minimum(j, npro - 1), 0)),
                  pl.BlockSpec((ms, D_MODEL), const),
                  pl.BlockSpec((1, D_MODEL), const),
                  pl.BlockSpec((pl.Element(bn), pl.Element(D_MODEL)), wrow),
                  pl.BlockSpec((pl.Element(LANES), pl.Element(D_MODEL)), lambda i, j: (DT_ROW, 0))],
        out_specs=(pl.BlockSpec((bm, bn), lambda i, j: (i, col(j))),
                   pl.BlockSpec((bm, LANES), lambda i, j: (i, 0)),
                   pl.BlockSpec((ms, bn), lambda i, j: (0, jnp.where(i == nm - 1, col(j), 0))),
                   pl.BlockSpec((ms, LANES), const)),
        scratch_shapes=[pltpu.VMEM((bm, D_MODEL), BF16), pltpu.VMEM((ms, D_MODEL), BF16)],
        compiler_params=pltpu.CompilerParams(
            dimension_semantics=("arbitrary", "arbitrary"), vmem_limit_bytes=VMEM_LIMIT),
        name="inproj",
    )(x, xs, g, wt, wt)


def _mixer_seq_kernel(proj_ref, dtr_ref, conv0_ref, ssm0_ref, ret0_ref,
                      convw_ref, convb_ref, dtb_ref, alog_ref, dskip_ref, sg_ref, rg_ref, invf_ref,
                      mix_ref, convo_ref, ssmo_ref, reto_ref,
                      cbuf_ref, rdec_ref, *, valid, pos_base):
    C = CHUNK
    b = pl.program_id(0)
    c = pl.program_id(1)
    rowi = lax.broadcasted_iota(jnp.int32, (C, 1), 0)
    rowf = rowi.astype(F32)
    ri = lax.broadcasted_iota(jnp.int32, (C, C), 0)
    ci = lax.broadcasted_iota(jnp.int32, (C, C), 1)
    causal = ri >= ci

    @pl.when((b == 0) & (c == 0))
    def _():
        diff = (ri - ci).astype(F32)
        for h in range(RET_HEADS):
            rdec_ref[h] = jnp.where(causal, jnp.exp(jnp.maximum(diff, 0.0) * RET_LOG_GAMMA[h]), 0.0)

    hist = CONV_PAD - (CONV_W - 1)

    @pl.when(c == 0)
    def _():
        cbuf_ref[hist:CONV_PAD, :] = conv0_ref[0]
        ssmo_ref[0] = ssm0_ref[0]
        reto_ref[0] = ret0_ref[0]

    xbc_raw = proj_ref[0, :, OFF_XBC:OFF_Q]
    cbuf_ref[CONV_PAD:CONV_PAD + C, :] = xbc_raw
    acc = convb_ref[...] + xbc_raw * convw_ref[CONV_W - 1:CONV_W, :]
    for i in range(CONV_W - 1):
        acc = acc + cbuf_ref[hist + i:hist + i + C, :] * convw_ref[i:i + 1, :]
    xbc = _silu(acc)
    new_prev = cbuf_ref[hist + valid:CONV_PAD + valid, :]
    cbuf_ref[hist:CONV_PAD, :] = new_prev
    convo_ref[0] = new_prev

    xs = xbc[:, :D_SSM]
    bmat = xbc[:, D_SSM:D_SSM + SSM_GROUPS * D_STATE].astype(BF16)
    cmat = xbc[:, D_SSM + SSM_GROUPS * D_STATE:].astype(BF16)

    dt = _softplus(dtr_ref[0] + dtb_ref[...])
    if valid < C:
        dt = jnp.where(rowi < valid, dt, 0.0)
    la = dt * (-jnp.exp(alog_ref[...]))
    tril = causal.astype(BF16)
    triu = (ri <= ci).astype(BF16)
    eye = (ri == ci).astype(BF16)
    lcum = _exact_left(tril, la)
    lcum_t = _exact_tn(la, triu)
    dt_t = _exact_tn(dt, eye)
    expand = _head_expand()
    lcum_x = _exact_right(lcum, expand)
    dt_x = _exact_right(dt, expand)
    last_x = lcum_x[C - 1:C, :]

    cbs = [_dot_nt(cmat[:, g * D_STATE:(g + 1) * D_STATE], bmat[:, g * D_STATE:(g + 1) * D_STATE])
           for g in range(SSM_GROUPS)]
    lane = lax.broadcasted_iota(jnp.int32, (C, LANES), 1)
    left = lane < SSM_HEADDIM
    y_intra = []
    for m in range(SSM_HEADS // 2):
        ws = []
        for h in (2 * m, 2 * m + 1):
            seg = lcum[:, h:h + 1] - lcum_t[h:h + 1, :]
            decay = jnp.exp(jnp.where(causal, seg, -jnp.inf))
            ws.append((cbs[h // (SSM_HEADS // SSM_GROUPS)] * decay * dt_t[h:h + 1, :]).astype(BF16))
        xm = xs[:, m * LANES:(m + 1) * LANES]
        xst = jnp.concatenate([jnp.where(left, xm, 0.0), jnp.where(left, 0.0, xm)], axis=0).astype(BF16)
        y_intra.append(_dot(jnp.concatenate(ws, axis=1), xst))
    y = jnp.concatenate(y_intra, axis=1)

    hstate = ssmo_ref[0]
    hb = hstate.astype(BF16)
    y_inter = jnp.concatenate(
        [_dot_nt(cmat[:, g * D_STATE:(g + 1) * D_STATE], hb[g * GROUP_DIM:(g + 1) * GROUP_DIM, :])
         for g in range(SSM_GROUPS)], axis=1)
    y = y + y_inter * jnp.exp(lcum_x) + dskip_ref[...] * xs

    xw = (xs * (jnp.exp(last_x - lcum_x) * dt_x)).astype(BF16)
    upd = jnp.concatenate(
        [_dot_tn(xw[:, g * GROUP_DIM:(g + 1) * GROUP_DIM], bmat[:, g * D_STATE:(g + 1) * D_STATE])
         for g in range(SSM_GROUPS)], axis=0)
    la_tot = _exact_tn(la, jnp.ones((C, LANES), BF16))
    er = lax.broadcasted_iota(jnp.int32, (D_SSM, LANES), 0)
    ec = lax.broadcasted_iota(jnp.int32, (D_SSM, LANES), 1)
    expand_t = (er // SSM_HEADDIM == ec).astype(BF16)
    chunk_decay = jnp.exp(_exact_left(expand_t, la_tot))
    ssmo_ref[0] = chunk_decay * hstate + upd

    z = proj_ref[0, :, OFF_Z:OFF_XBC]
    y = y * _silu(z)
    y1 = jnp.concatenate([_rms(y[:, g * GROUP_DIM:(g + 1) * GROUP_DIM]) for g in range(SSM_GROUPS)],
                         axis=1) * sg_ref[...]

    pos = (pos_base + c * C).astype(F32) + rowf
    ang = pos * invf_ref[...]
    cos = jnp.cos(ang)
    sin = jnp.sin(ang)
    half = RET_HEADDIM // 2
    y2 = []
    for h in range(RET_HEADS):
        lg = RET_LOG_GAMMA[h]
        q1 = proj_ref[0, :, OFF_Q + h * RET_HEADDIM:OFF_Q + h * RET_HEADDIM + half]
        q2 = proj_ref[0, :, OFF_Q + h * RET_HEADDIM + half:OFF_Q + (h + 1) * RET_HEADDIM]
        k1 = proj_ref[0, :, OFF_K + h * RET_HEADDIM:OFF_K + h * RET_HEADDIM + half]
        k2 = proj_ref[0, :, OFF_K + h * RET_HEADDIM + half:OFF_K + (h + 1) * RET_HEADDIM]
        vh = proj_ref[0, :, OFF_V + h * RET_HEADDIM:OFF_V + (h + 1) * RET_HEADDIM].astype(BF16)
        qr = jnp.concatenate([q1 * cos - q2 * sin, q1 * sin + q2 * cos], axis=1)
        kr = jnp.concatenate([k1 * cos - k2 * sin, k1 * sin + k2 * cos], axis=1) * (RET_HEADDIM ** -0.5)
        if valid < C:
            kr = jnp.where(rowi < valid, kr, 0.0)
        qb = qr.astype(BF16)
        scores = _dot_nt(qb, kr.astype(BF16)) * rdec_ref[h]
        s_old = reto_ref[0, h * RET_HEADDIM:(h + 1) * RET_HEADDIM, :]
        yr = _dot(scores.astype(BF16), vh) + _dot(qb, s_old.astype(BF16)) * jnp.exp((rowf + 1.0) * lg)
        kw = (kr * jnp.exp((valid - 1.0 - rowf) * lg)).astype(BF16)
        reto_ref[0, h * RET_HEADDIM:(h + 1) * RET_HEADDIM, :] = (
            float(np.exp(np.float32(valid * lg))) * s_old + _dot_tn(kw, vh))
        y2.append(_rms(yr))
    gate = proj_ref[0, :, OFF_G:PROJ_MAIN]
    y2 = jnp.concatenate(y2, axis=1) * rg_ref[...] * _silu(gate)

    mix_ref[0, :, :D_SSM] = y1.astype(BF16)
    mix_ref[0, :, D_SSM:] = y2.astype(BF16)


def _mixer_seq(proj, dtr, conv0, ssm0, ret0, params, *, nchunks, chunk_offset, valid, pos_base, name):
    nb = proj.shape[0]
    row = lambda b, c: (b, c + chunk_offset, 0)
    const3 = lambda b, c: (0, 0, 0)
    const2 = lambda b, c: (0, 0)
    per_b = lambda b, c: (b, 0, 0)
    pspecs = [pl.BlockSpec(p.shape, const2) for p in params]
    kern = functools.partial(_mixer_seq_kernel, valid=valid, pos_base=pos_base)
    return pl.pallas_call(
        kern,
        out_shape=(jax.ShapeDtypeStruct((nb, nchunks * CHUNK, D_MODEL), BF16),
                   jax.ShapeDtypeStruct((nb, CONV_W - 1, CONV_DIM), F32),
                   jax.ShapeDtypeStruct((nb, D_SSM, D_STATE), F32),
                   jax.ShapeDtypeStruct((nb, D_RET, RET_HEADDIM), F32)),
        grid=(nb, nchunks),
        in_specs=[pl.BlockSpec((1, CHUNK, PROJ_MAIN), row),
                  pl.BlockSpec((1, CHUNK, LANES), row),
                  pl.BlockSpec((1, CONV_W - 1, CONV_DIM), const3),
                  pl.BlockSpec((1, D_SSM, D_STATE), const3),
                  pl.BlockSpec((1, D_RET, RET_HEADDIM), const3)] + pspecs,
        out_specs=(pl.BlockSpec((1, CHUNK, D_MODEL), lambda b, c: (b, c, 0)),
                   pl.BlockSpec((1, CONV_W - 1, CONV_DIM), per_b),
                   pl.BlockSpec((1, D_SSM, D_STATE), per_b),
                   pl.BlockSpec((1, D_RET, RET_HEADDIM), per_b)),
        scratch_shapes=[pltpu.VMEM((CONV_PAD + CHUNK, CONV_DIM), F32),
                        pltpu.VMEM((RET_HEADS, CHUNK, CHUNK), F32)],
        compiler_params=pltpu.CompilerParams(
            dimension_semantics=("arbitrary", "arbitrary"), vmem_limit_bytes=VMEM_LIMIT),
        name=name,
    )(proj, dtr, conv0, ssm0, ret0, *params)


def _mixer_step_kernel(proj_ref, dtr_ref, conv_ref, ssm_ref, ret_ref,
                       convw_ref, convb_ref, dtb_ref, alog_ref, dskip_ref, sg_ref, rg_ref, invf_ref,
                       mix_ref, convo_ref, ssmo_ref, reto_ref, cols_ref):
    R = STEP_ROWS
    xbc_raw = proj_ref[:, OFF_XBC:OFF_Q]
    acc = convb_ref[...] + xbc_raw * convw_ref[3:4, :]
    for i in range(CONV_W - 1):
        acc = acc + conv_ref[i] * convw_ref[i:i + 1, :]
    xbc = _silu(acc)
    convo_ref[0] = conv_ref[1]
    convo_ref[1] = conv_ref[2]
    convo_ref[2] = xbc_raw

    xs = xbc[:, :D_SSM]
    bmat = xbc[:, D_SSM:D_SSM + SSM_GROUPS * D_STATE]
    cmat = xbc[:, D_SSM + SSM_GROUPS * D_STATE:]
    dt = _softplus(dtr_ref[...] + dtb_ref[...])
    la = dt * (-jnp.exp(alog_ref[...]))
    expand = _head_expand()
    dt_x = _exact_right(dt, expand)
    decay_x = jnp.exp(_exact_right(la, expand))
    xdt = xs * dt_x

    ang = jnp.float32(PAST_LEN) * invf_ref[...]
    cos = jnp.cos(ang)
    sin = jnp.sin(ang)
    half = RET_HEADDIM // 2
    qs, ks = [], []
    for h in range(RET_HEADS):
        q1 = proj_ref[:, OFF_Q + h * RET_HEADDIM:OFF_Q + h * RET_HEADDIM + half]
        q2 = proj_ref[:, OFF_Q + h * RET_HEADDIM + half:OFF_Q + (h + 1) * RET_HEADDIM]
        k1 = proj_ref[:, OFF_K + h * RET_HEADDIM:OFF_K + h * RET_HEADDIM + half]
        k2 = proj_ref[:, OFF_K + h * RET_HEADDIM + half:OFF_K + (h + 1) * RET_HEADDIM]
        qs += [q1 * cos - q2 * sin, q1 * sin + q2 * cos]
        ks += [(k1 * cos - k2 * sin) * (RET_HEADDIM ** -0.5), (k1 * sin + k2 * cos) * (RET_HEADDIM ** -0.5)]
    qr = jnp.concatenate(qs, axis=1)
    kr = jnp.concatenate(ks, axis=1)
    vv = proj_ref[:, OFF_V:OFF_G]

    allq = jnp.concatenate([decay_x, xdt, kr, qr], axis=1)
    hi = allq.astype(BF16).astype(F32)
    r1 = allq - hi
    mid = r1.astype(BF16).astype(F32)
    lo = (r1 - mid).astype(BF16).astype(F32)
    stack = jnp.concatenate([hi, mid, lo, jnp.zeros_like(hi)], axis=0).astype(BF16)
    krow = lax.broadcasted_iota(jnp.int32, (4 * R, LANES), 0)
    row8 = lax.broadcasted_iota(jnp.int32, (R, 1), 0)
    lane = lax.broadcasted_iota(jnp.int32, (1, LANES), 1)

    y_cols = jnp.zeros((D_SSM, LANES), F32)
    y_ret = jnp.zeros((R, D_RET), F32)
    for r in range(R):
        sel = ((krow % R == r) & (krow < 3 * R)).astype(BF16)
        cols_ref[...] = _dot_tn(stack, sel)
        ycol = []
        for g in range(SSM_GROUPS):
            rows = slice(g * GROUP_DIM, (g + 1) * GROUP_DIM)
            h_old = ssm_ref[r, rows, :]
            h_new = (h_old * cols_ref[g * GROUP_DIM:(g + 1) * GROUP_DIM, :]
                     + cols_ref[D_SSM + g * GROUP_DIM:D_SSM + (g + 1) * GROUP_DIM, :]
                     * bmat[r:r + 1, g * D_STATE:(g + 1) * D_STATE])
            ssmo_ref[r, rows, :] = h_new
            ycol.append(jnp.sum(h_new * cmat[r:r + 1, g * D_STATE:(g + 1) * D_STATE], axis=1, keepdims=True))
        y_cols = jnp.where(lane == r, jnp.concatenate(ycol, axis=0), y_cols)
        yrow = []
        for h in range(RET_HEADS):
            rows = slice(h * RET_HEADDIM, (h + 1) * RET_HEADDIM)
            kcol = cols_ref[2 * D_SSM + h * RET_HEADDIM:2 * D_SSM + (h + 1) * RET_HEADDIM, :]
            qcol = cols_ref[3 * D_SSM + h * RET_HEADDIM:3 * D_SSM + (h + 1) * RET_HEADDIM, :]
            gamma = float(np.exp(np.float32(RET_LOG_GAMMA[h])))
            s_new = (gamma * ret_ref[r, rows, :]
                     + jnp.concatenate([kcol, kcol], axis=1) * vv[r:r + 1, h * RET_HEADDIM:(h + 1) * RET_HEADDIM])
            reto_ref[r, rows, :] = s_new
            yrow.append(jnp.sum(jnp.concatenate([qcol, qcol], axis=1) * s_new, axis=0, keepdims=True))
        y_ret = jnp.where(row8 == r, jnp.concatenate(yrow, axis=1), y_ret)

    y_ssd = y_cols.T[:R, :]
    y = (y_ssd + dskip_ref[...] * xs) * _silu(proj_ref[:, OFF_Z:OFF_XBC])
    y1 = jnp.concatenate([_rms(y[:, g * GROUP_DIM:(g + 1) * GROUP_DIM]) for g in range(SSM_GROUPS)],
                         axis=1) * sg_ref[...]
    y2 = jnp.concatenate([_rms(y_ret[:, h * RET_HEADDIM:(h + 1) * RET_HEADDIM]) for h in range(RET_HEADS)],
                         axis=1) * rg_ref[...] * _silu(proj_ref[:, OFF_G:PROJ_MAIN])
    mix_ref[:, :D_SSM] = y1
    mix_ref[:, D_SSM:] = y2


def _mixer_step(proj, dtr, conv_t, ssm, ret, params, *, nb):
    R = STEP_ROWS
    rows2 = lambda i: (i, 0)
    rows3 = lambda i: (i, 0, 0)
    mid3 = lambda i: (0, i, 0)
    const2 = lambda i: (0, 0)
    pspecs = [pl.BlockSpec(p.shape, const2) for p in params]
    return pl.pallas_call(
        _mixer_step_kernel,
        out_shape=(jax.ShapeDtypeStruct((nb, D_MODEL), F32),
                   jax.ShapeDtypeStruct((CONV_W - 1, nb, CONV_DIM), F32),
                   jax.ShapeDtypeStruct((nb, D_SSM, D_STATE), F32),
                   jax.ShapeDtypeStruct((nb, D_RET, RET_HEADDIM), F32)),
        grid=(nb // R,),
        in_specs=[pl.BlockSpec((R, PROJ_MAIN), rows2),
                  pl.BlockSpec((R, LANES), rows2),
                  pl.BlockSpec((CONV_W - 1, R, CONV_DIM), mid3),
                  pl.BlockSpec((R, D_SSM, D_STATE), rows3),
                  pl.BlockSpec((R, D_RET, RET_HEADDIM), rows3)] + pspecs,
        out_specs=(pl.BlockSpec((R, D_MODEL), rows2),
                   pl.BlockSpec((CONV_W - 1, R, CONV_DIM), mid3),
                   pl.BlockSpec((R, D_SSM, D_STATE), rows3),
                   pl.BlockSpec((R, D_RET, RET_HEADDIM), rows3)),
        scratch_shapes=[pltpu.VMEM((4 * D_SSM, LANES), F32)],
        compiler_params=pltpu.CompilerParams(
            dimension_semantics=("arbitrary",), vmem_limit_bytes=VMEM_LIMIT),
        name="mixer_step",
    )(proj, dtr, conv_t, ssm, ret, *params)


def _outproj_kernel(mix_ref, mixs_ref, w_ref, h_ref, hs_ref, g1_ref, g2_ref, hout_ref, f_ref, houts_ref, fs_ref):
    def rows(mix_r, h_r, hout_r, f_r):
        y = _dot(mix_r[...].astype(BF16), w_ref[...])
        h = h_r[...] + _rms(y) * g1_ref[...]
        hout_r[...] = h
        f_r[...] = (_rms(h) * g2_ref[...]).astype(BF16)

    rows(mix_ref, h_ref, hout_ref, f_ref)

    @pl.when(pl.program_id(0) == pl.num_programs(0) - 1)
    def _():
        rows(mixs_ref, hs_ref, houts_ref, fs_ref)


def _outproj(mix, mixs, w, h, hs, g1, g2, *, bm):
    m = mix.shape[0]
    ms = mixs.shape[0]
    row = lambda i: (i, 0)
    const = lambda i: (0, 0)
    return pl.pallas_call(
        _outproj_kernel,
        out_shape=(jax.ShapeDtypeStruct((m, D_MODEL), F32), jax.ShapeDtypeStruct((m, D_MODEL), BF16),
                   jax.ShapeDtypeStruct((ms, D_MODEL), F32), jax.ShapeDtypeStruct((ms, D_MODEL), BF16)),
        grid=(m // bm,),
        in_specs=[pl.BlockSpec((bm, D_MODEL), row),
                  pl.BlockSpec((ms, D_MODEL), const),
                  pl.BlockSpec((D_MODEL, D_MODEL), const),
                  pl.BlockSpec((bm, D_MODEL), row),
                  pl.BlockSpec((ms, D_MODEL), const),
                  pl.BlockSpec((1, D_MODEL), const),
                  pl.BlockSpec((1, D_MODEL), const)],
        out_specs=(pl.BlockSpec((bm, D_MODEL), row), pl.BlockSpec((bm, D_MODEL), row),
                   pl.BlockSpec((ms, D_MODEL), const), pl.BlockSpec((ms, D_MODEL), const)),
        compiler_params=pltpu.CompilerParams(
            dimension_semantics=("arbitrary",), vmem_limit_bytes=VMEM_LIMIT),
        name="outproj",
    )(mix, mixs, w, h, hs, g1, g2)


def _ffn_up_kernel(f_ref, fs_ref, wg_ref, wu_ref, o_ref, os_ref):
    on_last = pl.program_id(0) == pl.num_programs(0) - 1
    sub = o_ref.shape[1] // 2
    for s in range(2):
        cols = slice(s * sub, (s + 1) * sub)
        wg = wg_ref[:, cols].astype(BF16)
        wu = wu_ref[:, cols].astype(BF16)

        def act(f):
            return (_silu(_dot(f, wg)) * _dot(f, wu)).astype(BF16)

        o_ref[:, cols] = act(f_ref[...])

        @pl.when(on_last)
        def _():
            os_ref[:, cols] = act(fs_ref[...])


def _ffn_up(f, fs, wg, wu, *, bm, bn):
    m = f.shape[0]
    ms = fs.shape[0]
    nm = m // bm
    return pl.pallas_call(
        _ffn_up_kernel,
        out_shape=(jax.ShapeDtypeStruct((m, D_FF), BF16), jax.ShapeDtypeStruct((ms, D_FF), BF16)),
        grid=(nm, D_FF // bn),
        in_specs=[pl.BlockSpec((bm, D_MODEL), lambda i, j: (i, 0)),
                  pl.BlockSpec((ms, D_MODEL), lambda i, j: (0, 0)),
                  pl.BlockSpec((D_MODEL, bn), lambda i, j: (0, j)),
                  pl.BlockSpec((D_MODEL, bn), lambda i, j: (0, j))],
        out_specs=(pl.BlockSpec((bm, bn), lambda i, j: (i, j)),
                   pl.BlockSpec((ms, bn), lambda i, j: (0, jnp.where(i == nm - 1, j, 0)))),
        compiler_params=pltpu.CompilerParams(
            dimension_semantics=("arbitrary", "arbitrary"), vmem_limit_bytes=VMEM_LIMIT),
        name="ffn_up",
    )(f, fs, wg, wu)


def _ffn_down_kernel(a_ref, as_ref, w_ref, h_ref, hs_ref, g_ref, o_ref, os_ref):
    k = pl.program_id(1)
    on_last = pl.program_id(0) == pl.num_programs(0) - 1

    @pl.when(k == 0)
    def _():
        o_ref[...] = jnp.zeros_like(o_ref)

        @pl.when(on_last)
        def _():
            os_ref[...] = jnp.zeros_like(os_ref)

    sub = o_ref.shape[1] // 2
    for s in range(2):
        cols = slice(s * sub, (s + 1) * sub)
        w = w_ref[:, cols].astype(BF16)
        o_ref[:, cols] += _dot(a_ref[...], w)

        @pl.when(on_last)
        def _():
            os_ref[:, cols] += _dot(as_ref[...], w)

    @pl.when(k == pl.num_programs(1) - 1)
    def _():
        def body(t, carry):
            rows = pl.ds(pl.multiple_of(t * NORM_ROWS, NORM_ROWS), NORM_ROWS)
            o_ref[rows, :] = h_ref[rows, :] + _rms(o_ref[rows, :]) * g_ref[...]
            return carry
        lax.fori_loop(0, o_ref.shape[0] // NORM_ROWS, body, 0)

        @pl.when(on_last)
        def _():
            os_ref[...] = hs_ref[...] + _rms(os_ref[...]) * g_ref[...]


def _ffn_down(a, a_s, w, h, hs, g, *, bm, bk):
    m = a.shape[0]
    ms = a_s.shape[0]
    return pl.pallas_call(
        _ffn_down_kernel,
        out_shape=(jax.ShapeDtypeStruct((m, D_MODEL), F32), jax.ShapeDtypeStruct((ms, D_MODEL), F32)),
        grid=(m // bm, D_FF // bk),
        in_specs=[pl.BlockSpec((bm, bk), lambda i, k: (i, k)),
                  pl.BlockSpec((ms, bk), lambda i, k: (0, k)),
                  pl.BlockSpec((bk, D_MODEL), lambda i, k: (k, 0)),
                  pl.BlockSpec((bm, D_MODEL), lambda i, k: (i, 0)),
                  pl.BlockSpec((ms, D_MODEL), lambda i, k: (0, 0)),
                  pl.BlockSpec((1, D_MODEL), lambda i, k: (0, 0))],
        out_specs=(pl.BlockSpec((bm, D_MODEL), lambda i, k: (i, 0)),
                   pl.BlockSpec((ms, D_MODEL), lambda i, k: (0, 0))),
        compiler_params=pltpu.CompilerParams(
            dimension_semantics=("arbitrary", "arbitrary"), vmem_limit_bytes=VMEM_LIMIT),
        name="ffn_down",
    )(a, a_s, w, h, hs, g)


def kernel(x_prompt, x_sample, state_conv, state_ssm, state_ret, meta_tokens, pre_mix_g, post_mix_g,
           pre_ffn_g, post_ffn_g, w_in, conv_w, conv_b, dt_bias, a_log, d_skip, ssm_norm_g, ret_norm_g,
           w_out, w_gate, w_up, w_down):
    bp, seq = x_prompt.shape[:2]
    bs = x_sample.shape[0]
    assert w_in.shape[0] == 1 and x_sample.shape[1] == 1 and seq % CHUNK == 0 and bs == CHUNK

    w_in_t = jnp.swapaxes(w_in[0], 0, 1)
    w_out_b = w_out[0].astype(BF16)
    pad16 = lambda v: jnp.pad(v, ((0, 0), (0, LANES - SSM_HEADS)))
    inv_freq = (ROPE_BASE ** (-jnp.arange(RET_HEADDIM // 2, dtype=F32) / (RET_HEADDIM // 2)))[None, :]
    params = (conv_w[0], conv_b, pad16(dt_bias), pad16(a_log),
              jnp.repeat(d_skip, SSM_HEADDIM, axis=1), ssm_norm_g, ret_norm_g, inv_freq)

    xp = x_prompt.reshape(bp * seq, D_MODEL)
    xs_rows = x_sample.reshape(bs, D_MODEL)
    x_small = jnp.concatenate(
        [xs_rows, meta_tokens.astype(F32), jnp.zeros((CHUNK - N_META, D_MODEL), F32)], axis=0)
    proj_p, dtr_p, proj_s, dtr_s = _inproj(xp, x_small, pre_mix_g, w_in_t, bm=2048, xr=1024, bn=512)

    zc = jnp.zeros((1, CONV_W - 1, CONV_DIM), F32)
    zs = jnp.zeros((1, D_SSM, D_STATE), F32)
    zr = jnp.zeros((1, D_RET, RET_HEADDIM), F32)
    _, m_conv, m_ssm, m_ret = _mixer_seq(
        proj_s.reshape(1, 2 * CHUNK, PROJ_MAIN), dtr_s.reshape(1, 2 * CHUNK, LANES), zc, zs, zr, params,
        nchunks=1, chunk_offset=1, valid=N_META, pos_base=0, name="mixer_meta")

    mix_p, p_conv, p_ssm, p_ret = _mixer_seq(
        proj_p.reshape(bp, seq, PROJ_MAIN), dtr_p.reshape(bp, seq, LANES), m_conv, m_ssm, m_ret, params,
        nchunks=seq // CHUNK, chunk_offset=0, valid=CHUNK, pos_base=N_META, name="mixer_prompt")

    conv_t = jnp.transpose(state_conv[0], (1, 0, 2))
    mix_s, s_conv_t, s_ssm, s_ret = _mixer_step(
        proj_s, dtr_s, conv_t, state_ssm[0].reshape(bs, D_SSM, D_STATE),
        state_ret[0].reshape(bs, D_RET, RET_HEADDIM), params, nb=bs)

    h1_p, f_p, h1_s, f_s = _outproj(mix_p.reshape(bp * seq, D_MODEL), mix_s, w_out_b, xp, xs_rows,
                                    post_mix_g, pre_ffn_g, bm=512)
    act_p, act_s = _ffn_up(f_p, f_s, w_gate[0], w_up[0], bm=2048, bn=512)
    y_p, y_s = _ffn_down(act_p, act_s, w_down[0], h1_p, h1_s, post_ffn_g, bm=1024, bk=512)

    return (y_p.reshape(bp, seq, D_MODEL),
            y_s.reshape(bs, 1, D_MODEL),
            p_conv[None],
            p_ssm.reshape(1, bp, SSM_HEADS, SSM_HEADDIM, D_STATE),
            p_ret.reshape(1, bp, RET_HEADS, RET_HEADDIM, RET_HEADDIM),
            jnp.transpose(s_conv_t, (1, 0, 2))[None],
            s_ssm.reshape(1, bs, SSM_HEADS, SSM_HEADDIM, D_STATE),
            s_ret.reshape(1, bs, RET_HEADS, RET_HEADDIM, RET_HEADDIM))
```

```python
import functools

import numpy as np
import jax
import jax.numpy as jnp
from jax import lax
from jax.experimental import pallas as pl
from jax.experimental.pallas import tpu as pltpu

F32 = jnp.float32
BF16 = jnp.bfloat16

D_MODEL = 2048
N_META = 16
CHUNK = 128
D_SSM = 1024
D_RET = 1024
SSM_HEADDIM = 64
SSM_HEADS = 16
SSM_GROUPS = 2
GROUP_DIM = D_SSM // SSM_GROUPS
D_STATE = 128
CONV_W = 4
CONV_DIM = D_SSM + 2 * SSM_GROUPS * D_STATE
RET_HEADS = 4
RET_HEADDIM = 256
ROPE_BASE = 10000.0
D_FF = 5632
EPS = 1e-6
PAST_LEN = 16384

LANES = 128
SUBLANES = 8
STEP_ROWS = SUBLANES
CONV_PAD = SUBLANES

OFF_Z = 0
OFF_XBC = D_SSM
OFF_Q = OFF_XBC + CONV_DIM
OFF_K = OFF_Q + D_RET
OFF_V = OFF_K + D_RET
OFF_G = OFF_V + D_RET
PROJ_MAIN = OFF_G + D_RET

VMEM_LIMIT = 56 * 1024 * 1024

RET_LOG_GAMMA = [float(np.log1p(-np.float32(2.0) ** np.float32(-5.0 - h)).astype(np.float32))
                 for h in range(RET_HEADS)]


def _silu(x):
    return x / (1.0 + jnp.exp(-x))


def _softplus(x):
    return jnp.maximum(x, 0.0) + jnp.log1p(jnp.exp(-jnp.abs(x)))


def _rms(x):
    return x * lax.rsqrt(jnp.mean(x * x, axis=-1, keepdims=True) + EPS)


def _split3(x):
    hi = x.astype(BF16)
    r = x - hi.astype(F32)
    mid = r.astype(BF16)
    lo = (r - mid.astype(F32)).astype(BF16)
    return hi, mid, lo


def _dot(a, b):
    return jnp.dot(a, b, preferred_element_type=F32)


def _dot_nt(a, b):
    return lax.dot_general(a, b, (((1,), (1,)), ((), ())), preferred_element_type=F32)


def _dot_tn(a, b):
    return lax.dot_general(a, b, (((0,), (0,)), ((), ())), preferred_element_type=F32)


def _exact_right(x, sel):
    hi, mid, lo = _split3(x)
    return _dot(hi, sel) + _dot(mid, sel) + _dot(lo, sel)


def _exact_left(sel, x):
    hi, mid, lo = _split3(x)
    return _dot(sel, hi) + _dot(sel, mid) + _dot(sel, lo)


def _exact_tn(x, sel):
    hi, mid, lo = _split3(x)
    return _dot_tn(hi, sel) + _dot_tn(mid, sel) + _dot_tn(lo, sel)


def _head_expand():
    r = lax.broadcasted_iota(jnp.int32, (LANES, D_SSM), 0)
    c = lax.broadcasted_iota(jnp.int32, (LANES, D_SSM), 1)
    return (c // SSM_HEADDIM == r).astype(BF16)


NORM_ROWS = 256
DT_ROW = D_SSM + CONV_DIM


def _inproj_kernel(x_ref, xs_ref, g_ref, wt_ref, wdt_ref, o_ref, odt_ref, os_ref, odts_ref, u_ref, us_ref,
                   *, npro, nsplit):
    i = pl.program_id(0)
    j = pl.program_id(1)
    on_last = i == pl.num_programs(0) - 1
    xr = x_ref.shape[0]

    @pl.when(j < npro)
    def _():
        wdt = wdt_ref[...].astype(BF16)
        lane = lax.broadcasted_iota(jnp.int32, (NORM_ROWS, LANES), 1)

        def norm_rows(src_ref, src, dst_ref, dt_ref, dst):
            u = (_rms(src_ref[src, :]) * g_ref[...]).astype(BF16)
            dst_ref[dst, :] = u
            dt_ref[dst, :] = jnp.where(lane < SSM_HEADS, _dot_nt(u, wdt), 0.0)

        def body(t, carry):
            src = pl.ds(pl.multiple_of(t * NORM_ROWS, NORM_ROWS), NORM_ROWS)
            dst = pl.ds(pl.multiple_of(j * xr + t * NORM_ROWS, NORM_ROWS), NORM_ROWS)
            norm_rows(x_ref, src, u_ref, odt_ref, dst)
            return carry
        lax.fori_loop(0, xr // NORM_ROWS, body, 0)

        @pl.when(on_last & (j == 0))
        def _():
            for t in range(xs_ref.shape[0] // NORM_ROWS):
                rows = pl.ds(t * NORM_ROWS, NORM_ROWS)
                norm_rows(xs_ref, rows, us_ref, odts_ref, rows)

    def column_tile(with_side):
        sub = wt_ref.shape[0] // nsplit
        for s in range(nsplit):
            cols = slice(s * sub, (s + 1) * sub)
            w = wt_ref[cols, :].astype(BF16)
            o_ref[:, cols] = _dot_nt(u_ref[...], w)
            if with_side:
                os_ref[:, cols] = _dot_nt(us_ref[...], w)

    pl.when((j >= npro) & on_last)(functools.partial(column_tile, True))
    pl.when((j >= npro) & jnp.logical_not(on_last))(functools.partial(column_tile, False))


def _inproj(x, xs, g, wt, *, bm, xr, bn):
    m = x.shape[0]
    ms = xs.shape[0]
    nm = m // bm
    npro = bm // xr
    assert DT_ROW % bn == 0 and ms % NORM_ROWS == 0

    def wrow(i, j):
        t = jnp.maximum(j - npro, 0)
        skip = jnp.where(t * bn >= DT_ROW, SSM_HEADS // SUBLANES, 0)
        return ((t * (bn // SUBLANES) + skip) * SUBLANES, 0)

    col = lambda j: jnp.maximum(j - npro, 0)
    const = lambda i, j: (0, 0)
    return pl.pallas_call(
        functools.partial(_inproj_kernel, npro=npro, nsplit=2),
        out_shape=(jax.ShapeDtypeStruct((m, PROJ_MAIN), F32), jax.ShapeDtypeStruct((m, LANES), F32),
                   jax.ShapeDtypeStruct((ms, PROJ_MAIN), F32), jax.ShapeDtypeStruct((ms, LANES), F32)),
        grid=(nm, npro + PROJ_MAIN // bn),
        in_specs=[pl.BlockSpec((xr, D_MODEL), lambda i, j: (i * npro + jnp.minimum(j, npro - 1), 0)),
                  pl.BlockSpec((ms, D_MODEL), const),
                  pl.BlockSpec((1, D_MODEL), const),
                  pl.BlockSpec((pl.Element(bn), pl.Element(D_MODEL)), wrow),
                  pl.BlockSpec((pl.Element(LANES), pl.Element(D_MODEL)), lambda i, j: (DT_ROW, 0))],
        out_specs=(pl.BlockSpec((bm, bn), lambda i, j: (i, col(j))),
                   pl.BlockSpec((bm, LANES), lambda i, j: (i, 0)),
                   pl.BlockSpec((ms, bn), lambda i, j: (0, jnp.where(i == nm - 1, col(j), 0))),
                   pl.BlockSpec((ms, LANES), const)),
        scratch_shapes=[pltpu.VMEM((bm, D_MODEL), BF16), pltpu.VMEM((ms, D_MODEL), BF16)],
        compiler_params=pltpu.CompilerParams(
            dimension_semantics=("arbitrary", "arbitrary"), vmem_limit_bytes=VMEM_LIMIT),
        name="inproj",
    )(x, xs, g, wt, wt)


def _mixer_seq_kernel(proj_ref, dtr_ref, conv0_ref, ssm0_ref, ret0_ref,
                      convw_ref, convb_ref, dtb_ref, alog_ref, dskip_ref, sg_ref, rg_ref, invf_ref,
                      mix_ref, convo_ref, ssmo_ref, reto_ref,
                      cbuf_ref, rdec_ref, *, valid, pos_base):
    C = CHUNK
    b = pl.program_id(0)
    c = pl.program_id(1)
    rowi = lax.broadcasted_iota(jnp.int32, (C, 1), 0)
    rowf = rowi.astype(F32)
    ri = lax.broadcasted_iota(jnp.int32, (C, C), 0)
    ci = lax.broadcasted_iota(jnp.int32, (C, C), 1)
    causal = ri >= ci

    @pl.when((b == 0) & (c == 0))
    def _():
        diff = (ri - ci).astype(F32)
        for h in range(RET_HEADS):
            rdec_ref[h] = jnp.where(causal, jnp.exp(jnp.maximum(diff, 0.0) * RET_LOG_GAMMA[h]), 0.0)

    hist = CONV_PAD - (CONV_W - 1)

    @pl.when(c == 0)
    def _():
        cbuf_ref[hist:CONV_PAD, :] = conv0_ref[0]
        ssmo_ref[0] = ssm0_ref[0]
        reto_ref[0] = ret0_ref[0]

    xbc_raw = proj_ref[0, :, OFF_XBC:OFF_Q]
    cbuf_ref[CONV_PAD:CONV_PAD + C, :] = xbc_raw
    acc = convb_ref[...] + xbc_raw * convw_ref[CONV_W - 1:CONV_W, :]
    for i in range(CONV_W - 1):
        acc = acc + cbuf_ref[hist + i:hist + i + C, :] * convw_ref[i:i + 1, :]
    xbc = _silu(acc)
    new_prev = cbuf_ref[hist + valid:CONV_PAD + valid, :]
    cbuf_ref[hist:CONV_PAD, :] = new_prev
    convo_ref[0] = new_prev

    xs = xbc[:, :D_SSM]
    bmat = xbc[:, D_SSM:D_SSM + SSM_GROUPS * D_STATE].astype(BF16)
    cmat = xbc[:, D_SSM + SSM_GROUPS * D_STATE:].astype(BF16)

    dt = _softplus(dtr_ref[0] + dtb_ref[...])
    if valid < C:
        dt = jnp.where(rowi < valid, dt, 0.0)
    la = dt * (-jnp.exp(alog_ref[...]))
    tril = causal.astype(BF16)
    triu = (ri <= ci).astype(BF16)
    eye = (ri == ci).astype(BF16)
    lcum = _exact_left(tril, la)
    lcum_t = _exact_tn(la, triu)
    dt_t = _exact_tn(dt, eye)
    expand = _head_expand()
    lcum_x = _exact_right(lcum, expand)
    dt_x = _exact_right(dt, expand)
    last_x = lcum_x[C - 1:C, :]

    cbs = [_dot_nt(cmat[:, g * D_STATE:(g + 1) * D_STATE], bmat[:, g * D_STATE:(g + 1) * D_STATE])
           for g in range(SSM_GROUPS)]
    lane = lax.broadcasted_iota(jnp.int32, (C, LANES), 1)
    left = lane < SSM_HEADDIM
    y_intra = []
    for m in range(SSM_HEADS // 2):
        ws = []
        for h in (2 * m, 2 * m + 1):
            seg = lcum[:, h:h + 1] - lcum_t[h:h + 1, :]
            decay = jnp.exp(jnp.where(causal, seg, -jnp.inf))
            ws.append((cbs[h // (SSM_HEADS // SSM_GROUPS)] * decay * dt_t[h:h + 1, :]).astype(BF16))
        xm = xs[:, m * LANES:(m + 1) * LANES]
        xst = jnp.concatenate([jnp.where(left, xm, 0.0), jnp.where(left, 0.0, xm)], axis=0).astype(BF16)
        y_intra.append(_dot(jnp.concatenate(ws, axis=1), xst))
    y = jnp.concatenate(y_intra, axis=1)

    hstate = ssmo_ref[0]
    hb = hstate.astype(BF16)
    y_inter = jnp.concatenate(
        [_dot_nt(cmat[:, g * D_STATE:(g + 1) * D_STATE], hb[g * GROUP_DIM:(g + 1) * GROUP_DIM, :])
         for g in range(SSM_GROUPS)], axis=1)
    y = y + y_inter * jnp.exp(lcum_x) + dskip_ref[...] * xs

    xw = (xs * (jnp.exp(last_x - lcum_x) * dt_x)).astype(BF16)
    upd = jnp.concatenate(
        [_dot_tn(xw[:, g * GROUP_DIM:(g + 1) * GROUP_DIM], bmat[:, g * D_STATE:(g + 1) * D_STATE])
         for g in range(SSM_GROUPS)], axis=0)
    la_tot = _exact_tn(la, jnp.ones((C, LANES), BF16))
    er = lax.broadcasted_iota(jnp.int32, (D_SSM, LANES), 0)
    ec = lax.broadcasted_iota(jnp.int32, (D_SSM, LANES), 1)
    expand_t = (er // SSM_HEADDIM == ec).astype(BF16)
    chunk_decay = jnp.exp(_exact_left(expand_t, la_tot))
    ssmo_ref[0] = chunk_decay * hstate + upd

    z = proj_ref[0, :, OFF_Z:OFF_XBC]
    y = y * _silu(z)
    y1 = jnp.concatenate([_rms(y[:, g * GROUP_DIM:(g + 1) * GROUP_DIM]) for g in range(SSM_GROUPS)],
                         axis=1) * sg_ref[...]

    pos = (pos_base + c * C).astype(F32) + rowf
    ang = pos * invf_ref[...]
    cos = jnp.cos(ang)
    sin = jnp.sin(ang)
    half = RET_HEADDIM // 2
    y2 = []
    for h in range(RET_HEADS):
        lg = RET_LOG_GAMMA[h]
        q1 = proj_ref[0, :, OFF_Q + h * RET_HEADDIM:OFF_Q + h * RET_HEADDIM + half]
        q2 = proj_ref[0, :, OFF_Q + h * RET_HEADDIM + half:OFF_Q + (h + 1) * RET_HEADDIM]
        k1 = proj_ref[0, :, OFF_K + h * RET_HEADDIM:OFF_K + h * RET_HEADDIM + half]
        k2 = proj_ref[0, :, OFF_K + h * RET_HEADDIM + half:OFF_K + (h + 1) * RET_HEADDIM]
        vh = proj_ref[0, :, OFF_V + h * RET_HEADDIM:OFF_V + (h + 1) * RET_HEADDIM].astype(BF16)
        qr = jnp.concatenate([q1 * cos - q2 * sin, q1 * sin + q2 * cos], axis=1)
        kr = jnp.concatenate([k1 * cos - k2 * sin, k1 * sin + k2 * cos], axis=1) * (RET_HEADDIM ** -0.5)
        if valid < C:
            kr = jnp.where(rowi < valid, kr, 0.0)
        qb = qr.astype(BF16)
        scores = _dot_nt(qb, kr.astype(BF16)) * rdec_ref[h]
        s_old = reto_ref[0, h * RET_HEADDIM:(h + 1) * RET_HEADDIM, :]
        yr = _dot(scores.astype(BF16), vh) + _dot(qb, s_old.astype(BF16)) * jnp.exp((rowf + 1.0) * lg)
        kw = (kr * jnp.exp((valid - 1.0 - rowf) * lg)).astype(BF16)
        reto_ref[0, h * RET_HEADDIM:(h + 1) * RET_HEADDIM, :] = (
            float(np.exp(np.float32(valid * lg))) * s_old + _dot_tn(kw, vh))
        y2.append(_rms(yr))
    gate = proj_ref[0, :, OFF_G:PROJ_MAIN]
    y2 = jnp.concatenate(y2, axis=1) * rg_ref[...] * _silu(gate)

    mix_ref[0, :, :D_SSM] = y1.astype(BF16)
    mix_ref[0, :, D_SSM:] = y2.astype(BF16)


def _mixer_seq(proj, dtr, conv0, ssm0, ret0, params, *, nchunks, chunk_offset, valid, pos_base, name):
    nb = proj.shape[0]
    row = lambda b, c: (b, c + chunk_offset, 0)
    const3 = lambda b, c: (0, 0, 0)
    const2 = lambda b, c: (0, 0)
    per_b = lambda b, c: (b, 0, 0)
    pspecs = [pl.BlockSpec(p.shape, const2) for p in params]
    kern = functools.partial(_mixer_seq_kernel, valid=valid, pos_base=pos_base)
    return pl.pallas_call(
        kern,
        out_shape=(jax.ShapeDtypeStruct((nb, nchunks * CHUNK, D_MODEL), BF16),
                   jax.ShapeDtypeStruct((nb, CONV_W - 1, CONV_DIM), F32),
                   jax.ShapeDtypeStruct((nb, D_SSM, D_STATE), F32),
                   jax.ShapeDtypeStruct((nb, D_RET, RET_HEADDIM), F32)),
        grid=(nb, nchunks),
        in_specs=[pl.BlockSpec((1, CHUNK, PROJ_MAIN), row),
                  pl.BlockSpec((1, CHUNK, LANES), row),
                  pl.BlockSpec((1, CONV_W - 1, CONV_DIM), const3),
                  pl.BlockSpec((1, D_SSM, D_STATE), const3),
                  pl.BlockSpec((1, D_RET, RET_HEADDIM), const3)] + pspecs,
        out_specs=(pl.BlockSpec((1, CHUNK, D_MODEL), lambda b, c: (b, c, 0)),
                   pl.BlockSpec((1, CONV_W - 1, CONV_DIM), per_b),
                   pl.BlockSpec((1, D_SSM, D_STATE), per_b),
                   pl.BlockSpec((1, D_RET, RET_HEADDIM), per_b)),
        scratch_shapes=[pltpu.VMEM((CONV_PAD + CHUNK, CONV_DIM), F32),
                        pltpu.VMEM((RET_HEADS, CHUNK, CHUNK), F32)],
        compiler_params=pltpu.CompilerParams(
            dimension_semantics=("arbitrary", "arbitrary"), vmem_limit_bytes=VMEM_LIMIT),
        name=name,
    )(proj, dtr, conv0, ssm0, ret0, *params)


def _mixer_step_kernel(proj_ref, dtr_ref, conv_ref, ssm_ref, ret_ref,
                       convw_ref, convb_ref, dtb_ref, alog_ref, dskip_ref, sg_ref, rg_ref, invf_ref,
                       mix_ref, convo_ref, ssmo_ref, reto_ref, cols_ref):
    R = STEP_ROWS
    xbc_raw = proj_ref[:, OFF_XBC:OFF_Q]
    acc = convb_ref[...] + xbc_raw * convw_ref[3:4, :]
    for i in range(CONV_W - 1):
        acc = acc + conv_ref[i] * convw_ref[i:i + 1, :]
    xbc = _silu(acc)
    convo_ref[0] = conv_ref[1]
    convo_ref[1] = conv_ref[2]
    convo_ref[2] = xbc_raw

    xs = xbc[:, :D_SSM]
    bmat = xbc[:, D_SSM:D_SSM + SSM_GROUPS * D_STATE]
    cmat = xbc[:, D_SSM + SSM_GROUPS * D_STATE:]
    dt = _softplus(dtr_ref[...] + dtb_ref[...])
    la = dt * (-jnp.exp(alog_ref[...]))
    expand = _head_expand()
    dt_x = _exact_right(dt, expand)
    decay_x = jnp.exp(_exact_right(la, expand))
    xdt = xs * dt_x

    ang = jnp.float32(PAST_LEN) * invf_ref[...]
    cos = jnp.cos(ang)
    sin = jnp.sin(ang)
    half = RET_HEADDIM // 2
    qs, ks = [], []
    for h in range(RET_HEADS):
        q1 = proj_ref[:, OFF_Q + h * RET_HEADDIM:OFF_Q + h * RET_HEADDIM + half]
        q2 = proj_ref[:, OFF_Q + h * RET_HEADDIM + half:OFF_Q + (h + 1) * RET_HEADDIM]
        k1 = proj_ref[:, OFF_K + h * RET_HEADDIM:OFF_K + h * RET_HEADDIM + half]
        k2 = proj_ref[:, OFF_K + h * RET_HEADDIM + half:OFF_K + (h + 1) * RET_HEADDIM]
        qs += [q1 * cos - q2 * sin, q1 * sin + q2 * cos]
        ks += [(k1 * cos - k2 * sin) * (RET_HEADDIM ** -0.5), (k1 * sin + k2 * cos) * (RET_HEADDIM ** -0.5)]
    qr = jnp.concatenate(qs, axis=1)
    kr = jnp.concatenate(ks, axis=1)
    vv = proj_ref[:, OFF_V:OFF_G]

    allq = jnp.concatenate([decay_x, xdt, kr, qr], axis=1)
    hi = allq.astype(BF16).astype(F32)
    r1 = allq - hi
    mid = r1.astype(BF16).astype(F32)
    lo = (r1 - mid).astype(BF16).astype(F32)
    stack = jnp.concatenate([hi, mid, lo, jnp.zeros_like(hi)], axis=0).astype(BF16)
    krow = lax.broadcasted_iota(jnp.int32, (4 * R, LANES), 0)
    row8 = lax.broadcasted_iota(jnp.int32, (R, 1), 0)
    lane = lax.broadcasted_iota(jnp.int32, (1, LANES), 1)

    y_cols = jnp.zeros((D_SSM, LANES), F32)
    y_ret = jnp.zeros((R, D_RET), F32)
    for r in range(R):
        sel = ((krow % R == r) & (krow < 3 * R)).astype(BF16)
        cols_ref[...] = _dot_tn(stack, sel)
        ycol = []
        for g in range(SSM_GROUPS):
            rows = slice(g * GROUP_DIM, (g + 1) * GROUP_DIM)
            h_old = ssm_ref[r, rows, :]
            h_new = (h_old * cols_ref[g * GROUP_DIM:(g + 1) * GROUP_DIM, :]
                     + cols_ref[D_SSM + g * GROUP_DIM:D_SSM + (g + 1) * GROUP_DIM, :]
                     * bmat[r:r + 1, g * D_STATE:(g + 1) * D_STATE])
            ssmo_ref[r, rows, :] = h_new
            ycol.append(jnp.sum(h_new * cmat[r:r + 1, g * D_STATE:(g + 1) * D_STATE], axis=1, keepdims=True))
        y_cols = jnp.where(lane == r, jnp.concatenate(ycol, axis=0), y_cols)
        yrow = []
        for h in range(RET_HEADS):
            rows = slice(h * RET_HEADDIM, (h + 1) * RET_HEADDIM)
            kcol = cols_ref[2 * D_SSM + h * RET_HEADDIM:2 * D_SSM + (h + 1) * RET_HEADDIM, :]
            qcol = cols_ref[3 * D_SSM + h * RET_HEADDIM:3 * D_SSM + (h + 1) * RET_HEADDIM, :]
            gamma = float(np.exp(np.float32(RET_LOG_GAMMA[h])))
            s_new = (gamma * ret_ref[r, rows, :]
                     + jnp.concatenate([kcol, kcol], axis=1) * vv[r:r + 1, h * RET_HEADDIM:(h + 1) * RET_HEADDIM])
            reto_ref[r, rows, :] = s_new
            yrow.append(jnp.sum(jnp.concatenate([qcol, qcol], axis=1) * s_new, axis=0, keepdims=True))
        y_ret = jnp.where(row8 == r, jnp.concatenate(yrow, axis=1), y_ret)

    y_ssd = y_cols.T[:R, :]
    y = (y_ssd + dskip_ref[...] * xs) * _silu(proj_ref[:, OFF_Z:OFF_XBC])
    y1 = jnp.concatenate([_rms(y[:, g * GROUP_DIM:(g + 1) * GROUP_DIM]) for g in range(SSM_GROUPS)],
                         axis=1) * sg_ref[...]
    y2 = jnp.concatenate([_rms(y_ret[:, h * RET_HEADDIM:(h + 1) * RET_HEADDIM]) for h in range(RET_HEADS)],
                         axis=1) * rg_ref[...] * _silu(proj_ref[:, OFF_G:PROJ_MAIN])
    mix_ref[:, :D_SSM] = y1
    mix_ref[:, D_SSM:] = y2


def _mixer_step(proj, dtr, conv_t, ssm, ret, params, *, nb):
    R = STEP_ROWS
    rows2 = lambda i: (i, 0)
    rows3 = lambda i: (i, 0, 0)
    mid3 = lambda i: (0, i, 0)
    const2 = lambda i: (0, 0)
    pspecs = [pl.BlockSpec(p.shape, const2) for p in params]
    return pl.pallas_call(
        _mixer_step_kernel,
        out_shape=(jax.ShapeDtypeStruct((nb, D_MODEL), F32),
                   jax.ShapeDtypeStruct((CONV_W - 1, nb, CONV_DIM), F32),
                   jax.ShapeDtypeStruct((nb, D_SSM, D_STATE), F32),
                   jax.ShapeDtypeStruct((nb, D_RET, RET_HEADDIM), F32)),
        grid=(nb // R,),
        in_specs=[pl.BlockSpec((R, PROJ_MAIN), rows2),
                  pl.BlockSpec((R, LANES), rows2),
                  pl.BlockSpec((CONV_W - 1, R, CONV_DIM), mid3),
                  pl.BlockSpec((R, D_SSM, D_STATE), rows3),
                  pl.BlockSpec((R, D_RET, RET_HEADDIM), rows3)] + pspecs,
        out_specs=(pl.BlockSpec((R, D_MODEL), rows2),
                   pl.BlockSpec((CONV_W - 1, R, CONV_DIM), mid3),
                   pl.BlockSpec((R, D_SSM, D_STATE), rows3),
                   pl.BlockSpec((R, D_RET, RET_HEADDIM), rows3)),
        scratch_shapes=[pltpu.VMEM((4 * D_SSM, LANES), F32)],
        compiler_params=pltpu.CompilerParams(
            dimension_semantics=("arbitrary",), vmem_limit_bytes=VMEM_LIMIT),
        name="mixer_step",
    )(proj, dtr, conv_t, ssm, ret, *params)


def _outproj_kernel(mix_ref, mixs_ref, w_ref, h_ref, hs_ref, g1_ref, g2_ref, hout_ref, f_ref, houts_ref, fs_ref):
    def rows(mix_r, h_r, hout_r, f_r):
        y = _dot(mix_r[...].astype(BF16), w_ref[...])
        h = h_r[...] + _rms(y) * g1_ref[...]
        hout_r[...] = h
        f_r[...] = (_rms(h) * g2_ref[...]).astype(BF16)

    rows(mix_ref, h_ref, hout_ref, f_ref)

    @pl.when(pl.program_id(0) == pl.num_programs(0) - 1)
    def _():
        rows(mixs_ref, hs_ref, houts_ref, fs_ref)


def _outproj(mix, mixs, w, h, hs, g1, g2, *, bm):
    m = mix.shape[0]
    ms = mixs.shape[0]
    row = lambda i: (i, 0)
    const = lambda i: (0, 0)
    return pl.pallas_call(
        _outproj_kernel,
        out_shape=(jax.ShapeDtypeStruct((m, D_MODEL), F32), jax.ShapeDtypeStruct((m, D_MODEL), BF16),
                   jax.ShapeDtypeStruct((ms, D_MODEL), F32), jax.ShapeDtypeStruct((ms, D_MODEL), BF16)),
        grid=(m // bm,),
        in_specs=[pl.BlockSpec((bm, D_MODEL), row),
                  pl.BlockSpec((ms, D_MODEL), const),
                  pl.BlockSpec((D_MODEL, D_MODEL), const),
                  pl.BlockSpec((bm, D_MODEL), row),
                  pl.BlockSpec((ms, D_MODEL), const),
                  pl.BlockSpec((1, D_MODEL), const),
                  pl.BlockSpec((1, D_MODEL), const)],
        out_specs=(pl.BlockSpec((bm, D_MODEL), row), pl.BlockSpec((bm, D_MODEL), row),
                   pl.BlockSpec((ms, D_MODEL), const), pl.BlockSpec((ms, D_MODEL), const)),
        compiler_params=pltpu.CompilerParams(
            dimension_semantics=("arbitrary",), vmem_limit_bytes=VMEM_LIMIT),
        name="outproj",
    )(mix, mixs, w, h, hs, g1, g2)


def _ffn_up_kernel(f_ref, fs_ref, wg_ref, wu_ref, o_ref, os_ref):
    def column_tile(with_side):
        sub = o_ref.shape[1] // 2
        for s in range(2):
            cols = slice(s * sub, (s + 1) * sub)
            wg = wg_ref[:, cols].astype(BF16)
            wu = wu_ref[:, cols].astype(BF16)

            def act(f):
                return (_silu(_dot(f, wg)) * _dot(f, wu)).astype(BF16)

            o_ref[:, cols] = act(f_ref[...])
            if with_side:
                os_ref[:, cols] = act(fs_ref[...])

    on_last = pl.program_id(0) == pl.num_programs(0) - 1
    pl.when(on_last)(functools.partial(column_tile, True))
    pl.when(jnp.logical_not(on_last))(functools.partial(column_tile, False))


def _ffn_up(f, fs, wg, wu, *, bm, bn):
    m = f.shape[0]
    ms = fs.shape[0]
    nm = m // bm
    return pl.pallas_call(
        _ffn_up_kernel,
        out_shape=(jax.ShapeDtypeStruct((m, D_FF), BF16), jax.ShapeDtypeStruct((ms, D_FF), BF16)),
        grid=(nm, D_FF // bn),
        in_specs=[pl.BlockSpec((bm, D_MODEL), lambda i, j: (i, 0)),
                  pl.BlockSpec((ms, D_MODEL), lambda i, j: (0, 0)),
                  pl.BlockSpec((D_MODEL, bn), lambda i, j: (0, j)),
                  pl.BlockSpec((D_MODEL, bn), lambda i, j: (0, j))],
        out_specs=(pl.BlockSpec((bm, bn), lambda i, j: (i, j)),
                   pl.BlockSpec((ms, bn), lambda i, j: (0, jnp.where(i == nm - 1, j, 0)))),
        compiler_params=pltpu.CompilerParams(
            dimension_semantics=("arbitrary", "arbitrary"), vmem_limit_bytes=VMEM_LIMIT),
        name="ffn_up",
    )(f, fs, wg, wu)


def _ffn_down_kernel(a_ref, as_ref, w_ref, h_ref, hs_ref, g_ref, o_ref, os_ref, y_ref, ys_ref, *, bn):
    j = pl.program_id(1)
    on_last = pl.program_id(0) == pl.num_programs(0) - 1

    def column_tile(with_side):
        y_ref[j] = _dot(a_ref[...], w_ref[...])
        if with_side:
            ys_ref[j] = _dot(as_ref[...], w_ref[...])

    pl.when(on_last)(functools.partial(column_tile, True))
    pl.when(jnp.logical_not(on_last))(functools.partial(column_tile, False))

    def finish(y_r, h_r, o_r):
        nt = y_r.shape[0]
        ssq = sum(jnp.sum(y_r[t] * y_r[t], axis=-1, keepdims=True) for t in range(nt))
        scale = lax.rsqrt(ssq / (nt * bn) + EPS)
        for t in range(nt):
            cols = slice(t * bn, (t + 1) * bn)
            o_r[:, cols] = h_r[:, cols] + y_r[t] * scale * g_ref[:, cols]

    @pl.when(j == pl.num_programs(1) - 1)
    def _():
        finish(y_ref, h_ref, o_ref)
        pl.when(on_last)(functools.partial(finish, ys_ref, hs_ref, os_ref))


def _ffn_down(a, a_s, w, h, hs, g, *, bm, bn):
    m = a.shape[0]
    ms = a_s.shape[0]
    const = lambda i, j: (0, 0)
    return pl.pallas_call(
        functools.partial(_ffn_down_kernel, bn=bn),
        out_shape=(jax.ShapeDtypeStruct((m, D_MODEL), F32), jax.ShapeDtypeStruct((ms, D_MODEL), F32)),
        grid=(m // bm, D_MODEL // bn),
        in_specs=[pl.BlockSpec((bm, D_FF), lambda i, j: (i, 0)),
                  pl.BlockSpec((ms, D_FF), const),
                  pl.BlockSpec((D_FF, bn), lambda i, j: (0, j)),
                  pl.BlockSpec((bm, D_MODEL), lambda i, j: (i, 0)),
                  pl.BlockSpec((ms, D_MODEL), const),
                  pl.BlockSpec((1, D_MODEL), const)],
        out_specs=(pl.BlockSpec((bm, D_MODEL), lambda i, j: (i, 0)),
                   pl.BlockSpec((ms, D_MODEL), const)),
        scratch_shapes=[pltpu.VMEM((D_MODEL // bn, bm, bn), F32), pltpu.VMEM((D_MODEL // bn, ms, bn), F32)],
        compiler_params=pltpu.CompilerParams(
            dimension_semantics=("arbitrary", "arbitrary"), vmem_limit_bytes=VMEM_LIMIT),
        name="ffn_down",
    )(a, a_s, w, h, hs, g)


def kernel(x_prompt, x_sample, state_conv, state_ssm, state_ret, meta_tokens, pre_mix_g, post_mix_g,
           pre_ffn_g, post_ffn_g, w_in, conv_w, conv_b, dt_bias, a_log, d_skip, ssm_norm_g, ret_norm_g,
           w_out, w_gate, w_up, w_down):
    bp, seq = x_prompt.shape[:2]
    bs = x_sample.shape[0]
    assert w_in.shape[0] == 1 and x_sample.shape[1] == 1 and seq % CHUNK == 0 and bs == CHUNK

    w_in_t = jnp.swapaxes(w_in[0], 0, 1)
    w_out_b = w_out[0].astype(BF16)
    pad16 = lambda v: jnp.pad(v, ((0, 0), (0, LANES - SSM_HEADS)))
    inv_freq = (ROPE_BASE ** (-jnp.arange(RET_HEADDIM // 2, dtype=F32) / (RET_HEADDIM // 2)))[None, :]
    params = (conv_w[0], conv_b, pad16(dt_bias), pad16(a_log),
              jnp.repeat(d_skip, SSM_HEADDIM, axis=1), ssm_norm_g, ret_norm_g, inv_freq)

    xp = x_prompt.reshape(bp * seq, D_MODEL)
    xs_rows = x_sample.reshape(bs, D_MODEL)
    x_small = jnp.concatenate(
        [xs_rows, meta_tokens.astype(F32), jnp.zeros((CHUNK - N_META, D_MODEL), F32)], axis=0)
    proj_p, dtr_p, proj_s, dtr_s = _inproj(xp, x_small, pre_mix_g, w_in_t, bm=2048, xr=1024, bn=512)

    zc = jnp.zeros((1, CONV_W - 1, CONV_DIM), F32)
    zs = jnp.zeros((1, D_SSM, D_STATE), F32)
    zr = jnp.zeros((1, D_RET, RET_HEADDIM), F32)
    _, m_conv, m_ssm, m_ret = _mixer_seq(
        proj_s.reshape(1, 2 * CHUNK, PROJ_MAIN), dtr_s.reshape(1, 2 * CHUNK, LANES), zc, zs, zr, params,
        nchunks=1, chunk_offset=1, valid=N_META, pos_base=0, name="mixer_meta")

    mix_p, p_conv, p_ssm, p_ret = _mixer_seq(
        proj_p.reshape(bp, seq, PROJ_MAIN), dtr_p.reshape(bp, seq, LANES), m_conv, m_ssm, m_ret, params,
        nchunks=seq // CHUNK, chunk_offset=0, valid=CHUNK, pos_base=N_META, name="mixer_prompt")

    conv_t = jnp.transpose(state_conv[0], (1, 0, 2))
    mix_s, s_conv_t, s_ssm, s_ret = _mixer_step(
        proj_s, dtr_s, conv_t, state_ssm[0].reshape(bs, D_SSM, D_STATE),
        state_ret[0].reshape(bs, D_RET, RET_HEADDIM), params, nb=bs)

    h1_p, f_p, h1_s, f_s = _outproj(mix_p.reshape(bp * seq, D_MODEL), mix_s, w_out_b, xp, xs_rows,
                                    post_mix_g, pre_ffn_g, bm=512)
    act_p, act_s = _ffn_up(f_p, f_s, w_gate[0], w_up[0], bm=2048, bn=512)
    y_p, y_s = _ffn_down(act_p, act_s, w_down[0].astype(BF16), h1_p, h1_s, post_ffn_g, bm=512, bn=512)

    return (y_p.reshape(bp, seq, D_MODEL),
            y_s.reshape(bs, 1, D_MODEL),
            p_conv[None],
            p_ssm.reshape(1, bp, SSM_HEADS, SSM_HEADDIM, D_STATE),
            p_ret.reshape(1, bp, RET_HEADS, RET_HEADDIM, RET_HEADDIM),
            jnp.transpose(s_conv_t, (1, 0, 2))[None],
            s_ssm.reshape(1, bs, SSM_HEADS, SSM_HEADDIM, D_STATE),
            s_ret.reshape(1, bs, RET_HEADS, RET_HEADDIM, RET_HEADDIM))
```

```python
import functools

import numpy as np
import jax
import jax.numpy as jnp
from jax import lax
from jax.experimental import pallas as pl
from jax.experimental.pallas import tpu as pltpu

F32 = jnp.float32
BF16 = jnp.bfloat16

D_MODEL = 2048
N_META = 16
CHUNK = 128
D_SSM = 1024
D_RET = 1024
SSM_HEADDIM = 64
SSM_HEADS = 16
SSM_GROUPS = 2
GROUP_DIM = D_SSM // SSM_GROUPS
D_STATE = 128
CONV_W = 4
CONV_DIM = D_SSM + 2 * SSM_GROUPS * D_STATE
RET_HEADS = 4
RET_HEADDIM = 256
ROPE_BASE = 10000.0
D_FF = 5632
EPS = 1e-6
PAST_LEN = 16384

LANES = 128
SUBLANES = 8
STEP_ROWS = SUBLANES
CONV_PAD = SUBLANES

OFF_Z = 0
OFF_XBC = D_SSM
OFF_Q = OFF_XBC + CONV_DIM
OFF_K = OFF_Q + D_RET
OFF_V = OFF_K + D_RET
OFF_G = OFF_V + D_RET
PROJ_MAIN = OFF_G + D_RET

VMEM_LIMIT = 56 * 1024 * 1024

RET_LOG_GAMMA = [float(np.log1p(-np.float32(2.0) ** np.float32(-5.0 - h)).astype(np.float32))
                 for h in range(RET_HEADS)]


def _silu(x):
    return x / (1.0 + jnp.exp(-x))


def _softplus(x):
    return jnp.maximum(x, 0.0) + jnp.log1p(jnp.exp(-jnp.abs(x)))


def _rms(x):
    return x * lax.rsqrt(jnp.mean(x * x, axis=-1, keepdims=True) + EPS)


def _split3(x):
    hi = x.astype(BF16)
    r = x - hi.astype(F32)
    mid = r.astype(BF16)
    lo = (r - mid.astype(F32)).astype(BF16)
    return hi, mid, lo


def _dot(a, b):
    return jnp.dot(a, b, preferred_element_type=F32)


def _dot_nt(a, b):
    return lax.dot_general(a, b, (((1,), (1,)), ((), ())), preferred_element_type=F32)


def _dot_tn(a, b):
    return lax.dot_general(a, b, (((0,), (0,)), ((), ())), preferred_element_type=F32)


def _exact_right(x, sel):
    hi, mid, lo = _split3(x)
    return _dot(hi, sel) + _dot(mid, sel) + _dot(lo, sel)


def _exact_left(sel, x):
    hi, mid, lo = _split3(x)
    return _dot(sel, hi) + _dot(sel, mid) + _dot(sel, lo)


def _exact_tn(x, sel):
    hi, mid, lo = _split3(x)
    return _dot_tn(hi, sel) + _dot_tn(mid, sel) + _dot_tn(lo, sel)


def _head_expand():
    r = lax.broadcasted_iota(jnp.int32, (LANES, D_SSM), 0)
    c = lax.broadcasted_iota(jnp.int32, (LANES, D_SSM), 1)
    return (c // SSM_HEADDIM == r).astype(BF16)


NORM_ROWS = 256
DT_ROW = D_SSM + CONV_DIM


def _inproj_kernel(x_ref, xs_ref, g_ref, wt_ref, wdt_ref, o_ref, odt_ref, os_ref, odts_ref, u_ref, us_ref,
                   *, npro, nsplit):
    i = pl.program_id(0)
    j = pl.program_id(1)
    on_last = i == pl.num_programs(0) - 1
    xr = x_ref.shape[0]

    @pl.when(j < npro)
    def _():
        wdt = wdt_ref[...].astype(BF16)
        lane = lax.broadcasted_iota(jnp.int32, (NORM_ROWS, LANES), 1)

        def norm_rows(src_ref, src, dst_ref, dt_ref, dst):
            u = (_rms(src_ref[src, :]) * g_ref[...]).astype(BF16)
            dst_ref[dst, :] = u
            dt_ref[dst, :] = jnp.where(lane < SSM_HEADS, _dot_nt(u, wdt), 0.0)

        def body(t, carry):
            src = pl.ds(pl.multiple_of(t * NORM_ROWS, NORM_ROWS), NORM_ROWS)
            dst = pl.ds(pl.multiple_of(j * xr + t * NORM_ROWS, NORM_ROWS), NORM_ROWS)
            norm_rows(x_ref, src, u_ref, odt_ref, dst)
            return carry
        lax.fori_loop(0, xr // NORM_ROWS, body, 0)

        @pl.when(on_last & (j == 0))
        def _():
            for t in range(xs_ref.shape[0] // NORM_ROWS):
                rows = pl.ds(t * NORM_ROWS, NORM_ROWS)
                norm_rows(xs_ref, rows, us_ref, odts_ref, rows)

    def column_tile(with_side):
        sub = wt_ref.shape[0] // nsplit
        for s in range(nsplit):
            cols = slice(s * sub, (s + 1) * sub)
            w = wt_ref[cols, :].astype(BF16)
            o_ref[:, cols] = _dot_nt(u_ref[...], w)
            if with_side:
                os_ref[:, cols] = _dot_nt(us_ref[...], w)

    pl.when((j >= npro) & on_last)(functools.partial(column_tile, True))
    pl.when((j >= npro) & jnp.logical_not(on_last))(functools.partial(column_tile, False))


def _inproj(x, xs, g, wt, *, bm, xr, bn):
    m = x.shape[0]
    ms = xs.shape[0]
    nm = m // bm
    npro = bm // xr
    assert DT_ROW % bn == 0 and ms % NORM_ROWS == 0

    def wrow(i, j):
        t = jnp.maximum(j - npro, 0)
        skip = jnp.where(t * bn >= DT_ROW, SSM_HEADS // SUBLANES, 0)
        return ((t * (bn // SUBLANES) + skip) * SUBLANES, 0)

    col = lambda j: jnp.maximum(j - npro, 0)
    const = lambda i, j: (0, 0)
    return pl.pallas_call(
        functools.partial(_inproj_kernel, npro=npro, nsplit=2),
        out_shape=(jax.ShapeDtypeStruct((m, PROJ_MAIN), F32), jax.ShapeDtypeStruct((m, LANES), F32),
                   jax.ShapeDtypeStruct((ms, PROJ_MAIN), F32), jax.ShapeDtypeStruct((ms, LANES), F32)),
        grid=(nm, npro + PROJ_MAIN // bn),
        in_specs=[pl.BlockSpec((xr, D_MODEL), lambda i, j: (i * npro + jnp.minimum(j, npro - 1), 0)),
                  pl.BlockSpec((ms, D_MODEL), const),
                  pl.BlockSpec((1, D_MODEL), const),
                  pl.BlockSpec((pl.Element(bn), pl.Element(D_MODEL)), wrow),
                  pl.BlockSpec((pl.Element(LANES), pl.Element(D_MODEL)), lambda i, j: (DT_ROW, 0))],
        out_specs=(pl.BlockSpec((bm, bn), lambda i, j: (i, col(j))),
                   pl.BlockSpec((bm, LANES), lambda i, j: (i, 0)),
                   pl.BlockSpec((ms, bn), lambda i, j: (0, jnp.where(i == nm - 1, col(j), 0))),
                   pl.BlockSpec((ms, LANES), const)),
        scratch_shapes=[pltpu.VMEM((bm, D_MODEL), BF16), pltpu.VMEM((ms, D_MODEL), BF16)],
        compiler_params=pltpu.CompilerParams(
            dimension_semantics=("arbitrary", "arbitrary"), vmem_limit_bytes=VMEM_LIMIT),
        name="inproj",
    )(x, xs, g, wt, wt)


N_MIXER_IN = 13
N_MIXER_OUT = 4


def _mixer_seq_kernel(*refs, valid, pos_base, ncast):
    ins = refs[:N_MIXER_IN]
    cast_in = refs[N_MIXER_IN:N_MIXER_IN + ncast]
    outs = refs[N_MIXER_IN + ncast:N_MIXER_IN + ncast + N_MIXER_OUT]
    cast_out = refs[N_MIXER_IN + ncast + N_MIXER_OUT:N_MIXER_IN + 2 * ncast + N_MIXER_OUT]
    scratch = refs[N_MIXER_IN + 2 * ncast + N_MIXER_OUT:]
    _mixer_seq_body(*ins, *outs, *scratch, valid=valid, pos_base=pos_base)
    for src, dst in zip(cast_in, cast_out):
        dst[...] = src[...].astype(BF16)


def _mixer_seq_body(proj_ref, dtr_ref, conv0_ref, ssm0_ref, ret0_ref,
                    convw_ref, convb_ref, dtb_ref, alog_ref, dskip_ref, sg_ref, rg_ref, invf_ref,
                    mix_ref, convo_ref, ssmo_ref, reto_ref,
                    cbuf_ref, rdec_ref, *, valid, pos_base):
    C = CHUNK
    b = pl.program_id(0)
    c = pl.program_id(1)
    rowi = lax.broadcasted_iota(jnp.int32, (C, 1), 0)
    rowf = rowi.astype(F32)
    ri = lax.broadcasted_iota(jnp.int32, (C, C), 0)
    ci = lax.broadcasted_iota(jnp.int32, (C, C), 1)
    causal = ri >= ci

    @pl.when((b == 0) & (c == 0))
    def _():
        diff = (ri - ci).astype(F32)
        for h in range(RET_HEADS):
            rdec_ref[h] = jnp.where(causal, jnp.exp(jnp.maximum(diff, 0.0) * RET_LOG_GAMMA[h]), 0.0)

    hist = CONV_PAD - (CONV_W - 1)

    @pl.when(c == 0)
    def _():
        cbuf_ref[hist:CONV_PAD, :] = conv0_ref[0]
        ssmo_ref[0] = ssm0_ref[0]
        reto_ref[0] = ret0_ref[0]

    xbc_raw = proj_ref[0, :, OFF_XBC:OFF_Q]
    cbuf_ref[CONV_PAD:CONV_PAD + C, :] = xbc_raw
    acc = convb_ref[...] + xbc_raw * convw_ref[CONV_W - 1:CONV_W, :]
    for i in range(CONV_W - 1):
        acc = acc + cbuf_ref[hist + i:hist + i + C, :] * convw_ref[i:i + 1, :]
    xbc = _silu(acc)
    new_prev = cbuf_ref[hist + valid:CONV_PAD + valid, :]
    cbuf_ref[hist:CONV_PAD, :] = new_prev
    convo_ref[0] = new_prev

    xs = xbc[:, :D_SSM]
    bmat = xbc[:, D_SSM:D_SSM + SSM_GROUPS * D_STATE].astype(BF16)
    cmat = xbc[:, D_SSM + SSM_GROUPS * D_STATE:].astype(BF16)

    dt = _softplus(dtr_ref[0] + dtb_ref[...])
    if valid < C:
        dt = jnp.where(rowi < valid, dt, 0.0)
    la = dt * (-jnp.exp(alog_ref[...]))
    tril = causal.astype(BF16)
    triu = (ri <= ci).astype(BF16)
    eye = (ri == ci).astype(BF16)
    lcum = _exact_left(tril, la)
    lcum_t = _exact_tn(la, triu)
    dt_t = _exact_tn(dt, eye)
    expand = _head_expand()
    lcum_x = _exact_right(lcum, expand)
    dt_x = _exact_right(dt, expand)
    last_x = lcum_x[C - 1:C, :]

    cbs = [_dot_nt(cmat[:, g * D_STATE:(g + 1) * D_STATE], bmat[:, g * D_STATE:(g + 1) * D_STATE])
           for g in range(SSM_GROUPS)]
    lane = lax.broadcasted_iota(jnp.int32, (C, LANES), 1)
    left = lane < SSM_HEADDIM
    y_intra = []
    for m in range(SSM_HEADS // 2):
        ws = []
        for h in (2 * m, 2 * m + 1):
            seg = lcum[:, h:h + 1] - lcum_t[h:h + 1, :]
            decay = jnp.exp(jnp.where(causal, seg, -jnp.inf))
            ws.append((cbs[h // (SSM_HEADS // SSM_GROUPS)] * decay * dt_t[h:h + 1, :]).astype(BF16))
        xm = xs[:, m * LANES:(m + 1) * LANES]
        xst = jnp.concatenate([jnp.where(left, xm, 0.0), jnp.where(left, 0.0, xm)], axis=0).astype(BF16)
        y_intra.append(_dot(jnp.concatenate(ws, axis=1), xst))
    y = jnp.concatenate(y_intra, axis=1)

    hstate = ssmo_ref[0]
    hb = hstate.astype(BF16)
    y_inter = jnp.concatenate(
        [_dot_nt(cmat[:, g * D_STATE:(g + 1) * D_STATE], hb[g * GROUP_DIM:(g + 1) * GROUP_DIM, :])
         for g in range(SSM_GROUPS)], axis=1)
    y = y + y_inter * jnp.exp(lcum_x) + dskip_ref[...] * xs

    xw = (xs * (jnp.exp(last_x - lcum_x) * dt_x)).astype(BF16)
    upd = jnp.concatenate(
        [_dot_tn(xw[:, g * GROUP_DIM:(g + 1) * GROUP_DIM], bmat[:, g * D_STATE:(g + 1) * D_STATE])
         for g in range(SSM_GROUPS)], axis=0)
    la_tot = _exact_tn(la, jnp.ones((C, LANES), BF16))
    er = lax.broadcasted_iota(jnp.int32, (D_SSM, LANES), 0)
    ec = lax.broadcasted_iota(jnp.int32, (D_SSM, LANES), 1)
    expand_t = (er // SSM_HEADDIM == ec).astype(BF16)
    chunk_decay = jnp.exp(_exact_left(expand_t, la_tot))
    ssmo_ref[0] = chunk_decay * hstate + upd

    z = proj_ref[0, :, OFF_Z:OFF_XBC]
    y = y * _silu(z)
    y1 = jnp.concatenate([_rms(y[:, g * GROUP_DIM:(g + 1) * GROUP_DIM]) for g in range(SSM_GROUPS)],
                         axis=1) * sg_ref[...]

    pos = (pos_base + c * C).astype(F32) + rowf
    ang = pos * invf_ref[...]
    cos = jnp.cos(ang)
    sin = jnp.sin(ang)
    half = RET_HEADDIM // 2
    y2 = []
    for h in range(RET_HEADS):
        lg = RET_LOG_GAMMA[h]
        q1 = proj_ref[0, :, OFF_Q + h * RET_HEADDIM:OFF_Q + h * RET_HEADDIM + half]
        q2 = proj_ref[0, :, OFF_Q + h * RET_HEADDIM + half:OFF_Q + (h + 1) * RET_HEADDIM]
        k1 = proj_ref[0, :, OFF_K + h * RET_HEADDIM:OFF_K + h * RET_HEADDIM + half]
        k2 = proj_ref[0, :, OFF_K + h * RET_HEADDIM + half:OFF_K + (h + 1) * RET_HEADDIM]
        vh = proj_ref[0, :, OFF_V + h * RET_HEADDIM:OFF_V + (h + 1) * RET_HEADDIM].astype(BF16)
        qr = jnp.concatenate([q1 * cos - q2 * sin, q1 * sin + q2 * cos], axis=1)
        kr = jnp.concatenate([k1 * cos - k2 * sin, k1 * sin + k2 * cos], axis=1) * (RET_HEADDIM ** -0.5)
        if valid < C:
            kr = jnp.where(rowi < valid, kr, 0.0)
        qb = qr.astype(BF16)
        scores = _dot_nt(qb, kr.astype(BF16)) * rdec_ref[h]
        s_old = reto_ref[0, h * RET_HEADDIM:(h + 1) * RET_HEADDIM, :]
        yr = _dot(scores.astype(BF16), vh) + _dot(qb, s_old.astype(BF16)) * jnp.exp((rowf + 1.0) * lg)
        kw = (kr * jnp.exp((valid - 1.0 - rowf) * lg)).astype(BF16)
        reto_ref[0, h * RET_HEADDIM:(h + 1) * RET_HEADDIM, :] = (
            float(np.exp(np.float32(valid * lg))) * s_old + _dot_tn(kw, vh))
        y2.append(_rms(yr))
    gate = proj_ref[0, :, OFF_G:PROJ_MAIN]
    y2 = jnp.concatenate(y2, axis=1) * rg_ref[...] * _silu(gate)

    mix_ref[0, :, :D_SSM] = y1.astype(BF16)
    mix_ref[0, :, D_SSM:] = y2.astype(BF16)


def _mixer_seq(proj, dtr, conv0, ssm0, ret0, params, *, nchunks, chunk_offset, valid, pos_base, name,
               cast=()):
    nb = proj.shape[0]
    nsteps = nb * nchunks
    row = lambda b, c: (b, c + chunk_offset, 0)
    const3 = lambda b, c: (0, 0, 0)
    const2 = lambda b, c: (0, 0)
    per_b = lambda b, c: (b, 0, 0)
    pspecs = [pl.BlockSpec(p.shape, const2) for p in params]
    cast_specs = []
    for w, nblk in cast:
        assert nsteps % nblk == 0 and w.shape[0] % nblk == 0
        every = nsteps // nblk
        cast_specs.append(pl.BlockSpec((w.shape[0] // nblk, w.shape[1]),
                                       lambda b, c, every=every: ((b * nchunks + c) // every, 0)))
    kern = functools.partial(_mixer_seq_kernel, valid=valid, pos_base=pos_base, ncast=len(cast))
    return pl.pallas_call(
        kern,
        out_shape=(jax.ShapeDtypeStruct((nb, nchunks * CHUNK, D_MODEL), BF16),
                   jax.ShapeDtypeStruct((nb, CONV_W - 1, CONV_DIM), F32),
                   jax.ShapeDtypeStruct((nb, D_SSM, D_STATE), F32),
                   jax.ShapeDtypeStruct((nb, D_RET, RET_HEADDIM), F32))
        + tuple(jax.ShapeDtypeStruct(w.shape, BF16) for w, _ in cast),
        grid=(nb, nchunks),
        in_specs=[pl.BlockSpec((1, CHUNK, PROJ_MAIN), row),
                  pl.BlockSpec((1, CHUNK, LANES), row),
                  pl.BlockSpec((1, CONV_W - 1, CONV_DIM), const3),
                  pl.BlockSpec((1, D_SSM, D_STATE), const3),
                  pl.BlockSpec((1, D_RET, RET_HEADDIM), const3)] + pspecs + cast_specs,
        out_specs=(pl.BlockSpec((1, CHUNK, D_MODEL), lambda b, c: (b, c, 0)),
                   pl.BlockSpec((1, CONV_W - 1, CONV_DIM), per_b),
                   pl.BlockSpec((1, D_SSM, D_STATE), per_b),
                   pl.BlockSpec((1, D_RET, RET_HEADDIM), per_b)) + tuple(cast_specs),
        scratch_shapes=[pltpu.VMEM((CONV_PAD + CHUNK, CONV_DIM), F32),
                        pltpu.VMEM((RET_HEADS, CHUNK, CHUNK), F32)],
        compiler_params=pltpu.CompilerParams(
            dimension_semantics=("arbitrary", "arbitrary"), vmem_limit_bytes=VMEM_LIMIT),
        name=name,
    )(proj, dtr, conv0, ssm0, ret0, *params, *[w for w, _ in cast])


def _mixer_step_kernel(proj_ref, dtr_ref, conv_ref, ssm_ref, ret_ref,
                       convw_ref, convb_ref, dtb_ref, alog_ref, dskip_ref, sg_ref, rg_ref, invf_ref,
                       mix_ref, convo_ref, ssmo_ref, reto_ref, cols_ref):
    R = STEP_ROWS
    xbc_raw = proj_ref[:, OFF_XBC:OFF_Q]
    acc = convb_ref[...] + xbc_raw * convw_ref[3:4, :]
    for i in range(CONV_W - 1):
        acc = acc + conv_ref[i] * convw_ref[i:i + 1, :]
    xbc = _silu(acc)
    convo_ref[0] = conv_ref[1]
    convo_ref[1] = conv_ref[2]
    convo_ref[2] = xbc_raw

    xs = xbc[:, :D_SSM]
    bmat = xbc[:, D_SSM:D_SSM + SSM_GROUPS * D_STATE]
    cmat = xbc[:, D_SSM + SSM_GROUPS * D_STATE:]
    dt = _softplus(dtr_ref[...] + dtb_ref[...])
    la = dt * (-jnp.exp(alog_ref[...]))
    expand = _head_expand()
    dt_x = _exact_right(dt, expand)
    decay_x = jnp.exp(_exact_right(la, expand))
    xdt = xs * dt_x

    ang = jnp.float32(PAST_LEN) * invf_ref[...]
    cos = jnp.cos(ang)
    sin = jnp.sin(ang)
    half = RET_HEADDIM // 2
    qs, ks = [], []
    for h in range(RET_HEADS):
        q1 = proj_ref[:, OFF_Q + h * RET_HEADDIM:OFF_Q + h * RET_HEADDIM + half]
        q2 = proj_ref[:, OFF_Q + h * RET_HEADDIM + half:OFF_Q + (h + 1) * RET_HEADDIM]
        k1 = proj_ref[:, OFF_K + h * RET_HEADDIM:OFF_K + h * RET_HEADDIM + half]
        k2 = proj_ref[:, OFF_K + h * RET_HEADDIM + half:OFF_K + (h + 1) * RET_HEADDIM]
        qs += [q1 * cos - q2 * sin, q1 * sin + q2 * cos]
        ks += [(k1 * cos - k2 * sin) * (RET_HEADDIM ** -0.5), (k1 * sin + k2 * cos) * (RET_HEADDIM ** -0.5)]
    qr = jnp.concatenate(qs, axis=1)
    kr = jnp.concatenate(ks, axis=1)
    vv = proj_ref[:, OFF_V:OFF_G]

    allq = jnp.concatenate([decay_x, xdt, kr, qr], axis=1)
    hi = allq.astype(BF16).astype(F32)
    r1 = allq - hi
    mid = r1.astype(BF16).astype(F32)
    lo = (r1 - mid).astype(BF16).astype(F32)
    stack = jnp.concatenate([hi, mid, lo, jnp.zeros_like(hi)], axis=0).astype(BF16)
    krow = lax.broadcasted_iota(jnp.int32, (4 * R, LANES), 0)
    row8 = lax.broadcasted_iota(jnp.int32, (R, 1), 0)
    lane = lax.broadcasted_iota(jnp.int32, (1, LANES), 1)

    y_cols = jnp.zeros((D_SSM, LANES), F32)
    y_ret = jnp.zeros((R, D_RET), F32)
    for r in range(R):
        sel = ((krow % R == r) & (krow < 3 * R)).astype(BF16)
        cols_ref[...] = _dot_tn(stack, sel)
        ycol = []
        for g in range(SSM_GROUPS):
            rows = slice(g * GROUP_DIM, (g + 1) * GROUP_DIM)
            h_old = ssm_ref[r, rows, :]
            h_new = (h_old * cols_ref[g * GROUP_DIM:(g + 1) * GROUP_DIM, :]
                     + cols_ref[D_SSM + g * GROUP_DIM:D_SSM + (g + 1) * GROUP_DIM, :]
                     * bmat[r:r + 1, g * D_STATE:(g + 1) * D_STATE])
            ssmo_ref[r, rows, :] = h_new
            ycol.append(jnp.sum(h_new * cmat[r:r + 1, g * D_STATE:(g + 1) * D_STATE], axis=1, keepdims=True))
        y_cols = jnp.where(lane == r, jnp.concatenate(ycol, axis=0), y_cols)
        yrow = []
        for h in range(RET_HEADS):
            rows = slice(h * RET_HEADDIM, (h + 1) * RET_HEADDIM)
            kcol = cols_ref[2 * D_SSM + h * RET_HEADDIM:2 * D_SSM + (h + 1) * RET_HEADDIM, :]
            qcol = cols_ref[3 * D_SSM + h * RET_HEADDIM:3 * D_SSM + (h + 1) * RET_HEADDIM, :]
            gamma = float(np.exp(np.float32(RET_LOG_GAMMA[h])))
            s_new = (gamma * ret_ref[r, rows, :]
                     + jnp.concatenate([kcol, kcol], axis=1) * vv[r:r + 1, h * RET_HEADDIM:(h + 1) * RET_HEADDIM])
            reto_ref[r, rows, :] = s_new
            yrow.append(jnp.sum(jnp.concatenate([qcol, qcol], axis=1) * s_new, axis=0, keepdims=True))
        y_ret = jnp.where(row8 == r, jnp.concatenate(yrow, axis=1), y_ret)

    y_ssd = y_cols.T[:R, :]
    y = (y_ssd + dskip_ref[...] * xs) * _silu(proj_ref[:, OFF_Z:OFF_XBC])
    y1 = jnp.concatenate([_rms(y[:, g * GROUP_DIM:(g + 1) * GROUP_DIM]) for g in range(SSM_GROUPS)],
                         axis=1) * sg_ref[...]
    y2 = jnp.concatenate([_rms(y_ret[:, h * RET_HEADDIM:(h + 1) * RET_HEADDIM]) for h in range(RET_HEADS)],
                         axis=1) * rg_ref[...] * _silu(proj_ref[:, OFF_G:PROJ_MAIN])
    mix_ref[:, :D_SSM] = y1
    mix_ref[:, D_SSM:] = y2


def _mixer_step(proj, dtr, conv_t, ssm, ret, params, *, nb):
    R = STEP_ROWS
    rows2 = lambda i: (i, 0)
    rows3 = lambda i: (i, 0, 0)
    mid3 = lambda i: (0, i, 0)
    const2 = lambda i: (0, 0)
    pspecs = [pl.BlockSpec(p.shape, const2) for p in params]
    return pl.pallas_call(
        _mixer_step_kernel,
        out_shape=(jax.ShapeDtypeStruct((nb, D_MODEL), F32),
                   jax.ShapeDtypeStruct((CONV_W - 1, nb, CONV_DIM), F32),
                   jax.ShapeDtypeStruct((nb, D_SSM, D_STATE), F32),
                   jax.ShapeDtypeStruct((nb, D_RET, RET_HEADDIM), F32)),
        grid=(nb // R,),
        in_specs=[pl.BlockSpec((R, PROJ_MAIN), rows2),
                  pl.BlockSpec((R, LANES), rows2),
                  pl.BlockSpec((CONV_W - 1, R, CONV_DIM), mid3),
                  pl.BlockSpec((R, D_SSM, D_STATE), rows3),
                  pl.BlockSpec((R, D_RET, RET_HEADDIM), rows3)] + pspecs,
        out_specs=(pl.BlockSpec((R, D_MODEL), rows2),
                   pl.BlockSpec((CONV_W - 1, R, CONV_DIM), mid3),
                   pl.BlockSpec((R, D_SSM, D_STATE), rows3),
                   pl.BlockSpec((R, D_RET, RET_HEADDIM), rows3)),
        scratch_shapes=[pltpu.VMEM((4 * D_SSM, LANES), F32)],
        compiler_params=pltpu.CompilerParams(
            dimension_semantics=("arbitrary",), vmem_limit_bytes=VMEM_LIMIT),
        name="mixer_step",
    )(proj, dtr, conv_t, ssm, ret, *params)


def _outproj_kernel(mix_ref, mixs_ref, w_ref, h_ref, hs_ref, g1_ref, g2_ref, hout_ref, f_ref, houts_ref, fs_ref):
    def rows(mix_r, h_r, hout_r, f_r):
        y = _dot(mix_r[...].astype(BF16), w_ref[...])
        h = h_r[...] + _rms(y) * g1_ref[...]
        hout_r[...] = h
        f_r[...] = (_rms(h) * g2_ref[...]).astype(BF16)

    rows(mix_ref, h_ref, hout_ref, f_ref)

    @pl.when(pl.program_id(0) == pl.num_programs(0) - 1)
    def _():
        rows(mixs_ref, hs_ref, houts_ref, fs_ref)


def _outproj(mix, mixs, w, h, hs, g1, g2, *, bm):
    m = mix.shape[0]
    ms = mixs.shape[0]
    row = lambda i: (i, 0)
    const = lambda i: (0, 0)
    return pl.pallas_call(
        _outproj_kernel,
        out_shape=(jax.ShapeDtypeStruct((m, D_MODEL), F32), jax.ShapeDtypeStruct((m, D_MODEL), BF16),
                   jax.ShapeDtypeStruct((ms, D_MODEL), F32), jax.ShapeDtypeStruct((ms, D_MODEL), BF16)),
        grid=(m // bm,),
        in_specs=[pl.BlockSpec((bm, D_MODEL), row),
                  pl.BlockSpec((ms, D_MODEL), const),
                  pl.BlockSpec((D_MODEL, D_MODEL), const),
                  pl.BlockSpec((bm, D_MODEL), row),
                  pl.BlockSpec((ms, D_MODEL), const),
                  pl.BlockSpec((1, D_MODEL), const),
                  pl.BlockSpec((1, D_MODEL), const)],
        out_specs=(pl.BlockSpec((bm, D_MODEL), row), pl.BlockSpec((bm, D_MODEL), row),
                   pl.BlockSpec((ms, D_MODEL), const), pl.BlockSpec((ms, D_MODEL), const)),
        compiler_params=pltpu.CompilerParams(
            dimension_semantics=("arbitrary",), vmem_limit_bytes=VMEM_LIMIT),
        name="outproj",
    )(mix, mixs, w, h, hs, g1, g2)


def _ffn_up_kernel(f_ref, fs_ref, wg_ref, wu_ref, o_ref, os_ref):
    def column_tile(with_side):
        sub = o_ref.shape[1] // 2
        for s in range(2):
            cols = slice(s * sub, (s + 1) * sub)
            wg = wg_ref[:, cols]
            wu = wu_ref[:, cols]

            def act(f):
                return (_silu(_dot(f, wg)) * _dot(f, wu)).astype(BF16)

            o_ref[:, cols] = act(f_ref[...])
            if with_side:
                os_ref[:, cols] = act(fs_ref[...])

    on_last = pl.program_id(0) == pl.num_programs(0) - 1
    pl.when(on_last)(functools.partial(column_tile, True))
    pl.when(jnp.logical_not(on_last))(functools.partial(column_tile, False))


def _ffn_up(f, fs, wg, wu, *, bm, bn):
    m = f.shape[0]
    ms = fs.shape[0]
    nm = m // bm
    return pl.pallas_call(
        _ffn_up_kernel,
        out_shape=(jax.ShapeDtypeStruct((m, D_FF), BF16), jax.ShapeDtypeStruct((ms, D_FF), BF16)),
        grid=(nm, D_FF // bn),
        in_specs=[pl.BlockSpec((bm, D_MODEL), lambda i, j: (i, 0)),
                  pl.BlockSpec((ms, D_MODEL), lambda i, j: (0, 0)),
                  pl.BlockSpec((D_MODEL, bn), lambda i, j: (0, j)),
                  pl.BlockSpec((D_MODEL, bn), lambda i, j: (0, j))],
        out_specs=(pl.BlockSpec((bm, bn), lambda i, j: (i, j)),
                   pl.BlockSpec((ms, bn), lambda i, j: (0, jnp.where(i == nm - 1, j, 0)))),
        compiler_params=pltpu.CompilerParams(
            dimension_semantics=("arbitrary", "arbitrary"), vmem_limit_bytes=VMEM_LIMIT),
        name="ffn_up",
    )(f, fs, wg, wu)


def _ffn_down_kernel(a_ref, as_ref, w_ref, h_ref, hs_ref, g_ref, o_ref, os_ref, y_ref, ys_ref, *, bn):
    j = pl.program_id(1)
    on_last = pl.program_id(0) == pl.num_programs(0) - 1

    def column_tile(with_side):
        y_ref[j] = _dot(a_ref[...], w_ref[...])
        if with_side:
            ys_ref[j] = _dot(as_ref[...], w_ref[...])

    pl.when(on_last)(functools.partial(column_tile, True))
    pl.when(jnp.logical_not(on_last))(functools.partial(column_tile, False))

    def finish(y_r, h_r, o_r):
        nt = y_r.shape[0]
        ssq = sum(jnp.sum(y_r[t] * y_r[t], axis=-1, keepdims=True) for t in range(nt))
        scale = lax.rsqrt(ssq / (nt * bn) + EPS)
        for t in range(nt):
            cols = slice(t * bn, (t + 1) * bn)
            o_r[:, cols] = h_r[:, cols] + y_r[t] * scale * g_ref[:, cols]

    @pl.when(j == pl.num_programs(1) - 1)
    def _():
        finish(y_ref, h_ref, o_ref)
        pl.when(on_last)(functools.partial(finish, ys_ref, hs_ref, os_ref))


def _ffn_down(a, a_s, w, h, hs, g, *, bm, bn):
    m = a.shape[0]
    ms = a_s.shape[0]
    const = lambda i, j: (0, 0)
    return pl.pallas_call(
        functools.partial(_ffn_down_kernel, bn=bn),
        out_shape=(jax.ShapeDtypeStruct((m, D_MODEL), F32), jax.ShapeDtypeStruct((ms, D_MODEL), F32)),
        grid=(m // bm, D_MODEL // bn),
        in_specs=[pl.BlockSpec((bm, D_FF), lambda i, j: (i, 0)),
                  pl.BlockSpec((ms, D_FF), const),
                  pl.BlockSpec((D_FF, bn), lambda i, j: (0, j)),
                  pl.BlockSpec((bm, D_MODEL), lambda i, j: (i, 0)),
                  pl.BlockSpec((ms, D_MODEL), const),
                  pl.BlockSpec((1, D_MODEL), const)],
        out_specs=(pl.BlockSpec((bm, D_MODEL), lambda i, j: (i, 0)),
                   pl.BlockSpec((ms, D_MODEL), const)),
        scratch_shapes=[pltpu.VMEM((D_MODEL // bn, bm, bn), F32), pltpu.VMEM((D_MODEL // bn, ms, bn), F32)],
        compiler_params=pltpu.CompilerParams(
            dimension_semantics=("arbitrary", "arbitrary"), vmem_limit_bytes=VMEM_LIMIT),
        name="ffn_down",
    )(a, a_s, w, h, hs, g)


def kernel(x_prompt, x_sample, state_conv, state_ssm, state_ret, meta_tokens, pre_mix_g, post_mix_g,
           pre_ffn_g, post_ffn_g, w_in, conv_w, conv_b, dt_bias, a_log, d_skip, ssm_norm_g, ret_norm_g,
           w_out, w_gate, w_up, w_down):
    bp, seq = x_prompt.shape[:2]
    bs = x_sample.shape[0]
    assert w_in.shape[0] == 1 and x_sample.shape[1] == 1 and seq % CHUNK == 0 and bs == CHUNK

    w_in_t = jnp.swapaxes(w_in[0], 0, 1)
    pad16 = lambda v: jnp.pad(v, ((0, 0), (0, LANES - SSM_HEADS)))
    inv_freq = (ROPE_BASE ** (-jnp.arange(RET_HEADDIM // 2, dtype=F32) / (RET_HEADDIM // 2)))[None, :]
    params = (conv_w[0], conv_b, pad16(dt_bias), pad16(a_log),
              jnp.repeat(d_skip, SSM_HEADDIM, axis=1), ssm_norm_g, ret_norm_g, inv_freq)

    xp = x_prompt.reshape(bp * seq, D_MODEL)
    xs_rows = x_sample.reshape(bs, D_MODEL)
    x_small = jnp.concatenate(
        [xs_rows, meta_tokens.astype(F32), jnp.zeros((CHUNK - N_META, D_MODEL), F32)], axis=0)
    proj_p, dtr_p, proj_s, dtr_s = _inproj(xp, x_small, pre_mix_g, w_in_t, bm=2048, xr=1024, bn=512)

    zc = jnp.zeros((1, CONV_W - 1, CONV_DIM), F32)
    zs = jnp.zeros((1, D_SSM, D_STATE), F32)
    zr = jnp.zeros((1, D_RET, RET_HEADDIM), F32)
    _, m_conv, m_ssm, m_ret = _mixer_seq(
        proj_s.reshape(1, 2 * CHUNK, PROJ_MAIN), dtr_s.reshape(1, 2 * CHUNK, LANES), zc, zs, zr, params,
        nchunks=1, chunk_offset=1, valid=N_META, pos_base=0, name="mixer_meta")[:4]

    nsteps = bp * (seq // CHUNK)
    mix_p, p_conv, p_ssm, p_ret, w_out_b, w_gate_b, w_up_b, w_down_b = _mixer_seq(
        proj_p.reshape(bp, seq, PROJ_MAIN), dtr_p.reshape(bp, seq, LANES), m_conv, m_ssm, m_ret, params,
        nchunks=seq // CHUNK, chunk_offset=0, valid=CHUNK, pos_base=N_META, name="mixer_prompt",
        cast=((w_out[0], nsteps), (w_gate[0], nsteps), (w_up[0], nsteps), (w_down[0], nsteps // 2)))

    conv_t = jnp.transpose(state_conv[0], (1, 0, 2))
    mix_s, s_conv_t, s_ssm, s_ret = _mixer_step(
        proj_s, dtr_s, conv_t, state_ssm[0].reshape(bs, D_SSM, D_STATE),
        state_ret[0].reshape(bs, D_RET, RET_HEADDIM), params, nb=bs)

    h1_p, f_p, h1_s, f_s = _outproj(mix_p.reshape(bp * seq, D_MODEL), mix_s, w_out_b, xp, xs_rows,
                                    post_mix_g, pre_ffn_g, bm=512)
    act_p, act_s = _ffn_up(f_p, f_s, w_gate_b, w_up_b, bm=2048, bn=512)
    y_p, y_s = _ffn_down(act_p, act_s, w_down_b, h1_p, h1_s, post_ffn_g, bm=512, bn=512)

    return (y_p.reshape(bp, seq, D_MODEL),
            y_s.reshape(bs, 1, D_MODEL),
            p_conv[None],
            p_ssm.reshape(1, bp, SSM_HEADS, SSM_HEADDIM, D_STATE),
            p_ret.reshape(1, bp, RET_HEADS, RET_HEADDIM, RET_HEADDIM),
            jnp.transpose(s_conv_t, (1, 0, 2))[None],
            s_ssm.reshape(1, bs, SSM_HEADS, SSM_HEADDIM, D_STATE),
            s_ret.reshape(1, bs, RET_HEADS, RET_HEADDIM, RET_HEADDIM))
```

```python
import functools

import numpy as np
import jax
import jax.numpy as jnp
from jax import lax
from jax.experimental import pallas as pl
from jax.experimental.pallas import tpu as pltpu

F32 = jnp.float32
BF16 = jnp.bfloat16

D_MODEL = 2048
N_META = 16
CHUNK = 128
D_SSM = 1024
D_RET = 1024
SSM_HEADDIM = 64
SSM_HEADS = 16
SSM_GROUPS = 2
GROUP_DIM = D_SSM // SSM_GROUPS
D_STATE = 128
CONV_W = 4
CONV_DIM = D_SSM + 2 * SSM_GROUPS * D_STATE
RET_HEADS = 4
RET_HEADDIM = 256
ROPE_BASE = 10000.0
D_FF = 5632
EPS = 1e-6
PAST_LEN = 16384

LANES = 128
SUBLANES = 8
STEP_ROWS = SUBLANES
CONV_PAD = SUBLANES

OFF_Z = 0
OFF_XBC = D_SSM
OFF_Q = OFF_XBC + CONV_DIM
OFF_K = OFF_Q + D_RET
OFF_V = OFF_K + D_RET
OFF_G = OFF_V + D_RET
PROJ_MAIN = OFF_G + D_RET

VMEM_LIMIT = 56 * 1024 * 1024

RET_LOG_GAMMA = [float(np.log1p(-np.float32(2.0) ** np.float32(-5.0 - h)).astype(np.float32))
                 for h in range(RET_HEADS)]


def _silu(x):
    return x / (1.0 + jnp.exp(-x))


def _softplus(x):
    return jnp.maximum(x, 0.0) + jnp.log1p(jnp.exp(-jnp.abs(x)))


def _rms(x):
    return x * lax.rsqrt(jnp.mean(x * x, axis=-1, keepdims=True) + EPS)


def _split3(x):
    hi = x.astype(BF16)
    r = x - hi.astype(F32)
    mid = r.astype(BF16)
    lo = (r - mid.astype(F32)).astype(BF16)
    return hi, mid, lo


def _dot(a, b):
    return jnp.dot(a, b, preferred_element_type=F32)


def _dot_nt(a, b):
    return lax.dot_general(a, b, (((1,), (1,)), ((), ())), preferred_element_type=F32)


def _dot_tn(a, b):
    return lax.dot_general(a, b, (((0,), (0,)), ((), ())), preferred_element_type=F32)


def _exact_right(x, sel):
    hi, mid, lo = _split3(x)
    return _dot(hi, sel) + _dot(mid, sel) + _dot(lo, sel)


def _exact_left(sel, x):
    hi, mid, lo = _split3(x)
    return _dot(sel, hi) + _dot(sel, mid) + _dot(sel, lo)


def _exact_tn(x, sel):
    hi, mid, lo = _split3(x)
    return _dot_tn(hi, sel) + _dot_tn(mid, sel) + _dot_tn(lo, sel)


def _head_expand():
    r = lax.broadcasted_iota(jnp.int32, (LANES, D_SSM), 0)
    c = lax.broadcasted_iota(jnp.int32, (LANES, D_SSM), 1)
    return (c // SSM_HEADDIM == r).astype(BF16)


NORM_ROWS = 256
DT_ROW = D_SSM + CONV_DIM


def _inproj_kernel(x_ref, xs_ref, g_ref, wt_ref, wdt_ref, o_ref, odt_ref, os_ref, odts_ref, u_ref, us_ref,
                   *, npro, nsplit):
    i = pl.program_id(0)
    j = pl.program_id(1)
    on_last = i == pl.num_programs(0) - 1
    xr = x_ref.shape[0]

    @pl.when(j < npro)
    def _():
        wdt = wdt_ref[...].astype(BF16)
        lane = lax.broadcasted_iota(jnp.int32, (NORM_ROWS, LANES), 1)

        def norm_rows(src_ref, src, dst_ref, dt_ref, dst):
            u = (_rms(src_ref[src, :]) * g_ref[...]).astype(BF16)
            dst_ref[dst, :] = u
            dt_ref[dst, :] = jnp.where(lane < SSM_HEADS, _dot_nt(u, wdt), 0.0)

        def body(t, carry):
            src = pl.ds(pl.multiple_of(t * NORM_ROWS, NORM_ROWS), NORM_ROWS)
            dst = pl.ds(pl.multiple_of(j * xr + t * NORM_ROWS, NORM_ROWS), NORM_ROWS)
            norm_rows(x_ref, src, u_ref, odt_ref, dst)
            return carry
        lax.fori_loop(0, xr // NORM_ROWS, body, 0)

        @pl.when(on_last & (j == 0))
        def _():
            for t in range(xs_ref.shape[0] // NORM_ROWS):
                rows = pl.ds(t * NORM_ROWS, NORM_ROWS)
                norm_rows(xs_ref, rows, us_ref, odts_ref, rows)

    def column_tile(with_side):
        sub = wt_ref.shape[0] // nsplit
        for s in range(nsplit):
            cols = slice(s * sub, (s + 1) * sub)
            w = wt_ref[cols, :].astype(BF16)
            o_ref[:, cols] = _dot_nt(u_ref[...], w)
            if with_side:
                os_ref[:, cols] = _dot_nt(us_ref[...], w)

    pl.when((j >= npro) & on_last)(functools.partial(column_tile, True))
    pl.when((j >= npro) & jnp.logical_not(on_last))(functools.partial(column_tile, False))


def _inproj(x, xs, g, wt, *, bm, xr, bn):
    m = x.shape[0]
    ms = xs.shape[0]
    nm = m // bm
    npro = bm // xr
    assert DT_ROW % bn == 0 and ms % NORM_ROWS == 0

    def wrow(i, j):
        t = jnp.maximum(j - npro, 0)
        skip = jnp.where(t * bn >= DT_ROW, SSM_HEADS // SUBLANES, 0)
        return ((t * (bn // SUBLANES) + skip) * SUBLANES, 0)

    col = lambda j: jnp.maximum(j - npro, 0)
    const = lambda i, j: (0, 0)
    return pl.pallas_call(
        functools.partial(_inproj_kernel, npro=npro, nsplit=2),
        out_shape=(jax.ShapeDtypeStruct((m, PROJ_MAIN), F32), jax.ShapeDtypeStruct((m, LANES), F32),
                   jax.ShapeDtypeStruct((ms, PROJ_MAIN), F32), jax.ShapeDtypeStruct((ms, LANES), F32)),
        grid=(nm, npro + PROJ_MAIN // bn),
        in_specs=[pl.BlockSpec((xr, D_MODEL), lambda i, j: (i * npro + jnp.minimum(j, npro - 1), 0)),
                  pl.BlockSpec((ms, D_MODEL), const),
                  pl.BlockSpec((1, D_MODEL), const),
                  pl.BlockSpec((pl.Element(bn), pl.Element(D_MODEL)), wrow),
                  pl.BlockSpec((pl.Element(LANES), pl.Element(D_MODEL)), lambda i, j: (DT_ROW, 0))],
        out_specs=(pl.BlockSpec((bm, bn), lambda i, j: (i, col(j))),
                   pl.BlockSpec((bm, LANES), lambda i, j: (i, 0)),
                   pl.BlockSpec((ms, bn), lambda i, j: (0, jnp.where(i == nm - 1, col(j), 0))),
                   pl.BlockSpec((ms, LANES), const)),
        scratch_shapes=[pltpu.VMEM((bm, D_MODEL), BF16), pltpu.VMEM((ms, D_MODEL), BF16)],
        compiler_params=pltpu.CompilerParams(
            dimension_semantics=("arbitrary", "arbitrary"), vmem_limit_bytes=VMEM_LIMIT),
        name="inproj",
    )(x, xs, g, wt, wt)


N_MIXER_IN = 13
N_MIXER_OUT = 4


def _mixer_seq_kernel(*refs, valid, pos_base, ncast):
    ins = refs[:N_MIXER_IN]
    cast_in = refs[N_MIXER_IN:N_MIXER_IN + ncast]
    outs = refs[N_MIXER_IN + ncast:N_MIXER_IN + ncast + N_MIXER_OUT]
    cast_out = refs[N_MIXER_IN + ncast + N_MIXER_OUT:N_MIXER_IN + 2 * ncast + N_MIXER_OUT]
    scratch = refs[N_MIXER_IN + 2 * ncast + N_MIXER_OUT:]
    _mixer_seq_body(*ins, *outs, *scratch, valid=valid, pos_base=pos_base)
    for src, dst in zip(cast_in, cast_out):
        dst[...] = src[...].astype(BF16)


def _mixer_seq_body(proj_ref, dtr_ref, conv0_ref, ssm0_ref, ret0_ref,
                    convw_ref, convb_ref, dtb_ref, alog_ref, dskip_ref, sg_ref, rg_ref, invf_ref,
                    mix_ref, convo_ref, ssmo_ref, reto_ref,
                    cbuf_ref, rdec_ref, *, valid, pos_base):
    C = CHUNK
    b = pl.program_id(0)
    c = pl.program_id(1)
    rowi = lax.broadcasted_iota(jnp.int32, (C, 1), 0)
    rowf = rowi.astype(F32)
    ri = lax.broadcasted_iota(jnp.int32, (C, C), 0)
    ci = lax.broadcasted_iota(jnp.int32, (C, C), 1)
    causal = ri >= ci

    @pl.when((b == 0) & (c == 0))
    def _():
        diff = (ri - ci).astype(F32)
        for h in range(RET_HEADS):
            rdec_ref[h] = jnp.where(causal, jnp.exp(jnp.maximum(diff, 0.0) * RET_LOG_GAMMA[h]), 0.0)

    hist = CONV_PAD - (CONV_W - 1)

    @pl.when(c == 0)
    def _():
        cbuf_ref[hist:CONV_PAD, :] = conv0_ref[0]
        ssmo_ref[0] = ssm0_ref[0]
        reto_ref[0] = ret0_ref[0]

    xbc_raw = proj_ref[0, :, OFF_XBC:OFF_Q]
    cbuf_ref[CONV_PAD:CONV_PAD + C, :] = xbc_raw
    acc = convb_ref[...] + xbc_raw * convw_ref[CONV_W - 1:CONV_W, :]
    for i in range(CONV_W - 1):
        acc = acc + cbuf_ref[hist + i:hist + i + C, :] * convw_ref[i:i + 1, :]
    xbc = _silu(acc)
    new_prev = cbuf_ref[hist + valid:CONV_PAD + valid, :]
    cbuf_ref[hist:CONV_PAD, :] = new_prev
    convo_ref[0] = new_prev

    xs = xbc[:, :D_SSM]
    bmat = xbc[:, D_SSM:D_SSM + SSM_GROUPS * D_STATE].astype(BF16)
    cmat = xbc[:, D_SSM + SSM_GROUPS * D_STATE:].astype(BF16)

    dt = _softplus(dtr_ref[0] + dtb_ref[...])
    if valid < C:
        dt = jnp.where(rowi < valid, dt, 0.0)
    la = dt * (-jnp.exp(alog_ref[...]))
    tril = causal.astype(BF16)
    triu = (ri <= ci).astype(BF16)
    eye = (ri == ci).astype(BF16)
    lcum = _exact_left(tril, la)
    lcum_t = _exact_tn(la, triu)
    dt_t = _exact_tn(dt, eye)
    expand = _head_expand()
    lcum_x = _exact_right(lcum, expand)
    dt_x = _exact_right(dt, expand)
    last_x = lcum_x[C - 1:C, :]

    cbs = [_dot_nt(cmat[:, g * D_STATE:(g + 1) * D_STATE], bmat[:, g * D_STATE:(g + 1) * D_STATE])
           for g in range(SSM_GROUPS)]
    lane = lax.broadcasted_iota(jnp.int32, (C, LANES), 1)
    left = lane < SSM_HEADDIM
    y_intra = []
    for m in range(SSM_HEADS // 2):
        ws = []
        for h in (2 * m, 2 * m + 1):
            seg = lcum[:, h:h + 1] - lcum_t[h:h + 1, :]
            decay = jnp.exp(jnp.where(causal, seg, -jnp.inf))
            ws.append((cbs[h // (SSM_HEADS // SSM_GROUPS)] * decay * dt_t[h:h + 1, :]).astype(BF16))
        xm = xs[:, m * LANES:(m + 1) * LANES]
        xst = jnp.concatenate([jnp.where(left, xm, 0.0), jnp.where(left, 0.0, xm)], axis=0).astype(BF16)
        y_intra.append(_dot(jnp.concatenate(ws, axis=1), xst))
    y = jnp.concatenate(y_intra, axis=1)

    hstate = ssmo_ref[0]
    hb = hstate.astype(BF16)
    y_inter = jnp.concatenate(
        [_dot_nt(cmat[:, g * D_STATE:(g + 1) * D_STATE], hb[g * GROUP_DIM:(g + 1) * GROUP_DIM, :])
         for g in range(SSM_GROUPS)], axis=1)
    y = y + y_inter * jnp.exp(lcum_x) + dskip_ref[...] * xs

    xw = (xs * (jnp.exp(last_x - lcum_x) * dt_x)).astype(BF16)
    upd = jnp.concatenate(
        [_dot_tn(xw[:, g * GROUP_DIM:(g + 1) * GROUP_DIM], bmat[:, g * D_STATE:(g + 1) * D_STATE])
         for g in range(SSM_GROUPS)], axis=0)
    la_tot = _exact_tn(la, jnp.ones((C, LANES), BF16))
    er = lax.broadcasted_iota(jnp.int32, (D_SSM, LANES), 0)
    ec = lax.broadcasted_iota(jnp.int32, (D_SSM, LANES), 1)
    expand_t = (er // SSM_HEADDIM == ec).astype(BF16)
    chunk_decay = jnp.exp(_exact_left(expand_t, la_tot))
    ssmo_ref[0] = chunk_decay * hstate + upd

    z = proj_ref[0, :, OFF_Z:OFF_XBC]
    y = y * _silu(z)
    y1 = jnp.concatenate([_rms(y[:, g * GROUP_DIM:(g + 1) * GROUP_DIM]) for g in range(SSM_GROUPS)],
                         axis=1) * sg_ref[...]

    pos = (pos_base + c * C).astype(F32) + rowf
    ang = pos * invf_ref[...]
    cos = jnp.cos(ang)
    sin = jnp.sin(ang)
    half = RET_HEADDIM // 2
    y2 = []
    for h in range(RET_HEADS):
        lg = RET_LOG_GAMMA[h]
        q1 = proj_ref[0, :, OFF_Q + h * RET_HEADDIM:OFF_Q + h * RET_HEADDIM + half]
        q2 = proj_ref[0, :, OFF_Q + h * RET_HEADDIM + half:OFF_Q + (h + 1) * RET_HEADDIM]
        k1 = proj_ref[0, :, OFF_K + h * RET_HEADDIM:OFF_K + h * RET_HEADDIM + half]
        k2 = proj_ref[0, :, OFF_K + h * RET_HEADDIM + half:OFF_K + (h + 1) * RET_HEADDIM]
        vh = proj_ref[0, :, OFF_V + h * RET_HEADDIM:OFF_V + (h + 1) * RET_HEADDIM].astype(BF16)
        qr = jnp.concatenate([q1 * cos - q2 * sin, q1 * sin + q2 * cos], axis=1)
        kr = jnp.concatenate([k1 * cos - k2 * sin, k1 * sin + k2 * cos], axis=1) * (RET_HEADDIM ** -0.5)
        if valid < C:
            kr = jnp.where(rowi < valid, kr, 0.0)
        qb = qr.astype(BF16)
        scores = _dot_nt(qb, kr.astype(BF16)) * rdec_ref[h]
        s_old = reto_ref[0, h * RET_HEADDIM:(h + 1) * RET_HEADDIM, :]
        yr = _dot(scores.astype(BF16), vh) + _dot(qb, s_old.astype(BF16)) * jnp.exp((rowf + 1.0) * lg)
        kw = (kr * jnp.exp((valid - 1.0 - rowf) * lg)).astype(BF16)
        reto_ref[0, h * RET_HEADDIM:(h + 1) * RET_HEADDIM, :] = (
            float(np.exp(np.float32(valid * lg))) * s_old + _dot_tn(kw, vh))
        y2.append(_rms(yr))
    gate = proj_ref[0, :, OFF_G:PROJ_MAIN]
    y2 = jnp.concatenate(y2, axis=1) * rg_ref[...] * _silu(gate)

    mix_ref[0, :, :D_SSM] = y1.astype(BF16)
    mix_ref[0, :, D_SSM:] = y2.astype(BF16)


def _mixer_seq(proj, dtr, conv0, ssm0, ret0, params, *, nchunks, chunk_offset, valid, pos_base, name,
               cast=()):
    nb = proj.shape[0]
    nsteps = nb * nchunks
    row = lambda b, c: (b, c + chunk_offset, 0)
    const3 = lambda b, c: (0, 0, 0)
    const2 = lambda b, c: (0, 0)
    per_b = lambda b, c: (b, 0, 0)
    pspecs = [pl.BlockSpec(p.shape, const2) for p in params]
    cast_specs = []
    for w, nblk in cast:
        assert nsteps % nblk == 0 and w.shape[0] % nblk == 0
        every = nsteps // nblk
        cast_specs.append(pl.BlockSpec((w.shape[0] // nblk, w.shape[1]),
                                       lambda b, c, every=every: ((b * nchunks + c) // every, 0)))
    kern = functools.partial(_mixer_seq_kernel, valid=valid, pos_base=pos_base, ncast=len(cast))
    return pl.pallas_call(
        kern,
        out_shape=(jax.ShapeDtypeStruct((nb, nchunks * CHUNK, D_MODEL), BF16),
                   jax.ShapeDtypeStruct((nb, CONV_W - 1, CONV_DIM), F32),
                   jax.ShapeDtypeStruct((nb, D_SSM, D_STATE), F32),
                   jax.ShapeDtypeStruct((nb, D_RET, RET_HEADDIM), F32))
        + tuple(jax.ShapeDtypeStruct(w.shape, BF16) for w, _ in cast),
        grid=(nb, nchunks),
        in_specs=[pl.BlockSpec((1, CHUNK, PROJ_MAIN), row),
                  pl.BlockSpec((1, CHUNK, LANES), row),
                  pl.BlockSpec((1, CONV_W - 1, CONV_DIM), const3),
                  pl.BlockSpec((1, D_SSM, D_STATE), const3),
                  pl.BlockSpec((1, D_RET, RET_HEADDIM), const3)] + pspecs + cast_specs,
        out_specs=(pl.BlockSpec((1, CHUNK, D_MODEL), lambda b, c: (b, c, 0)),
                   pl.BlockSpec((1, CONV_W - 1, CONV_DIM), per_b),
                   pl.BlockSpec((1, D_SSM, D_STATE), per_b),
                   pl.BlockSpec((1, D_RET, RET_HEADDIM), per_b)) + tuple(cast_specs),
        scratch_shapes=[pltpu.VMEM((CONV_PAD + CHUNK, CONV_DIM), F32),
                        pltpu.VMEM((RET_HEADS, CHUNK, CHUNK), F32)],
        compiler_params=pltpu.CompilerParams(
            dimension_semantics=("arbitrary", "arbitrary"), vmem_limit_bytes=VMEM_LIMIT),
        name=name,
    )(proj, dtr, conv0, ssm0, ret0, *params, *[w for w, _ in cast])


def _mixer_step_kernel(proj_ref, dtr_ref, conv_ref, ssm_ref, ret_ref,
                       convw_ref, convb_ref, dtb_ref, alog_ref, dskip_ref, sg_ref, rg_ref, invf_ref,
                       mix_ref, convo_ref, ssmo_ref, reto_ref, cols_ref):
    R = STEP_ROWS
    xbc_raw = proj_ref[:, OFF_XBC:OFF_Q]
    acc = convb_ref[...] + xbc_raw * convw_ref[3:4, :]
    for i in range(CONV_W - 1):
        acc = acc + conv_ref[i] * convw_ref[i:i + 1, :]
    xbc = _silu(acc)
    convo_ref[0] = conv_ref[1]
    convo_ref[1] = conv_ref[2]
    convo_ref[2] = xbc_raw

    xs = xbc[:, :D_SSM]
    bmat = xbc[:, D_SSM:D_SSM + SSM_GROUPS * D_STATE]
    cmat = xbc[:, D_SSM + SSM_GROUPS * D_STATE:]
    dt = _softplus(dtr_ref[...] + dtb_ref[...])
    la = dt * (-jnp.exp(alog_ref[...]))
    expand = _head_expand()
    dt_x = _exact_right(dt, expand)
    decay_x = jnp.exp(_exact_right(la, expand))
    xdt = xs * dt_x

    ang = jnp.float32(PAST_LEN) * invf_ref[...]
    cos = jnp.cos(ang)
    sin = jnp.sin(ang)
    half = RET_HEADDIM // 2
    qs, ks = [], []
    for h in range(RET_HEADS):
        q1 = proj_ref[:, OFF_Q + h * RET_HEADDIM:OFF_Q + h * RET_HEADDIM + half]
        q2 = proj_ref[:, OFF_Q + h * RET_HEADDIM + half:OFF_Q + (h + 1) * RET_HEADDIM]
        k1 = proj_ref[:, OFF_K + h * RET_HEADDIM:OFF_K + h * RET_HEADDIM + half]
        k2 = proj_ref[:, OFF_K + h * RET_HEADDIM + half:OFF_K + (h + 1) * RET_HEADDIM]
        qs += [q1 * cos - q2 * sin, q1 * sin + q2 * cos]
        ks += [(k1 * cos - k2 * sin) * (RET_HEADDIM ** -0.5), (k1 * sin + k2 * cos) * (RET_HEADDIM ** -0.5)]
    qr = jnp.concatenate(qs, axis=1)
    kr = jnp.concatenate(ks, axis=1)
    vv = proj_ref[:, OFF_V:OFF_G]

    allq = jnp.concatenate([decay_x, xdt, kr, qr], axis=1)
    hi = allq.astype(BF16).astype(F32)
    r1 = allq - hi
    mid = r1.astype(BF16).astype(F32)
    lo = (r1 - mid).astype(BF16).astype(F32)
    stack = jnp.concatenate([hi, mid, lo, jnp.zeros_like(hi)], axis=0).astype(BF16)
    krow = lax.broadcasted_iota(jnp.int32, (4 * R, LANES), 0)
    row8 = lax.broadcasted_iota(jnp.int32, (R, 1), 0)
    lane = lax.broadcasted_iota(jnp.int32, (1, LANES), 1)

    y_cols = jnp.zeros((D_SSM, LANES), F32)
    y_ret = jnp.zeros((R, D_RET), F32)
    for r in range(R):
        sel = ((krow % R == r) & (krow < 3 * R)).astype(BF16)
        cols_ref[...] = _dot_tn(stack, sel)
        ycol = []
        for g in range(SSM_GROUPS):
            rows = slice(g * GROUP_DIM, (g + 1) * GROUP_DIM)
            h_old = ssm_ref[r, rows, :]
            h_new = (h_old * cols_ref[g * GROUP_DIM:(g + 1) * GROUP_DIM, :]
                     + cols_ref[D_SSM + g * GROUP_DIM:D_SSM + (g + 1) * GROUP_DIM, :]
                     * bmat[r:r + 1, g * D_STATE:(g + 1) * D_STATE])
            ssmo_ref[r, rows, :] = h_new
            ycol.append(jnp.sum(h_new * cmat[r:r + 1, g * D_STATE:(g + 1) * D_STATE], axis=1, keepdims=True))
        y_cols = jnp.where(lane == r, jnp.concatenate(ycol, axis=0), y_cols)
        yrow = []
        for h in range(RET_HEADS):
            rows = slice(h * RET_HEADDIM, (h + 1) * RET_HEADDIM)
            kcol = cols_ref[2 * D_SSM + h * RET_HEADDIM:2 * D_SSM + (h + 1) * RET_HEADDIM, :]
            qcol = cols_ref[3 * D_SSM + h * RET_HEADDIM:3 * D_SSM + (h + 1) * RET_HEADDIM, :]
            gamma = float(np.exp(np.float32(RET_LOG_GAMMA[h])))
            s_new = (gamma * ret_ref[r, rows, :]
                     + jnp.concatenate([kcol, kcol], axis=1) * vv[r:r + 1, h * RET_HEADDIM:(h + 1) * RET_HEADDIM])
            reto_ref[r, rows, :] = s_new
            yrow.append(jnp.sum(jnp.concatenate([qcol, qcol], axis=1) * s_new, axis=0, keepdims=True))
        y_ret = jnp.where(row8 == r, jnp.concatenate(yrow, axis=1), y_ret)

    y_ssd = y_cols.T[:R, :]
    y = (y_ssd + dskip_ref[...] * xs) * _silu(proj_ref[:, OFF_Z:OFF_XBC])
    y1 = jnp.concatenate([_rms(y[:, g * GROUP_DIM:(g + 1) * GROUP_DIM]) for g in range(SSM_GROUPS)],
                         axis=1) * sg_ref[...]
    y2 = jnp.concatenate([_rms(y_ret[:, h * RET_HEADDIM:(h + 1) * RET_HEADDIM]) for h in range(RET_HEADS)],
                         axis=1) * rg_ref[...] * _silu(proj_ref[:, OFF_G:PROJ_MAIN])
    mix_ref[:, :D_SSM] = y1
    mix_ref[:, D_SSM:] = y2


def _mixer_step(proj, dtr, conv_t, ssm, ret, params, *, nb):
    R = STEP_ROWS
    rows2 = lambda i: (i, 0)
    rows3 = lambda i: (i, 0, 0)
    mid3 = lambda i: (0, i, 0)
    const2 = lambda i: (0, 0)
    pspecs = [pl.BlockSpec(p.shape, const2) for p in params]
    return pl.pallas_call(
        _mixer_step_kernel,
        out_shape=(jax.ShapeDtypeStruct((nb, D_MODEL), F32),
                   jax.ShapeDtypeStruct((CONV_W - 1, nb, CONV_DIM), F32),
                   jax.ShapeDtypeStruct((nb, D_SSM, D_STATE), F32),
                   jax.ShapeDtypeStruct((nb, D_RET, RET_HEADDIM), F32)),
        grid=(nb // R,),
        in_specs=[pl.BlockSpec((R, PROJ_MAIN), rows2),
                  pl.BlockSpec((R, LANES), rows2),
                  pl.BlockSpec((CONV_W - 1, R, CONV_DIM), mid3),
                  pl.BlockSpec((R, D_SSM, D_STATE), rows3),
                  pl.BlockSpec((R, D_RET, RET_HEADDIM), rows3)] + pspecs,
        out_specs=(pl.BlockSpec((R, D_MODEL), rows2),
                   pl.BlockSpec((CONV_W - 1, R, CONV_DIM), mid3),
                   pl.BlockSpec((R, D_SSM, D_STATE), rows3),
                   pl.BlockSpec((R, D_RET, RET_HEADDIM), rows3)),
        scratch_shapes=[pltpu.VMEM((4 * D_SSM, LANES), F32)],
        compiler_params=pltpu.CompilerParams(
            dimension_semantics=("arbitrary",), vmem_limit_bytes=VMEM_LIMIT),
        name="mixer_step",
    )(proj, dtr, conv_t, ssm, ret, *params)


def _outproj_kernel(mix_ref, mixs_ref, w_ref, h_ref, hs_ref, g1_ref, g2_ref, hout_ref, f_ref, houts_ref, fs_ref):
    def rows(mix_r, h_r, hout_r, f_r):
        y = _dot(mix_r[...].astype(BF16), w_ref[...])
        h = h_r[...] + _rms(y) * g1_ref[...]
        hout_r[...] = h
        f_r[...] = (_rms(h) * g2_ref[...]).astype(BF16)

    rows(mix_ref, h_ref, hout_ref, f_ref)

    @pl.when(pl.program_id(0) == pl.num_programs(0) - 1)
    def _():
        rows(mixs_ref, hs_ref, houts_ref, fs_ref)


def _outproj(mix, mixs, w, h, hs, g1, g2, *, bm):
    m = mix.shape[0]
    ms = mixs.shape[0]
    row = lambda i: (i, 0)
    const = lambda i: (0, 0)
    return pl.pallas_call(
        _outproj_kernel,
        out_shape=(jax.ShapeDtypeStruct((m, D_MODEL), F32), jax.ShapeDtypeStruct((m, D_MODEL), BF16),
                   jax.ShapeDtypeStruct((ms, D_MODEL), F32), jax.ShapeDtypeStruct((ms, D_MODEL), BF16)),
        grid=(m // bm,),
        in_specs=[pl.BlockSpec((bm, D_MODEL), row),
                  pl.BlockSpec((ms, D_MODEL), const),
                  pl.BlockSpec((D_MODEL, D_MODEL), const),
                  pl.BlockSpec((bm, D_MODEL), row),
                  pl.BlockSpec((ms, D_MODEL), const),
                  pl.BlockSpec((1, D_MODEL), const),
                  pl.BlockSpec((1, D_MODEL), const)],
        out_specs=(pl.BlockSpec((bm, D_MODEL), row), pl.BlockSpec((bm, D_MODEL), row),
                   pl.BlockSpec((ms, D_MODEL), const), pl.BlockSpec((ms, D_MODEL), const)),
        compiler_params=pltpu.CompilerParams(
            dimension_semantics=("arbitrary",), vmem_limit_bytes=VMEM_LIMIT),
        name="outproj",
    )(mix, mixs, w, h, hs, g1, g2)


FFN_SPLIT = 2


def _ffn_kernel(f_ref, fs_ref, wg_ref, wu_ref, wd_ref, h_hbm, hs_ref, g_ref, o_ref, os_ref, hbuf_ref, hsem):
    i = pl.program_id(0)
    j = pl.program_id(1)
    on_last = i == pl.num_programs(0) - 1
    bm = o_ref.shape[0]

    def h_copy():
        return pltpu.make_async_copy(h_hbm.at[pl.ds(pl.multiple_of(i * bm, bm), bm), :], hbuf_ref, hsem)

    @pl.when(j == 0)
    def _():
        h_copy().start()
        o_ref[...] = jnp.zeros_like(o_ref)

        @pl.when(on_last)
        def _():
            os_ref[...] = jnp.zeros_like(os_ref)

    def ff_tile(with_side):
        fsub = wg_ref.shape[1] // FFN_SPLIT
        nsub = o_ref.shape[1] // FFN_SPLIT

        def through(f, o_r, s):
            ff = slice(s * fsub, (s + 1) * fsub)
            a = (_silu(_dot(f, wg_ref[:, ff])) * _dot(f, wu_ref[:, ff])).astype(BF16)
            for n in range(FFN_SPLIT):
                nn = slice(n * nsub, (n + 1) * nsub)
                o_r[:, nn] += _dot(a, wd_ref[ff, nn])

        for s in range(FFN_SPLIT):
            through(f_ref[...], o_ref, s)
            if with_side:
                through(fs_ref[...], os_ref, s)

    pl.when(on_last)(functools.partial(ff_tile, True))
    pl.when(jnp.logical_not(on_last))(functools.partial(ff_tile, False))

    @pl.when(j == pl.num_programs(1) - 1)
    def _():
        h_copy().wait()

        def body(t, carry):
            rows = pl.ds(pl.multiple_of(t * NORM_ROWS, NORM_ROWS), NORM_ROWS)
            o_ref[rows, :] = hbuf_ref[rows, :] + _rms(o_ref[rows, :]) * g_ref[...]
            return carry
        lax.fori_loop(0, bm // NORM_ROWS, body, 0)

        @pl.when(on_last)
        def _():
            os_ref[...] = hs_ref[...] + _rms(os_ref[...]) * g_ref[...]


def _ffn(f, fs, wg, wu, wd, h, hs, g, *, bm, bf):
    m = f.shape[0]
    ms = fs.shape[0]
    const = lambda i, j: (0, 0)
    return pl.pallas_call(
        _ffn_kernel,
        out_shape=(jax.ShapeDtypeStruct((m, D_MODEL), F32), jax.ShapeDtypeStruct((ms, D_MODEL), F32)),
        grid=(m // bm, D_FF // bf),
        in_specs=[pl.BlockSpec((bm, D_MODEL), lambda i, j: (i, 0)),
                  pl.BlockSpec((ms, D_MODEL), const),
                  pl.BlockSpec((D_MODEL, bf), lambda i, j: (0, j)),
                  pl.BlockSpec((D_MODEL, bf), lambda i, j: (0, j)),
                  pl.BlockSpec((bf, D_MODEL), lambda i, j: (j, 0)),
                  pl.BlockSpec(memory_space=pl.ANY),
                  pl.BlockSpec((ms, D_MODEL), const),
                  pl.BlockSpec((1, D_MODEL), const)],
        out_specs=(pl.BlockSpec((bm, D_MODEL), lambda i, j: (i, 0)),
                   pl.BlockSpec((ms, D_MODEL), const)),
        scratch_shapes=[pltpu.VMEM((bm, D_MODEL), F32), pltpu.SemaphoreType.DMA],
        compiler_params=pltpu.CompilerParams(
            dimension_semantics=("arbitrary", "arbitrary"), vmem_limit_bytes=VMEM_LIMIT),
        name="ffn",
    )(f, fs, wg, wu, wd, h, hs, g)


def kernel(x_prompt, x_sample, state_conv, state_ssm, state_ret, meta_tokens, pre_mix_g, post_mix_g,
           pre_ffn_g, post_ffn_g, w_in, conv_w, conv_b, dt_bias, a_log, d_skip, ssm_norm_g, ret_norm_g,
           w_out, w_gate, w_up, w_down):
    bp, seq = x_prompt.shape[:2]
    bs = x_sample.shape[0]
    assert w_in.shape[0] == 1 and x_sample.shape[1] == 1 and seq % CHUNK == 0 and bs == CHUNK

    w_in_t = jnp.swapaxes(w_in[0], 0, 1)
    pad16 = lambda v: jnp.pad(v, ((0, 0), (0, LANES - SSM_HEADS)))
    inv_freq = (ROPE_BASE ** (-jnp.arange(RET_HEADDIM // 2, dtype=F32) / (RET_HEADDIM // 2)))[None, :]
    params = (conv_w[0], conv_b, pad16(dt_bias), pad16(a_log),
              jnp.repeat(d_skip, SSM_HEADDIM, axis=1), ssm_norm_g, ret_norm_g, inv_freq)

    xp = x_prompt.reshape(bp * seq, D_MODEL)
    xs_rows = x_sample.reshape(bs, D_MODEL)
    x_small = jnp.concatenate(
        [xs_rows, meta_tokens.astype(F32), jnp.zeros((CHUNK - N_META, D_MODEL), F32)], axis=0)
    proj_p, dtr_p, proj_s, dtr_s = _inproj(xp, x_small, pre_mix_g, w_in_t, bm=2048, xr=1024, bn=512)

    zc = jnp.zeros((1, CONV_W - 1, CONV_DIM), F32)
    zs = jnp.zeros((1, D_SSM, D_STATE), F32)
    zr = jnp.zeros((1, D_RET, RET_HEADDIM), F32)
    _, m_conv, m_ssm, m_ret = _mixer_seq(
        proj_s.reshape(1, 2 * CHUNK, PROJ_MAIN), dtr_s.reshape(1, 2 * CHUNK, LANES), zc, zs, zr, params,
        nchunks=1, chunk_offset=1, valid=N_META, pos_base=0, name="mixer_meta")[:4]

    nsteps = bp * (seq // CHUNK)
    mix_p, p_conv, p_ssm, p_ret, w_out_b, w_gate_b, w_up_b, w_down_b = _mixer_seq(
        proj_p.reshape(bp, seq, PROJ_MAIN), dtr_p.reshape(bp, seq, LANES), m_conv, m_ssm, m_ret, params,
        nchunks=seq // CHUNK, chunk_offset=0, valid=CHUNK, pos_base=N_META, name="mixer_prompt",
        cast=((w_out[0], nsteps), (w_gate[0], nsteps), (w_up[0], nsteps), (w_down[0], nsteps // 2)))

    conv_t = jnp.transpose(state_conv[0], (1, 0, 2))
    mix_s, s_conv_t, s_ssm, s_ret = _mixer_step(
        proj_s, dtr_s, conv_t, state_ssm[0].reshape(bs, D_SSM, D_STATE),
        state_ret[0].reshape(bs, D_RET, RET_HEADDIM), params, nb=bs)

    h1_p, f_p, h1_s, f_s = _outproj(mix_p.reshape(bp * seq, D_MODEL), mix_s, w_out_b, xp, xs_rows,
                                    post_mix_g, pre_ffn_g, bm=512)
    y_p, y_s = _ffn(f_p, f_s, w_gate_b, w_up_b, w_down_b, h1_p, h1_s, post_ffn_g, bm=1024, bf=512)

    return (y_p.reshape(bp, seq, D_MODEL),
            y_s.reshape(bs, 1, D_MODEL),
            p_conv[None],
            p_ssm.reshape(1, bp, SSM_HEADS, SSM_HEADDIM, D_STATE),
            p_ret.reshape(1, bp, RET_HEADS, RET_HEADDIM, RET_HEADDIM),
            jnp.transpose(s_conv_t, (1, 0, 2))[None],
            s_ssm.reshape(1, bs, SSM_HEADS, SSM_HEADDIM, D_STATE),
            s_ret.reshape(1, bs, RET_HEADS, RET_HEADDIM, RET_HEADDIM))
```

```python
import functools

import numpy as np
import jax
import jax.numpy as jnp
from jax import lax
from jax.experimental import pallas as pl
from jax.experimental.pallas import tpu as pltpu

F32 = jnp.float32
BF16 = jnp.bfloat16

D_MODEL = 2048
N_META = 16
CHUNK = 128
D_SSM = 1024
D_RET = 1024
SSM_HEADDIM = 64
SSM_HEADS = 16
SSM_GROUPS = 2
GROUP_DIM = D_SSM // SSM_GROUPS
D_STATE = 128
CONV_W = 4
CONV_DIM = D_SSM + 2 * SSM_GROUPS * D_STATE
RET_HEADS = 4
RET_HEADDIM = 256
ROPE_BASE = 10000.0
D_FF = 5632
EPS = 1e-6
PAST_LEN = 16384
LOG2E = float(np.log2(np.e))

LANES = 128
SUBLANES = 8
STEP_ROWS = SUBLANES
CONV_PAD = SUBLANES

OFF_Z = 0
OFF_XBC = D_SSM
OFF_Q = OFF_XBC + CONV_DIM
OFF_K = OFF_Q + D_RET
OFF_V = OFF_K + D_RET
OFF_G = OFF_V + D_RET
PROJ_MAIN = OFF_G + D_RET

VMEM_LIMIT = 56 * 1024 * 1024

RET_LOG_GAMMA = [float(np.log1p(-np.float32(2.0) ** np.float32(-5.0 - h)).astype(np.float32))
                 for h in range(RET_HEADS)]


def _silu(x):
    return x / (1.0 + jnp.exp(-x))


def _softplus(x):
    return jnp.maximum(x, 0.0) + jnp.log1p(jnp.exp(-jnp.abs(x)))


def _rms(x):
    return x * lax.rsqrt(jnp.mean(x * x, axis=-1, keepdims=True) + EPS)


def _split3(x):
    hi = x.astype(BF16)
    r = x - hi.astype(F32)
    mid = r.astype(BF16)
    lo = (r - mid.astype(F32)).astype(BF16)
    return hi, mid, lo


def _dot(a, b):
    return jnp.dot(a, b, preferred_element_type=F32)


def _dot_nt(a, b):
    return lax.dot_general(a, b, (((1,), (1,)), ((), ())), preferred_element_type=F32)


def _dot_tn(a, b):
    return lax.dot_general(a, b, (((0,), (0,)), ((), ())), preferred_element_type=F32)


def _exact_right(x, sel):
    hi, mid, lo = _split3(x)
    return _dot(hi, sel) + _dot(mid, sel) + _dot(lo, sel)


def _select_right(x, sel):
    hi = x.astype(BF16)
    lo = (x - hi.astype(F32)).astype(BF16)
    return _dot(hi, sel) + _dot(lo, sel)


def _exact_left(sel, x):
    hi, mid, lo = _split3(x)
    return _dot(sel, hi) + _dot(sel, mid) + _dot(sel, lo)


def _exact_tn(x, sel):
    hi, mid, lo = _split3(x)
    return _dot_tn(hi, sel) + _dot_tn(mid, sel) + _dot_tn(lo, sel)


def _head_expand():
    r = lax.broadcasted_iota(jnp.int32, (LANES, D_SSM), 0)
    c = lax.broadcasted_iota(jnp.int32, (LANES, D_SSM), 1)
    return (c // SSM_HEADDIM == r).astype(BF16)


NORM_ROWS = 256
DT_ROW = D_SSM + CONV_DIM


def _inproj_kernel(x_ref, xs_ref, g_ref, wt_ref, wdt_ref, o_ref, odt_ref, os_ref, odts_ref, u_ref, us_ref,
                   *, npro, nsplit):
    i = pl.program_id(0)
    j = pl.program_id(1)
    on_last = i == pl.num_programs(0) - 1
    xr = x_ref.shape[0]

    @pl.when(j < npro)
    def _():
        wdt = wdt_ref[...].astype(BF16)
        lane = lax.broadcasted_iota(jnp.int32, (NORM_ROWS, LANES), 1)

        def norm_rows(src_ref, src, dst_ref, dt_ref, dst):
            u = (_rms(src_ref[src, :]) * g_ref[...]).astype(BF16)
            dst_ref[dst, :] = u
            dt_ref[dst, :] = jnp.where(lane < SSM_HEADS, _dot_nt(u, wdt), 0.0)

        def body(t, carry):
            src = pl.ds(pl.multiple_of(t * NORM_ROWS, NORM_ROWS), NORM_ROWS)
            dst = pl.ds(pl.multiple_of(j * xr + t * NORM_ROWS, NORM_ROWS), NORM_ROWS)
            norm_rows(x_ref, src, u_ref, odt_ref, dst)
            return carry
        lax.fori_loop(0, xr // NORM_ROWS, body, 0)

        @pl.when(on_last & (j == 0))
        def _():
            for t in range(xs_ref.shape[0] // NORM_ROWS):
                rows = pl.ds(t * NORM_ROWS, NORM_ROWS)
                norm_rows(xs_ref, rows, us_ref, odts_ref, rows)

    def column_tile(with_side):
        sub = wt_ref.shape[0] // nsplit
        for s in range(nsplit):
            cols = slice(s * sub, (s + 1) * sub)
            w = wt_ref[cols, :].astype(BF16)
            o_ref[:, cols] = _dot_nt(u_ref[...], w)
            if with_side:
                os_ref[:, cols] = _dot_nt(us_ref[...], w)

    pl.when((j >= npro) & on_last)(functools.partial(column_tile, True))
    pl.when((j >= npro) & jnp.logical_not(on_last))(functools.partial(column_tile, False))


def _inproj(x, xs, g, wt, *, bm, xr, bn):
    m = x.shape[0]
    ms = xs.shape[0]
    nm = m // bm
    npro = bm // xr
    assert DT_ROW % bn == 0 and ms % NORM_ROWS == 0

    def wrow(i, j):
        t = jnp.maximum(j - npro, 0)
        skip = jnp.where(t * bn >= DT_ROW, SSM_HEADS // SUBLANES, 0)
        return ((t * (bn // SUBLANES) + skip) * SUBLANES, 0)

    col = lambda j: jnp.maximum(j - npro, 0)
    const = lambda i, j: (0, 0)
    return pl.pallas_call(
        functools.partial(_inproj_kernel, npro=npro, nsplit=2),
        out_shape=(jax.ShapeDtypeStruct((m, PROJ_MAIN), F32), jax.ShapeDtypeStruct((m, LANES), F32),
                   jax.ShapeDtypeStruct((ms, PROJ_MAIN), F32), jax.ShapeDtypeStruct((ms, LANES), F32)),
        grid=(nm, npro + PROJ_MAIN // bn),
        in_specs=[pl.BlockSpec((xr, D_MODEL), lambda i, j: (i * npro + jnp.minimum(j, npro - 1), 0)),
                  pl.BlockSpec((ms, D_MODEL), const),
                  pl.BlockSpec((1, D_MODEL), const),
                  pl.BlockSpec((pl.Element(bn), pl.Element(D_MODEL)), wrow),
                  pl.BlockSpec((pl.Element(LANES), pl.Element(D_MODEL)), lambda i, j: (DT_ROW, 0))],
        out_specs=(pl.BlockSpec((bm, bn), lambda i, j: (i, col(j))),
                   pl.BlockSpec((bm, LANES), lambda i, j: (i, 0)),
                   pl.BlockSpec((ms, bn), lambda i, j: (0, jnp.where(i == nm - 1, col(j), 0))),
                   pl.BlockSpec((ms, LANES), const)),
        scratch_shapes=[pltpu.VMEM((bm, D_MODEL), BF16), pltpu.VMEM((ms, D_MODEL), BF16)],
        compiler_params=pltpu.CompilerParams(
            dimension_semantics=("arbitrary", "arbitrary"), vmem_limit_bytes=VMEM_LIMIT),
        name="inproj",
    )(x, xs, g, wt, wt)


N_MIXER_IN = 13
N_MIXER_OUT = 4


def _mixer_seq_kernel(*refs, valid, pos_base, ncast):
    ins = refs[:N_MIXER_IN]
    cast_in = refs[N_MIXER_IN:N_MIXER_IN + ncast]
    outs = refs[N_MIXER_IN + ncast:N_MIXER_IN + ncast + N_MIXER_OUT]
    cast_out = refs[N_MIXER_IN + ncast + N_MIXER_OUT:N_MIXER_IN + 2 * ncast + N_MIXER_OUT]
    scratch = refs[N_MIXER_IN + 2 * ncast + N_MIXER_OUT:]
    _mixer_seq_body(*ins, *outs, *scratch, valid=valid, pos_base=pos_base)
    for src, dst in zip(cast_in, cast_out):
        dst[...] = src[...].astype(BF16)


def _mixer_seq_body(proj_ref, dtr_ref, conv0_ref, ssm0_ref, ret0_ref,
                    convw_ref, convb_ref, dtb_ref, alog_ref, dskip_ref, sg_ref, rg_ref, invf_ref,
                    mix_ref, convo_ref, ssmo_ref, reto_ref,
                    cbuf_ref, rdec_ref, trig_ref, *, valid, pos_base):
    C = CHUNK
    b = pl.program_id(0)
    c = pl.program_id(1)
    rowi = lax.broadcasted_iota(jnp.int32, (C, 1), 0)
    rowf = rowi.astype(F32)
    ri = lax.broadcasted_iota(jnp.int32, (C, C), 0)
    ci = lax.broadcasted_iota(jnp.int32, (C, C), 1)
    causal = ri >= ci

    @pl.when((b == 0) & (c == 0))
    def _():
        diff = (ri - ci).astype(F32)
        for h in range(RET_HEADS):
            rdec_ref[h] = jnp.where(causal, jnp.exp(jnp.maximum(diff, 0.0) * RET_LOG_GAMMA[h]), 0.0)
        row_ang = rowf * invf_ref[...]
        trig_ref[0] = jnp.cos(row_ang)
        trig_ref[1] = jnp.sin(row_ang)

    hist = CONV_PAD - (CONV_W - 1)

    @pl.when(c == 0)
    def _():
        cbuf_ref[hist:CONV_PAD, :] = conv0_ref[0]
        ssmo_ref[0] = ssm0_ref[0]
        reto_ref[0] = ret0_ref[0]

    xbc_raw = proj_ref[0, :, OFF_XBC:OFF_Q]
    cbuf_ref[CONV_PAD:CONV_PAD + C, :] = xbc_raw
    acc = convb_ref[...] + xbc_raw * convw_ref[CONV_W - 1:CONV_W, :]
    for i in range(CONV_W - 1):
        acc = acc + cbuf_ref[hist + i:hist + i + C, :] * convw_ref[i:i + 1, :]
    xbc = _silu(acc)
    new_prev = cbuf_ref[hist + valid:CONV_PAD + valid, :]
    cbuf_ref[hist:CONV_PAD, :] = new_prev
    convo_ref[0] = new_prev

    xs = xbc[:, :D_SSM]
    bmat = xbc[:, D_SSM:D_SSM + SSM_GROUPS * D_STATE].astype(BF16)
    cmat = xbc[:, D_SSM + SSM_GROUPS * D_STATE:].astype(BF16)

    dt = _softplus(dtr_ref[0] + dtb_ref[...])
    if valid < C:
        dt = jnp.where(rowi < valid, dt, 0.0)
    la = dt * (-jnp.exp(alog_ref[...]))
    tril = causal.astype(BF16)
    triu = (ri <= ci).astype(BF16)
    eye = (ri == ci).astype(BF16)
    lcum = _exact_left(tril, la)
    lcum_t = _exact_tn(la, triu)
    dt_t = _exact_tn(dt, eye)
    expand = _head_expand()
    carry_scale = _select_right(jnp.exp(lcum), expand)
    tail_scale = _select_right(jnp.exp(lcum[C - 1:C, :] - lcum) * dt, expand)
    lcum2 = lcum * LOG2E
    lcum2_t = lcum_t * LOG2E

    cbs = [_dot_nt(cmat[:, g * D_STATE:(g + 1) * D_STATE], bmat[:, g * D_STATE:(g + 1) * D_STATE])
           for g in range(SSM_GROUPS)]
    lane = lax.broadcasted_iota(jnp.int32, (C, LANES), 1)
    left = lane < SSM_HEADDIM
    y_intra = []
    for m in range(SSM_HEADS // 2):
        ws = []
        for h in (2 * m, 2 * m + 1):
            seg2 = lcum2[:, h:h + 1] - lcum2_t[h:h + 1, :]
            decay = jnp.exp2(jnp.where(causal, seg2, -jnp.inf))
            ws.append((cbs[h // (SSM_HEADS // SSM_GROUPS)] * decay * dt_t[h:h + 1, :]).astype(BF16))
        xm = xs[:, m * LANES:(m + 1) * LANES]
        xst = jnp.concatenate([jnp.where(left, xm, 0.0), jnp.where(left, 0.0, xm)], axis=0).astype(BF16)
        y_intra.append(_dot(jnp.concatenate(ws, axis=1), xst))
    y = jnp.concatenate(y_intra, axis=1)

    hstate = ssmo_ref[0]
    hb = hstate.astype(BF16)
    y_inter = jnp.concatenate(
        [_dot_nt(cmat[:, g * D_STATE:(g + 1) * D_STATE], hb[g * GROUP_DIM:(g + 1) * GROUP_DIM, :])
         for g in range(SSM_GROUPS)], axis=1)
    y = y + y_inter * carry_scale + dskip_ref[...] * xs

    xw = (xs * tail_scale).astype(BF16)
    upd = jnp.concatenate(
        [_dot_tn(xw[:, g * GROUP_DIM:(g + 1) * GROUP_DIM], bmat[:, g * D_STATE:(g + 1) * D_STATE])
         for g in range(SSM_GROUPS)], axis=0)
    la_tot = _exact_tn(la, jnp.ones((C, LANES), BF16))
    er = lax.broadcasted_iota(jnp.int32, (D_SSM, LANES), 0)
    ec = lax.broadcasted_iota(jnp.int32, (D_SSM, LANES), 1)
    expand_t = (er // SSM_HEADDIM == ec).astype(BF16)
    chunk_decay = jnp.exp(_exact_left(expand_t, la_tot))
    ssmo_ref[0] = chunk_decay * hstate + upd

    z = proj_ref[0, :, OFF_Z:OFF_XBC]
    y = y * _silu(z)
    y1 = jnp.concatenate([_rms(y[:, g * GROUP_DIM:(g + 1) * GROUP_DIM]) for g in range(SSM_GROUPS)],
                         axis=1) * sg_ref[...]

    ang0 = (pos_base + c * C).astype(F32) * invf_ref[...]
    cos0, sin0 = jnp.cos(ang0), jnp.sin(ang0)
    cos = cos0 * trig_ref[0] - sin0 * trig_ref[1]
    sin = sin0 * trig_ref[0] + cos0 * trig_ref[1]
    kscale = RET_HEADDIM ** -0.5
    cos_k, sin_k = cos * kscale, sin * kscale
    half = RET_HEADDIM // 2
    y2 = []
    for h in range(RET_HEADS):
        lg = RET_LOG_GAMMA[h]
        q1 = proj_ref[0, :, OFF_Q + h * RET_HEADDIM:OFF_Q + h * RET_HEADDIM + half]
        q2 = proj_ref[0, :, OFF_Q + h * RET_HEADDIM + half:OFF_Q + (h + 1) * RET_HEADDIM]
        k1 = proj_ref[0, :, OFF_K + h * RET_HEADDIM:OFF_K + h * RET_HEADDIM + half]
        k2 = proj_ref[0, :, OFF_K + h * RET_HEADDIM + half:OFF_K + (h + 1) * RET_HEADDIM]
        vh = proj_ref[0, :, OFF_V + h * RET_HEADDIM:OFF_V + (h + 1) * RET_HEADDIM].astype(BF16)
        qr = jnp.concatenate([q1 * cos - q2 * sin, q1 * sin + q2 * cos], axis=1)
        kr = jnp.concatenate([k1 * cos_k - k2 * sin_k, k1 * sin_k + k2 * cos_k], axis=1)
        if valid < C:
            kr = jnp.where(rowi < valid, kr, 0.0)
        qb = qr.astype(BF16)
        scores = _dot_nt(qb, kr.astype(BF16)) * rdec_ref[h]
        s_old = reto_ref[0, h * RET_HEADDIM:(h + 1) * RET_HEADDIM, :]
        yr = _dot(scores.astype(BF16), vh) + _dot(qb, s_old.astype(BF16)) * jnp.exp((rowf + 1.0) * lg)
        kw = (kr * jnp.exp((valid - 1.0 - rowf) * lg)).astype(BF16)
        reto_ref[0, h * RET_HEADDIM:(h + 1) * RET_HEADDIM, :] = (
            float(np.exp(np.float32(valid * lg))) * s_old + _dot_tn(kw, vh))
        y2.append(_rms(yr))
    gate = proj_ref[0, :, OFF_G:PROJ_MAIN]
    y2 = jnp.concatenate(y2, axis=1) * rg_ref[...] * _silu(gate)

    mix_ref[0, :, :D_SSM] = y1.astype(BF16)
    mix_ref[0, :, D_SSM:] = y2.astype(BF16)


def _mixer_seq(proj, dtr, conv0, ssm0, ret0, params, *, nchunks, chunk_offset, valid, pos_base, name,
               cast=()):
    nb = proj.shape[0]
    nsteps = nb * nchunks
    row = lambda b, c: (b, c + chunk_offset, 0)
    const3 = lambda b, c: (0, 0, 0)
    const2 = lambda b, c: (0, 0)
    per_b = lambda b, c: (b, 0, 0)
    pspecs = [pl.BlockSpec(p.shape, const2) for p in params]
    cast_specs = []
    for w, nblk in cast:
        assert nsteps % nblk == 0 and w.shape[0] % nblk == 0
        every = nsteps // nblk
        cast_specs.append(pl.BlockSpec((w.shape[0] // nblk, w.shape[1]),
                                       lambda b, c, every=every: ((b * nchunks + c) // every, 0)))
    kern = functools.partial(_mixer_seq_kernel, valid=valid, pos_base=pos_base, ncast=len(cast))
    return pl.pallas_call(
        kern,
        out_shape=(jax.ShapeDtypeStruct((nb, nchunks * CHUNK, D_MODEL), BF16),
                   jax.ShapeDtypeStruct((nb, CONV_W - 1, CONV_DIM), F32),
                   jax.ShapeDtypeStruct((nb, D_SSM, D_STATE), F32),
                   jax.ShapeDtypeStruct((nb, D_RET, RET_HEADDIM), F32))
        + tuple(jax.ShapeDtypeStruct(w.shape, BF16) for w, _ in cast),
        grid=(nb, nchunks),
        in_specs=[pl.BlockSpec((1, CHUNK, PROJ_MAIN), row),
                  pl.BlockSpec((1, CHUNK, LANES), row),
                  pl.BlockSpec((1, CONV_W - 1, CONV_DIM), const3),
                  pl.BlockSpec((1, D_SSM, D_STATE), const3),
                  pl.BlockSpec((1, D_RET, RET_HEADDIM), const3)] + pspecs + cast_specs,
        out_specs=(pl.BlockSpec((1, CHUNK, D_MODEL), lambda b, c: (b, c, 0)),
                   pl.BlockSpec((1, CONV_W - 1, CONV_DIM), per_b),
                   pl.BlockSpec((1, D_SSM, D_STATE), per_b),
                   pl.BlockSpec((1, D_RET, RET_HEADDIM), per_b)) + tuple(cast_specs),
        scratch_shapes=[pltpu.VMEM((CONV_PAD + CHUNK, CONV_DIM), F32),
                        pltpu.VMEM((RET_HEADS, CHUNK, CHUNK), F32),
                        pltpu.VMEM((2, CHUNK, RET_HEADDIM // 2), F32)],
        compiler_params=pltpu.CompilerParams(
            dimension_semantics=("arbitrary", "arbitrary"), vmem_limit_bytes=VMEM_LIMIT),
        name=name,
    )(proj, dtr, conv0, ssm0, ret0, *params, *[w for w, _ in cast])


def _mixer_step_kernel(proj_ref, dtr_ref, conv_ref, ssm_ref, ret_ref,
                       convw_ref, convb_ref, dtb_ref, alog_ref, dskip_ref, sg_ref, rg_ref, invf_ref,
                       mix_ref, convo_ref, ssmo_ref, reto_ref, cols_ref):
    R = STEP_ROWS
    xbc_raw = proj_ref[:, OFF_XBC:OFF_Q]
    acc = convb_ref[...] + xbc_raw * convw_ref[3:4, :]
    for i in range(CONV_W - 1):
        acc = acc + conv_ref[i] * convw_ref[i:i + 1, :]
    xbc = _silu(acc)
    convo_ref[0] = conv_ref[1]
    convo_ref[1] = conv_ref[2]
    convo_ref[2] = xbc_raw

    xs = xbc[:, :D_SSM]
    bmat = xbc[:, D_SSM:D_SSM + SSM_GROUPS * D_STATE]
    cmat = xbc[:, D_SSM + SSM_GROUPS * D_STATE:]
    dt = _softplus(dtr_ref[...] + dtb_ref[...])
    la = dt * (-jnp.exp(alog_ref[...]))
    expand = _head_expand()
    dt_x = _exact_right(dt, expand)
    decay_x = jnp.exp(_exact_right(la, expand))
    xdt = xs * dt_x

    ang = jnp.float32(PAST_LEN) * invf_ref[...]
    cos = jnp.cos(ang)
    sin = jnp.sin(ang)
    half = RET_HEADDIM // 2
    qs, ks = [], []
    for h in range(RET_HEADS):
        q1 = proj_ref[:, OFF_Q + h * RET_HEADDIM:OFF_Q + h * RET_HEADDIM + half]
        q2 = proj_ref[:, OFF_Q + h * RET_HEADDIM + half:OFF_Q + (h + 1) * RET_HEADDIM]
        k1 = proj_ref[:, OFF_K + h * RET_HEADDIM:OFF_K + h * RET_HEADDIM + half]
        k2 = proj_ref[:, OFF_K + h * RET_HEADDIM + half:OFF_K + (h + 1) * RET_HEADDIM]
        qs += [q1 * cos - q2 * sin, q1 * sin + q2 * cos]
        ks += [(k1 * cos - k2 * sin) * (RET_HEADDIM ** -0.5), (k1 * sin + k2 * cos) * (RET_HEADDIM ** -0.5)]
    qr = jnp.concatenate(qs, axis=1)
    kr = jnp.concatenate(ks, axis=1)
    vv = proj_ref[:, OFF_V:OFF_G]

    allq = jnp.concatenate([decay_x, xdt, kr, qr], axis=1)
    hi = allq.astype(BF16).astype(F32)
    r1 = allq - hi
    mid = r1.astype(BF16).astype(F32)
    lo = (r1 - mid).astype(BF16).astype(F32)
    stack = jnp.concatenate([hi, mid, lo, jnp.zeros_like(hi)], axis=0).astype(BF16)
    krow = lax.broadcasted_iota(jnp.int32, (4 * R, LANES), 0)
    row8 = lax.broadcasted_iota(jnp.int32, (R, 1), 0)
    lane = lax.broadcasted_iota(jnp.int32, (1, LANES), 1)

    y_cols = jnp.zeros((D_SSM, LANES), F32)
    y_ret = jnp.zeros((R, D_RET), F32)
    for r in range(R):
        sel = ((krow % R == r) & (krow < 3 * R)).astype(BF16)
        cols_ref[...] = _dot_tn(stack, sel)
        ycol = []
        for g in range(SSM_GROUPS):
            rows = slice(g * GROUP_DIM, (g + 1) * GROUP_DIM)
            h_old = ssm_ref[r, rows, :]
            h_new = (h_old * cols_ref[g * GROUP_DIM:(g + 1) * GROUP_DIM, :]
                     + cols_ref[D_SSM + g * GROUP_DIM:D_SSM + (g + 1) * GROUP_DIM, :]
                     * bmat[r:r + 1, g * D_STATE:(g + 1) * D_STATE])
            ssmo_ref[r, rows, :] = h_new
            ycol.append(jnp.sum(h_new * cmat[r:r + 1, g * D_STATE:(g + 1) * D_STATE], axis=1, keepdims=True))
        y_cols = jnp.where(lane == r, jnp.concatenate(ycol, axis=0), y_cols)
        yrow = []
        for h in range(RET_HEADS):
            rows = slice(h * RET_HEADDIM, (h + 1) * RET_HEADDIM)
            kcol = cols_ref[2 * D_SSM + h * RET_HEADDIM:2 * D_SSM + (h + 1) * RET_HEADDIM, :]
            qcol = cols_ref[3 * D_SSM + h * RET_HEADDIM:3 * D_SSM + (h + 1) * RET_HEADDIM, :]
            gamma = float(np.exp(np.float32(RET_LOG_GAMMA[h])))
            s_new = (gamma * ret_ref[r, rows, :]
                     + jnp.concatenate([kcol, kcol], axis=1) * vv[r:r + 1, h * RET_HEADDIM:(h + 1) * RET_HEADDIM])
            reto_ref[r, rows, :] = s_new
            yrow.append(jnp.sum(jnp.concatenate([qcol, qcol], axis=1) * s_new, axis=0, keepdims=True))
        y_ret = jnp.where(row8 == r, jnp.concatenate(yrow, axis=1), y_ret)

    y_ssd = y_cols.T[:R, :]
    y = (y_ssd + dskip_ref[...] * xs) * _silu(proj_ref[:, OFF_Z:OFF_XBC])
    y1 = jnp.concatenate([_rms(y[:, g * GROUP_DIM:(g + 1) * GROUP_DIM]) for g in range(SSM_GROUPS)],
                         axis=1) * sg_ref[...]
    y2 = jnp.concatenate([_rms(y_ret[:, h * RET_HEADDIM:(h + 1) * RET_HEADDIM]) for h in range(RET_HEADS)],
                         axis=1) * rg_ref[...] * _silu(proj_ref[:, OFF_G:PROJ_MAIN])
    mix_ref[:, :D_SSM] = y1
    mix_ref[:, D_SSM:] = y2


def _mixer_step(proj, dtr, conv_t, ssm, ret, params, *, nb):
    R = STEP_ROWS
    rows2 = lambda i: (i, 0)
    rows3 = lambda i: (i, 0, 0)
    mid3 = lambda i: (0, i, 0)
    const2 = lambda i: (0, 0)
    pspecs = [pl.BlockSpec(p.shape, const2) for p in params]
    return pl.pallas_call(
        _mixer_step_kernel,
        out_shape=(jax.ShapeDtypeStruct((nb, D_MODEL), F32),
                   jax.ShapeDtypeStruct((CONV_W - 1, nb, CONV_DIM), F32),
                   jax.ShapeDtypeStruct((nb, D_SSM, D_STATE), F32),
                   jax.ShapeDtypeStruct((nb, D_RET, RET_HEADDIM), F32)),
        grid=(nb // R,),
        in_specs=[pl.BlockSpec((R, PROJ_MAIN), rows2),
                  pl.BlockSpec((R, LANES), rows2),
                  pl.BlockSpec((CONV_W - 1, R, CONV_DIM), mid3),
                  pl.BlockSpec((R, D_SSM, D_STATE), rows3),
                  pl.BlockSpec((R, D_RET, RET_HEADDIM), rows3)] + pspecs,
        out_specs=(pl.BlockSpec((R, D_MODEL), rows2),
                   pl.BlockSpec((CONV_W - 1, R, CONV_DIM), mid3),
                   pl.BlockSpec((R, D_SSM, D_STATE), rows3),
                   pl.BlockSpec((R, D_RET, RET_HEADDIM), rows3)),
        scratch_shapes=[pltpu.VMEM((4 * D_SSM, LANES), F32)],
        compiler_params=pltpu.CompilerParams(
            dimension_semantics=("arbitrary",), vmem_limit_bytes=VMEM_LIMIT),
        name="mixer_step",
    )(proj, dtr, conv_t, ssm, ret, *params)


def _outproj_kernel(mix_ref, mixs_ref, w_ref, h_ref, hs_ref, g1_ref, g2_ref, hout_ref, f_ref, houts_ref, fs_ref):
    def rows(mix_r, h_r, hout_r, f_r):
        y = _dot(mix_r[...].astype(BF16), w_ref[...])
        h = h_r[...] + _rms(y) * g1_ref[...]
        hout_r[...] = h
        f_r[...] = (_rms(h) * g2_ref[...]).astype(BF16)

    rows(mix_ref, h_ref, hout_ref, f_ref)

    @pl.when(pl.program_id(0) == pl.num_programs(0) - 1)
    def _():
        rows(mixs_ref, hs_ref, houts_ref, fs_ref)


def _outproj(mix, mixs, w, h, hs, g1, g2, *, bm):
    m = mix.shape[0]
    ms = mixs.shape[0]
    row = lambda i: (i, 0)
    const = lambda i: (0, 0)
    return pl.pallas_call(
        _outproj_kernel,
        out_shape=(jax.ShapeDtypeStruct((m, D_MODEL), F32), jax.ShapeDtypeStruct((m, D_MODEL), BF16),
                   jax.ShapeDtypeStruct((ms, D_MODEL), F32), jax.ShapeDtypeStruct((ms, D_MODEL), BF16)),
        grid=(m // bm,),
        in_specs=[pl.BlockSpec((bm, D_MODEL), row),
                  pl.BlockSpec((ms, D_MODEL), const),
                  pl.BlockSpec((D_MODEL, D_MODEL), const),
                  pl.BlockSpec((bm, D_MODEL), row),
                  pl.BlockSpec((ms, D_MODEL), const),
                  pl.BlockSpec((1, D_MODEL), const),
                  pl.BlockSpec((1, D_MODEL), const)],
        out_specs=(pl.BlockSpec((bm, D_MODEL), row), pl.BlockSpec((bm, D_MODEL), row),
                   pl.BlockSpec((ms, D_MODEL), const), pl.BlockSpec((ms, D_MODEL), const)),
        compiler_params=pltpu.CompilerParams(
            dimension_semantics=("arbitrary",), vmem_limit_bytes=VMEM_LIMIT),
        name="outproj",
    )(mix, mixs, w, h, hs, g1, g2)


FFN_SPLIT = 2


def _ffn_kernel(f_ref, fs_ref, wg_ref, wu_ref, wd_ref, h_hbm, hs_ref, g_ref, o_ref, os_ref, hbuf_ref, hsem):
    i = pl.program_id(0)
    j = pl.program_id(1)
    on_last = i == pl.num_programs(0) - 1
    bm = o_ref.shape[0]

    def h_copy():
        return pltpu.make_async_copy(h_hbm.at[pl.ds(pl.multiple_of(i * bm, bm), bm), :], hbuf_ref, hsem)

    @pl.when(j == 0)
    def _():
        h_copy().start()
        o_ref[...] = jnp.zeros_like(o_ref)

        @pl.when(on_last)
        def _():
            os_ref[...] = jnp.zeros_like(os_ref)

    def ff_tile(with_side):
        fsub = wg_ref.shape[1] // FFN_SPLIT
        nsub = o_ref.shape[1] // FFN_SPLIT

        def through(f, o_r, s):
            ff = slice(s * fsub, (s + 1) * fsub)
            a = (_silu(_dot(f, wg_ref[:, ff])) * _dot(f, wu_ref[:, ff])).astype(BF16)
            for n in range(FFN_SPLIT):
                nn = slice(n * nsub, (n + 1) * nsub)
                o_r[:, nn] += _dot(a, wd_ref[ff, nn])

        for s in range(FFN_SPLIT):
            through(f_ref[...], o_ref, s)
            if with_side:
                through(fs_ref[...], os_ref, s)

    pl.when(on_last)(functools.partial(ff_tile, True))
    pl.when(jnp.logical_not(on_last))(functools.partial(ff_tile, False))

    @pl.when(j == pl.num_programs(1) - 1)
    def _():
        h_copy().wait()

        def body(t, carry):
            rows = pl.ds(pl.multiple_of(t * NORM_ROWS, NORM_ROWS), NORM_ROWS)
            o_ref[rows, :] = hbuf_ref[rows, :] + _rms(o_ref[rows, :]) * g_ref[...]
            return carry
        lax.fori_loop(0, bm // NORM_ROWS, body, 0)

        @pl.when(on_last)
        def _():
            os_ref[...] = hs_ref[...] + _rms(os_ref[...]) * g_ref[...]


def _ffn(f, fs, wg, wu, wd, h, hs, g, *, bm, bf):
    m = f.shape[0]
    ms = fs.shape[0]
    const = lambda i, j: (0, 0)
    return pl.pallas_call(
        _ffn_kernel,
        out_shape=(jax.ShapeDtypeStruct((m, D_MODEL), F32), jax.ShapeDtypeStruct((ms, D_MODEL), F32)),
        grid=(m // bm, D_FF // bf),
        in_specs=[pl.BlockSpec((bm, D_MODEL), lambda i, j: (i, 0)),
                  pl.BlockSpec((ms, D_MODEL), const),
                  pl.BlockSpec((D_MODEL, bf), lambda i, j: (0, j)),
                  pl.BlockSpec((D_MODEL, bf), lambda i, j: (0, j)),
                  pl.BlockSpec((bf, D_MODEL), lambda i, j: (j, 0)),
                  pl.BlockSpec(memory_space=pl.ANY),
                  pl.BlockSpec((ms, D_MODEL), const),
                  pl.BlockSpec((1, D_MODEL), const)],
        out_specs=(pl.BlockSpec((bm, D_MODEL), lambda i, j: (i, 0)),
                   pl.BlockSpec((ms, D_MODEL), const)),
        scratch_shapes=[pltpu.VMEM((bm, D_MODEL), F32), pltpu.SemaphoreType.DMA],
        compiler_params=pltpu.CompilerParams(
            dimension_semantics=("arbitrary", "arbitrary"), vmem_limit_bytes=VMEM_LIMIT),
        name="ffn",
    )(f, fs, wg, wu, wd, h, hs, g)


def kernel(x_prompt, x_sample, state_conv, state_ssm, state_ret, meta_tokens, pre_mix_g, post_mix_g,
           pre_ffn_g, post_ffn_g, w_in, conv_w, conv_b, dt_bias, a_log, d_skip, ssm_norm_g, ret_norm_g,
           w_out, w_gate, w_up, w_down):
    bp, seq = x_prompt.shape[:2]
    bs = x_sample.shape[0]
    assert w_in.shape[0] == 1 and x_sample.shape[1] == 1 and seq % CHUNK == 0 and bs == CHUNK

    w_in_t = jnp.swapaxes(w_in[0], 0, 1)
    pad16 = lambda v: jnp.pad(v, ((0, 0), (0, LANES - SSM_HEADS)))
    inv_freq = (ROPE_BASE ** (-jnp.arange(RET_HEADDIM // 2, dtype=F32) / (RET_HEADDIM // 2)))[None, :]
    params = (conv_w[0], conv_b, pad16(dt_bias), pad16(a_log),
              jnp.repeat(d_skip, SSM_HEADDIM, axis=1), ssm_norm_g, ret_norm_g, inv_freq)

    xp = x_prompt.reshape(bp * seq, D_MODEL)
    xs_rows = x_sample.reshape(bs, D_MODEL)
    x_small = jnp.concatenate(
        [xs_rows, meta_tokens.astype(F32), jnp.zeros((CHUNK - N_META, D_MODEL), F32)], axis=0)
    proj_p, dtr_p, proj_s, dtr_s = _inproj(xp, x_small, pre_mix_g, w_in_t, bm=2048, xr=1024, bn=512)

    zc = jnp.zeros((1, CONV_W - 1, CONV_DIM), F32)
    zs = jnp.zeros((1, D_SSM, D_STATE), F32)
    zr = jnp.zeros((1, D_RET, RET_HEADDIM), F32)
    _, m_conv, m_ssm, m_ret = _mixer_seq(
        proj_s.reshape(1, 2 * CHUNK, PROJ_MAIN), dtr_s.reshape(1, 2 * CHUNK, LANES), zc, zs, zr, params,
        nchunks=1, chunk_offset=1, valid=N_META, pos_base=0, name="mixer_meta")[:4]

    nsteps = bp * (seq // CHUNK)
    mix_p, p_conv, p_ssm, p_ret, w_out_b, w_gate_b, w_up_b, w_down_b = _mixer_seq(
        proj_p.reshape(bp, seq, PROJ_MAIN), dtr_p.reshape(bp, seq, LANES), m_conv, m_ssm, m_ret, params,
        nchunks=seq // CHUNK, chunk_offset=0, valid=CHUNK, pos_base=N_META, name="mixer_prompt",
        cast=((w_out[0], nsteps), (w_gate[0], nsteps), (w_up[0], nsteps), (w_down[0], nsteps // 2)))

    conv_t = jnp.transpose(state_conv[0], (1, 0, 2))
    mix_s, s_conv_t, s_ssm, s_ret = _mixer_step(
        proj_s, dtr_s, conv_t, state_ssm[0].reshape(bs, D_SSM, D_STATE),
        state_ret[0].reshape(bs, D_RET, RET_HEADDIM), params, nb=bs)

    h1_p, f_p, h1_s, f_s = _outproj(mix_p.reshape(bp * seq, D_MODEL), mix_s, w_out_b, xp, xs_rows,
                                    post_mix_g, pre_ffn_g, bm=512)
    y_p, y_s = _ffn(f_p, f_s, w_gate_b, w_up_b, w_down_b, h1_p, h1_s, post_ffn_g, bm=1024, bf=512)

    return (y_p.reshape(bp, seq, D_MODEL),
            y_s.reshape(bs, 1, D_MODEL),
            p_conv[None],
            p_ssm.reshape(1, bp, SSM_HEADS, SSM_HEADDIM, D_STATE),
            p_ret.reshape(1, bp, RET_HEADS, RET_HEADDIM, RET_HEADDIM),
            jnp.transpose(s_conv_t, (1, 0, 2))[None],
            s_ssm.reshape(1, bs, SSM_HEADS, SSM_HEADDIM, D_STATE),
            s_ret.reshape(1, bs, RET_HEADS, RET_HEADDIM, RET_HEADDIM))
```

```python
import functools

import numpy as np
import jax
import jax.numpy as jnp
from jax import lax
from jax.experimental import pallas as pl
from jax.experimental.pallas import tpu as pltpu

F32 = jnp.float32
BF16 = jnp.bfloat16

D_MODEL = 2048
N_META = 16
CHUNK = 128
D_SSM = 1024
D_RET = 1024
SSM_HEADDIM = 64
SSM_HEADS = 16
SSM_GROUPS = 2
GROUP_DIM = D_SSM // SSM_GROUPS
D_STATE = 128
CONV_W = 4
CONV_DIM = D_SSM + 2 * SSM_GROUPS * D_STATE
RET_HEADS = 4
RET_HEADDIM = 256
ROPE_BASE = 10000.0
D_FF = 5632
EPS = 1e-6
PAST_LEN = 16384
LOG2E = float(np.log2(np.e))

LANES = 128
SUBLANES = 8
STEP_ROWS = SUBLANES
CONV_PAD = SUBLANES

OFF_Z = 0
OFF_XBC = D_SSM
OFF_Q = OFF_XBC + CONV_DIM
OFF_K = OFF_Q + D_RET
OFF_V = OFF_K + D_RET
OFF_G = OFF_V + D_RET
PROJ_MAIN = OFF_G + D_RET

VMEM_LIMIT = 56 * 1024 * 1024

RET_LOG_GAMMA = [float(np.log1p(-np.float32(2.0) ** np.float32(-5.0 - h)).astype(np.float32))
                 for h in range(RET_HEADS)]


def _silu(x):
    return x / (1.0 + jnp.exp(-x))


def _softplus(x):
    return jnp.maximum(x, 0.0) + jnp.log1p(jnp.exp(-jnp.abs(x)))


def _rms(x):
    return x * lax.rsqrt(jnp.mean(x * x, axis=-1, keepdims=True) + EPS)


def _split3(x):
    hi = x.astype(BF16)
    r = x - hi.astype(F32)
    mid = r.astype(BF16)
    lo = (r - mid.astype(F32)).astype(BF16)
    return hi, mid, lo


def _dot(a, b):
    return jnp.dot(a, b, preferred_element_type=F32)


def _dot_nt(a, b):
    return lax.dot_general(a, b, (((1,), (1,)), ((), ())), preferred_element_type=F32)


def _dot_tn(a, b):
    return lax.dot_general(a, b, (((0,), (0,)), ((), ())), preferred_element_type=F32)


def _exact_right(x, sel):
    hi, mid, lo = _split3(x)
    return _dot(hi, sel) + _dot(mid, sel) + _dot(lo, sel)


def _select_right(x, sel):
    hi = x.astype(BF16)
    lo = (x - hi.astype(F32)).astype(BF16)
    return _dot(hi, sel) + _dot(lo, sel)


def _exact_left(sel, x):
    hi, mid, lo = _split3(x)
    return _dot(sel, hi) + _dot(sel, mid) + _dot(sel, lo)


def _exact_tn(x, sel):
    hi, mid, lo = _split3(x)
    return _dot_tn(hi, sel) + _dot_tn(mid, sel) + _dot_tn(lo, sel)


def _head_expand():
    r = lax.broadcasted_iota(jnp.int32, (LANES, D_SSM), 0)
    c = lax.broadcasted_iota(jnp.int32, (LANES, D_SSM), 1)
    return (c // SSM_HEADDIM == r).astype(BF16)


NORM_ROWS = 256
DT_ROW = D_SSM + CONV_DIM


def _inproj_kernel(x_ref, xs_ref, g_ref, wt_ref, wdt_ref, o_ref, odt_ref, os_ref, odts_ref, u_ref, us_ref,
                   *, npro, nsplit):
    i = pl.program_id(0)
    j = pl.program_id(1)
    on_last = i == pl.num_programs(0) - 1
    xr = x_ref.shape[0]

    @pl.when(j < npro)
    def _():
        wdt = wdt_ref[...].astype(BF16)
        lane = lax.broadcasted_iota(jnp.int32, (NORM_ROWS, LANES), 1)

        def norm_rows(src_ref, src, dst_ref, dt_ref, dst):
            u = (_rms(src_ref[src, :]) * g_ref[...]).astype(BF16)
            dst_ref[dst, :] = u
            dt_ref[dst, :] = jnp.where(lane < SSM_HEADS, _dot_nt(u, wdt), 0.0)

        def body(t, carry):
            src = pl.ds(pl.multiple_of(t * NORM_ROWS, NORM_ROWS), NORM_ROWS)
            dst = pl.ds(pl.multiple_of(j * xr + t * NORM_ROWS, NORM_ROWS), NORM_ROWS)
            norm_rows(x_ref, src, u_ref, odt_ref, dst)
            return carry
        lax.fori_loop(0, xr // NORM_ROWS, body, 0)

        @pl.when(on_last & (j == 0))
        def _():
            for t in range(xs_ref.shape[0] // NORM_ROWS):
                rows = pl.ds(t * NORM_ROWS, NORM_ROWS)
                norm_rows(xs_ref, rows, us_ref, odts_ref, rows)

    def column_tile(with_side):
        sub = wt_ref.shape[0] // nsplit
        for s in range(nsplit):
            cols = slice(s * sub, (s + 1) * sub)
            w = wt_ref[cols, :].astype(BF16)
            o_ref[:, cols] = _dot_nt(u_ref[...], w).astype(o_ref.dtype)
            if with_side:
                os_ref[:, cols] = _dot_nt(us_ref[...], w)

    pl.when((j >= npro) & on_last)(functools.partial(column_tile, True))
    pl.when((j >= npro) & jnp.logical_not(on_last))(functools.partial(column_tile, False))


def _inproj(x, xs, g, wt, *, bm, xr, bn):
    m = x.shape[0]
    ms = xs.shape[0]
    nm = m // bm
    npro = bm // xr
    assert DT_ROW % bn == 0 and ms % NORM_ROWS == 0

    def wrow(i, j):
        t = jnp.maximum(j - npro, 0)
        skip = jnp.where(t * bn >= DT_ROW, SSM_HEADS // SUBLANES, 0)
        return ((t * (bn // SUBLANES) + skip) * SUBLANES, 0)

    col = lambda j: jnp.maximum(j - npro, 0)
    const = lambda i, j: (0, 0)
    return pl.pallas_call(
        functools.partial(_inproj_kernel, npro=npro, nsplit=2),
        out_shape=(jax.ShapeDtypeStruct((m, PROJ_MAIN), BF16), jax.ShapeDtypeStruct((m, LANES), F32),
                   jax.ShapeDtypeStruct((ms, PROJ_MAIN), F32), jax.ShapeDtypeStruct((ms, LANES), F32)),
        grid=(nm, npro + PROJ_MAIN // bn),
        in_specs=[pl.BlockSpec((xr, D_MODEL), lambda i, j: (i * npro + jnp.minimum(j, npro - 1), 0)),
                  pl.BlockSpec((ms, D_MODEL), const),
                  pl.BlockSpec((1, D_MODEL), const),
                  pl.BlockSpec((pl.Element(bn), pl.Element(D_MODEL)), wrow),
                  pl.BlockSpec((pl.Element(LANES), pl.Element(D_MODEL)), lambda i, j: (DT_ROW, 0))],
        out_specs=(pl.BlockSpec((bm, bn), lambda i, j: (i, col(j))),
                   pl.BlockSpec((bm, LANES), lambda i, j: (i, 0)),
                   pl.BlockSpec((ms, bn), lambda i, j: (0, jnp.where(i == nm - 1, col(j), 0))),
                   pl.BlockSpec((ms, LANES), const)),
        scratch_shapes=[pltpu.VMEM((bm, D_MODEL), BF16), pltpu.VMEM((ms, D_MODEL), BF16)],
        compiler_params=pltpu.CompilerParams(
            dimension_semantics=("arbitrary", "arbitrary"), vmem_limit_bytes=VMEM_LIMIT),
        name="inproj",
    )(x, xs, g, wt, wt)


N_MIXER_IN = 13
N_MIXER_OUT = 4


def _mixer_seq_kernel(*refs, valid, pos_base, ncast):
    ins = refs[:N_MIXER_IN]
    cast_in = refs[N_MIXER_IN:N_MIXER_IN + ncast]
    outs = refs[N_MIXER_IN + ncast:N_MIXER_IN + ncast + N_MIXER_OUT]
    cast_out = refs[N_MIXER_IN + ncast + N_MIXER_OUT:N_MIXER_IN + 2 * ncast + N_MIXER_OUT]
    scratch = refs[N_MIXER_IN + 2 * ncast + N_MIXER_OUT:]
    _mixer_seq_body(*ins, *outs, *scratch, valid=valid, pos_base=pos_base)
    for src, dst in zip(cast_in, cast_out):
        dst[...] = src[...].astype(BF16)


def _mixer_seq_body(proj_ref, dtr_ref, conv0_ref, ssm0_ref, ret0_ref,
                    convw_ref, convb_ref, dtb_ref, alog_ref, dskip_ref, sg_ref, rg_ref, invf_ref,
                    mix_ref, convo_ref, ssmo_ref, reto_ref,
                    cbuf_ref, rdec_ref, trig_ref, *, valid, pos_base):
    C = CHUNK
    b = pl.program_id(0)
    c = pl.program_id(1)
    rowi = lax.broadcasted_iota(jnp.int32, (C, 1), 0)
    rowf = rowi.astype(F32)
    ri = lax.broadcasted_iota(jnp.int32, (C, C), 0)
    ci = lax.broadcasted_iota(jnp.int32, (C, C), 1)
    causal = ri >= ci

    @pl.when((b == 0) & (c == 0))
    def _():
        diff = (ri - ci).astype(F32)
        for h in range(RET_HEADS):
            rdec_ref[h] = jnp.where(causal, jnp.exp(jnp.maximum(diff, 0.0) * RET_LOG_GAMMA[h]), 0.0)
        row_ang = rowf * invf_ref[...]
        trig_ref[0] = jnp.cos(row_ang)
        trig_ref[1] = jnp.sin(row_ang)

    hist = CONV_PAD - (CONV_W - 1)

    @pl.when(c == 0)
    def _():
        cbuf_ref[hist:CONV_PAD, :] = conv0_ref[0]
        ssmo_ref[0] = ssm0_ref[0]
        reto_ref[0] = ret0_ref[0]

    xbc_raw = proj_ref[0, :, OFF_XBC:OFF_Q].astype(F32)
    cbuf_ref[CONV_PAD:CONV_PAD + C, :] = xbc_raw
    acc = convb_ref[...] + xbc_raw * convw_ref[CONV_W - 1:CONV_W, :]
    for i in range(CONV_W - 1):
        acc = acc + cbuf_ref[hist + i:hist + i + C, :] * convw_ref[i:i + 1, :]
    xbc = _silu(acc)
    new_prev = cbuf_ref[hist + valid:CONV_PAD + valid, :]
    cbuf_ref[hist:CONV_PAD, :] = new_prev
    convo_ref[0] = new_prev

    xs = xbc[:, :D_SSM]
    bmat = xbc[:, D_SSM:D_SSM + SSM_GROUPS * D_STATE].astype(BF16)
    cmat = xbc[:, D_SSM + SSM_GROUPS * D_STATE:].astype(BF16)

    dt = _softplus(dtr_ref[0] + dtb_ref[...])
    if valid < C:
        dt = jnp.where(rowi < valid, dt, 0.0)
    la = dt * (-jnp.exp(alog_ref[...]))
    tril = causal.astype(BF16)
    triu = (ri <= ci).astype(BF16)
    eye = (ri == ci).astype(BF16)
    lcum = _exact_left(tril, la)
    lcum_t = _exact_tn(la, triu)
    dt_t = _exact_tn(dt, eye)
    expand = _head_expand()
    carry_scale = _select_right(jnp.exp(lcum), expand)
    tail_scale = _select_right(jnp.exp(lcum[C - 1:C, :] - lcum) * dt, expand)
    lcum2 = lcum * LOG2E
    lcum2_t = lcum_t * LOG2E

    cbs = [_dot_nt(cmat[:, g * D_STATE:(g + 1) * D_STATE], bmat[:, g * D_STATE:(g + 1) * D_STATE])
           for g in range(SSM_GROUPS)]
    lane = lax.broadcasted_iota(jnp.int32, (C, LANES), 1)
    left = lane < SSM_HEADDIM
    y_intra = []
    for m in range(SSM_HEADS // 2):
        ws = []
        for h in (2 * m, 2 * m + 1):
            seg2 = lcum2[:, h:h + 1] - lcum2_t[h:h + 1, :]
            decay = jnp.exp2(jnp.where(causal, seg2, -jnp.inf))
            ws.append((cbs[h // (SSM_HEADS // SSM_GROUPS)] * decay * dt_t[h:h + 1, :]).astype(BF16))
        xm = xs[:, m * LANES:(m + 1) * LANES]
        xst = jnp.concatenate([jnp.where(left, xm, 0.0), jnp.where(left, 0.0, xm)], axis=0).astype(BF16)
        y_intra.append(_dot(jnp.concatenate(ws, axis=1), xst))
    y = jnp.concatenate(y_intra, axis=1)

    hstate = ssmo_ref[0]
    hb = hstate.astype(BF16)
    y_inter = jnp.concatenate(
        [_dot_nt(cmat[:, g * D_STATE:(g + 1) * D_STATE], hb[g * GROUP_DIM:(g + 1) * GROUP_DIM, :])
         for g in range(SSM_GROUPS)], axis=1)
    y = y + y_inter * carry_scale + dskip_ref[...] * xs

    xw = (xs * tail_scale).astype(BF16)
    upd = jnp.concatenate(
        [_dot_tn(xw[:, g * GROUP_DIM:(g + 1) * GROUP_DIM], bmat[:, g * D_STATE:(g + 1) * D_STATE])
         for g in range(SSM_GROUPS)], axis=0)
    la_tot = _exact_tn(la, jnp.ones((C, LANES), BF16))
    er = lax.broadcasted_iota(jnp.int32, (D_SSM, LANES), 0)
    ec = lax.broadcasted_iota(jnp.int32, (D_SSM, LANES), 1)
    expand_t = (er // SSM_HEADDIM == ec).astype(BF16)
    chunk_decay = jnp.exp(_exact_left(expand_t, la_tot))
    ssmo_ref[0] = chunk_decay * hstate + upd

    z = proj_ref[0, :, OFF_Z:OFF_XBC].astype(F32)
    y = y * _silu(z)
    y1 = jnp.concatenate([_rms(y[:, g * GROUP_DIM:(g + 1) * GROUP_DIM]) for g in range(SSM_GROUPS)],
                         axis=1) * sg_ref[...]

    ang0 = (pos_base + c * C).astype(F32) * invf_ref[...]
    cos0, sin0 = jnp.cos(ang0), jnp.sin(ang0)
    cos = cos0 * trig_ref[0] - sin0 * trig_ref[1]
    sin = sin0 * trig_ref[0] + cos0 * trig_ref[1]
    kscale = RET_HEADDIM ** -0.5
    cos_k, sin_k = cos * kscale, sin * kscale
    half = RET_HEADDIM // 2
    y2 = []
    for h in range(RET_HEADS):
        lg = RET_LOG_GAMMA[h]
        q1 = proj_ref[0, :, OFF_Q + h * RET_HEADDIM:OFF_Q + h * RET_HEADDIM + half].astype(F32)
        q2 = proj_ref[0, :, OFF_Q + h * RET_HEADDIM + half:OFF_Q + (h + 1) * RET_HEADDIM].astype(F32)
        k1 = proj_ref[0, :, OFF_K + h * RET_HEADDIM:OFF_K + h * RET_HEADDIM + half].astype(F32)
        k2 = proj_ref[0, :, OFF_K + h * RET_HEADDIM + half:OFF_K + (h + 1) * RET_HEADDIM].astype(F32)
        vh = proj_ref[0, :, OFF_V + h * RET_HEADDIM:OFF_V + (h + 1) * RET_HEADDIM].astype(BF16)
        qr = jnp.concatenate([q1 * cos - q2 * sin, q1 * sin + q2 * cos], axis=1)
        kr = jnp.concatenate([k1 * cos_k - k2 * sin_k, k1 * sin_k + k2 * cos_k], axis=1)
        if valid < C:
            kr = jnp.where(rowi < valid, kr, 0.0)
        qb = qr.astype(BF16)
        scores = _dot_nt(qb, kr.astype(BF16)) * rdec_ref[h]
        s_old = reto_ref[0, h * RET_HEADDIM:(h + 1) * RET_HEADDIM, :]
        yr = _dot(scores.astype(BF16), vh) + _dot(qb, s_old.astype(BF16)) * jnp.exp((rowf + 1.0) * lg)
        kw = (kr * jnp.exp((valid - 1.0 - rowf) * lg)).astype(BF16)
        reto_ref[0, h * RET_HEADDIM:(h + 1) * RET_HEADDIM, :] = (
            float(np.exp(np.float32(valid * lg))) * s_old + _dot_tn(kw, vh))
        y2.append(_rms(yr))
    gate = proj_ref[0, :, OFF_G:PROJ_MAIN].astype(F32)
    y2 = jnp.concatenate(y2, axis=1) * rg_ref[...] * _silu(gate)

    mix_ref[0, :, :D_SSM] = y1.astype(BF16)
    mix_ref[0, :, D_SSM:] = y2.astype(BF16)


def _mixer_seq(proj, dtr, conv0, ssm0, ret0, params, *, nchunks, chunk_offset, valid, pos_base, name,
               cast=()):
    nb = proj.shape[0]
    nsteps = nb * nchunks
    row = lambda b, c: (b, c + chunk_offset, 0)
    const3 = lambda b, c: (0, 0, 0)
    const2 = lambda b, c: (0, 0)
    per_b = lambda b, c: (b, 0, 0)
    pspecs = [pl.BlockSpec(p.shape, const2) for p in params]
    cast_specs = []
    for w, nblk in cast:
        assert nsteps % nblk == 0 and w.shape[0] % nblk == 0
        every = nsteps // nblk
        cast_specs.append(pl.BlockSpec((w.shape[0] // nblk, w.shape[1]),
                                       lambda b, c, every=every: ((b * nchunks + c) // every, 0)))
    kern = functools.partial(_mixer_seq_kernel, valid=valid, pos_base=pos_base, ncast=len(cast))
    return pl.pallas_call(
        kern,
        out_shape=(jax.ShapeDtypeStruct((nb, nchunks * CHUNK, D_MODEL), BF16),
                   jax.ShapeDtypeStruct((nb, CONV_W - 1, CONV_DIM), F32),
                   jax.ShapeDtypeStruct((nb, D_SSM, D_STATE), F32),
                   jax.ShapeDtypeStruct((nb, D_RET, RET_HEADDIM), F32))
        + tuple(jax.ShapeDtypeStruct(w.shape, BF16) for w, _ in cast),
        grid=(nb, nchunks),
        in_specs=[pl.BlockSpec((1, CHUNK, PROJ_MAIN), row),
                  pl.BlockSpec((1, CHUNK, LANES), row),
                  pl.BlockSpec((1, CONV_W - 1, CONV_DIM), const3),
                  pl.BlockSpec((1, D_SSM, D_STATE), const3),
                  pl.BlockSpec((1, D_RET, RET_HEADDIM), const3)] + pspecs + cast_specs,
        out_specs=(pl.BlockSpec((1, CHUNK, D_MODEL), lambda b, c: (b, c, 0)),
                   pl.BlockSpec((1, CONV_W - 1, CONV_DIM), per_b),
                   pl.BlockSpec((1, D_SSM, D_STATE), per_b),
                   pl.BlockSpec((1, D_RET, RET_HEADDIM), per_b)) + tuple(cast_specs),
        scratch_shapes=[pltpu.VMEM((CONV_PAD + CHUNK, CONV_DIM), F32),
                        pltpu.VMEM((RET_HEADS, CHUNK, CHUNK), F32),
                        pltpu.VMEM((2, CHUNK, RET_HEADDIM // 2), F32)],
        compiler_params=pltpu.CompilerParams(
            dimension_semantics=("arbitrary", "arbitrary"), vmem_limit_bytes=VMEM_LIMIT),
        name=name,
    )(proj, dtr, conv0, ssm0, ret0, *params, *[w for w, _ in cast])


def _mixer_step_kernel(proj_ref, dtr_ref, conv_ref, ssm_ref, ret_ref,
                       convw_ref, convb_ref, dtb_ref, alog_ref, dskip_ref, sg_ref, rg_ref, invf_ref,
                       mix_ref, convo_ref, ssmo_ref, reto_ref, cols_ref):
    R = STEP_ROWS
    xbc_raw = proj_ref[:, OFF_XBC:OFF_Q]
    acc = convb_ref[...] + xbc_raw * convw_ref[3:4, :]
    for i in range(CONV_W - 1):
        acc = acc + conv_ref[i] * convw_ref[i:i + 1, :]
    xbc = _silu(acc)
    convo_ref[0] = conv_ref[1]
    convo_ref[1] = conv_ref[2]
    convo_ref[2] = xbc_raw

    xs = xbc[:, :D_SSM]
    bmat = xbc[:, D_SSM:D_SSM + SSM_GROUPS * D_STATE]
    cmat = xbc[:, D_SSM + SSM_GROUPS * D_STATE:]
    dt = _softplus(dtr_ref[...] + dtb_ref[...])
    la = dt * (-jnp.exp(alog_ref[...]))
    expand = _head_expand()
    dt_x = _exact_right(dt, expand)
    decay_x = jnp.exp(_exact_right(la, expand))
    xdt = xs * dt_x

    ang = jnp.float32(PAST_LEN) * invf_ref[...]
    cos = jnp.cos(ang)
    sin = jnp.sin(ang)
    half = RET_HEADDIM // 2
    qs, ks = [], []
    for h in range(RET_HEADS):
        q1 = proj_ref[:, OFF_Q + h * RET_HEADDIM:OFF_Q + h * RET_HEADDIM + half]
        q2 = proj_ref[:, OFF_Q + h * RET_HEADDIM + half:OFF_Q + (h + 1) * RET_HEADDIM]
        k1 = proj_ref[:, OFF_K + h * RET_HEADDIM:OFF_K + h * RET_HEADDIM + half]
        k2 = proj_ref[:, OFF_K + h * RET_HEADDIM + half:OFF_K + (h + 1) * RET_HEADDIM]
        qs += [q1 * cos - q2 * sin, q1 * sin + q2 * cos]
        ks += [(k1 * cos - k2 * sin) * (RET_HEADDIM ** -0.5), (k1 * sin + k2 * cos) * (RET_HEADDIM ** -0.5)]
    qr = jnp.concatenate(qs, axis=1)
    kr = jnp.concatenate(ks, axis=1)
    vv = proj_ref[:, OFF_V:OFF_G]

    allq = jnp.concatenate([decay_x, xdt, kr, qr], axis=1)
    hi = allq.astype(BF16).astype(F32)
    r1 = allq - hi
    mid = r1.astype(BF16).astype(F32)
    lo = (r1 - mid).astype(BF16).astype(F32)
    stack = jnp.concatenate([hi, mid, lo, jnp.zeros_like(hi)], axis=0).astype(BF16)
    krow = lax.broadcasted_iota(jnp.int32, (4 * R, LANES), 0)
    row8 = lax.broadcasted_iota(jnp.int32, (R, 1), 0)
    lane = lax.broadcasted_iota(jnp.int32, (1, LANES), 1)

    y_cols = jnp.zeros((D_SSM, LANES), F32)
    y_ret = jnp.zeros((R, D_RET), F32)
    for r in range(R):
        sel = ((krow % R == r) & (krow < 3 * R)).astype(BF16)
        cols_ref[...] = _dot_tn(stack, sel)
        ycol = []
        for g in range(SSM_GROUPS):
            rows = slice(g * GROUP_DIM, (g + 1) * GROUP_DIM)
            h_old = ssm_ref[r, rows, :]
            h_new = (h_old * cols_ref[g * GROUP_DIM:(g + 1) * GROUP_DIM, :]
                     + cols_ref[D_SSM + g * GROUP_DIM:D_SSM + (g + 1) * GROUP_DIM, :]
                     * bmat[r:r + 1, g * D_STATE:(g + 1) * D_STATE])
            ssmo_ref[r, rows, :] = h_new
            ycol.append(jnp.sum(h_new * cmat[r:r + 1, g * D_STATE:(g + 1) * D_STATE], axis=1, keepdims=True))
        y_cols = jnp.where(lane == r, jnp.concatenate(ycol, axis=0), y_cols)
        yrow = []
        for h in range(RET_HEADS):
            rows = slice(h * RET_HEADDIM, (h + 1) * RET_HEADDIM)
            kcol = cols_ref[2 * D_SSM + h * RET_HEADDIM:2 * D_SSM + (h + 1) * RET_HEADDIM, :]
            qcol = cols_ref[3 * D_SSM + h * RET_HEADDIM:3 * D_SSM + (h + 1) * RET_HEADDIM, :]
            gamma = float(np.exp(np.float32(RET_LOG_GAMMA[h])))
            s_new = (gamma * ret_ref[r, rows, :]
                     + jnp.concatenate([kcol, kcol], axis=1) * vv[r:r + 1, h * RET_HEADDIM:(h + 1) * RET_HEADDIM])
            reto_ref[r, rows, :] = s_new
            yrow.append(jnp.sum(jnp.concatenate([qcol, qcol], axis=1) * s_new, axis=0, keepdims=True))
        y_ret = jnp.where(row8 == r, jnp.concatenate(yrow, axis=1), y_ret)

    y_ssd = y_cols.T[:R, :]
    y = (y_ssd + dskip_ref[...] * xs) * _silu(proj_ref[:, OFF_Z:OFF_XBC])
    y1 = jnp.concatenate([_rms(y[:, g * GROUP_DIM:(g + 1) * GROUP_DIM]) for g in range(SSM_GROUPS)],
                         axis=1) * sg_ref[...]
    y2 = jnp.concatenate([_rms(y_ret[:, h * RET_HEADDIM:(h + 1) * RET_HEADDIM]) for h in range(RET_HEADS)],
                         axis=1) * rg_ref[...] * _silu(proj_ref[:, OFF_G:PROJ_MAIN])
    mix_ref[:, :D_SSM] = y1
    mix_ref[:, D_SSM:] = y2


def _mixer_step(proj, dtr, conv_t, ssm, ret, params, *, nb):
    R = STEP_ROWS
    rows2 = lambda i: (i, 0)
    rows3 = lambda i: (i, 0, 0)
    mid3 = lambda i: (0, i, 0)
    const2 = lambda i: (0, 0)
    pspecs = [pl.BlockSpec(p.shape, const2) for p in params]
    return pl.pallas_call(
        _mixer_step_kernel,
        out_shape=(jax.ShapeDtypeStruct((nb, D_MODEL), F32),
                   jax.ShapeDtypeStruct((CONV_W - 1, nb, CONV_DIM), F32),
                   jax.ShapeDtypeStruct((nb, D_SSM, D_STATE), F32),
                   jax.ShapeDtypeStruct((nb, D_RET, RET_HEADDIM), F32)),
        grid=(nb // R,),
        in_specs=[pl.BlockSpec((R, PROJ_MAIN), rows2),
                  pl.BlockSpec((R, LANES), rows2),
                  pl.BlockSpec((CONV_W - 1, R, CONV_DIM), mid3),
                  pl.BlockSpec((R, D_SSM, D_STATE), rows3),
                  pl.BlockSpec((R, D_RET, RET_HEADDIM), rows3)] + pspecs,
        out_specs=(pl.BlockSpec((R, D_MODEL), rows2),
                   pl.BlockSpec((CONV_W - 1, R, CONV_DIM), mid3),
                   pl.BlockSpec((R, D_SSM, D_STATE), rows3),
                   pl.BlockSpec((R, D_RET, RET_HEADDIM), rows3)),
        scratch_shapes=[pltpu.VMEM((4 * D_SSM, LANES), F32)],
        compiler_params=pltpu.CompilerParams(
            dimension_semantics=("arbitrary",), vmem_limit_bytes=VMEM_LIMIT),
        name="mixer_step",
    )(proj, dtr, conv_t, ssm, ret, *params)


def _outproj_kernel(mix_ref, mixs_ref, w_ref, h_ref, hs_ref, g1_ref, g2_ref, hout_ref, f_ref, houts_ref, fs_ref):
    def rows(mix_r, h_r, hout_r, f_r):
        y = _dot(mix_r[...].astype(BF16), w_ref[...])
        h = h_r[...] + _rms(y) * g1_ref[...]
        hout_r[...] = h
        f_r[...] = (_rms(h) * g2_ref[...]).astype(BF16)

    rows(mix_ref, h_ref, hout_ref, f_ref)

    @pl.when(pl.program_id(0) == pl.num_programs(0) - 1)
    def _():
        rows(mixs_ref, hs_ref, houts_ref, fs_ref)


def _outproj(mix, mixs, w, h, hs, g1, g2, *, bm):
    m = mix.shape[0]
    ms = mixs.shape[0]
    row = lambda i: (i, 0)
    const = lambda i: (0, 0)
    return pl.pallas_call(
        _outproj_kernel,
        out_shape=(jax.ShapeDtypeStruct((m, D_MODEL), F32), jax.ShapeDtypeStruct((m, D_MODEL), BF16),
                   jax.ShapeDtypeStruct((ms, D_MODEL), F32), jax.ShapeDtypeStruct((ms, D_MODEL), BF16)),
        grid=(m // bm,),
        in_specs=[pl.BlockSpec((bm, D_MODEL), row),
                  pl.BlockSpec((ms, D_MODEL), const),
                  pl.BlockSpec((D_MODEL, D_MODEL), const),
                  pl.BlockSpec((bm, D_MODEL), row),
                  pl.BlockSpec((ms, D_MODEL), const),
                  pl.BlockSpec((1, D_MODEL), const),
                  pl.BlockSpec((1, D_MODEL), const)],
        out_specs=(pl.BlockSpec((bm, D_MODEL), row), pl.BlockSpec((bm, D_MODEL), row),
                   pl.BlockSpec((ms, D_MODEL), const), pl.BlockSpec((ms, D_MODEL), const)),
        compiler_params=pltpu.CompilerParams(
            dimension_semantics=("arbitrary",), vmem_limit_bytes=VMEM_LIMIT),
        name="outproj",
    )(mix, mixs, w, h, hs, g1, g2)


FFN_SPLIT = 2


def _ffn_kernel(f_ref, fs_ref, wg_ref, wu_ref, wd_ref, h_hbm, hs_ref, g_ref, o_ref, os_ref, hbuf_ref, hsem):
    i = pl.program_id(0)
    j = pl.program_id(1)
    on_last = i == pl.num_programs(0) - 1
    bm = o_ref.shape[0]

    def h_copy():
        return pltpu.make_async_copy(h_hbm.at[pl.ds(pl.multiple_of(i * bm, bm), bm), :], hbuf_ref, hsem)

    @pl.when(j == 0)
    def _():
        h_copy().start()
        o_ref[...] = jnp.zeros_like(o_ref)

        @pl.when(on_last)
        def _():
            os_ref[...] = jnp.zeros_like(os_ref)

    def ff_tile(with_side):
        fsub = wg_ref.shape[1] // FFN_SPLIT
        nsub = o_ref.shape[1] // FFN_SPLIT

        def through(f, o_r, s):
            ff = slice(s * fsub, (s + 1) * fsub)
            a = (_silu(_dot(f, wg_ref[:, ff])) * _dot(f, wu_ref[:, ff])).astype(BF16)
            for n in range(FFN_SPLIT):
                nn = slice(n * nsub, (n + 1) * nsub)
                o_r[:, nn] += _dot(a, wd_ref[ff, nn])

        for s in range(FFN_SPLIT):
            through(f_ref[...], o_ref, s)
            if with_side:
                through(fs_ref[...], os_ref, s)

    pl.when(on_last)(functools.partial(ff_tile, True))
    pl.when(jnp.logical_not(on_last))(functools.partial(ff_tile, False))

    @pl.when(j == pl.num_programs(1) - 1)
    def _():
        h_copy().wait()

        def body(t, carry):
            rows = pl.ds(pl.multiple_of(t * NORM_ROWS, NORM_ROWS), NORM_ROWS)
            o_ref[rows, :] = hbuf_ref[rows, :] + _rms(o_ref[rows, :]) * g_ref[...]
            return carry
        lax.fori_loop(0, bm // NORM_ROWS, body, 0)

        @pl.when(on_last)
        def _():
            os_ref[...] = hs_ref[...] + _rms(os_ref[...]) * g_ref[...]


def _ffn(f, fs, wg, wu, wd, h, hs, g, *, bm, bf):
    m = f.shape[0]
    ms = fs.shape[0]
    const = lambda i, j: (0, 0)
    return pl.pallas_call(
        _ffn_kernel,
        out_shape=(jax.ShapeDtypeStruct((m, D_MODEL), F32), jax.ShapeDtypeStruct((ms, D_MODEL), F32)),
        grid=(m // bm, D_FF // bf),
        in_specs=[pl.BlockSpec((bm, D_MODEL), lambda i, j: (i, 0)),
                  pl.BlockSpec((ms, D_MODEL), const),
                  pl.BlockSpec((D_MODEL, bf), lambda i, j: (0, j)),
                  pl.BlockSpec((D_MODEL, bf), lambda i, j: (0, j)),
                  pl.BlockSpec((bf, D_MODEL), lambda i, j: (j, 0)),
                  pl.BlockSpec(memory_space=pl.ANY),
                  pl.BlockSpec((ms, D_MODEL), const),
                  pl.BlockSpec((1, D_MODEL), const)],
        out_specs=(pl.BlockSpec((bm, D_MODEL), lambda i, j: (i, 0)),
                   pl.BlockSpec((ms, D_MODEL), const)),
        scratch_shapes=[pltpu.VMEM((bm, D_MODEL), F32), pltpu.SemaphoreType.DMA],
        compiler_params=pltpu.CompilerParams(
            dimension_semantics=("arbitrary", "arbitrary"), vmem_limit_bytes=VMEM_LIMIT),
        name="ffn",
    )(f, fs, wg, wu, wd, h, hs, g)


def kernel(x_prompt, x_sample, state_conv, state_ssm, state_ret, meta_tokens, pre_mix_g, post_mix_g,
           pre_ffn_g, post_ffn_g, w_in, conv_w, conv_b, dt_bias, a_log, d_skip, ssm_norm_g, ret_norm_g,
           w_out, w_gate, w_up, w_down):
    bp, seq = x_prompt.shape[:2]
    bs = x_sample.shape[0]
    assert w_in.shape[0] == 1 and x_sample.shape[1] == 1 and seq % CHUNK == 0 and bs == CHUNK

    w_in_t = jnp.swapaxes(w_in[0], 0, 1)
    pad16 = lambda v: jnp.pad(v, ((0, 0), (0, LANES - SSM_HEADS)))
    inv_freq = (ROPE_BASE ** (-jnp.arange(RET_HEADDIM // 2, dtype=F32) / (RET_HEADDIM // 2)))[None, :]
    params = (conv_w[0], conv_b, pad16(dt_bias), pad16(a_log),
              jnp.repeat(d_skip, SSM_HEADDIM, axis=1), ssm_norm_g, ret_norm_g, inv_freq)

    xp = x_prompt.reshape(bp * seq, D_MODEL)
    xs_rows = x_sample.reshape(bs, D_MODEL)
    x_small = jnp.concatenate(
        [xs_rows, meta_tokens.astype(F32), jnp.zeros((CHUNK - N_META, D_MODEL), F32)], axis=0)
    proj_p, dtr_p, proj_s, dtr_s = _inproj(xp, x_small, pre_mix_g, w_in_t, bm=2048, xr=1024, bn=512)

    zc = jnp.zeros((1, CONV_W - 1, CONV_DIM), F32)
    zs = jnp.zeros((1, D_SSM, D_STATE), F32)
    zr = jnp.zeros((1, D_RET, RET_HEADDIM), F32)
    _, m_conv, m_ssm, m_ret = _mixer_seq(
        proj_s.reshape(1, 2 * CHUNK, PROJ_MAIN), dtr_s.reshape(1, 2 * CHUNK, LANES), zc, zs, zr, params,
        nchunks=1, chunk_offset=1, valid=N_META, pos_base=0, name="mixer_meta")[:4]

    nsteps = bp * (seq // CHUNK)
    mix_p, p_conv, p_ssm, p_ret, w_out_b, w_gate_b, w_up_b, w_down_b = _mixer_seq(
        proj_p.reshape(bp, seq, PROJ_MAIN), dtr_p.reshape(bp, seq, LANES), m_conv, m_ssm, m_ret, params,
        nchunks=seq // CHUNK, chunk_offset=0, valid=CHUNK, pos_base=N_META, name="mixer_prompt",
        cast=((w_out[0], nsteps), (w_gate[0], nsteps), (w_up[0], nsteps), (w_down[0], nsteps // 2)))

    conv_t = jnp.transpose(state_conv[0], (1, 0, 2))
    mix_s, s_conv_t, s_ssm, s_ret = _mixer_step(
        proj_s, dtr_s, conv_t, state_ssm[0].reshape(bs, D_SSM, D_STATE),
        state_ret[0].reshape(bs, D_RET, RET_HEADDIM), params, nb=bs)

    h1_p, f_p, h1_s, f_s = _outproj(mix_p.reshape(bp * seq, D_MODEL), mix_s, w_out_b, xp, xs_rows,
                                    post_mix_g, pre_ffn_g, bm=512)
    y_p, y_s = _ffn(f_p, f_s, w_gate_b, w_up_b, w_down_b, h1_p, h1_s, post_ffn_g, bm=1024, bf=512)

    return (y_p.reshape(bp, seq, D_MODEL),
            y_s.reshape(bs, 1, D_MODEL),
            p_conv[None],
            p_ssm.reshape(1, bp, SSM_HEADS, SSM_HEADDIM, D_STATE),
            p_ret.reshape(1, bp, RET_HEADS, RET_HEADDIM, RET_HEADDIM),
            jnp.transpose(s_conv_t, (1, 0, 2))[None],
            s_ssm.reshape(1, bs, SSM_HEADS, SSM_HEADDIM, D_STATE),
            s_ret.reshape(1, bs, RET_HEADS, RET_HEADDIM, RET_HEADDIM))
```

```python
import functools

import numpy as np
import jax
import jax.numpy as jnp
from jax import lax
from jax.experimental import pallas as pl
from jax.experimental.pallas import tpu as pltpu

F32 = jnp.float32
BF16 = jnp.bfloat16

D_MODEL = 2048
N_META = 16
CHUNK = 128
D_SSM = 1024
D_RET = 1024
SSM_HEADDIM = 64
SSM_HEADS = 16
SSM_GROUPS = 2
GROUP_DIM = D_SSM // SSM_GROUPS
D_STATE = 128
CONV_W = 4
CONV_DIM = D_SSM + 2 * SSM_GROUPS * D_STATE
RET_HEADS = 4
RET_HEADDIM = 256
ROPE_BASE = 10000.0
D_FF = 5632
EPS = 1e-6
PAST_LEN = 16384
LOG2E = float(np.log2(np.e))

LANES = 128
SUBLANES = 8
STEP_ROWS = SUBLANES
CONV_PAD = SUBLANES

OFF_Z = 0
OFF_XBC = D_SSM
OFF_Q = OFF_XBC + CONV_DIM
OFF_K = OFF_Q + D_RET
OFF_V = OFF_K + D_RET
OFF_G = OFF_V + D_RET
PROJ_MAIN = OFF_G + D_RET

VMEM_LIMIT = 56 * 1024 * 1024
VMEM_LIMIT_FFN = 60 * 1024 * 1024

RET_LOG_GAMMA = [float(np.log1p(-np.float32(2.0) ** np.float32(-5.0 - h)).astype(np.float32))
                 for h in range(RET_HEADS)]


def _silu(x):
    return x / (1.0 + jnp.exp(-x))


def _softplus(x):
    return jnp.maximum(x, 0.0) + jnp.log1p(jnp.exp(-jnp.abs(x)))


def _rms(x):
    return x * lax.rsqrt(jnp.mean(x * x, axis=-1, keepdims=True) + EPS)


def _split3(x):
    hi = x.astype(BF16)
    r = x - hi.astype(F32)
    mid = r.astype(BF16)
    lo = (r - mid.astype(F32)).astype(BF16)
    return hi, mid, lo


def _dot(a, b):
    return jnp.dot(a, b, preferred_element_type=F32)


def _dot_nt(a, b):
    return lax.dot_general(a, b, (((1,), (1,)), ((), ())), preferred_element_type=F32)


def _dot_tn(a, b):
    return lax.dot_general(a, b, (((0,), (0,)), ((), ())), preferred_element_type=F32)


def _exact_right(x, sel):
    hi, mid, lo = _split3(x)
    return _dot(hi, sel) + _dot(mid, sel) + _dot(lo, sel)


def _select_right(x, sel):
    hi = x.astype(BF16)
    lo = (x - hi.astype(F32)).astype(BF16)
    return _dot(hi, sel) + _dot(lo, sel)


def _exact_left(sel, x):
    hi, mid, lo = _split3(x)
    return _dot(sel, hi) + _dot(sel, mid) + _dot(sel, lo)


def _exact_tn(x, sel):
    hi, mid, lo = _split3(x)
    return _dot_tn(hi, sel) + _dot_tn(mid, sel) + _dot_tn(lo, sel)


def _head_expand():
    r = lax.broadcasted_iota(jnp.int32, (LANES, D_SSM), 0)
    c = lax.broadcasted_iota(jnp.int32, (LANES, D_SSM), 1)
    return (c // SSM_HEADDIM == r).astype(BF16)


NORM_ROWS = 256
DT_ROW = D_SSM + CONV_DIM


def _inproj_kernel(x_ref, xs_ref, g_ref, wt_ref, wdt_ref, o_ref, odt_ref, os_ref, odts_ref, u_ref, us_ref,
                   *, npro, nsplit):
    i = pl.program_id(0)
    j = pl.program_id(1)
    on_last = i == pl.num_programs(0) - 1
    xr = x_ref.shape[0]

    @pl.when(j < npro)
    def _():
        wdt = wdt_ref[...].astype(BF16)
        lane = lax.broadcasted_iota(jnp.int32, (NORM_ROWS, LANES), 1)

        def norm_rows(src_ref, src, dst_ref, dt_ref, dst):
            u = (_rms(src_ref[src, :]) * g_ref[...]).astype(BF16)
            dst_ref[dst, :] = u
            dt_ref[dst, :] = jnp.where(lane < SSM_HEADS, _dot_nt(u, wdt), 0.0)

        def body(t, carry):
            src = pl.ds(pl.multiple_of(t * NORM_ROWS, NORM_ROWS), NORM_ROWS)
            dst = pl.ds(pl.multiple_of(j * xr + t * NORM_ROWS, NORM_ROWS), NORM_ROWS)
            norm_rows(x_ref, src, u_ref, odt_ref, dst)
            return carry
        lax.fori_loop(0, xr // NORM_ROWS, body, 0)

        @pl.when(on_last & (j == 0))
        def _():
            for t in range(xs_ref.shape[0] // NORM_ROWS):
                rows = pl.ds(t * NORM_ROWS, NORM_ROWS)
                norm_rows(xs_ref, rows, us_ref, odts_ref, rows)

    def column_tile(with_side):
        sub = wt_ref.shape[0] // nsplit
        for s in range(nsplit):
            cols = slice(s * sub, (s + 1) * sub)
            w = wt_ref[cols, :].astype(BF16)
            o_ref[:, cols] = _dot_nt(u_ref[...], w).astype(o_ref.dtype)
            if with_side:
                os_ref[:, cols] = _dot_nt(us_ref[...], w)

    pl.when((j >= npro) & on_last)(functools.partial(column_tile, True))
    pl.when((j >= npro) & jnp.logical_not(on_last))(functools.partial(column_tile, False))


def _inproj(x, xs, g, wt, *, bm, xr, bn):
    m = x.shape[0]
    ms = xs.shape[0]
    nm = m // bm
    npro = bm // xr
    assert DT_ROW % bn == 0 and ms % NORM_ROWS == 0

    def wrow(i, j):
        t = jnp.maximum(j - npro, 0)
        skip = jnp.where(t * bn >= DT_ROW, SSM_HEADS // SUBLANES, 0)
        return ((t * (bn // SUBLANES) + skip) * SUBLANES, 0)

    col = lambda j: jnp.maximum(j - npro, 0)
    const = lambda i, j: (0, 0)
    return pl.pallas_call(
        functools.partial(_inproj_kernel, npro=npro, nsplit=2),
        out_shape=(jax.ShapeDtypeStruct((m, PROJ_MAIN), BF16), jax.ShapeDtypeStruct((m, LANES), F32),
                   jax.ShapeDtypeStruct((ms, PROJ_MAIN), F32), jax.ShapeDtypeStruct((ms, LANES), F32)),
        grid=(nm, npro + PROJ_MAIN // bn),
        in_specs=[pl.BlockSpec((xr, D_MODEL), lambda i, j: (i * npro + jnp.minimum(j, npro - 1), 0)),
                  pl.BlockSpec((ms, D_MODEL), const),
                  pl.BlockSpec((1, D_MODEL), const),
                  pl.BlockSpec((pl.Element(bn), pl.Element(D_MODEL)), wrow),
                  pl.BlockSpec((pl.Element(LANES), pl.Element(D_MODEL)), lambda i, j: (DT_ROW, 0))],
        out_specs=(pl.BlockSpec((bm, bn), lambda i, j: (i, col(j))),
                   pl.BlockSpec((bm, LANES), lambda i, j: (i, 0)),
                   pl.BlockSpec((ms, bn), lambda i, j: (0, jnp.where(i == nm - 1, col(j), 0))),
                   pl.BlockSpec((ms, LANES), const)),
        scratch_shapes=[pltpu.VMEM((bm, D_MODEL), BF16), pltpu.VMEM((ms, D_MODEL), BF16)],
        compiler_params=pltpu.CompilerParams(
            dimension_semantics=("arbitrary", "arbitrary"), vmem_limit_bytes=VMEM_LIMIT),
        name="inproj",
    )(x, xs, g, wt, wt)


N_MIXER_IN = 13
N_MIXER_OUT = 4


def _mixer_seq_kernel(*refs, valid, pos_base, ncast):
    ins = refs[:N_MIXER_IN]
    cast_in = refs[N_MIXER_IN:N_MIXER_IN + ncast]
    outs = refs[N_MIXER_IN + ncast:N_MIXER_IN + ncast + N_MIXER_OUT]
    cast_out = refs[N_MIXER_IN + ncast + N_MIXER_OUT:N_MIXER_IN + 2 * ncast + N_MIXER_OUT]
    scratch = refs[N_MIXER_IN + 2 * ncast + N_MIXER_OUT:]
    _mixer_seq_body(*ins, *outs, *scratch, valid=valid, pos_base=pos_base)
    for src, dst in zip(cast_in, cast_out):
        dst[...] = src[...].astype(BF16)


def _mixer_seq_body(proj_ref, dtr_ref, conv0_ref, ssm0_ref, ret0_ref,
                    convw_ref, convb_ref, dtb_ref, alog_ref, dskip_ref, sg_ref, rg_ref, invf_ref,
                    mix_ref, convo_ref, ssmo_ref, reto_ref,
                    cbuf_ref, rdec_ref, trig_ref, *, valid, pos_base):
    C = CHUNK
    b = pl.program_id(0)
    c = pl.program_id(1)
    rowi = lax.broadcasted_iota(jnp.int32, (C, 1), 0)
    rowf = rowi.astype(F32)
    ri = lax.broadcasted_iota(jnp.int32, (C, C), 0)
    ci = lax.broadcasted_iota(jnp.int32, (C, C), 1)
    causal = ri >= ci

    @pl.when((b == 0) & (c == 0))
    def _():
        diff = (ri - ci).astype(F32)
        for h in range(RET_HEADS):
            rdec_ref[h] = jnp.where(causal, jnp.exp(jnp.maximum(diff, 0.0) * RET_LOG_GAMMA[h]), 0.0)
        row_ang = rowf * invf_ref[...]
        trig_ref[0] = jnp.cos(row_ang)
        trig_ref[1] = jnp.sin(row_ang)

    hist = CONV_PAD - (CONV_W - 1)

    @pl.when(c == 0)
    def _():
        cbuf_ref[hist:CONV_PAD, :] = conv0_ref[0]
        ssmo_ref[0] = ssm0_ref[0]
        reto_ref[0] = ret0_ref[0]

    xbc_raw = proj_ref[0, :, OFF_XBC:OFF_Q].astype(F32)
    cbuf_ref[CONV_PAD:CONV_PAD + C, :] = xbc_raw
    acc = convb_ref[...] + xbc_raw * convw_ref[CONV_W - 1:CONV_W, :]
    for i in range(CONV_W - 1):
        acc = acc + cbuf_ref[hist + i:hist + i + C, :] * convw_ref[i:i + 1, :]
    xbc = _silu(acc)
    new_prev = cbuf_ref[hist + valid:CONV_PAD + valid, :]
    cbuf_ref[hist:CONV_PAD, :] = new_prev
    convo_ref[0] = new_prev

    xs = xbc[:, :D_SSM]
    bmat = xbc[:, D_SSM:D_SSM + SSM_GROUPS * D_STATE].astype(BF16)
    cmat = xbc[:, D_SSM + SSM_GROUPS * D_STATE:].astype(BF16)

    dt = _softplus(dtr_ref[0] + dtb_ref[...])
    if valid < C:
        dt = jnp.where(rowi < valid, dt, 0.0)
    la = dt * (-jnp.exp(alog_ref[...]))
    tril = causal.astype(BF16)
    triu = (ri <= ci).astype(BF16)
    eye = (ri == ci).astype(BF16)
    lcum = _exact_left(tril, la)
    lcum_t = _exact_tn(la, triu)
    dt_t = _exact_tn(dt, eye)
    expand = _head_expand()
    carry_scale = _select_right(jnp.exp(lcum), expand)
    tail_scale = _select_right(jnp.exp(lcum[C - 1:C, :] - lcum) * dt, expand)
    lcum2 = lcum * LOG2E
    lcum2_t = lcum_t * LOG2E

    cbs = [_dot_nt(cmat[:, g * D_STATE:(g + 1) * D_STATE], bmat[:, g * D_STATE:(g + 1) * D_STATE])
           for g in range(SSM_GROUPS)]
    lane = lax.broadcasted_iota(jnp.int32, (C, LANES), 1)
    left = lane < SSM_HEADDIM
    y_intra = []
    for m in range(SSM_HEADS // 2):
        ws = []
        for h in (2 * m, 2 * m + 1):
            seg2 = lcum2[:, h:h + 1] - lcum2_t[h:h + 1, :]
            decay = jnp.exp2(jnp.where(causal, seg2, -jnp.inf))
            ws.append((cbs[h // (SSM_HEADS // SSM_GROUPS)] * decay * dt_t[h:h + 1, :]).astype(BF16))
        xm = xs[:, m * LANES:(m + 1) * LANES]
        xst = jnp.concatenate([jnp.where(left, xm, 0.0), jnp.where(left, 0.0, xm)], axis=0).astype(BF16)
        y_intra.append(_dot(jnp.concatenate(ws, axis=1), xst))
    y = jnp.concatenate(y_intra, axis=1)

    hstate = ssmo_ref[0]
    hb = hstate.astype(BF16)
    y_inter = jnp.concatenate(
        [_dot_nt(cmat[:, g * D_STATE:(g + 1) * D_STATE], hb[g * GROUP_DIM:(g + 1) * GROUP_DIM, :])
         for g in range(SSM_GROUPS)], axis=1)
    y = y + y_inter * carry_scale + dskip_ref[...] * xs

    xw = (xs * tail_scale).astype(BF16)
    upd = jnp.concatenate(
        [_dot_tn(xw[:, g * GROUP_DIM:(g + 1) * GROUP_DIM], bmat[:, g * D_STATE:(g + 1) * D_STATE])
         for g in range(SSM_GROUPS)], axis=0)
    la_tot = _exact_tn(la, jnp.ones((C, LANES), BF16))
    er = lax.broadcasted_iota(jnp.int32, (D_SSM, LANES), 0)
    ec = lax.broadcasted_iota(jnp.int32, (D_SSM, LANES), 1)
    expand_t = (er // SSM_HEADDIM == ec).astype(BF16)
    chunk_decay = jnp.exp(_exact_left(expand_t, la_tot))
    ssmo_ref[0] = chunk_decay * hstate + upd

    z = proj_ref[0, :, OFF_Z:OFF_XBC].astype(F32)
    y = y * _silu(z)
    y1 = jnp.concatenate([_rms(y[:, g * GROUP_DIM:(g + 1) * GROUP_DIM]) for g in range(SSM_GROUPS)],
                         axis=1) * sg_ref[...]

    ang0 = (pos_base + c * C).astype(F32) * invf_ref[...]
    cos0, sin0 = jnp.cos(ang0), jnp.sin(ang0)
    cos = cos0 * trig_ref[0] - sin0 * trig_ref[1]
    sin = sin0 * trig_ref[0] + cos0 * trig_ref[1]
    kscale = RET_HEADDIM ** -0.5
    cos_k, sin_k = cos * kscale, sin * kscale
    half = RET_HEADDIM // 2
    y2 = []
    for h in range(RET_HEADS):
        lg = RET_LOG_GAMMA[h]
        q1 = proj_ref[0, :, OFF_Q + h * RET_HEADDIM:OFF_Q + h * RET_HEADDIM + half].astype(F32)
        q2 = proj_ref[0, :, OFF_Q + h * RET_HEADDIM + half:OFF_Q + (h + 1) * RET_HEADDIM].astype(F32)
        k1 = proj_ref[0, :, OFF_K + h * RET_HEADDIM:OFF_K + h * RET_HEADDIM + half].astype(F32)
        k2 = proj_ref[0, :, OFF_K + h * RET_HEADDIM + half:OFF_K + (h + 1) * RET_HEADDIM].astype(F32)
        vh = proj_ref[0, :, OFF_V + h * RET_HEADDIM:OFF_V + (h + 1) * RET_HEADDIM].astype(BF16)
        qr = jnp.concatenate([q1 * cos - q2 * sin, q1 * sin + q2 * cos], axis=1)
        kr = jnp.concatenate([k1 * cos_k - k2 * sin_k, k1 * sin_k + k2 * cos_k], axis=1)
        if valid < C:
            kr = jnp.where(rowi < valid, kr, 0.0)
        qb = qr.astype(BF16)
        scores = _dot_nt(qb, kr.astype(BF16)) * rdec_ref[h]
        s_old = reto_ref[0, h * RET_HEADDIM:(h + 1) * RET_HEADDIM, :]
        yr = _dot(scores.astype(BF16), vh) + _dot(qb, s_old.astype(BF16)) * jnp.exp((rowf + 1.0) * lg)
        kw = (kr * jnp.exp((valid - 1.0 - rowf) * lg)).astype(BF16)
        reto_ref[0, h * RET_HEADDIM:(h + 1) * RET_HEADDIM, :] = (
            float(np.exp(np.float32(valid * lg))) * s_old + _dot_tn(kw, vh))
        y2.append(_rms(yr))
    gate = proj_ref[0, :, OFF_G:PROJ_MAIN].astype(F32)
    y2 = jnp.concatenate(y2, axis=1) * rg_ref[...] * _silu(gate)

    mix_ref[0, :, :D_SSM] = y1.astype(BF16)
    mix_ref[0, :, D_SSM:] = y2.astype(BF16)


def _mixer_seq(proj, dtr, conv0, ssm0, ret0, params, *, nchunks, chunk_offset, valid, pos_base, name,
               cast=()):
    nb = proj.shape[0]
    nsteps = nb * nchunks
    row = lambda b, c: (b, c + chunk_offset, 0)
    const3 = lambda b, c: (0, 0, 0)
    const2 = lambda b, c: (0, 0)
    per_b = lambda b, c: (b, 0, 0)
    pspecs = [pl.BlockSpec(p.shape, const2) for p in params]
    cast_specs = []
    for w, nblk in cast:
        assert nsteps % nblk == 0 and w.shape[0] % nblk == 0
        every = nsteps // nblk
        cast_specs.append(pl.BlockSpec((w.shape[0] // nblk, w.shape[1]),
                                       lambda b, c, every=every: ((b * nchunks + c) // every, 0)))
    kern = functools.partial(_mixer_seq_kernel, valid=valid, pos_base=pos_base, ncast=len(cast))
    return pl.pallas_call(
        kern,
        out_shape=(jax.ShapeDtypeStruct((nb, nchunks * CHUNK, D_MODEL), BF16),
                   jax.ShapeDtypeStruct((nb, CONV_W - 1, CONV_DIM), F32),
                   jax.ShapeDtypeStruct((nb, D_SSM, D_STATE), F32),
                   jax.ShapeDtypeStruct((nb, D_RET, RET_HEADDIM), F32))
        + tuple(jax.ShapeDtypeStruct(w.shape, BF16) for w, _ in cast),
        grid=(nb, nchunks),
        in_specs=[pl.BlockSpec((1, CHUNK, PROJ_MAIN), row),
                  pl.BlockSpec((1, CHUNK, LANES), row),
                  pl.BlockSpec((1, CONV_W - 1, CONV_DIM), const3),
                  pl.BlockSpec((1, D_SSM, D_STATE), const3),
                  pl.BlockSpec((1, D_RET, RET_HEADDIM), const3)] + pspecs + cast_specs,
        out_specs=(pl.BlockSpec((1, CHUNK, D_MODEL), lambda b, c: (b, c, 0)),
                   pl.BlockSpec((1, CONV_W - 1, CONV_DIM), per_b),
                   pl.BlockSpec((1, D_SSM, D_STATE), per_b),
                   pl.BlockSpec((1, D_RET, RET_HEADDIM), per_b)) + tuple(cast_specs),
        scratch_shapes=[pltpu.VMEM((CONV_PAD + CHUNK, CONV_DIM), F32),
                        pltpu.VMEM((RET_HEADS, CHUNK, CHUNK), F32),
                        pltpu.VMEM((2, CHUNK, RET_HEADDIM // 2), F32)],
        compiler_params=pltpu.CompilerParams(
            dimension_semantics=("arbitrary", "arbitrary"), vmem_limit_bytes=VMEM_LIMIT),
        name=name,
    )(proj, dtr, conv0, ssm0, ret0, *params, *[w for w, _ in cast])


def _mixer_step_kernel(proj_ref, dtr_ref, conv_ref, ssm_ref, ret_ref,
                       convw_ref, convb_ref, dtb_ref, alog_ref, dskip_ref, sg_ref, rg_ref, invf_ref,
                       mix_ref, convo_ref, ssmo_ref, reto_ref, cols_ref):
    R = STEP_ROWS
    xbc_raw = proj_ref[:, OFF_XBC:OFF_Q]
    acc = convb_ref[...] + xbc_raw * convw_ref[3:4, :]
    for i in range(CONV_W - 1):
        acc = acc + conv_ref[i] * convw_ref[i:i + 1, :]
    xbc = _silu(acc)
    convo_ref[0] = conv_ref[1]
    convo_ref[1] = conv_ref[2]
    convo_ref[2] = xbc_raw

    xs = xbc[:, :D_SSM]
    bmat = xbc[:, D_SSM:D_SSM + SSM_GROUPS * D_STATE]
    cmat = xbc[:, D_SSM + SSM_GROUPS * D_STATE:]
    dt = _softplus(dtr_ref[...] + dtb_ref[...])
    la = dt * (-jnp.exp(alog_ref[...]))
    expand = _head_expand()
    dt_x = _exact_right(dt, expand)
    decay_x = jnp.exp(_exact_right(la, expand))
    xdt = xs * dt_x

    ang = jnp.float32(PAST_LEN) * invf_ref[...]
    cos = jnp.cos(ang)
    sin = jnp.sin(ang)
    half = RET_HEADDIM // 2
    qs, ks = [], []
    for h in range(RET_HEADS):
        q1 = proj_ref[:, OFF_Q + h * RET_HEADDIM:OFF_Q + h * RET_HEADDIM + half]
        q2 = proj_ref[:, OFF_Q + h * RET_HEADDIM + half:OFF_Q + (h + 1) * RET_HEADDIM]
        k1 = proj_ref[:, OFF_K + h * RET_HEADDIM:OFF_K + h * RET_HEADDIM + half]
        k2 = proj_ref[:, OFF_K + h * RET_HEADDIM + half:OFF_K + (h + 1) * RET_HEADDIM]
        qs += [q1 * cos - q2 * sin, q1 * sin + q2 * cos]
        ks += [(k1 * cos - k2 * sin) * (RET_HEADDIM ** -0.5), (k1 * sin + k2 * cos) * (RET_HEADDIM ** -0.5)]
    qr = jnp.concatenate(qs, axis=1)
    kr = jnp.concatenate(ks, axis=1)
    vv = proj_ref[:, OFF_V:OFF_G]

    allq = jnp.concatenate([decay_x, xdt, kr, qr], axis=1)
    hi = allq.astype(BF16).astype(F32)
    r1 = allq - hi
    mid = r1.astype(BF16).astype(F32)
    lo = (r1 - mid).astype(BF16).astype(F32)
    stack = jnp.concatenate([hi, mid, lo, jnp.zeros_like(hi)], axis=0).astype(BF16)
    krow = lax.broadcasted_iota(jnp.int32, (4 * R, LANES), 0)
    row8 = lax.broadcasted_iota(jnp.int32, (R, 1), 0)
    lane = lax.broadcasted_iota(jnp.int32, (1, LANES), 1)

    y_cols = jnp.zeros((D_SSM, LANES), F32)
    y_ret = jnp.zeros((R, D_RET), F32)
    for r in range(R):
        sel = ((krow % R == r) & (krow < 3 * R)).astype(BF16)
        cols_ref[...] = _dot_tn(stack, sel)
        ycol = []
        for g in range(SSM_GROUPS):
            rows = slice(g * GROUP_DIM, (g + 1) * GROUP_DIM)
            h_old = ssm_ref[r, rows, :]
            h_new = (h_old * cols_ref[g * GROUP_DIM:(g + 1) * GROUP_DIM, :]
                     + cols_ref[D_SSM + g * GROUP_DIM:D_SSM + (g + 1) * GROUP_DIM, :]
                     * bmat[r:r + 1, g * D_STATE:(g + 1) * D_STATE])
            ssmo_ref[r, rows, :] = h_new
            ycol.append(jnp.sum(h_new * cmat[r:r + 1, g * D_STATE:(g + 1) * D_STATE], axis=1, keepdims=True))
        y_cols = jnp.where(lane == r, jnp.concatenate(ycol, axis=0), y_cols)
        yrow = []
        for h in range(RET_HEADS):
            rows = slice(h * RET_HEADDIM, (h + 1) * RET_HEADDIM)
            kcol = cols_ref[2 * D_SSM + h * RET_HEADDIM:2 * D_SSM + (h + 1) * RET_HEADDIM, :]
            qcol = cols_ref[3 * D_SSM + h * RET_HEADDIM:3 * D_SSM + (h + 1) * RET_HEADDIM, :]
            gamma = float(np.exp(np.float32(RET_LOG_GAMMA[h])))
            s_new = (gamma * ret_ref[r, rows, :]
                     + jnp.concatenate([kcol, kcol], axis=1) * vv[r:r + 1, h * RET_HEADDIM:(h + 1) * RET_HEADDIM])
            reto_ref[r, rows, :] = s_new
            yrow.append(jnp.sum(jnp.concatenate([qcol, qcol], axis=1) * s_new, axis=0, keepdims=True))
        y_ret = jnp.where(row8 == r, jnp.concatenate(yrow, axis=1), y_ret)

    y_ssd = y_cols.T[:R, :]
    y = (y_ssd + dskip_ref[...] * xs) * _silu(proj_ref[:, OFF_Z:OFF_XBC])
    y1 = jnp.concatenate([_rms(y[:, g * GROUP_DIM:(g + 1) * GROUP_DIM]) for g in range(SSM_GROUPS)],
                         axis=1) * sg_ref[...]
    y2 = jnp.concatenate([_rms(y_ret[:, h * RET_HEADDIM:(h + 1) * RET_HEADDIM]) for h in range(RET_HEADS)],
                         axis=1) * rg_ref[...] * _silu(proj_ref[:, OFF_G:PROJ_MAIN])
    mix_ref[:, :D_SSM] = y1
    mix_ref[:, D_SSM:] = y2


def _mixer_step(proj, dtr, conv_t, ssm, ret, params, *, nb):
    R = STEP_ROWS
    rows2 = lambda i: (i, 0)
    rows3 = lambda i: (i, 0, 0)
    mid3 = lambda i: (0, i, 0)
    const2 = lambda i: (0, 0)
    pspecs = [pl.BlockSpec(p.shape, const2) for p in params]
    return pl.pallas_call(
        _mixer_step_kernel,
        out_shape=(jax.ShapeDtypeStruct((nb, D_MODEL), F32),
                   jax.ShapeDtypeStruct((CONV_W - 1, nb, CONV_DIM), F32),
                   jax.ShapeDtypeStruct((nb, D_SSM, D_STATE), F32),
                   jax.ShapeDtypeStruct((nb, D_RET, RET_HEADDIM), F32)),
        grid=(nb // R,),
        in_specs=[pl.BlockSpec((R, PROJ_MAIN), rows2),
                  pl.BlockSpec((R, LANES), rows2),
                  pl.BlockSpec((CONV_W - 1, R, CONV_DIM), mid3),
                  pl.BlockSpec((R, D_SSM, D_STATE), rows3),
                  pl.BlockSpec((R, D_RET, RET_HEADDIM), rows3)] + pspecs,
        out_specs=(pl.BlockSpec((R, D_MODEL), rows2),
                   pl.BlockSpec((CONV_W - 1, R, CONV_DIM), mid3),
                   pl.BlockSpec((R, D_SSM, D_STATE), rows3),
                   pl.BlockSpec((R, D_RET, RET_HEADDIM), rows3)),
        scratch_shapes=[pltpu.VMEM((4 * D_SSM, LANES), F32)],
        compiler_params=pltpu.CompilerParams(
            dimension_semantics=("arbitrary",), vmem_limit_bytes=VMEM_LIMIT),
        name="mixer_step",
    )(proj, dtr, conv_t, ssm, ret, *params)


def _outproj_kernel(mix_ref, mixs_ref, w_ref, h_ref, hs_ref, g1_ref, g2_ref, hout_ref, f_ref, houts_ref, fs_ref):
    def rows(mix_r, h_r, hout_r, f_r):
        y = _dot(mix_r[...].astype(BF16), w_ref[...])
        h = h_r[...] + _rms(y) * g1_ref[...]
        hout_r[...] = h
        f_r[...] = (_rms(h) * g2_ref[...]).astype(BF16)

    rows(mix_ref, h_ref, hout_ref, f_ref)

    @pl.when(pl.program_id(0) == pl.num_programs(0) - 1)
    def _():
        rows(mixs_ref, hs_ref, houts_ref, fs_ref)


def _outproj(mix, mixs, w, h, hs, g1, g2, *, bm):
    m = mix.shape[0]
    ms = mixs.shape[0]
    row = lambda i: (i, 0)
    const = lambda i: (0, 0)
    return pl.pallas_call(
        _outproj_kernel,
        out_shape=(jax.ShapeDtypeStruct((m, D_MODEL), F32), jax.ShapeDtypeStruct((m, D_MODEL), BF16),
                   jax.ShapeDtypeStruct((ms, D_MODEL), F32), jax.ShapeDtypeStruct((ms, D_MODEL), BF16)),
        grid=(m // bm,),
        in_specs=[pl.BlockSpec((bm, D_MODEL), row),
                  pl.BlockSpec((ms, D_MODEL), const),
                  pl.BlockSpec((D_MODEL, D_MODEL), const),
                  pl.BlockSpec((bm, D_MODEL), row),
                  pl.BlockSpec((ms, D_MODEL), const),
                  pl.BlockSpec((1, D_MODEL), const),
                  pl.BlockSpec((1, D_MODEL), const)],
        out_specs=(pl.BlockSpec((bm, D_MODEL), row), pl.BlockSpec((bm, D_MODEL), row),
                   pl.BlockSpec((ms, D_MODEL), const), pl.BlockSpec((ms, D_MODEL), const)),
        compiler_params=pltpu.CompilerParams(
            dimension_semantics=("arbitrary",), vmem_limit_bytes=VMEM_LIMIT),
        name="outproj",
    )(mix, mixs, w, h, hs, g1, g2)


FFN_SPLIT = 2


def _ffn_kernel(f_ref, fs_ref, wg_ref, wu_ref, wd_ref, h_hbm, hs_ref, g_ref, o_ref, os_ref, acc_ref, hsem):
    i = pl.program_id(0)
    j = pl.program_id(1)
    nj = pl.num_programs(1)
    on_last = i == pl.num_programs(0) - 1
    bm = o_ref.shape[0]

    def h_copy(tile):
        return pltpu.make_async_copy(h_hbm.at[pl.ds(pl.multiple_of(tile * bm, bm), bm), :], o_ref, hsem)

    def finish_rows():
        for t in range(bm // NORM_ROWS):
            rows = slice(t * NORM_ROWS, (t + 1) * NORM_ROWS)
            o_ref[rows, :] = o_ref[rows, :] + _rms(acc_ref[rows, :]) * g_ref[...]
            acc_ref[rows, :] = jnp.zeros((NORM_ROWS, acc_ref.shape[1]), F32)

    @pl.when((i == 0) & (j == 0))
    def _():
        acc_ref[...] = jnp.zeros_like(acc_ref)

    @pl.when(j == 1)
    def _():
        h_copy(i).start()

    def ff_tile(with_side, finish_prev):
        fsub = wg_ref.shape[1] // FFN_SPLIT
        nsub = acc_ref.shape[1] // FFN_SPLIT
        if finish_prev:
            h_copy(i - 1).wait()
            finish_rows()
            if with_side:
                os_ref[...] = jnp.zeros_like(os_ref)

        def through(f, acc_r, s):
            ff = slice(s * fsub, (s + 1) * fsub)
            a = (_silu(_dot(f, wg_ref[:, ff])) * _dot(f, wu_ref[:, ff])).astype(BF16)
            for n in range(FFN_SPLIT):
                nn = slice(n * nsub, (n + 1) * nsub)
                acc_r[:, nn] += _dot(a, wd_ref[ff, nn])

        for s in range(FFN_SPLIT):
            through(f_ref[...], acc_ref, s)
            if with_side:
                through(fs_ref[...], os_ref, s)

    first = (j == 0) & (i > 0)
    rest = jnp.logical_not(first)
    not_last = jnp.logical_not(on_last)
    pl.when(first & on_last)(functools.partial(ff_tile, True, True))
    pl.when(first & not_last)(functools.partial(ff_tile, False, True))
    pl.when(rest & on_last)(functools.partial(ff_tile, True, False))
    pl.when(rest & not_last)(functools.partial(ff_tile, False, False))

    @pl.when(on_last & (j == nj - 1))
    def _():
        h_copy(i).wait()
        finish_rows()
        os_ref[...] = hs_ref[...] + _rms(os_ref[...]) * g_ref[...]


def _ffn(f, fs, wg, wu, wd, h, hs, g, *, bm, bf):
    m = f.shape[0]
    ms = fs.shape[0]
    assert m // bm > 1 and D_FF // bf > 1
    const = lambda i, j: (0, 0)
    once = pl.Buffered(1)
    return pl.pallas_call(
        _ffn_kernel,
        out_shape=(jax.ShapeDtypeStruct((m, D_MODEL), F32), jax.ShapeDtypeStruct((ms, D_MODEL), F32)),
        grid=(m // bm, D_FF // bf),
        in_specs=[pl.BlockSpec((bm, D_MODEL), lambda i, j: (i, 0)),
                  pl.BlockSpec((ms, D_MODEL), const, pipeline_mode=once),
                  pl.BlockSpec((D_MODEL, bf), lambda i, j: (0, j)),
                  pl.BlockSpec((D_MODEL, bf), lambda i, j: (0, j)),
                  pl.BlockSpec((bf, D_MODEL), lambda i, j: (j, 0)),
                  pl.BlockSpec(memory_space=pl.ANY),
                  pl.BlockSpec((ms, D_MODEL), const, pipeline_mode=once),
                  pl.BlockSpec((1, D_MODEL), const)],
        out_specs=(pl.BlockSpec((bm, D_MODEL), lambda i, j: (jnp.where((j == 0) & (i > 0), i - 1, i), 0)),
                   pl.BlockSpec((ms, D_MODEL), const)),
        scratch_shapes=[pltpu.VMEM((bm, D_MODEL), F32), pltpu.SemaphoreType.DMA],
        compiler_params=pltpu.CompilerParams(
            dimension_semantics=("arbitrary", "arbitrary"), vmem_limit_bytes=VMEM_LIMIT_FFN),
        name="ffn",
    )(f, fs, wg, wu, wd, h, hs, g)


def kernel(x_prompt, x_sample, state_conv, state_ssm, state_ret, meta_tokens, pre_mix_g, post_mix_g,
           pre_ffn_g, post_ffn_g, w_in, conv_w, conv_b, dt_bias, a_log, d_skip, ssm_norm_g, ret_norm_g,
           w_out, w_gate, w_up, w_down):
    bp, seq = x_prompt.shape[:2]
    bs = x_sample.shape[0]
    assert w_in.shape[0] == 1 and x_sample.shape[1] == 1 and seq % CHUNK == 0 and bs == CHUNK

    w_in_t = jnp.swapaxes(w_in[0], 0, 1)
    pad16 = lambda v: jnp.pad(v, ((0, 0), (0, LANES - SSM_HEADS)))
    inv_freq = (ROPE_BASE ** (-jnp.arange(RET_HEADDIM // 2, dtype=F32) / (RET_HEADDIM // 2)))[None, :]
    params = (conv_w[0], conv_b, pad16(dt_bias), pad16(a_log),
              jnp.repeat(d_skip, SSM_HEADDIM, axis=1), ssm_norm_g, ret_norm_g, inv_freq)

    xp = x_prompt.reshape(bp * seq, D_MODEL)
    xs_rows = x_sample.reshape(bs, D_MODEL)
    x_small = jnp.concatenate(
        [xs_rows, meta_tokens.astype(F32), jnp.zeros((CHUNK - N_META, D_MODEL), F32)], axis=0)
    proj_p, dtr_p, proj_s, dtr_s = _inproj(xp, x_small, pre_mix_g, w_in_t, bm=2048, xr=1024, bn=512)

    zc = jnp.zeros((1, CONV_W - 1, CONV_DIM), F32)
    zs = jnp.zeros((1, D_SSM, D_STATE), F32)
    zr = jnp.zeros((1, D_RET, RET_HEADDIM), F32)
    _, m_conv, m_ssm, m_ret = _mixer_seq(
        proj_s.reshape(1, 2 * CHUNK, PROJ_MAIN), dtr_s.reshape(1, 2 * CHUNK, LANES), zc, zs, zr, params,
        nchunks=1, chunk_offset=1, valid=N_META, pos_base=0, name="mixer_meta")[:4]

    nsteps = bp * (seq // CHUNK)
    mix_p, p_conv, p_ssm, p_ret, w_out_b, w_gate_b, w_up_b, w_down_b = _mixer_seq(
        proj_p.reshape(bp, seq, PROJ_MAIN), dtr_p.reshape(bp, seq, LANES), m_conv, m_ssm, m_ret, params,
        nchunks=seq // CHUNK, chunk_offset=0, valid=CHUNK, pos_base=N_META, name="mixer_prompt",
        cast=((w_out[0], nsteps), (w_gate[0], nsteps), (w_up[0], nsteps), (w_down[0], nsteps // 2)))

    conv_t = jnp.transpose(state_conv[0], (1, 0, 2))
    mix_s, s_conv_t, s_ssm, s_ret = _mixer_step(
        proj_s, dtr_s, conv_t, state_ssm[0].reshape(bs, D_SSM, D_STATE),
        state_ret[0].reshape(bs, D_RET, RET_HEADDIM), params, nb=bs)

    h1_p, f_p, h1_s, f_s = _outproj(mix_p.reshape(bp * seq, D_MODEL), mix_s, w_out_b, xp, xs_rows,
                                    post_mix_g, pre_ffn_g, bm=512)
    y_p, y_s = _ffn(f_p, f_s, w_gate_b, w_up_b, w_down_b, h1_p, h1_s, post_ffn_g, bm=1024, bf=512)

    return (y_p.reshape(bp, seq, D_MODEL),
            y_s.reshape(bs, 1, D_MODEL),
            p_conv[None],
            p_ssm.reshape(1, bp, SSM_HEADS, SSM_HEADDIM, D_STATE),
            p_ret.reshape(1, bp, RET_HEADS, RET_HEADDIM, RET_HEADDIM),
            jnp.transpose(s_conv_t, (1, 0, 2))[None],
            s_ssm.reshape(1, bs, SSM_HEADS, SSM_HEADDIM, D_STATE),
            s_ret.reshape(1, bs, RET_HEADS, RET_HEADDIM, RET_HEADDIM))
```

```python
import functools

import numpy as np
import jax
import jax.numpy as jnp
from jax import lax
from jax.experimental import pallas as pl
from jax.experimental.pallas import tpu as pltpu

F32 = jnp.float32
BF16 = jnp.bfloat16

D_MODEL = 2048
N_META = 16
CHUNK = 128
D_SSM = 1024
D_RET = 1024
SSM_HEADDIM = 64
SSM_HEADS = 16
SSM_GROUPS = 2
GROUP_DIM = D_SSM // SSM_GROUPS
D_STATE = 128
CONV_W = 4
CONV_DIM = D_SSM + 2 * SSM_GROUPS * D_STATE
RET_HEADS = 4
RET_HEADDIM = 256
ROPE_BASE = 10000.0
D_FF = 5632
EPS = 1e-6
PAST_LEN = 16384
LOG2E = float(np.log2(np.e))

LANES = 128
SUBLANES = 8
STEP_ROWS = SUBLANES
CONV_PAD = SUBLANES

OFF_Z = 0
OFF_XBC = D_SSM
OFF_Q = OFF_XBC + CONV_DIM
OFF_K = OFF_Q + D_RET
OFF_V = OFF_K + D_RET
OFF_G = OFF_V + D_RET
PROJ_MAIN = OFF_G + D_RET

VMEM_LIMIT = 56 * 1024 * 1024
VMEM_LIMIT_FFN = 60 * 1024 * 1024

RET_LOG_GAMMA = [float(np.log1p(-np.float32(2.0) ** np.float32(-5.0 - h)).astype(np.float32))
                 for h in range(RET_HEADS)]


def _silu(x):
    return x / (1.0 + jnp.exp2(x * (-LOG2E)))


def _softplus(x):
    return jnp.maximum(x, 0.0) + jnp.log1p(jnp.exp(-jnp.abs(x)))


def _rms(x):
    return x * lax.rsqrt(jnp.mean(x * x, axis=-1, keepdims=True) + EPS)


def _split3(x):
    hi = x.astype(BF16)
    r = x - hi.astype(F32)
    mid = r.astype(BF16)
    lo = (r - mid.astype(F32)).astype(BF16)
    return hi, mid, lo


def _dot(a, b):
    return jnp.dot(a, b, preferred_element_type=F32)


def _dot_nt(a, b):
    return lax.dot_general(a, b, (((1,), (1,)), ((), ())), preferred_element_type=F32)


def _dot_tn(a, b):
    return lax.dot_general(a, b, (((0,), (0,)), ((), ())), preferred_element_type=F32)


def _exact_right(x, sel):
    hi, mid, lo = x if isinstance(x, tuple) else _split3(x)
    return _dot(hi, sel) + _dot(mid, sel) + _dot(lo, sel)


def _select_right(x, sel):
    hi = x.astype(BF16)
    lo = (x - hi.astype(F32)).astype(BF16)
    return _dot(hi, sel) + _dot(lo, sel)


def _exact_left(sel, x):
    hi, mid, lo = x if isinstance(x, tuple) else _split3(x)
    return _dot(sel, hi) + _dot(sel, mid) + _dot(sel, lo)


def _exact_tn(x, sel):
    hi, mid, lo = x if isinstance(x, tuple) else _split3(x)
    return _dot_tn(hi, sel) + _dot_tn(mid, sel) + _dot_tn(lo, sel)


def _head_expand():
    r = lax.broadcasted_iota(jnp.int32, (LANES, D_SSM), 0)
    c = lax.broadcasted_iota(jnp.int32, (LANES, D_SSM), 1)
    return (c // SSM_HEADDIM == r).astype(BF16)


NORM_ROWS = 256
DT_ROW = D_SSM + CONV_DIM


def _inproj_kernel(x_ref, xs_ref, g_ref, wt_ref, wdt_ref, o_ref, odt_ref, os_ref, odts_ref, u_ref, us_ref,
                   *, npro, nsplit):
    i = pl.program_id(0)
    j = pl.program_id(1)
    on_last = i == pl.num_programs(0) - 1
    xr = x_ref.shape[0]

    @pl.when(j < npro)
    def _():
        wdt = wdt_ref[...].astype(BF16)
        lane = lax.broadcasted_iota(jnp.int32, (NORM_ROWS, LANES), 1)

        def norm_rows(src_ref, src, dst_ref, dt_ref, dst):
            u = (_rms(src_ref[src, :]) * g_ref[...]).astype(BF16)
            dst_ref[dst, :] = u
            dt_ref[dst, :] = jnp.where(lane < SSM_HEADS, _dot_nt(u, wdt), 0.0)

        def body(t, carry):
            src = pl.ds(pl.multiple_of(t * NORM_ROWS, NORM_ROWS), NORM_ROWS)
            dst = pl.ds(pl.multiple_of(j * xr + t * NORM_ROWS, NORM_ROWS), NORM_ROWS)
            norm_rows(x_ref, src, u_ref, odt_ref, dst)
            return carry
        lax.fori_loop(0, xr // NORM_ROWS, body, 0)

        @pl.when(on_last & (j == 0))
        def _():
            for t in range(xs_ref.shape[0] // NORM_ROWS):
                rows = pl.ds(t * NORM_ROWS, NORM_ROWS)
                norm_rows(xs_ref, rows, us_ref, odts_ref, rows)

    def column_tile(with_side):
        sub = wt_ref.shape[0] // nsplit
        for s in range(nsplit):
            cols = slice(s * sub, (s + 1) * sub)
            w = wt_ref[cols, :].astype(BF16)
            o_ref[:, cols] = _dot_nt(u_ref[...], w).astype(o_ref.dtype)
            if with_side:
                os_ref[:, cols] = _dot_nt(us_ref[...], w)

    pl.when((j >= npro) & on_last)(functools.partial(column_tile, True))
    pl.when((j >= npro) & jnp.logical_not(on_last))(functools.partial(column_tile, False))


def _inproj(x, xs, g, wt, *, bm, xr, bn):
    m = x.shape[0]
    ms = xs.shape[0]
    nm = m // bm
    npro = bm // xr
    assert DT_ROW % bn == 0 and ms % NORM_ROWS == 0

    def wrow(i, j):
        t = jnp.maximum(j - npro, 0)
        skip = jnp.where(t * bn >= DT_ROW, SSM_HEADS // SUBLANES, 0)
        return ((t * (bn // SUBLANES) + skip) * SUBLANES, 0)

    col = lambda j: jnp.maximum(j - npro, 0)
    const = lambda i, j: (0, 0)
    return pl.pallas_call(
        functools.partial(_inproj_kernel, npro=npro, nsplit=2),
        out_shape=(jax.ShapeDtypeStruct((m, PROJ_MAIN), BF16), jax.ShapeDtypeStruct((m, LANES), F32),
                   jax.ShapeDtypeStruct((ms, PROJ_MAIN), F32), jax.ShapeDtypeStruct((ms, LANES), F32)),
        grid=(nm, npro + PROJ_MAIN // bn),
        in_specs=[pl.BlockSpec((xr, D_MODEL), lambda i, j: (i * npro + jnp.minimum(j, npro - 1), 0)),
                  pl.BlockSpec((ms, D_MODEL), const),
                  pl.BlockSpec((1, D_MODEL), const),
                  pl.BlockSpec((pl.Element(bn), pl.Element(D_MODEL)), wrow),
                  pl.BlockSpec((pl.Element(LANES), pl.Element(D_MODEL)), lambda i, j: (DT_ROW, 0))],
        out_specs=(pl.BlockSpec((bm, bn), lambda i, j: (i, col(j))),
                   pl.BlockSpec((bm, LANES), lambda i, j: (i, 0)),
                   pl.BlockSpec((ms, bn), lambda i, j: (0, jnp.where(i == nm - 1, col(j), 0))),
                   pl.BlockSpec((ms, LANES), const)),
        scratch_shapes=[pltpu.VMEM((bm, D_MODEL), BF16), pltpu.VMEM((ms, D_MODEL), BF16)],
        compiler_params=pltpu.CompilerParams(
            dimension_semantics=("arbitrary", "arbitrary"), vmem_limit_bytes=VMEM_LIMIT),
        name="inproj",
    )(x, xs, g, wt, wt)


N_MIXER_IN = 13
N_MIXER_OUT = 4


def _mixer_seq_kernel(*refs, valid, pos_base, ncast):
    ins = refs[:N_MIXER_IN]
    cast_in = refs[N_MIXER_IN:N_MIXER_IN + ncast]
    outs = refs[N_MIXER_IN + ncast:N_MIXER_IN + ncast + N_MIXER_OUT]
    cast_out = refs[N_MIXER_IN + ncast + N_MIXER_OUT:N_MIXER_IN + 2 * ncast + N_MIXER_OUT]
    scratch = refs[N_MIXER_IN + 2 * ncast + N_MIXER_OUT:]
    _mixer_seq_body(*ins, *outs, *scratch, valid=valid, pos_base=pos_base)
    for src, dst in zip(cast_in, cast_out):
        dst[...] = src[...].astype(BF16)


def _mixer_seq_body(proj_ref, dtr_ref, conv0_ref, ssm0_ref, ret0_ref,
                    convw_ref, convb_ref, dtb_ref, alog_ref, dskip_ref, sg_ref, rg_ref, invf_ref,
                    mix_ref, convo_ref, ssmo_ref, reto_ref,
                    cbuf_ref, rdec_ref, cdec_ref, trig_ref, *, valid, pos_base):
    C = CHUNK
    b = pl.program_id(0)
    c = pl.program_id(1)
    rowi = lax.broadcasted_iota(jnp.int32, (C, 1), 0)
    rowf = rowi.astype(F32)
    ri = lax.broadcasted_iota(jnp.int32, (C, C), 0)
    ci = lax.broadcasted_iota(jnp.int32, (C, C), 1)
    causal = ri >= ci

    @pl.when((b == 0) & (c == 0))
    def _():
        diff = (ri - ci).astype(F32)
        for h in range(RET_HEADS):
            rdec_ref[h] = jnp.where(causal, jnp.exp(jnp.maximum(diff, 0.0) * RET_LOG_GAMMA[h]), 0.0)
            rows_l = jnp.broadcast_to(rowf, (C, LANES))
            cdec_ref[h] = jnp.exp((rows_l + 1.0) * RET_LOG_GAMMA[h])
            cdec_ref[RET_HEADS + h] = jnp.exp((valid - 1.0 - rows_l) * RET_LOG_GAMMA[h])
        row_ang = rowf * invf_ref[...]
        trig_ref[0] = jnp.cos(row_ang)
        trig_ref[1] = jnp.sin(row_ang)

    hist = CONV_PAD - (CONV_W - 1)

    @pl.when(c == 0)
    def _():
        cbuf_ref[hist:CONV_PAD, :] = conv0_ref[0]
        ssmo_ref[0] = ssm0_ref[0]
        reto_ref[0] = ret0_ref[0]

    xbc_raw = proj_ref[0, :, OFF_XBC:OFF_Q].astype(F32)
    cbuf_ref[CONV_PAD:CONV_PAD + C, :] = xbc_raw
    acc = convb_ref[...] + xbc_raw * convw_ref[CONV_W - 1:CONV_W, :]
    for i in range(CONV_W - 1):
        acc = acc + cbuf_ref[hist + i:hist + i + C, :] * convw_ref[i:i + 1, :]
    xbc = _silu(acc)
    new_prev = cbuf_ref[hist + valid:CONV_PAD + valid, :]
    cbuf_ref[hist:CONV_PAD, :] = new_prev
    convo_ref[0] = new_prev

    xs = xbc[:, :D_SSM]
    bmat = xbc[:, D_SSM:D_SSM + SSM_GROUPS * D_STATE].astype(BF16)
    cmat = xbc[:, D_SSM + SSM_GROUPS * D_STATE:].astype(BF16)

    dt = _softplus(dtr_ref[0] + dtb_ref[...])
    if valid < C:
        dt = jnp.where(rowi < valid, dt, 0.0)
    la = dt * (-jnp.exp(alog_ref[...]))
    tril = causal.astype(BF16)
    triu = (ri <= ci).astype(BF16)
    eye = (ri == ci).astype(BF16)
    la3 = _split3(la)
    lcum = _exact_left(tril, la3)
    lcum_t = _exact_tn(la3, triu)
    dt_t = _exact_tn(dt, eye)
    expand = _head_expand()
    carry_scale = _select_right(jnp.exp(lcum), expand)
    tail_scale = _select_right(jnp.exp(lcum[C - 1:C, :] - lcum) * dt, expand)
    lcum2 = lcum * LOG2E
    lcum2_t = lcum_t * LOG2E

    cbs = [_dot_nt(cmat[:, g * D_STATE:(g + 1) * D_STATE], bmat[:, g * D_STATE:(g + 1) * D_STATE])
           for g in range(SSM_GROUPS)]
    lane = lax.broadcasted_iota(jnp.int32, (C, LANES), 1)
    left = lane < SSM_HEADDIM
    y_intra = []
    for m in range(SSM_HEADS // 2):
        ws = []
        for h in (2 * m, 2 * m + 1):
            seg2 = lcum2[:, h:h + 1] - lcum2_t[h:h + 1, :]
            decay = jnp.exp2(jnp.where(causal, seg2, -jnp.inf))
            ws.append((cbs[h // (SSM_HEADS // SSM_GROUPS)] * decay * dt_t[h:h + 1, :]).astype(BF16))
        xm = xs[:, m * LANES:(m + 1) * LANES]
        xst = jnp.concatenate([jnp.where(left, xm, 0.0), jnp.where(left, 0.0, xm)], axis=0).astype(BF16)
        y_intra.append(_dot(jnp.concatenate(ws, axis=1), xst))
    y = jnp.concatenate(y_intra, axis=1)

    hstate = ssmo_ref[0]
    hb = hstate.astype(BF16)
    y_inter = jnp.concatenate(
        [_dot_nt(cmat[:, g * D_STATE:(g + 1) * D_STATE], hb[g * GROUP_DIM:(g + 1) * GROUP_DIM, :])
         for g in range(SSM_GROUPS)], axis=1)
    y = y + y_inter * carry_scale + dskip_ref[...] * xs

    xw = (xs * tail_scale).astype(BF16)
    upd = jnp.concatenate(
        [_dot_tn(xw[:, g * GROUP_DIM:(g + 1) * GROUP_DIM], bmat[:, g * D_STATE:(g + 1) * D_STATE])
         for g in range(SSM_GROUPS)], axis=0)
    la_tot = _exact_tn(la3, jnp.ones((C, LANES), BF16))
    er = lax.broadcasted_iota(jnp.int32, (D_SSM, LANES), 0)
    ec = lax.broadcasted_iota(jnp.int32, (D_SSM, LANES), 1)
    expand_t = (er // SSM_HEADDIM == ec).astype(BF16)
    chunk_decay = jnp.exp(_exact_left(expand_t, la_tot))
    ssmo_ref[0] = chunk_decay * hstate + upd

    z = proj_ref[0, :, OFF_Z:OFF_XBC].astype(F32)
    y = y * _silu(z)
    y1 = jnp.concatenate([_rms(y[:, g * GROUP_DIM:(g + 1) * GROUP_DIM]) for g in range(SSM_GROUPS)],
                         axis=1) * sg_ref[...]

    ang0 = (pos_base + c * C).astype(F32) * invf_ref[...]
    cos0, sin0 = jnp.cos(ang0), jnp.sin(ang0)
    cos = cos0 * trig_ref[0] - sin0 * trig_ref[1]
    sin = sin0 * trig_ref[0] + cos0 * trig_ref[1]
    kscale = RET_HEADDIM ** -0.5
    cos_k, sin_k = cos * kscale, sin * kscale
    half = RET_HEADDIM // 2
    y2 = []
    for h in range(RET_HEADS):
        lg = RET_LOG_GAMMA[h]
        q1 = proj_ref[0, :, OFF_Q + h * RET_HEADDIM:OFF_Q + h * RET_HEADDIM + half].astype(F32)
        q2 = proj_ref[0, :, OFF_Q + h * RET_HEADDIM + half:OFF_Q + (h + 1) * RET_HEADDIM].astype(F32)
        k1 = proj_ref[0, :, OFF_K + h * RET_HEADDIM:OFF_K + h * RET_HEADDIM + half].astype(F32)
        k2 = proj_ref[0, :, OFF_K + h * RET_HEADDIM + half:OFF_K + (h + 1) * RET_HEADDIM].astype(F32)
        vh = proj_ref[0, :, OFF_V + h * RET_HEADDIM:OFF_V + (h + 1) * RET_HEADDIM].astype(BF16)
        qr = jnp.concatenate([q1 * cos - q2 * sin, q1 * sin + q2 * cos], axis=1)
        kr = jnp.concatenate([k1 * cos_k - k2 * sin_k, k1 * sin_k + k2 * cos_k], axis=1)
        if valid < C:
            kr = jnp.where(rowi < valid, kr, 0.0)
        qb = qr.astype(BF16)
        scores = _dot_nt(qb, kr.astype(BF16)) * rdec_ref[h]
        s_old = reto_ref[0, h * RET_HEADDIM:(h + 1) * RET_HEADDIM, :]
        carry_dec = cdec_ref[h]
        tail_dec = cdec_ref[RET_HEADS + h]
        yr = (_dot(scores.astype(BF16), vh)
              + _dot(qb, s_old.astype(BF16)) * jnp.concatenate([carry_dec, carry_dec], axis=1))
        kw = (kr * jnp.concatenate([tail_dec, tail_dec], axis=1)).astype(BF16)
        reto_ref[0, h * RET_HEADDIM:(h + 1) * RET_HEADDIM, :] = (
            float(np.exp(np.float32(valid * lg))) * s_old + _dot_tn(kw, vh))
        y2.append(_rms(yr))
    gate = proj_ref[0, :, OFF_G:PROJ_MAIN].astype(F32)
    y2 = jnp.concatenate(y2, axis=1) * rg_ref[...] * _silu(gate)

    mix_ref[0, :, :D_SSM] = y1.astype(BF16)
    mix_ref[0, :, D_SSM:] = y2.astype(BF16)


def _mixer_seq(proj, dtr, conv0, ssm0, ret0, params, *, nchunks, chunk_offset, valid, pos_base, name,
               cast=()):
    nb = proj.shape[0]
    nsteps = nb * nchunks
    row = lambda b, c: (b, c + chunk_offset, 0)
    const3 = lambda b, c: (0, 0, 0)
    const2 = lambda b, c: (0, 0)
    per_b = lambda b, c: (b, 0, 0)
    pspecs = [pl.BlockSpec(p.shape, const2) for p in params]
    cast_specs = []
    for w, nblk in cast:
        assert nsteps % nblk == 0 and w.shape[0] % nblk == 0
        every = nsteps // nblk
        cast_specs.append(pl.BlockSpec((w.shape[0] // nblk, w.shape[1]),
                                       lambda b, c, every=every: ((b * nchunks + c) // every, 0)))
    kern = functools.partial(_mixer_seq_kernel, valid=valid, pos_base=pos_base, ncast=len(cast))
    return pl.pallas_call(
        kern,
        out_shape=(jax.ShapeDtypeStruct((nb, nchunks * CHUNK, D_MODEL), BF16),
                   jax.ShapeDtypeStruct((nb, CONV_W - 1, CONV_DIM), F32),
                   jax.ShapeDtypeStruct((nb, D_SSM, D_STATE), F32),
                   jax.ShapeDtypeStruct((nb, D_RET, RET_HEADDIM), F32))
        + tuple(jax.ShapeDtypeStruct(w.shape, BF16) for w, _ in cast),
        grid=(nb, nchunks),
        in_specs=[pl.BlockSpec((1, CHUNK, PROJ_MAIN), row),
                  pl.BlockSpec((1, CHUNK, LANES), row),
                  pl.BlockSpec((1, CONV_W - 1, CONV_DIM), const3),
                  pl.BlockSpec((1, D_SSM, D_STATE), const3),
                  pl.BlockSpec((1, D_RET, RET_HEADDIM), const3)] + pspecs + cast_specs,
        out_specs=(pl.BlockSpec((1, CHUNK, D_MODEL), lambda b, c: (b, c, 0)),
                   pl.BlockSpec((1, CONV_W - 1, CONV_DIM), per_b),
                   pl.BlockSpec((1, D_SSM, D_STATE), per_b),
                   pl.BlockSpec((1, D_RET, RET_HEADDIM), per_b)) + tuple(cast_specs),
        scratch_shapes=[pltpu.VMEM((CONV_PAD + CHUNK, CONV_DIM), F32),
                        pltpu.VMEM((RET_HEADS, CHUNK, CHUNK), F32),
                        pltpu.VMEM((2 * RET_HEADS, CHUNK, LANES), F32),
                        pltpu.VMEM((2, CHUNK, RET_HEADDIM // 2), F32)],
        compiler_params=pltpu.CompilerParams(
            dimension_semantics=("arbitrary", "arbitrary"), vmem_limit_bytes=VMEM_LIMIT),
        name=name,
    )(proj, dtr, conv0, ssm0, ret0, *params, *[w for w, _ in cast])


def _mixer_step_kernel(proj_ref, dtr_ref, conv_ref, ssm_ref, ret_ref,
                       convw_ref, convb_ref, dtb_ref, alog_ref, dskip_ref, sg_ref, rg_ref, invf_ref,
                       mix_ref, convo_ref, ssmo_ref, reto_ref, cols_ref):
    R = STEP_ROWS
    xbc_raw = proj_ref[:, OFF_XBC:OFF_Q]
    acc = convb_ref[...] + xbc_raw * convw_ref[3:4, :]
    for i in range(CONV_W - 1):
        acc = acc + conv_ref[i] * convw_ref[i:i + 1, :]
    xbc = _silu(acc)
    convo_ref[0] = conv_ref[1]
    convo_ref[1] = conv_ref[2]
    convo_ref[2] = xbc_raw

    xs = xbc[:, :D_SSM]
    bmat = xbc[:, D_SSM:D_SSM + SSM_GROUPS * D_STATE]
    cmat = xbc[:, D_SSM + SSM_GROUPS * D_STATE:]
    dt = _softplus(dtr_ref[...] + dtb_ref[...])
    la = dt * (-jnp.exp(alog_ref[...]))
    expand = _head_expand()
    dt_x = _exact_right(dt, expand)
    decay_x = jnp.exp(_exact_right(la, expand))
    xdt = xs * dt_x

    ang = jnp.float32(PAST_LEN) * invf_ref[...]
    cos = jnp.cos(ang)
    sin = jnp.sin(ang)
    half = RET_HEADDIM // 2
    qs, ks = [], []
    for h in range(RET_HEADS):
        q1 = proj_ref[:, OFF_Q + h * RET_HEADDIM:OFF_Q + h * RET_HEADDIM + half]
        q2 = proj_ref[:, OFF_Q + h * RET_HEADDIM + half:OFF_Q + (h + 1) * RET_HEADDIM]
        k1 = proj_ref[:, OFF_K + h * RET_HEADDIM:OFF_K + h * RET_HEADDIM + half]
        k2 = proj_ref[:, OFF_K + h * RET_HEADDIM + half:OFF_K + (h + 1) * RET_HEADDIM]
        qs += [q1 * cos - q2 * sin, q1 * sin + q2 * cos]
        ks += [(k1 * cos - k2 * sin) * (RET_HEADDIM ** -0.5), (k1 * sin + k2 * cos) * (RET_HEADDIM ** -0.5)]
    qr = jnp.concatenate(qs, axis=1)
    kr = jnp.concatenate(ks, axis=1)
    vv = proj_ref[:, OFF_V:OFF_G]

    allq = jnp.concatenate([decay_x, xdt, kr, qr], axis=1)
    hi = allq.astype(BF16).astype(F32)
    r1 = allq - hi
    mid = r1.astype(BF16).astype(F32)
    lo = (r1 - mid).astype(BF16).astype(F32)
    stack = jnp.concatenate([hi, mid, lo, jnp.zeros_like(hi)], axis=0).astype(BF16)
    krow = lax.broadcasted_iota(jnp.int32, (4 * R, LANES), 0)
    row8 = lax.broadcasted_iota(jnp.int32, (R, 1), 0)
    lane = lax.broadcasted_iota(jnp.int32, (1, LANES), 1)

    y_cols = jnp.zeros((D_SSM, LANES), F32)
    y_ret = jnp.zeros((R, D_RET), F32)
    for r in range(R):
        sel = ((krow % R == r) & (krow < 3 * R)).astype(BF16)
        cols_ref[...] = _dot_tn(stack, sel)
        ycol = []
        for g in range(SSM_GROUPS):
            rows = slice(g * GROUP_DIM, (g + 1) * GROUP_DIM)
            h_old = ssm_ref[r, rows, :]
            h_new = (h_old * cols_ref[g * GROUP_DIM:(g + 1) * GROUP_DIM, :]
                     + cols_ref[D_SSM + g * GROUP_DIM:D_SSM + (g + 1) * GROUP_DIM, :]
                     * bmat[r:r + 1, g * D_STATE:(g + 1) * D_STATE])
            ssmo_ref[r, rows, :] = h_new
            ycol.append(jnp.sum(h_new * cmat[r:r + 1, g * D_STATE:(g + 1) * D_STATE], axis=1, keepdims=True))
        y_cols = jnp.where(lane == r, jnp.concatenate(ycol, axis=0), y_cols)
        yrow = []
        for h in range(RET_HEADS):
            rows = slice(h * RET_HEADDIM, (h + 1) * RET_HEADDIM)
            kcol = cols_ref[2 * D_SSM + h * RET_HEADDIM:2 * D_SSM + (h + 1) * RET_HEADDIM, :]
            qcol = cols_ref[3 * D_SSM + h * RET_HEADDIM:3 * D_SSM + (h + 1) * RET_HEADDIM, :]
            gamma = float(np.exp(np.float32(RET_LOG_GAMMA[h])))
            s_new = (gamma * ret_ref[r, rows, :]
                     + jnp.concatenate([kcol, kcol], axis=1) * vv[r:r + 1, h * RET_HEADDIM:(h + 1) * RET_HEADDIM])
            reto_ref[r, rows, :] = s_new
            yrow.append(jnp.sum(jnp.concatenate([qcol, qcol], axis=1) * s_new, axis=0, keepdims=True))
        y_ret = jnp.where(row8 == r, jnp.concatenate(yrow, axis=1), y_ret)

    y_ssd = y_cols.T[:R, :]
    y = (y_ssd + dskip_ref[...] * xs) * _silu(proj_ref[:, OFF_Z:OFF_XBC])
    y1 = jnp.concatenate([_rms(y[:, g * GROUP_DIM:(g + 1) * GROUP_DIM]) for g in range(SSM_GROUPS)],
                         axis=1) * sg_ref[...]
    y2 = jnp.concatenate([_rms(y_ret[:, h * RET_HEADDIM:(h + 1) * RET_HEADDIM]) for h in range(RET_HEADS)],
                         axis=1) * rg_ref[...] * _silu(proj_ref[:, OFF_G:PROJ_MAIN])
    mix_ref[:, :D_SSM] = y1
    mix_ref[:, D_SSM:] = y2


def _mixer_step(proj, dtr, conv_t, ssm, ret, params, *, nb):
    R = STEP_ROWS
    rows2 = lambda i: (i, 0)
    rows3 = lambda i: (i, 0, 0)
    mid3 = lambda i: (0, i, 0)
    const2 = lambda i: (0, 0)
    pspecs = [pl.BlockSpec(p.shape, const2) for p in params]
    return pl.pallas_call(
        _mixer_step_kernel,
        out_shape=(jax.ShapeDtypeStruct((nb, D_MODEL), F32),
                   jax.ShapeDtypeStruct((CONV_W - 1, nb, CONV_DIM), F32),
                   jax.ShapeDtypeStruct((nb, D_SSM, D_STATE), F32),
                   jax.ShapeDtypeStruct((nb, D_RET, RET_HEADDIM), F32)),
        grid=(nb // R,),
        in_specs=[pl.BlockSpec((R, PROJ_MAIN), rows2),
                  pl.BlockSpec((R, LANES), rows2),
                  pl.BlockSpec((CONV_W - 1, R, CONV_DIM), mid3),
                  pl.BlockSpec((R, D_SSM, D_STATE), rows3),
                  pl.BlockSpec((R, D_RET, RET_HEADDIM), rows3)] + pspecs,
        out_specs=(pl.BlockSpec((R, D_MODEL), rows2),
                   pl.BlockSpec((CONV_W - 1, R, CONV_DIM), mid3),
                   pl.BlockSpec((R, D_SSM, D_STATE), rows3),
                   pl.BlockSpec((R, D_RET, RET_HEADDIM), rows3)),
        scratch_shapes=[pltpu.VMEM((4 * D_SSM, LANES), F32)],
        compiler_params=pltpu.CompilerParams(
            dimension_semantics=("arbitrary",), vmem_limit_bytes=VMEM_LIMIT),
        name="mixer_step",
    )(proj, dtr, conv_t, ssm, ret, *params)


def _outproj_kernel(mix_ref, mixs_ref, w_ref, h_ref, hs_ref, g1_ref, g2_ref, hout_ref, f_ref, houts_ref, fs_ref):
    def rows(mix_r, h_r, hout_r, f_r):
        y = _dot(mix_r[...].astype(BF16), w_ref[...])
        h = h_r[...] + _rms(y) * g1_ref[...]
        hout_r[...] = h
        f_r[...] = (_rms(h) * g2_ref[...]).astype(BF16)

    rows(mix_ref, h_ref, hout_ref, f_ref)

    @pl.when(pl.program_id(0) == pl.num_programs(0) - 1)
    def _():
        rows(mixs_ref, hs_ref, houts_ref, fs_ref)


def _outproj(mix, mixs, w, h, hs, g1, g2, *, bm):
    m = mix.shape[0]
    ms = mixs.shape[0]
    row = lambda i: (i, 0)
    const = lambda i: (0, 0)
    return pl.pallas_call(
        _outproj_kernel,
        out_shape=(jax.ShapeDtypeStruct((m, D_MODEL), F32), jax.ShapeDtypeStruct((m, D_MODEL), BF16),
                   jax.ShapeDtypeStruct((ms, D_MODEL), F32), jax.ShapeDtypeStruct((ms, D_MODEL), BF16)),
        grid=(m // bm,),
        in_specs=[pl.BlockSpec((bm, D_MODEL), row),
                  pl.BlockSpec((ms, D_MODEL), const),
                  pl.BlockSpec((D_MODEL, D_MODEL), const),
                  pl.BlockSpec((bm, D_MODEL), row),
                  pl.BlockSpec((ms, D_MODEL), const),
                  pl.BlockSpec((1, D_MODEL), const),
                  pl.BlockSpec((1, D_MODEL), const)],
        out_specs=(pl.BlockSpec((bm, D_MODEL), row), pl.BlockSpec((bm, D_MODEL), row),
                   pl.BlockSpec((ms, D_MODEL), const), pl.BlockSpec((ms, D_MODEL), const)),
        compiler_params=pltpu.CompilerParams(
            dimension_semantics=("arbitrary",), vmem_limit_bytes=VMEM_LIMIT),
        name="outproj",
    )(mix, mixs, w, h, hs, g1, g2)


FFN_SPLIT = 2


def _ffn_kernel(f_ref, fs_ref, wg_ref, wu_ref, wd_ref, h_hbm, hs_ref, g_ref, o_ref, os_ref, acc_ref, hsem):
    i = pl.program_id(0)
    j = pl.program_id(1)
    nj = pl.num_programs(1)
    on_last = i == pl.num_programs(0) - 1
    bm = o_ref.shape[0]

    def h_copy(tile):
        return pltpu.make_async_copy(h_hbm.at[pl.ds(pl.multiple_of(tile * bm, bm), bm), :], o_ref, hsem)

    def finish_rows():
        for t in range(bm // NORM_ROWS):
            rows = slice(t * NORM_ROWS, (t + 1) * NORM_ROWS)
            o_ref[rows, :] = o_ref[rows, :] + _rms(acc_ref[rows, :]) * g_ref[...]
            acc_ref[rows, :] = jnp.zeros((NORM_ROWS, acc_ref.shape[1]), F32)

    @pl.when((i == 0) & (j == 0))
    def _():
        acc_ref[...] = jnp.zeros_like(acc_ref)

    @pl.when(j == 1)
    def _():
        h_copy(i).start()

    def ff_tile(with_side, finish_prev):
        fsub = wg_ref.shape[1] // FFN_SPLIT
        nsub = acc_ref.shape[1] // FFN_SPLIT
        if finish_prev:
            h_copy(i - 1).wait()
            finish_rows()
            if with_side:
                os_ref[...] = jnp.zeros_like(os_ref)

        def through(f, acc_r, s):
            ff = slice(s * fsub, (s + 1) * fsub)
            a = (_silu(_dot(f, wg_ref[:, ff])) * _dot(f, wu_ref[:, ff])).astype(BF16)
            for n in range(FFN_SPLIT):
                nn = slice(n * nsub, (n + 1) * nsub)
                acc_r[:, nn] += _dot(a, wd_ref[ff, nn])

        for s in range(FFN_SPLIT):
            through(f_ref[...], acc_ref, s)
            if with_side:
                through(fs_ref[...], os_ref, s)

    first = (j == 0) & (i > 0)
    rest = jnp.logical_not(first)
    not_last = jnp.logical_not(on_last)
    pl.when(first & on_last)(functools.partial(ff_tile, True, True))
    pl.when(first & not_last)(functools.partial(ff_tile, False, True))
    pl.when(rest & on_last)(functools.partial(ff_tile, True, False))
    pl.when(rest & not_last)(functools.partial(ff_tile, False, False))

    @pl.when(on_last & (j == nj - 1))
    def _():
        h_copy(i).wait()
        finish_rows()
        os_ref[...] = hs_ref[...] + _rms(os_ref[...]) * g_ref[...]


def _ffn(f, fs, wg, wu, wd, h, hs, g, *, bm, bf):
    m = f.shape[0]
    ms = fs.shape[0]
    assert m // bm > 1 and D_FF // bf > 1
    const = lambda i, j: (0, 0)
    once = pl.Buffered(1)
    return pl.pallas_call(
        _ffn_kernel,
        out_shape=(jax.ShapeDtypeStruct((m, D_MODEL), F32), jax.ShapeDtypeStruct((ms, D_MODEL), F32)),
        grid=(m // bm, D_FF // bf),
        in_specs=[pl.BlockSpec((bm, D_MODEL), lambda i, j: (i, 0)),
                  pl.BlockSpec((ms, D_MODEL), const, pipeline_mode=once),
                  pl.BlockSpec((D_MODEL, bf), lambda i, j: (0, j)),
                  pl.BlockSpec((D_MODEL, bf), lambda i, j: (0, j)),
                  pl.BlockSpec((bf, D_MODEL), lambda i, j: (j, 0)),
                  pl.BlockSpec(memory_space=pl.ANY),
                  pl.BlockSpec((ms, D_MODEL), const, pipeline_mode=once),
                  pl.BlockSpec((1, D_MODEL), const)],
        out_specs=(pl.BlockSpec((bm, D_MODEL), lambda i, j: (jnp.where((j == 0) & (i > 0), i - 1, i), 0)),
                   pl.BlockSpec((ms, D_MODEL), const)),
        scratch_shapes=[pltpu.VMEM((bm, D_MODEL), F32), pltpu.SemaphoreType.DMA],
        compiler_params=pltpu.CompilerParams(
            dimension_semantics=("arbitrary", "arbitrary"), vmem_limit_bytes=VMEM_LIMIT_FFN),
        name="ffn",
    )(f, fs, wg, wu, wd, h, hs, g)


def kernel(x_prompt, x_sample, state_conv, state_ssm, state_ret, meta_tokens, pre_mix_g, post_mix_g,
           pre_ffn_g, post_ffn_g, w_in, conv_w, conv_b, dt_bias, a_log, d_skip, ssm_norm_g, ret_norm_g,
           w_out, w_gate, w_up, w_down):
    bp, seq = x_prompt.shape[:2]
    bs = x_sample.shape[0]
    assert w_in.shape[0] == 1 and x_sample.shape[1] == 1 and seq % CHUNK == 0 and bs == CHUNK

    w_in_t = jnp.swapaxes(w_in[0], 0, 1)
    pad16 = lambda v: jnp.pad(v, ((0, 0), (0, LANES - SSM_HEADS)))
    inv_freq = (ROPE_BASE ** (-jnp.arange(RET_HEADDIM // 2, dtype=F32) / (RET_HEADDIM // 2)))[None, :]
    params = (conv_w[0], conv_b, pad16(dt_bias), pad16(a_log),
              jnp.repeat(d_skip, SSM_HEADDIM, axis=1), ssm_norm_g, ret_norm_g, inv_freq)

    xp = x_prompt.reshape(bp * seq, D_MODEL)
    xs_rows = x_sample.reshape(bs, D_MODEL)
    x_small = jnp.concatenate(
        [xs_rows, meta_tokens.astype(F32), jnp.zeros((CHUNK - N_META, D_MODEL), F32)], axis=0)
    proj_p, dtr_p, proj_s, dtr_s = _inproj(xp, x_small, pre_mix_g, w_in_t, bm=2048, xr=1024, bn=512)

    zc = jnp.zeros((1, CONV_W - 1, CONV_DIM), F32)
    zs = jnp.zeros((1, D_SSM, D_STATE), F32)
    zr = jnp.zeros((1, D_RET, RET_HEADDIM), F32)
    _, m_conv, m_ssm, m_ret = _mixer_seq(
        proj_s.reshape(1, 2 * CHUNK, PROJ_MAIN), dtr_s.reshape(1, 2 * CHUNK, LANES), zc, zs, zr, params,
        nchunks=1, chunk_offset=1, valid=N_META, pos_base=0, name="mixer_meta")[:4]

    nsteps = bp * (seq // CHUNK)
    mix_p, p_conv, p_ssm, p_ret, w_out_b, w_gate_b, w_up_b, w_down_b = _mixer_seq(
        proj_p.reshape(bp, seq, PROJ_MAIN), dtr_p.reshape(bp, seq, LANES), m_conv, m_ssm, m_ret, params,
        nchunks=seq // CHUNK, chunk_offset=0, valid=CHUNK, pos_base=N_META, name="mixer_prompt",
        cast=((w_out[0], nsteps), (w_gate[0], nsteps), (w_up[0], nsteps), (w_down[0], nsteps // 2)))

    conv_t = jnp.transpose(state_conv[0], (1, 0, 2))
    mix_s, s_conv_t, s_ssm, s_ret = _mixer_step(
        proj_s, dtr_s, conv_t, state_ssm[0].reshape(bs, D_SSM, D_STATE),
        state_ret[0].reshape(bs, D_RET, RET_HEADDIM), params, nb=bs)

    h1_p, f_p, h1_s, f_s = _outproj(mix_p.reshape(bp * seq, D_MODEL), mix_s, w_out_b, xp, xs_rows,
                                    post_mix_g, pre_ffn_g, bm=512)
    y_p, y_s = _ffn(f_p, f_s, w_gate_b, w_up_b, w_down_b, h1_p, h1_s, post_ffn_g, bm=1024, bf=512)

    return (y_p.reshape(bp, seq, D_MODEL),
            y_s.reshape(bs, 1, D_MODEL),
            p_conv[None],
            p_ssm.reshape(1, bp, SSM_HEADS, SSM_HEADDIM, D_STATE),
            p_ret.reshape(1, bp, RET_HEADS, RET_HEADDIM, RET_HEADDIM),
            jnp.transpose(s_conv_t, (1, 0, 2))[None],
            s_ssm.reshape(1, bs, SSM_HEADS, SSM_HEADDIM, D_STATE),
            s_ret.reshape(1, bs, RET_HEADS, RET_HEADDIM, RET_HEADDIM))
```

```python
import functools

import numpy as np
import jax
import jax.numpy as jnp
from jax import lax
from jax.experimental import pallas as pl
from jax.experimental.pallas import tpu as pltpu

F32 = jnp.float32
BF16 = jnp.bfloat16

D_MODEL = 2048
N_META = 16
CHUNK = 128
D_SSM = 1024
D_RET = 1024
SSM_HEADDIM = 64
SSM_HEADS = 16
SSM_GROUPS = 2
GROUP_DIM = D_SSM // SSM_GROUPS
D_STATE = 128
CONV_W = 4
CONV_DIM = D_SSM + 2 * SSM_GROUPS * D_STATE
RET_HEADS = 4
RET_HEADDIM = 256
ROPE_BASE = 10000.0
D_FF = 5632
EPS = 1e-6
PAST_LEN = 16384
LOG2E = float(np.log2(np.e))

LANES = 128
SUBLANES = 8
STEP_ROWS = SUBLANES
CONV_PAD = SUBLANES

OFF_Z = 0
OFF_XBC = D_SSM
OFF_Q = OFF_XBC + CONV_DIM
OFF_K = OFF_Q + D_RET
OFF_V = OFF_K + D_RET
OFF_G = OFF_V + D_RET
PROJ_MAIN = OFF_G + D_RET

VMEM_LIMIT = 56 * 1024 * 1024
VMEM_LIMIT_FFN = 60 * 1024 * 1024

RET_LOG_GAMMA = [float(np.log1p(-np.float32(2.0) ** np.float32(-5.0 - h)).astype(np.float32))
                 for h in range(RET_HEADS)]


def _silu(x):
    return x / (1.0 + jnp.exp2(x * (-LOG2E)))


def _softplus(x):
    return jnp.maximum(x, 0.0) + jnp.log1p(jnp.exp(-jnp.abs(x)))


def _rms(x):
    return x * lax.rsqrt(jnp.mean(x * x, axis=-1, keepdims=True) + EPS)


def _split3(x):
    hi = x.astype(BF16)
    r = x - hi.astype(F32)
    mid = r.astype(BF16)
    lo = (r - mid.astype(F32)).astype(BF16)
    return hi, mid, lo


def _dot(a, b):
    return jnp.dot(a, b, preferred_element_type=F32)


def _dot_nt(a, b):
    return lax.dot_general(a, b, (((1,), (1,)), ((), ())), preferred_element_type=F32)


def _dot_tn(a, b):
    return lax.dot_general(a, b, (((0,), (0,)), ((), ())), preferred_element_type=F32)


def _exact_right(x, sel):
    hi, mid, lo = x if isinstance(x, tuple) else _split3(x)
    return _dot(hi, sel) + _dot(mid, sel) + _dot(lo, sel)


def _select_right(x, sel):
    hi = x.astype(BF16)
    lo = (x - hi.astype(F32)).astype(BF16)
    return _dot(hi, sel) + _dot(lo, sel)


def _exact_left(sel, x):
    hi, mid, lo = x if isinstance(x, tuple) else _split3(x)
    return _dot(sel, hi) + _dot(sel, mid) + _dot(sel, lo)


def _exact_tn(x, sel):
    hi, mid, lo = x if isinstance(x, tuple) else _split3(x)
    return _dot_tn(hi, sel) + _dot_tn(mid, sel) + _dot_tn(lo, sel)


def _head_expand():
    r = lax.broadcasted_iota(jnp.int32, (LANES, D_SSM), 0)
    c = lax.broadcasted_iota(jnp.int32, (LANES, D_SSM), 1)
    return (c // SSM_HEADDIM == r).astype(BF16)


NORM_ROWS = 256
DT_ROW = D_SSM + CONV_DIM


def _inproj_kernel(x_ref, xs_ref, g_ref, wt_ref, wdt_ref, o_ref, odt_ref, os_ref, odts_ref, u_ref, us_ref,
                   *, npro, nsplit):
    i = pl.program_id(0)
    j = pl.program_id(1)
    on_last = i == pl.num_programs(0) - 1
    xr = x_ref.shape[0]

    @pl.when(j < npro)
    def _():
        wdt = wdt_ref[...].astype(BF16)
        lane = lax.broadcasted_iota(jnp.int32, (NORM_ROWS, LANES), 1)

        def norm_rows(src_ref, src, dst_ref, dt_ref, dst):
            u = (_rms(src_ref[src, :]) * g_ref[...]).astype(BF16)
            dst_ref[dst, :] = u
            dt_ref[dst, :] = jnp.where(lane < SSM_HEADS, _dot_nt(u, wdt), 0.0)

        def body(t, carry):
            src = pl.ds(pl.multiple_of(t * NORM_ROWS, NORM_ROWS), NORM_ROWS)
            dst = pl.ds(pl.multiple_of(j * xr + t * NORM_ROWS, NORM_ROWS), NORM_ROWS)
            norm_rows(x_ref, src, u_ref, odt_ref, dst)
            return carry
        lax.fori_loop(0, xr // NORM_ROWS, body, 0)

        @pl.when(on_last & (j == 0))
        def _():
            for t in range(xs_ref.shape[0] // NORM_ROWS):
                rows = pl.ds(t * NORM_ROWS, NORM_ROWS)
                norm_rows(xs_ref, rows, us_ref, odts_ref, rows)

    def column_tile(with_side):
        sub = wt_ref.shape[0] // nsplit
        for s in range(nsplit):
            cols = slice(s * sub, (s + 1) * sub)
            w = wt_ref[cols, :].astype(BF16)
            o_ref[:, cols] = _dot_nt(u_ref[...], w).astype(o_ref.dtype)
            if with_side:
                os_ref[:, cols] = _dot_nt(us_ref[...], w)

    pl.when((j >= npro) & on_last)(functools.partial(column_tile, True))
    pl.when((j >= npro) & jnp.logical_not(on_last))(functools.partial(column_tile, False))


def _inproj(x, xs, g, wt, *, bm, xr, bn):
    m = x.shape[0]
    ms = xs.shape[0]
    nm = m // bm
    npro = bm // xr
    assert DT_ROW % bn == 0 and ms % NORM_ROWS == 0

    def wrow(i, j):
        t = jnp.maximum(j - npro, 0)
        skip = jnp.where(t * bn >= DT_ROW, SSM_HEADS // SUBLANES, 0)
        return ((t * (bn // SUBLANES) + skip) * SUBLANES, 0)

    col = lambda j: jnp.maximum(j - npro, 0)
    const = lambda i, j: (0, 0)
    return pl.pallas_call(
        functools.partial(_inproj_kernel, npro=npro, nsplit=2),
        out_shape=(jax.ShapeDtypeStruct((m, PROJ_MAIN), BF16), jax.ShapeDtypeStruct((m, LANES), F32),
                   jax.ShapeDtypeStruct((ms, PROJ_MAIN), F32), jax.ShapeDtypeStruct((ms, LANES), F32)),
        grid=(nm, npro + PROJ_MAIN // bn),
        in_specs=[pl.BlockSpec((xr, D_MODEL), lambda i, j: (i * npro + jnp.minimum(j, npro - 1), 0)),
                  pl.BlockSpec((ms, D_MODEL), const),
                  pl.BlockSpec((1, D_MODEL), const),
                  pl.BlockSpec((pl.Element(bn), pl.Element(D_MODEL)), wrow),
                  pl.BlockSpec((pl.Element(LANES), pl.Element(D_MODEL)), lambda i, j: (DT_ROW, 0))],
        out_specs=(pl.BlockSpec((bm, bn), lambda i, j: (i, col(j))),
                   pl.BlockSpec((bm, LANES), lambda i, j: (i, 0)),
                   pl.BlockSpec((ms, bn), lambda i, j: (0, jnp.where(i == nm - 1, col(j), 0))),
                   pl.BlockSpec((ms, LANES), const)),
        scratch_shapes=[pltpu.VMEM((bm, D_MODEL), BF16), pltpu.VMEM((ms, D_MODEL), BF16)],
        compiler_params=pltpu.CompilerParams(
            dimension_semantics=("arbitrary", "arbitrary"), vmem_limit_bytes=VMEM_LIMIT),
        name="inproj",
    )(x, xs, g, wt, wt)


N_MIXER_IN = 13
N_MIXER_OUT = 4


def _mixer_seq_kernel(*refs, valid, pos_base, ncast):
    ins = refs[:N_MIXER_IN]
    cast_in = refs[N_MIXER_IN:N_MIXER_IN + ncast]
    outs = refs[N_MIXER_IN + ncast:N_MIXER_IN + ncast + N_MIXER_OUT]
    cast_out = refs[N_MIXER_IN + ncast + N_MIXER_OUT:N_MIXER_IN + 2 * ncast + N_MIXER_OUT]
    cbuf_ref, rdec_ref, cdec_ref, trig_ref = refs[N_MIXER_IN + 2 * ncast + N_MIXER_OUT:]
    proj_ref, dtr_ref, conv0_ref, ssm0_ref, ret0_ref = ins[:5]
    params = ins[5:]
    mix_ref, convo_ref, ssmo_ref, reto_ref = outs
    streams = proj_ref.shape[0]
    c = pl.program_id(1)

    @pl.when((pl.program_id(0) == 0) & (c == 0))
    def _():
        _mixer_tables(rdec_ref, cdec_ref, trig_ref, params[-1], valid)

    @pl.when(c == 0)
    def _():
        for s in range(streams):
            cbuf_ref[s, CONV_PAD - (CONV_W - 1):CONV_PAD, :] = conv0_ref[0]
            ssmo_ref[s] = ssm0_ref[0]
            reto_ref[s] = ret0_ref[0]

    for s in range(streams):
        one = pl.ds(s, 1)
        _mixer_chunk(proj_ref.at[one], dtr_ref.at[one], *params,
                     mix_ref.at[one], convo_ref.at[one], ssmo_ref.at[one], reto_ref.at[one],
                     cbuf_ref.at[s], rdec_ref, cdec_ref, trig_ref, valid=valid, pos_base=pos_base)
    for src, dst in zip(cast_in, cast_out):
        dst[...] = src[...].astype(BF16)


def _mixer_tables(rdec_ref, cdec_ref, trig_ref, invf_ref, valid):
    C = CHUNK
    rowf = lax.broadcasted_iota(jnp.int32, (C, 1), 0).astype(F32)
    ri = lax.broadcasted_iota(jnp.int32, (C, C), 0)
    ci = lax.broadcasted_iota(jnp.int32, (C, C), 1)
    diff = (ri - ci).astype(F32)
    rows_l = jnp.broadcast_to(rowf, (C, LANES))
    for h in range(RET_HEADS):
        rdec_ref[h] = jnp.where(ri >= ci, jnp.exp(jnp.maximum(diff, 0.0) * RET_LOG_GAMMA[h]), 0.0)
        cdec_ref[h] = jnp.exp((rows_l + 1.0) * RET_LOG_GAMMA[h])
        cdec_ref[RET_HEADS + h] = jnp.exp((valid - 1.0 - rows_l) * RET_LOG_GAMMA[h])
    row_ang = rowf * invf_ref[...]
    trig_ref[0] = jnp.cos(row_ang)
    trig_ref[1] = jnp.sin(row_ang)


def _mixer_chunk(proj_ref, dtr_ref,
                 convw_ref, convb_ref, dtb_ref, alog_ref, dskip_ref, sg_ref, rg_ref, invf_ref,
                 mix_ref, convo_ref, ssmo_ref, reto_ref,
                 cbuf_ref, rdec_ref, cdec_ref, trig_ref, *, valid, pos_base):
    C = CHUNK
    c = pl.program_id(1)
    rowi = lax.broadcasted_iota(jnp.int32, (C, 1), 0)
    ri = lax.broadcasted_iota(jnp.int32, (C, C), 0)
    ci = lax.broadcasted_iota(jnp.int32, (C, C), 1)
    causal = ri >= ci
    hist = CONV_PAD - (CONV_W - 1)

    xbc_raw = proj_ref[0, :, OFF_XBC:OFF_Q].astype(F32)
    cbuf_ref[CONV_PAD:CONV_PAD + C, :] = xbc_raw
    acc = convb_ref[...] + xbc_raw * convw_ref[CONV_W - 1:CONV_W, :]
    for i in range(CONV_W - 1):
        acc = acc + cbuf_ref[hist + i:hist + i + C, :] * convw_ref[i:i + 1, :]
    xbc = _silu(acc)
    new_prev = cbuf_ref[hist + valid:CONV_PAD + valid, :]
    cbuf_ref[hist:CONV_PAD, :] = new_prev
    convo_ref[0] = new_prev

    xs = xbc[:, :D_SSM]
    bmat = xbc[:, D_SSM:D_SSM + SSM_GROUPS * D_STATE].astype(BF16)
    cmat = xbc[:, D_SSM + SSM_GROUPS * D_STATE:].astype(BF16)

    dt = _softplus(dtr_ref[0] + dtb_ref[...])
    if valid < C:
        dt = jnp.where(rowi < valid, dt, 0.0)
    la = dt * (-jnp.exp(alog_ref[...]))
    tril = causal.astype(BF16)
    triu = (ri <= ci).astype(BF16)
    eye = (ri == ci).astype(BF16)
    la3 = _split3(la)
    lcum = _exact_left(tril, la3)
    lcum_t = _exact_tn(la3, triu)
    dt_t = _exact_tn(dt, eye)
    expand = _head_expand()
    carry_scale = _select_right(jnp.exp(lcum), expand)
    tail_scale = _select_right(jnp.exp(lcum[C - 1:C, :] - lcum) * dt, expand)
    lcum2 = lcum * LOG2E
    lcum2_t = lcum_t * LOG2E

    cbs = [_dot_nt(cmat[:, g * D_STATE:(g + 1) * D_STATE], bmat[:, g * D_STATE:(g + 1) * D_STATE])
           for g in range(SSM_GROUPS)]
    lane = lax.broadcasted_iota(jnp.int32, (C, LANES), 1)
    left = lane < SSM_HEADDIM
    y_intra = []
    for m in range(SSM_HEADS // 2):
        ws = []
        for h in (2 * m, 2 * m + 1):
            seg2 = lcum2[:, h:h + 1] - lcum2_t[h:h + 1, :]
            decay = jnp.exp2(jnp.where(causal, seg2, -jnp.inf))
            ws.append((cbs[h // (SSM_HEADS // SSM_GROUPS)] * decay * dt_t[h:h + 1, :]).astype(BF16))
        xm = xs[:, m * LANES:(m + 1) * LANES]
        xst = jnp.concatenate([jnp.where(left, xm, 0.0), jnp.where(left, 0.0, xm)], axis=0).astype(BF16)
        y_intra.append(_dot(jnp.concatenate(ws, axis=1), xst))
    y = jnp.concatenate(y_intra, axis=1)

    hstate = ssmo_ref[0]
    hb = hstate.astype(BF16)
    y_inter = jnp.concatenate(
        [_dot_nt(cmat[:, g * D_STATE:(g + 1) * D_STATE], hb[g * GROUP_DIM:(g + 1) * GROUP_DIM, :])
         for g in range(SSM_GROUPS)], axis=1)
    y = y + y_inter * carry_scale + dskip_ref[...] * xs

    xw = (xs * tail_scale).astype(BF16)
    upd = jnp.concatenate(
        [_dot_tn(xw[:, g * GROUP_DIM:(g + 1) * GROUP_DIM], bmat[:, g * D_STATE:(g + 1) * D_STATE])
         for g in range(SSM_GROUPS)], axis=0)
    la_tot = _exact_tn(la3, jnp.ones((C, LANES), BF16))
    er = lax.broadcasted_iota(jnp.int32, (D_SSM, LANES), 0)
    ec = lax.broadcasted_iota(jnp.int32, (D_SSM, LANES), 1)
    expand_t = (er // SSM_HEADDIM == ec).astype(BF16)
    chunk_decay = jnp.exp(_exact_left(expand_t, la_tot))
    ssmo_ref[0] = chunk_decay * hstate + upd

    z = proj_ref[0, :, OFF_Z:OFF_XBC].astype(F32)
    y = y * _silu(z)
    y1 = jnp.concatenate([_rms(y[:, g * GROUP_DIM:(g + 1) * GROUP_DIM]) for g in range(SSM_GROUPS)],
                         axis=1) * sg_ref[...]

    ang0 = (pos_base + c * C).astype(F32) * invf_ref[...]
    cos0, sin0 = jnp.cos(ang0), jnp.sin(ang0)
    cos = cos0 * trig_ref[0] - sin0 * trig_ref[1]
    sin = sin0 * trig_ref[0] + cos0 * trig_ref[1]
    kscale = RET_HEADDIM ** -0.5
    cos_k, sin_k = cos * kscale, sin * kscale
    half = RET_HEADDIM // 2
    y2 = []
    for h in range(RET_HEADS):
        lg = RET_LOG_GAMMA[h]
        q1 = proj_ref[0, :, OFF_Q + h * RET_HEADDIM:OFF_Q + h * RET_HEADDIM + half].astype(F32)
        q2 = proj_ref[0, :, OFF_Q + h * RET_HEADDIM + half:OFF_Q + (h + 1) * RET_HEADDIM].astype(F32)
        k1 = proj_ref[0, :, OFF_K + h * RET_HEADDIM:OFF_K + h * RET_HEADDIM + half].astype(F32)
        k2 = proj_ref[0, :, OFF_K + h * RET_HEADDIM + half:OFF_K + (h + 1) * RET_HEADDIM].astype(F32)
        vh = proj_ref[0, :, OFF_V + h * RET_HEADDIM:OFF_V + (h + 1) * RET_HEADDIM].astype(BF16)
        qr = jnp.concatenate([q1 * cos - q2 * sin, q1 * sin + q2 * cos], axis=1)
        kr = jnp.concatenate([k1 * cos_k - k2 * sin_k, k1 * sin_k + k2 * cos_k], axis=1)
        if valid < C:
            kr = jnp.where(rowi < valid, kr, 0.0)
        qb = qr.astype(BF16)
        scores = _dot_nt(qb, kr.astype(BF16)) * rdec_ref[h]
        s_old = reto_ref[0, h * RET_HEADDIM:(h + 1) * RET_HEADDIM, :]
        carry_dec = cdec_ref[h]
        tail_dec = cdec_ref[RET_HEADS + h]
        yr = (_dot(scores.astype(BF16), vh)
              + _dot(qb, s_old.astype(BF16)) * jnp.concatenate([carry_dec, carry_dec], axis=1))
        kw = (kr * jnp.concatenate([tail_dec, tail_dec], axis=1)).astype(BF16)
        reto_ref[0, h * RET_HEADDIM:(h + 1) * RET_HEADDIM, :] = (
            float(np.exp(np.float32(valid * lg))) * s_old + _dot_tn(kw, vh))
        y2.append(_rms(yr))
    gate = proj_ref[0, :, OFF_G:PROJ_MAIN].astype(F32)
    y2 = jnp.concatenate(y2, axis=1) * rg_ref[...] * _silu(gate)

    mix_ref[0, :, :D_SSM] = y1.astype(BF16)
    mix_ref[0, :, D_SSM:] = y2.astype(BF16)


def _mixer_seq(proj, dtr, conv0, ssm0, ret0, params, *, nchunks, chunk_offset, valid, pos_base, name,
               cast=(), streams=1):
    nb = proj.shape[0]
    assert nb % streams == 0
    nsteps = (nb // streams) * nchunks
    row = lambda b, c: (b, c + chunk_offset, 0)
    const3 = lambda b, c: (0, 0, 0)
    const2 = lambda b, c: (0, 0)
    per_b = lambda b, c: (b, 0, 0)
    pspecs = [pl.BlockSpec(p.shape, const2) for p in params]
    cast_specs = []
    for w, nblk in cast:
        assert nsteps % nblk == 0 and w.shape[0] % nblk == 0
        every = nsteps // nblk
        cast_specs.append(pl.BlockSpec((w.shape[0] // nblk, w.shape[1]),
                                       lambda b, c, every=every: ((b * nchunks + c) // every, 0)))
    kern = functools.partial(_mixer_seq_kernel, valid=valid, pos_base=pos_base, ncast=len(cast))
    return pl.pallas_call(
        kern,
        out_shape=(jax.ShapeDtypeStruct((nb, nchunks * CHUNK, D_MODEL), BF16),
                   jax.ShapeDtypeStruct((nb, CONV_W - 1, CONV_DIM), F32),
                   jax.ShapeDtypeStruct((nb, D_SSM, D_STATE), F32),
                   jax.ShapeDtypeStruct((nb, D_RET, RET_HEADDIM), F32))
        + tuple(jax.ShapeDtypeStruct(w.shape, BF16) for w, _ in cast),
        grid=(nb // streams, nchunks),
        in_specs=[pl.BlockSpec((streams, CHUNK, PROJ_MAIN), row),
                  pl.BlockSpec((streams, CHUNK, LANES), row),
                  pl.BlockSpec((1, CONV_W - 1, CONV_DIM), const3),
                  pl.BlockSpec((1, D_SSM, D_STATE), const3),
                  pl.BlockSpec((1, D_RET, RET_HEADDIM), const3)] + pspecs + cast_specs,
        out_specs=(pl.BlockSpec((streams, CHUNK, D_MODEL), lambda b, c: (b, c, 0)),
                   pl.BlockSpec((streams, CONV_W - 1, CONV_DIM), per_b),
                   pl.BlockSpec((streams, D_SSM, D_STATE), per_b),
                   pl.BlockSpec((streams, D_RET, RET_HEADDIM), per_b)) + tuple(cast_specs),
        scratch_shapes=[pltpu.VMEM((streams, CONV_PAD + CHUNK, CONV_DIM), F32),
                        pltpu.VMEM((RET_HEADS, CHUNK, CHUNK), F32),
                        pltpu.VMEM((2 * RET_HEADS, CHUNK, LANES), F32),
                        pltpu.VMEM((2, CHUNK, RET_HEADDIM // 2), F32)],
        compiler_params=pltpu.CompilerParams(
            dimension_semantics=("arbitrary", "arbitrary"), vmem_limit_bytes=VMEM_LIMIT),
        name=name,
    )(proj, dtr, conv0, ssm0, ret0, *params, *[w for w, _ in cast])


def _mixer_step_kernel(proj_ref, dtr_ref, conv_ref, ssm_ref, ret_ref,
                       convw_ref, convb_ref, dtb_ref, alog_ref, dskip_ref, sg_ref, rg_ref, invf_ref,
                       mix_ref, convo_ref, ssmo_ref, reto_ref, cols_ref):
    R = STEP_ROWS
    xbc_raw = proj_ref[:, OFF_XBC:OFF_Q]
    acc = convb_ref[...] + xbc_raw * convw_ref[3:4, :]
    for i in range(CONV_W - 1):
        acc = acc + conv_ref[i] * convw_ref[i:i + 1, :]
    xbc = _silu(acc)
    convo_ref[0] = conv_ref[1]
    convo_ref[1] = conv_ref[2]
    convo_ref[2] = xbc_raw

    xs = xbc[:, :D_SSM]
    bmat = xbc[:, D_SSM:D_SSM + SSM_GROUPS * D_STATE]
    cmat = xbc[:, D_SSM + SSM_GROUPS * D_STATE:]
    dt = _softplus(dtr_ref[...] + dtb_ref[...])
    la = dt * (-jnp.exp(alog_ref[...]))
    expand = _head_expand()
    dt_x = _exact_right(dt, expand)
    decay_x = jnp.exp(_exact_right(la, expand))
    xdt = xs * dt_x

    ang = jnp.float32(PAST_LEN) * invf_ref[...]
    cos = jnp.cos(ang)
    sin = jnp.sin(ang)
    half = RET_HEADDIM // 2
    qs, ks = [], []
    for h in range(RET_HEADS):
        q1 = proj_ref[:, OFF_Q + h * RET_HEADDIM:OFF_Q + h * RET_HEADDIM + half]
        q2 = proj_ref[:, OFF_Q + h * RET_HEADDIM + half:OFF_Q + (h + 1) * RET_HEADDIM]
        k1 = proj_ref[:, OFF_K + h * RET_HEADDIM:OFF_K + h * RET_HEADDIM + half]
        k2 = proj_ref[:, OFF_K + h * RET_HEADDIM + half:OFF_K + (h + 1) * RET_HEADDIM]
        qs += [q1 * cos - q2 * sin, q1 * sin + q2 * cos]
        ks += [(k1 * cos - k2 * sin) * (RET_HEADDIM ** -0.5), (k1 * sin + k2 * cos) * (RET_HEADDIM ** -0.5)]
    qr = jnp.concatenate(qs, axis=1)
    kr = jnp.concatenate(ks, axis=1)
    vv = proj_ref[:, OFF_V:OFF_G]

    allq = jnp.concatenate([decay_x, xdt, kr, qr], axis=1)
    hi = allq.astype(BF16).astype(F32)
    r1 = allq - hi
    mid = r1.astype(BF16).astype(F32)
    lo = (r1 - mid).astype(BF16).astype(F32)
    stack = jnp.concatenate([hi, mid, lo, jnp.zeros_like(hi)], axis=0).astype(BF16)
    krow = lax.broadcasted_iota(jnp.int32, (4 * R, LANES), 0)
    row8 = lax.broadcasted_iota(jnp.int32, (R, 1), 0)
    lane = lax.broadcasted_iota(jnp.int32, (1, LANES), 1)

    y_cols = jnp.zeros((D_SSM, LANES), F32)
    y_ret = jnp.zeros((R, D_RET), F32)
    for r in range(R):
        sel = ((krow % R == r) & (krow < 3 * R)).astype(BF16)
        cols_ref[...] = _dot_tn(stack, sel)
        ycol = []
        for g in range(SSM_GROUPS):
            rows = slice(g * GROUP_DIM, (g + 1) * GROUP_DIM)
            h_old = ssm_ref[r, rows, :]
            h_new = (h_old * cols_ref[g * GROUP_DIM:(g + 1) * GROUP_DIM, :]
                     + cols_ref[D_SSM + g * GROUP_DIM:D_SSM + (g + 1) * GROUP_DIM, :]
                     * bmat[r:r + 1, g * D_STATE:(g + 1) * D_STATE])
            ssmo_ref[r, rows, :] = h_new
            ycol.append(jnp.sum(h_new * cmat[r:r + 1, g * D_STATE:(g + 1) * D_STATE], axis=1, keepdims=True))
        y_cols = jnp.where(lane == r, jnp.concatenate(ycol, axis=0), y_cols)
        yrow = []
        for h in range(RET_HEADS):
            rows = slice(h * RET_HEADDIM, (h + 1) * RET_HEADDIM)
            kcol = cols_ref[2 * D_SSM + h * RET_HEADDIM:2 * D_SSM + (h + 1) * RET_HEADDIM, :]
            qcol = cols_ref[3 * D_SSM + h * RET_HEADDIM:3 * D_SSM + (h + 1) * RET_HEADDIM, :]
            gamma = float(np.exp(np.float32(RET_LOG_GAMMA[h])))
            s_new = (gamma * ret_ref[r, rows, :]
                     + jnp.concatenate([kcol, kcol], axis=1) * vv[r:r + 1, h * RET_HEADDIM:(h + 1) * RET_HEADDIM])
            reto_ref[r, rows, :] = s_new
            yrow.append(jnp.sum(jnp.concatenate([qcol, qcol], axis=1) * s_new, axis=0, keepdims=True))
        y_ret = jnp.where(row8 == r, jnp.concatenate(yrow, axis=1), y_ret)

    y_ssd = y_cols.T[:R, :]
    y = (y_ssd + dskip_ref[...] * xs) * _silu(proj_ref[:, OFF_Z:OFF_XBC])
    y1 = jnp.concatenate([_rms(y[:, g * GROUP_DIM:(g + 1) * GROUP_DIM]) for g in range(SSM_GROUPS)],
                         axis=1) * sg_ref[...]
    y2 = jnp.concatenate([_rms(y_ret[:, h * RET_HEADDIM:(h + 1) * RET_HEADDIM]) for h in range(RET_HEADS)],
                         axis=1) * rg_ref[...] * _silu(proj_ref[:, OFF_G:PROJ_MAIN])
    mix_ref[:, :D_SSM] = y1
    mix_ref[:, D_SSM:] = y2


def _mixer_step(proj, dtr, conv_t, ssm, ret, params, *, nb):
    R = STEP_ROWS
    rows2 = lambda i: (i, 0)
    rows3 = lambda i: (i, 0, 0)
    mid3 = lambda i: (0, i, 0)
    const2 = lambda i: (0, 0)
    pspecs = [pl.BlockSpec(p.shape, const2) for p in params]
    return pl.pallas_call(
        _mixer_step_kernel,
        out_shape=(jax.ShapeDtypeStruct((nb, D_MODEL), F32),
                   jax.ShapeDtypeStruct((CONV_W - 1, nb, CONV_DIM), F32),
                   jax.ShapeDtypeStruct((nb, D_SSM, D_STATE), F32),
                   jax.ShapeDtypeStruct((nb, D_RET, RET_HEADDIM), F32)),
        grid=(nb // R,),
        in_specs=[pl.BlockSpec((R, PROJ_MAIN), rows2),
                  pl.BlockSpec((R, LANES), rows2),
                  pl.BlockSpec((CONV_W - 1, R, CONV_DIM), mid3),
                  pl.BlockSpec((R, D_SSM, D_STATE), rows3),
                  pl.BlockSpec((R, D_RET, RET_HEADDIM), rows3)] + pspecs,
        out_specs=(pl.BlockSpec((R, D_MODEL), rows2),
                   pl.BlockSpec((CONV_W - 1, R, CONV_DIM), mid3),
                   pl.BlockSpec((R, D_SSM, D_STATE), rows3),
                   pl.BlockSpec((R, D_RET, RET_HEADDIM), rows3)),
        scratch_shapes=[pltpu.VMEM((4 * D_SSM, LANES), F32)],
        compiler_params=pltpu.CompilerParams(
            dimension_semantics=("arbitrary",), vmem_limit_bytes=VMEM_LIMIT),
        name="mixer_step",
    )(proj, dtr, conv_t, ssm, ret, *params)


def _outproj_kernel(mix_ref, mixs_ref, w_ref, h_ref, hs_ref, g1_ref, g2_ref, hout_ref, f_ref, houts_ref, fs_ref):
    def rows(mix_r, h_r, hout_r, f_r):
        y = _dot(mix_r[...].astype(BF16), w_ref[...])
        h = h_r[...] + _rms(y) * g1_ref[...]
        hout_r[...] = h
        f_r[...] = (_rms(h) * g2_ref[...]).astype(BF16)

    rows(mix_ref, h_ref, hout_ref, f_ref)

    @pl.when(pl.program_id(0) == pl.num_programs(0) - 1)
    def _():
        rows(mixs_ref, hs_ref, houts_ref, fs_ref)


def _outproj(mix, mixs, w, h, hs, g1, g2, *, bm):
    m = mix.shape[0]
    ms = mixs.shape[0]
    row = lambda i: (i, 0)
    const = lambda i: (0, 0)
    return pl.pallas_call(
        _outproj_kernel,
        out_shape=(jax.ShapeDtypeStruct((m, D_MODEL), F32), jax.ShapeDtypeStruct((m, D_MODEL), BF16),
                   jax.ShapeDtypeStruct((ms, D_MODEL), F32), jax.ShapeDtypeStruct((ms, D_MODEL), BF16)),
        grid=(m // bm,),
        in_specs=[pl.BlockSpec((bm, D_MODEL), row),
                  pl.BlockSpec((ms, D_MODEL), const),
                  pl.BlockSpec((D_MODEL, D_MODEL), const),
                  pl.BlockSpec((bm, D_MODEL), row),
                  pl.BlockSpec((ms, D_MODEL), const),
                  pl.BlockSpec((1, D_MODEL), const),
                  pl.BlockSpec((1, D_MODEL), const)],
        out_specs=(pl.BlockSpec((bm, D_MODEL), row), pl.BlockSpec((bm, D_MODEL), row),
                   pl.BlockSpec((ms, D_MODEL), const), pl.BlockSpec((ms, D_MODEL), const)),
        compiler_params=pltpu.CompilerParams(
            dimension_semantics=("arbitrary",), vmem_limit_bytes=VMEM_LIMIT),
        name="outproj",
    )(mix, mixs, w, h, hs, g1, g2)


FFN_SPLIT = 2


def _ffn_kernel(f_ref, fs_ref, wg_ref, wu_ref, wd_ref, h_hbm, hs_ref, g_ref, o_ref, os_ref, acc_ref, hsem):
    i = pl.program_id(0)
    j = pl.program_id(1)
    nj = pl.num_programs(1)
    on_last = i == pl.num_programs(0) - 1
    bm = o_ref.shape[0]

    def h_copy(tile):
        return pltpu.make_async_copy(h_hbm.at[pl.ds(pl.multiple_of(tile * bm, bm), bm), :], o_ref, hsem)

    def finish_rows():
        for t in range(bm // NORM_ROWS):
            rows = slice(t * NORM_ROWS, (t + 1) * NORM_ROWS)
            o_ref[rows, :] = o_ref[rows, :] + _rms(acc_ref[rows, :]) * g_ref[...]
            acc_ref[rows, :] = jnp.zeros((NORM_ROWS, acc_ref.shape[1]), F32)

    @pl.when((i == 0) & (j == 0))
    def _():
        acc_ref[...] = jnp.zeros_like(acc_ref)

    @pl.when(j == 1)
    def _():
        h_copy(i).start()

    def ff_tile(with_side, finish_prev):
        fsub = wg_ref.shape[1] // FFN_SPLIT
        nsub = acc_ref.shape[1] // FFN_SPLIT
        if finish_prev:
            h_copy(i - 1).wait()
            finish_rows()
            if with_side:
                os_ref[...] = jnp.zeros_like(os_ref)

        def through(f, acc_r, s):
            ff = slice(s * fsub, (s + 1) * fsub)
            a = (_silu(_dot(f, wg_ref[:, ff])) * _dot(f, wu_ref[:, ff])).astype(BF16)
            for n in range(FFN_SPLIT):
                nn = slice(n * nsub, (n + 1) * nsub)
                acc_r[:, nn] += _dot(a, wd_ref[ff, nn])

        for s in range(FFN_SPLIT):
            through(f_ref[...], acc_ref, s)
            if with_side:
                through(fs_ref[...], os_ref, s)

    first = (j == 0) & (i > 0)
    rest = jnp.logical_not(first)
    not_last = jnp.logical_not(on_last)
    pl.when(first & on_last)(functools.partial(ff_tile, True, True))
    pl.when(first & not_last)(functools.partial(ff_tile, False, True))
    pl.when(rest & on_last)(functools.partial(ff_tile, True, False))
    pl.when(rest & not_last)(functools.partial(ff_tile, False, False))

    @pl.when(on_last & (j == nj - 1))
    def _():
        h_copy(i).wait()
        finish_rows()
        os_ref[...] = hs_ref[...] + _rms(os_ref[...]) * g_ref[...]


def _ffn(f, fs, wg, wu, wd, h, hs, g, *, bm, bf):
    m = f.shape[0]
    ms = fs.shape[0]
    assert m // bm > 1 and D_FF // bf > 1
    const = lambda i, j: (0, 0)
    once = pl.Buffered(1)
    return pl.pallas_call(
        _ffn_kernel,
        out_shape=(jax.ShapeDtypeStruct((m, D_MODEL), F32), jax.ShapeDtypeStruct((ms, D_MODEL), F32)),
        grid=(m // bm, D_FF // bf),
        in_specs=[pl.BlockSpec((bm, D_MODEL), lambda i, j: (i, 0)),
                  pl.BlockSpec((ms, D_MODEL), const, pipeline_mode=once),
                  pl.BlockSpec((D_MODEL, bf), lambda i, j: (0, j)),
                  pl.BlockSpec((D_MODEL, bf), lambda i, j: (0, j)),
                  pl.BlockSpec((bf, D_MODEL), lambda i, j: (j, 0)),
                  pl.BlockSpec(memory_space=pl.ANY),
                  pl.BlockSpec((ms, D_MODEL), const, pipeline_mode=once),
                  pl.BlockSpec((1, D_MODEL), const)],
        out_specs=(pl.BlockSpec((bm, D_MODEL), lambda i, j: (jnp.where((j == 0) & (i > 0), i - 1, i), 0)),
                   pl.BlockSpec((ms, D_MODEL), const)),
        scratch_shapes=[pltpu.VMEM((bm, D_MODEL), F32), pltpu.SemaphoreType.DMA],
        compiler_params=pltpu.CompilerParams(
            dimension_semantics=("arbitrary", "arbitrary"), vmem_limit_bytes=VMEM_LIMIT_FFN),
        name="ffn",
    )(f, fs, wg, wu, wd, h, hs, g)


def kernel(x_prompt, x_sample, state_conv, state_ssm, state_ret, meta_tokens, pre_mix_g, post_mix_g,
           pre_ffn_g, post_ffn_g, w_in, conv_w, conv_b, dt_bias, a_log, d_skip, ssm_norm_g, ret_norm_g,
           w_out, w_gate, w_up, w_down):
    bp, seq = x_prompt.shape[:2]
    bs = x_sample.shape[0]
    assert w_in.shape[0] == 1 and x_sample.shape[1] == 1 and seq % CHUNK == 0 and bs == CHUNK

    w_in_t = jnp.swapaxes(w_in[0], 0, 1)
    pad16 = lambda v: jnp.pad(v, ((0, 0), (0, LANES - SSM_HEADS)))
    inv_freq = (ROPE_BASE ** (-jnp.arange(RET_HEADDIM // 2, dtype=F32) / (RET_HEADDIM // 2)))[None, :]
    params = (conv_w[0], conv_b, pad16(dt_bias), pad16(a_log),
              jnp.repeat(d_skip, SSM_HEADDIM, axis=1), ssm_norm_g, ret_norm_g, inv_freq)

    xp = x_prompt.reshape(bp * seq, D_MODEL)
    xs_rows = x_sample.reshape(bs, D_MODEL)
    x_small = jnp.concatenate(
        [xs_rows, meta_tokens.astype(F32), jnp.zeros((CHUNK - N_META, D_MODEL), F32)], axis=0)
    proj_p, dtr_p, proj_s, dtr_s = _inproj(xp, x_small, pre_mix_g, w_in_t, bm=2048, xr=1024, bn=512)

    zc = jnp.zeros((1, CONV_W - 1, CONV_DIM), F32)
    zs = jnp.zeros((1, D_SSM, D_STATE), F32)
    zr = jnp.zeros((1, D_RET, RET_HEADDIM), F32)
    _, m_conv, m_ssm, m_ret = _mixer_seq(
        proj_s.reshape(1, 2 * CHUNK, PROJ_MAIN), dtr_s.reshape(1, 2 * CHUNK, LANES), zc, zs, zr, params,
        nchunks=1, chunk_offset=1, valid=N_META, pos_base=0, name="mixer_meta")[:4]

    streams = 2 if bp % 2 == 0 else 1
    nsteps = (bp // streams) * (seq // CHUNK)
    mix_p, p_conv, p_ssm, p_ret, w_out_b, w_gate_b, w_up_b, w_down_b = _mixer_seq(
        proj_p.reshape(bp, seq, PROJ_MAIN), dtr_p.reshape(bp, seq, LANES), m_conv, m_ssm, m_ret, params,
        nchunks=seq // CHUNK, chunk_offset=0, valid=CHUNK, pos_base=N_META, name="mixer_prompt",
        cast=((w_out[0], nsteps), (w_gate[0], nsteps), (w_up[0], nsteps), (w_down[0], nsteps // 2)),
        streams=streams)

    conv_t = jnp.transpose(state_conv[0], (1, 0, 2))
    mix_s, s_conv_t, s_ssm, s_ret = _mixer_step(
        proj_s, dtr_s, conv_t, state_ssm[0].reshape(bs, D_SSM, D_STATE),
        state_ret[0].reshape(bs, D_RET, RET_HEADDIM), params, nb=bs)

    h1_p, f_p, h1_s, f_s = _outproj(mix_p.reshape(bp * seq, D_MODEL), mix_s, w_out_b, xp, xs_rows,
                                    post_mix_g, pre_ffn_g, bm=512)
    y_p, y_s = _ffn(f_p, f_s, w_gate_b, w_up_b, w_down_b, h1_p, h1_s, post_ffn_g, bm=1024, bf=512)

    return (y_p.reshape(bp, seq, D_MODEL),
            y_s.reshape(bs, 1, D_MODEL),
            p_conv[None],
            p_ssm.reshape(1, bp, SSM_HEADS, SSM_HEADDIM, D_STATE),
            p_ret.reshape(1, bp, RET_HEADS, RET_HEADDIM, RET_HEADDIM),
            jnp.transpose(s_conv_t, (1, 0, 2))[None],
            s_ssm.reshape(1, bs, SSM_HEADS, SSM_HEADDIM, D_STATE),
            s_ret.reshape(1, bs, RET_HEADS, RET_HEADDIM, RET_HEADDIM))
```

```python
import functools

import numpy as np
import jax
import jax.numpy as jnp
from jax import lax
from jax.experimental import pallas as pl
from jax.experimental.pallas import tpu as pltpu

F32 = jnp.float32
BF16 = jnp.bfloat16

D_MODEL = 2048
N_META = 16
CHUNK = 128
D_SSM = 1024
D_RET = 1024
SSM_HEADDIM = 64
SSM_HEADS = 16
SSM_GROUPS = 2
GROUP_DIM = D_SSM // SSM_GROUPS
D_STATE = 128
CONV_W = 4
CONV_DIM = D_SSM + 2 * SSM_GROUPS * D_STATE
RET_HEADS = 4
RET_HEADDIM = 256
ROPE_BASE = 10000.0
D_FF = 5632
EPS = 1e-6
PAST_LEN = 16384
LOG2E = float(np.log2(np.e))

LANES = 128
SUBLANES = 8
STEP_ROWS = SUBLANES
CONV_PAD = SUBLANES

OFF_Z = 0
OFF_XBC = D_SSM
OFF_Q = OFF_XBC + CONV_DIM
OFF_K = OFF_Q + D_RET
OFF_V = OFF_K + D_RET
OFF_G = OFF_V + D_RET
PROJ_MAIN = OFF_G + D_RET

VMEM_LIMIT = 56 * 1024 * 1024
VMEM_LIMIT_FFN = 60 * 1024 * 1024

INPROJ_ROWS = 2048
INPROJ_NORM_ROWS = 1024
INPROJ_COLS = 512
OUTPROJ_ROWS = 512
FFN_ROWS = 1024
FFN_COLS = 512

RET_LOG_GAMMA = [float(np.log1p(-np.float32(2.0) ** np.float32(-5.0 - h)).astype(np.float32))
                 for h in range(RET_HEADS)]


def _silu(x):
    return x / (1.0 + jnp.exp2(x * (-LOG2E)))


def _softplus(x):
    return jnp.maximum(x, 0.0) + jnp.log1p(jnp.exp(-jnp.abs(x)))


def _rms(x):
    return x * lax.rsqrt(jnp.mean(x * x, axis=-1, keepdims=True) + EPS)


def _split3(x):
    hi = x.astype(BF16)
    r = x - hi.astype(F32)
    mid = r.astype(BF16)
    lo = (r - mid.astype(F32)).astype(BF16)
    return hi, mid, lo


def _dot(a, b):
    return jnp.dot(a, b, preferred_element_type=F32)


def _dot_nt(a, b):
    return lax.dot_general(a, b, (((1,), (1,)), ((), ())), preferred_element_type=F32)


def _dot_tn(a, b):
    return lax.dot_general(a, b, (((0,), (0,)), ((), ())), preferred_element_type=F32)


def _exact_right(x, sel):
    hi, mid, lo = x if isinstance(x, tuple) else _split3(x)
    return _dot(hi, sel) + _dot(mid, sel) + _dot(lo, sel)


def _select_right(x, sel):
    hi = x.astype(BF16)
    lo = (x - hi.astype(F32)).astype(BF16)
    return _dot(hi, sel) + _dot(lo, sel)


def _exact_left(sel, x):
    hi, mid, lo = x if isinstance(x, tuple) else _split3(x)
    return _dot(sel, hi) + _dot(sel, mid) + _dot(sel, lo)


def _exact_tn(x, sel):
    hi, mid, lo = x if isinstance(x, tuple) else _split3(x)
    return _dot_tn(hi, sel) + _dot_tn(mid, sel) + _dot_tn(lo, sel)


def _head_expand():
    r = lax.broadcasted_iota(jnp.int32, (LANES, D_SSM), 0)
    c = lax.broadcasted_iota(jnp.int32, (LANES, D_SSM), 1)
    return (c // SSM_HEADDIM == r).astype(BF16)


NORM_ROWS = 256
DT_ROW = D_SSM + CONV_DIM


def _inproj_kernel(x_ref, xs_ref, g_ref, wt_ref, wdt_ref, o_ref, odt_ref, os_ref, odts_ref, u_ref, us_ref,
                   *, npro, nsplit):
    i = pl.program_id(0)
    j = pl.program_id(1)
    on_last = i == pl.num_programs(0) - 1
    xr = x_ref.shape[0]

    @pl.when(j < npro)
    def _():
        wdt = wdt_ref[...].astype(BF16)
        lane = lax.broadcasted_iota(jnp.int32, (NORM_ROWS, LANES), 1)

        def norm_rows(src_ref, src, dst_ref, dt_ref, dst):
            u = (_rms(src_ref[src, :]) * g_ref[...]).astype(BF16)
            dst_ref[dst, :] = u
            dt_ref[dst, :] = jnp.where(lane < SSM_HEADS, _dot_nt(u, wdt), 0.0)

        def body(t, carry):
            src = pl.ds(pl.multiple_of(t * NORM_ROWS, NORM_ROWS), NORM_ROWS)
            dst = pl.ds(pl.multiple_of(j * xr + t * NORM_ROWS, NORM_ROWS), NORM_ROWS)
            norm_rows(x_ref, src, u_ref, odt_ref, dst)
            return carry
        lax.fori_loop(0, xr // NORM_ROWS, body, 0)

        @pl.when(on_last & (j == 0))
        def _():
            for t in range(xs_ref.shape[0] // NORM_ROWS):
                rows = pl.ds(t * NORM_ROWS, NORM_ROWS)
                norm_rows(xs_ref, rows, us_ref, odts_ref, rows)

    def column_tile(with_side):
        sub = wt_ref.shape[0] // nsplit
        for s in range(nsplit):
            cols = slice(s * sub, (s + 1) * sub)
            w = wt_ref[cols, :].astype(BF16)
            o_ref[:, cols] = _dot_nt(u_ref[...], w).astype(o_ref.dtype)
            if with_side:
                os_ref[:, cols] = _dot_nt(us_ref[...], w)

    pl.when((j >= npro) & on_last)(functools.partial(column_tile, True))
    pl.when((j >= npro) & jnp.logical_not(on_last))(functools.partial(column_tile, False))


def _inproj(x, xs, g, wt, *, bm, xr, bn):
    m = x.shape[0]
    ms = xs.shape[0]
    nm = m // bm
    npro = bm // xr
    assert DT_ROW % bn == 0 and ms % NORM_ROWS == 0

    def wrow(i, j):
        t = jnp.maximum(j - npro, 0)
        skip = jnp.where(t * bn >= DT_ROW, SSM_HEADS // SUBLANES, 0)
        return ((t * (bn // SUBLANES) + skip) * SUBLANES, 0)

    col = lambda j: jnp.maximum(j - npro, 0)
    const = lambda i, j: (0, 0)
    return pl.pallas_call(
        functools.partial(_inproj_kernel, npro=npro, nsplit=2),
        out_shape=(jax.ShapeDtypeStruct((m, PROJ_MAIN), BF16), jax.ShapeDtypeStruct((m, LANES), F32),
                   jax.ShapeDtypeStruct((ms, PROJ_MAIN), F32), jax.ShapeDtypeStruct((ms, LANES), F32)),
        grid=(nm, npro + PROJ_MAIN // bn),
        in_specs=[pl.BlockSpec((xr, D_MODEL), lambda i, j: (i * npro + jnp.minimum(j, npro - 1), 0)),
                  pl.BlockSpec((ms, D_MODEL), const),
                  pl.BlockSpec((1, D_MODEL), const),
                  pl.BlockSpec((pl.Element(bn), pl.Element(D_MODEL)), wrow),
                  pl.BlockSpec((pl.Element(LANES), pl.Element(D_MODEL)), lambda i, j: (DT_ROW, 0))],
        out_specs=(pl.BlockSpec((bm, bn), lambda i, j: (i, col(j))),
                   pl.BlockSpec((bm, LANES), lambda i, j: (i, 0)),
                   pl.BlockSpec((ms, bn), lambda i, j: (0, jnp.where(i == nm - 1, col(j), 0))),
                   pl.BlockSpec((ms, LANES), const)),
        scratch_shapes=[pltpu.VMEM((bm, D_MODEL), BF16), pltpu.VMEM((ms, D_MODEL), BF16)],
        compiler_params=pltpu.CompilerParams(
            dimension_semantics=("arbitrary", "arbitrary"), vmem_limit_bytes=VMEM_LIMIT),
        name="inproj",
    )(x, xs, g, wt, wt)


N_MIXER_IN = 13
N_MIXER_INIT = 3
N_MIXER_OUT = 4


def _mixer_seq_kernel(*refs, valid, pos_base, ncast, has_init):
    n_in = N_MIXER_IN if has_init else N_MIXER_IN - N_MIXER_INIT
    ins = refs[:n_in]
    cast_in = refs[n_in:n_in + ncast]
    outs = refs[n_in + ncast:n_in + ncast + N_MIXER_OUT]
    cast_out = refs[n_in + ncast + N_MIXER_OUT:n_in + 2 * ncast + N_MIXER_OUT]
    cbuf_ref, rdec_ref, cdec_ref, trig_ref = refs[n_in + 2 * ncast + N_MIXER_OUT:]
    proj_ref, dtr_ref = ins[:2]
    init = ins[2:2 + N_MIXER_INIT] if has_init else None
    params = ins[2 + N_MIXER_INIT:] if has_init else ins[2:]
    mix_ref, convo_ref, ssmo_ref, reto_ref = outs
    streams = proj_ref.shape[0]
    c = pl.program_id(1)

    @pl.when((pl.program_id(0) == 0) & (c == 0))
    def _():
        _mixer_tables(rdec_ref, cdec_ref, trig_ref, params[-1], valid)

    @pl.when(c == 0)
    def _():
        for s in range(streams):
            if init is None:
                cbuf_ref[s, CONV_PAD - (CONV_W - 1):CONV_PAD, :] = jnp.zeros((CONV_W - 1, CONV_DIM), F32)
                ssmo_ref[s] = jnp.zeros(ssmo_ref.shape[1:], F32)
                reto_ref[s] = jnp.zeros(reto_ref.shape[1:], F32)
            else:
                conv0_ref, ssm0_ref, ret0_ref = init
                cbuf_ref[s, CONV_PAD - (CONV_W - 1):CONV_PAD, :] = conv0_ref[0]
                ssmo_ref[s] = ssm0_ref[0]
                reto_ref[s] = ret0_ref[0]

    for s in range(streams):
        one = pl.ds(s, 1)
        _mixer_chunk(proj_ref.at[one], dtr_ref.at[one], *params,
                     mix_ref.at[one], convo_ref.at[one], ssmo_ref.at[one], reto_ref.at[one],
                     cbuf_ref.at[s], rdec_ref, cdec_ref, trig_ref, valid=valid, pos_base=pos_base)
    for src, dst in zip(cast_in, cast_out):
        dst[...] = src[...].astype(BF16)


def _mixer_tables(rdec_ref, cdec_ref, trig_ref, invf_ref, valid):
    C = CHUNK
    rowf = lax.broadcasted_iota(jnp.int32, (C, 1), 0).astype(F32)
    ri = lax.broadcasted_iota(jnp.int32, (C, C), 0)
    ci = lax.broadcasted_iota(jnp.int32, (C, C), 1)
    diff = (ri - ci).astype(F32)
    rows_l = jnp.broadcast_to(rowf, (C, LANES))
    for h in range(RET_HEADS):
        rdec_ref[h] = jnp.where(ri >= ci, jnp.exp(jnp.maximum(diff, 0.0) * RET_LOG_GAMMA[h]), 0.0)
        cdec_ref[h] = jnp.exp((rows_l + 1.0) * RET_LOG_GAMMA[h])
        cdec_ref[RET_HEADS + h] = jnp.exp((valid - 1.0 - rows_l) * RET_LOG_GAMMA[h])
    row_ang = rowf * invf_ref[...]
    trig_ref[0] = jnp.cos(row_ang)
    trig_ref[1] = jnp.sin(row_ang)


def _mixer_chunk(proj_ref, dtr_ref,
                 convw_ref, convb_ref, dtb_ref, alog_ref, dskip_ref, sg_ref, rg_ref, invf_ref,
                 mix_ref, convo_ref, ssmo_ref, reto_ref,
                 cbuf_ref, rdec_ref, cdec_ref, trig_ref, *, valid, pos_base):
    C = CHUNK
    c = pl.program_id(1)
    rowi = lax.broadcasted_iota(jnp.int32, (C, 1), 0)
    ri = lax.broadcasted_iota(jnp.int32, (C, C), 0)
    ci = lax.broadcasted_iota(jnp.int32, (C, C), 1)
    causal = ri >= ci
    hist = CONV_PAD - (CONV_W - 1)

    xbc_raw = proj_ref[0, :, OFF_XBC:OFF_Q].astype(F32)
    cbuf_ref[CONV_PAD:CONV_PAD + C, :] = xbc_raw
    acc = convb_ref[...] + xbc_raw * convw_ref[CONV_W - 1:CONV_W, :]
    for i in range(CONV_W - 1):
        acc = acc + cbuf_ref[hist + i:hist + i + C, :] * convw_ref[i:i + 1, :]
    xbc = _silu(acc)
    new_prev = cbuf_ref[hist + valid:CONV_PAD + valid, :]
    cbuf_ref[hist:CONV_PAD, :] = new_prev
    convo_ref[0] = new_prev

    xs = xbc[:, :D_SSM]
    bmat = xbc[:, D_SSM:D_SSM + SSM_GROUPS * D_STATE].astype(BF16)
    cmat = xbc[:, D_SSM + SSM_GROUPS * D_STATE:].astype(BF16)

    dt = _softplus(dtr_ref[0] + dtb_ref[...])
    if valid < C:
        dt = jnp.where(rowi < valid, dt, 0.0)
    la = dt * (-jnp.exp(alog_ref[...]))
    tril = causal.astype(BF16)
    triu = (ri <= ci).astype(BF16)
    eye = (ri == ci).astype(BF16)
    la3 = _split3(la)
    lcum = _exact_left(tril, la3)
    lcum_t = _exact_tn(la3, triu)
    dt_t = _exact_tn(dt, eye)
    expand = _head_expand()
    carry_scale = _select_right(jnp.exp(lcum), expand)
    tail_scale = _select_right(jnp.exp(lcum[C - 1:C, :] - lcum) * dt, expand)
    lcum2 = lcum * LOG2E
    lcum2_t = lcum_t * LOG2E

    cbs = [_dot_nt(cmat[:, g * D_STATE:(g + 1) * D_STATE], bmat[:, g * D_STATE:(g + 1) * D_STATE])
           for g in range(SSM_GROUPS)]
    lane = lax.broadcasted_iota(jnp.int32, (C, LANES), 1)
    left = lane < SSM_HEADDIM
    y_intra = []
    for m in range(SSM_HEADS // 2):
        ws = []
        for h in (2 * m, 2 * m + 1):
            seg2 = lcum2[:, h:h + 1] - lcum2_t[h:h + 1, :]
            decay = jnp.exp2(jnp.where(causal, seg2, -jnp.inf))
            ws.append((cbs[h // (SSM_HEADS // SSM_GROUPS)] * decay * dt_t[h:h + 1, :]).astype(BF16))
        xm = xs[:, m * LANES:(m + 1) * LANES]
        xst = jnp.concatenate([jnp.where(left, xm, 0.0), jnp.where(left, 0.0, xm)], axis=0).astype(BF16)
        y_intra.append(_dot(jnp.concatenate(ws, axis=1), xst))
    y = jnp.concatenate(y_intra, axis=1)

    hstate = ssmo_ref[0]
    hb = hstate.astype(BF16)
    y_inter = jnp.concatenate(
        [_dot_nt(cmat[:, g * D_STATE:(g + 1) * D_STATE], hb[g * GROUP_DIM:(g + 1) * GROUP_DIM, :])
         for g in range(SSM_GROUPS)], axis=1)
    y = y + y_inter * carry_scale + dskip_ref[...] * xs

    xw = (xs * tail_scale).astype(BF16)
    upd = jnp.concatenate(
        [_dot_tn(xw[:, g * GROUP_DIM:(g + 1) * GROUP_DIM], bmat[:, g * D_STATE:(g + 1) * D_STATE])
         for g in range(SSM_GROUPS)], axis=0)
    la_tot = _exact_tn(la3, jnp.ones((C, LANES), BF16))
    er = lax.broadcasted_iota(jnp.int32, (D_SSM, LANES), 0)
    ec = lax.broadcasted_iota(jnp.int32, (D_SSM, LANES), 1)
    expand_t = (er // SSM_HEADDIM == ec).astype(BF16)
    chunk_decay = jnp.exp(_exact_left(expand_t, la_tot))
    ssmo_ref[0] = chunk_decay * hstate + upd

    z = proj_ref[0, :, OFF_Z:OFF_XBC].astype(F32)
    y = y * _silu(z)
    y1 = jnp.concatenate([_rms(y[:, g * GROUP_DIM:(g + 1) * GROUP_DIM]) for g in range(SSM_GROUPS)],
                         axis=1) * sg_ref[...]

    ang0 = (pos_base + c * C).astype(F32) * invf_ref[...]
    cos0, sin0 = jnp.cos(ang0), jnp.sin(ang0)
    cos = cos0 * trig_ref[0] - sin0 * trig_ref[1]
    sin = sin0 * trig_ref[0] + cos0 * trig_ref[1]
    kscale = RET_HEADDIM ** -0.5
    cos_k, sin_k = cos * kscale, sin * kscale
    half = RET_HEADDIM // 2
    y2 = []
    for h in range(RET_HEADS):
        lg = RET_LOG_GAMMA[h]
        q1 = proj_ref[0, :, OFF_Q + h * RET_HEADDIM:OFF_Q + h * RET_HEADDIM + half].astype(F32)
        q2 = proj_ref[0, :, OFF_Q + h * RET_HEADDIM + half:OFF_Q + (h + 1) * RET_HEADDIM].astype(F32)
        k1 = proj_ref[0, :, OFF_K + h * RET_HEADDIM:OFF_K + h * RET_HEADDIM + half].astype(F32)
        k2 = proj_ref[0, :, OFF_K + h * RET_HEADDIM + half:OFF_K + (h + 1) * RET_HEADDIM].astype(F32)
        vh = proj_ref[0, :, OFF_V + h * RET_HEADDIM:OFF_V + (h + 1) * RET_HEADDIM].astype(BF16)
        qr = jnp.concatenate([q1 * cos - q2 * sin, q1 * sin + q2 * cos], axis=1)
        kr = jnp.concatenate([k1 * cos_k - k2 * sin_k, k1 * sin_k + k2 * cos_k], axis=1)
        if valid < C:
            kr = jnp.where(rowi < valid, kr, 0.0)
        qb = qr.astype(BF16)
        scores = _dot_nt(qb, kr.astype(BF16)) * rdec_ref[h]
        s_old = reto_ref[0, h * RET_HEADDIM:(h + 1) * RET_HEADDIM, :]
        carry_dec = cdec_ref[h]
        tail_dec = cdec_ref[RET_HEADS + h]
        yr = (_dot(scores.astype(BF16), vh)
              + _dot(qb, s_old.astype(BF16)) * jnp.concatenate([carry_dec, carry_dec], axis=1))
        kw = (kr * jnp.concatenate([tail_dec, tail_dec], axis=1)).astype(BF16)
        reto_ref[0, h * RET_HEADDIM:(h + 1) * RET_HEADDIM, :] = (
            float(np.exp(np.float32(valid * lg))) * s_old + _dot_tn(kw, vh))
        y2.append(_rms(yr))
    gate = proj_ref[0, :, OFF_G:PROJ_MAIN].astype(F32)
    y2 = jnp.concatenate(y2, axis=1) * rg_ref[...] * _silu(gate)

    mix_ref[0, :, :D_SSM] = y1.astype(BF16)
    mix_ref[0, :, D_SSM:] = y2.astype(BF16)


def _mixer_seq(proj, dtr, init, params, *, nchunks, chunk_offset, valid, pos_base, name,
               cast=(), streams=1):
    nb = proj.shape[0]
    assert nb % streams == 0
    nsteps = (nb // streams) * nchunks
    row = lambda b, c: (b, c + chunk_offset, 0)
    const3 = lambda b, c: (0, 0, 0)
    const2 = lambda b, c: (0, 0)
    per_b = lambda b, c: (b, 0, 0)
    pspecs = [pl.BlockSpec(p.shape, const2) for p in params]
    cast_specs = []
    for w, nblk in cast:
        assert nsteps % nblk == 0 and w.shape[0] % nblk == 0
        every = nsteps // nblk
        cast_specs.append(pl.BlockSpec((w.shape[0] // nblk, w.shape[1]),
                                       lambda b, c, every=every: ((b * nchunks + c) // every, 0)))
    init = tuple(init or ())
    assert len(init) in (0, N_MIXER_INIT)
    init_specs = [pl.BlockSpec((1,) + a.shape[1:], const3) for a in init]
    kern = functools.partial(_mixer_seq_kernel, valid=valid, pos_base=pos_base, ncast=len(cast),
                             has_init=bool(init))
    return pl.pallas_call(
        kern,
        out_shape=(jax.ShapeDtypeStruct((nb, nchunks * CHUNK, D_MODEL), BF16),
                   jax.ShapeDtypeStruct((nb, CONV_W - 1, CONV_DIM), F32),
                   jax.ShapeDtypeStruct((nb, D_SSM, D_STATE), F32),
                   jax.ShapeDtypeStruct((nb, D_RET, RET_HEADDIM), F32))
        + tuple(jax.ShapeDtypeStruct(w.shape, BF16) for w, _ in cast),
        grid=(nb // streams, nchunks),
        in_specs=[pl.BlockSpec((streams, CHUNK, PROJ_MAIN), row),
                  pl.BlockSpec((streams, CHUNK, LANES), row)] + init_specs + pspecs + cast_specs,
        out_specs=(pl.BlockSpec((streams, CHUNK, D_MODEL), lambda b, c: (b, c, 0)),
                   pl.BlockSpec((streams, CONV_W - 1, CONV_DIM), per_b),
                   pl.BlockSpec((streams, D_SSM, D_STATE), per_b),
                   pl.BlockSpec((streams, D_RET, RET_HEADDIM), per_b)) + tuple(cast_specs),
        scratch_shapes=[pltpu.VMEM((streams, CONV_PAD + CHUNK, CONV_DIM), F32),
                        pltpu.VMEM((RET_HEADS, CHUNK, CHUNK), F32),
                        pltpu.VMEM((2 * RET_HEADS, CHUNK, LANES), F32),
                        pltpu.VMEM((2, CHUNK, RET_HEADDIM // 2), F32)],
        compiler_params=pltpu.CompilerParams(
            dimension_semantics=("arbitrary", "arbitrary"), vmem_limit_bytes=VMEM_LIMIT),
        name=name,
    )(proj, dtr, *init, *params, *[w for w, _ in cast])


def _mixer_step_kernel(proj_ref, dtr_ref, conv_ref, ssm_ref, ret_ref,
                       convw_ref, convb_ref, dtb_ref, alog_ref, dskip_ref, sg_ref, rg_ref, invf_ref,
                       mix_ref, convo_ref, ssmo_ref, reto_ref, cols_ref):
    R = STEP_ROWS
    xbc_raw = proj_ref[:, OFF_XBC:OFF_Q]
    acc = convb_ref[...] + xbc_raw * convw_ref[CONV_W - 1:CONV_W, :]
    for i in range(CONV_W - 1):
        acc = acc + conv_ref[i] * convw_ref[i:i + 1, :]
    xbc = _silu(acc)
    convo_ref[0] = conv_ref[1]
    convo_ref[1] = conv_ref[2]
    convo_ref[2] = xbc_raw

    xs = xbc[:, :D_SSM]
    bmat = xbc[:, D_SSM:D_SSM + SSM_GROUPS * D_STATE]
    cmat = xbc[:, D_SSM + SSM_GROUPS * D_STATE:]
    dt = _softplus(dtr_ref[...] + dtb_ref[...])
    la = dt * (-jnp.exp(alog_ref[...]))
    expand = _head_expand()
    dt_x = _exact_right(dt, expand)
    decay_x = jnp.exp(_exact_right(la, expand))
    xdt = xs * dt_x

    ang = jnp.float32(PAST_LEN) * invf_ref[...]
    cos = jnp.cos(ang)
    sin = jnp.sin(ang)
    half = RET_HEADDIM // 2
    qs, ks = [], []
    for h in range(RET_HEADS):
        q1 = proj_ref[:, OFF_Q + h * RET_HEADDIM:OFF_Q + h * RET_HEADDIM + half]
        q2 = proj_ref[:, OFF_Q + h * RET_HEADDIM + half:OFF_Q + (h + 1) * RET_HEADDIM]
        k1 = proj_ref[:, OFF_K + h * RET_HEADDIM:OFF_K + h * RET_HEADDIM + half]
        k2 = proj_ref[:, OFF_K + h * RET_HEADDIM + half:OFF_K + (h + 1) * RET_HEADDIM]
        qs += [q1 * cos - q2 * sin, q1 * sin + q2 * cos]
        ks += [(k1 * cos - k2 * sin) * (RET_HEADDIM ** -0.5), (k1 * sin + k2 * cos) * (RET_HEADDIM ** -0.5)]
    qr = jnp.concatenate(qs, axis=1)
    kr = jnp.concatenate(ks, axis=1)
    vv = proj_ref[:, OFF_V:OFF_G]

    allq = jnp.concatenate([decay_x, xdt, kr, qr], axis=1)
    hi = allq.astype(BF16).astype(F32)
    r1 = allq - hi
    mid = r1.astype(BF16).astype(F32)
    lo = (r1 - mid).astype(BF16).astype(F32)
    stack = jnp.concatenate([hi, mid, lo, jnp.zeros_like(hi)], axis=0).astype(BF16)
    krow = lax.broadcasted_iota(jnp.int32, (4 * R, LANES), 0)
    row8 = lax.broadcasted_iota(jnp.int32, (R, 1), 0)
    lane = lax.broadcasted_iota(jnp.int32, (1, LANES), 1)

    y_cols = jnp.zeros((D_SSM, LANES), F32)
    y_ret = jnp.zeros((R, D_RET), F32)
    for r in range(R):
        sel = ((krow % R == r) & (krow < 3 * R)).astype(BF16)
        cols_ref[...] = _dot_tn(stack, sel)
        ycol = []
        for g in range(SSM_GROUPS):
            rows = slice(g * GROUP_DIM, (g + 1) * GROUP_DIM)
            h_old = ssm_ref[r, rows, :]
            h_new = (h_old * cols_ref[g * GROUP_DIM:(g + 1) * GROUP_DIM, :]
                     + cols_ref[D_SSM + g * GROUP_DIM:D_SSM + (g + 1) * GROUP_DIM, :]
                     * bmat[r:r + 1, g * D_STATE:(g + 1) * D_STATE])
            ssmo_ref[r, rows, :] = h_new
            ycol.append(jnp.sum(h_new * cmat[r:r + 1, g * D_STATE:(g + 1) * D_STATE], axis=1, keepdims=True))
        y_cols = jnp.where(lane == r, jnp.concatenate(ycol, axis=0), y_cols)
        yrow = []
        for h in range(RET_HEADS):
            rows = slice(h * RET_HEADDIM, (h + 1) * RET_HEADDIM)
            kcol = cols_ref[2 * D_SSM + h * RET_HEADDIM:2 * D_SSM + (h + 1) * RET_HEADDIM, :]
            qcol = cols_ref[3 * D_SSM + h * RET_HEADDIM:3 * D_SSM + (h + 1) * RET_HEADDIM, :]
            gamma = float(np.exp(np.float32(RET_LOG_GAMMA[h])))
            s_new = (gamma * ret_ref[r, rows, :]
                     + jnp.concatenate([kcol, kcol], axis=1) * vv[r:r + 1, h * RET_HEADDIM:(h + 1) * RET_HEADDIM])
            reto_ref[r, rows, :] = s_new
            yrow.append(jnp.sum(jnp.concatenate([qcol, qcol], axis=1) * s_new, axis=0, keepdims=True))
        y_ret = jnp.where(row8 == r, jnp.concatenate(yrow, axis=1), y_ret)

    y_ssd = y_cols.T[:R, :]
    y = (y_ssd + dskip_ref[...] * xs) * _silu(proj_ref[:, OFF_Z:OFF_XBC])
    y1 = jnp.concatenate([_rms(y[:, g * GROUP_DIM:(g + 1) * GROUP_DIM]) for g in range(SSM_GROUPS)],
                         axis=1) * sg_ref[...]
    y2 = jnp.concatenate([_rms(y_ret[:, h * RET_HEADDIM:(h + 1) * RET_HEADDIM]) for h in range(RET_HEADS)],
                         axis=1) * rg_ref[...] * _silu(proj_ref[:, OFF_G:PROJ_MAIN])
    mix_ref[:, :D_SSM] = y1
    mix_ref[:, D_SSM:] = y2


def _mixer_step(proj, dtr, conv_t, ssm, ret, params, *, nb):
    R = STEP_ROWS
    rows2 = lambda i: (i, 0)
    rows3 = lambda i: (i, 0, 0)
    mid3 = lambda i: (0, i, 0)
    const2 = lambda i: (0, 0)
    pspecs = [pl.BlockSpec(p.shape, const2) for p in params]
    return pl.pallas_call(
        _mixer_step_kernel,
        out_shape=(jax.ShapeDtypeStruct((nb, D_MODEL), F32),
                   jax.ShapeDtypeStruct((CONV_W - 1, nb, CONV_DIM), F32),
                   jax.ShapeDtypeStruct((nb, D_SSM, D_STATE), F32),
                   jax.ShapeDtypeStruct((nb, D_RET, RET_HEADDIM), F32)),
        grid=(nb // R,),
        in_specs=[pl.BlockSpec((R, PROJ_MAIN), rows2),
                  pl.BlockSpec((R, LANES), rows2),
                  pl.BlockSpec((CONV_W - 1, R, CONV_DIM), mid3),
                  pl.BlockSpec((R, D_SSM, D_STATE), rows3),
                  pl.BlockSpec((R, D_RET, RET_HEADDIM), rows3)] + pspecs,
        out_specs=(pl.BlockSpec((R, D_MODEL), rows2),
                   pl.BlockSpec((CONV_W - 1, R, CONV_DIM), mid3),
                   pl.BlockSpec((R, D_SSM, D_STATE), rows3),
                   pl.BlockSpec((R, D_RET, RET_HEADDIM), rows3)),
        scratch_shapes=[pltpu.VMEM((4 * D_SSM, LANES), F32)],
        compiler_params=pltpu.CompilerParams(
            dimension_semantics=("arbitrary",), vmem_limit_bytes=VMEM_LIMIT),
        name="mixer_step",
    )(proj, dtr, conv_t, ssm, ret, *params)


def _outproj_kernel(mix_ref, mixs_ref, w_ref, h_ref, hs_ref, g1_ref, g2_ref, hout_ref, f_ref, houts_ref, fs_ref):
    def rows(mix_r, h_r, hout_r, f_r):
        y = _dot(mix_r[...].astype(BF16), w_ref[...])
        h = h_r[...] + _rms(y) * g1_ref[...]
        hout_r[...] = h
        f_r[...] = (_rms(h) * g2_ref[...]).astype(BF16)

    rows(mix_ref, h_ref, hout_ref, f_ref)

    @pl.when(pl.program_id(0) == pl.num_programs(0) - 1)
    def _():
        rows(mixs_ref, hs_ref, houts_ref, fs_ref)


def _outproj(mix, mixs, w, h, hs, g1, g2, *, bm):
    m = mix.shape[0]
    ms = mixs.shape[0]
    row = lambda i: (i, 0)
    const = lambda i: (0, 0)
    return pl.pallas_call(
        _outproj_kernel,
        out_shape=(jax.ShapeDtypeStruct((m, D_MODEL), F32), jax.ShapeDtypeStruct((m, D_MODEL), BF16),
                   jax.ShapeDtypeStruct((ms, D_MODEL), F32), jax.ShapeDtypeStruct((ms, D_MODEL), BF16)),
        grid=(m // bm,),
        in_specs=[pl.BlockSpec((bm, D_MODEL), row),
                  pl.BlockSpec((ms, D_MODEL), const),
                  pl.BlockSpec((D_MODEL, D_MODEL), const),
                  pl.BlockSpec((bm, D_MODEL), row),
                  pl.BlockSpec((ms, D_MODEL), const),
                  pl.BlockSpec((1, D_MODEL), const),
                  pl.BlockSpec((1, D_MODEL), const)],
        out_specs=(pl.BlockSpec((bm, D_MODEL), row), pl.BlockSpec((bm, D_MODEL), row),
                   pl.BlockSpec((ms, D_MODEL), const), pl.BlockSpec((ms, D_MODEL), const)),
        compiler_params=pltpu.CompilerParams(
            dimension_semantics=("arbitrary",), vmem_limit_bytes=VMEM_LIMIT),
        name="outproj",
    )(mix, mixs, w, h, hs, g1, g2)


FFN_SPLIT = 2


def _ffn_kernel(f_ref, fs_ref, wg_ref, wu_ref, wd_ref, h_hbm, hs_ref, g_ref, o_ref, os_ref, acc_ref, hsem):
    i = pl.program_id(0)
    j = pl.program_id(1)
    nj = pl.num_programs(1)
    on_last = i == pl.num_programs(0) - 1
    bm = o_ref.shape[0]

    def h_copy(tile):
        return pltpu.make_async_copy(h_hbm.at[pl.ds(pl.multiple_of(tile * bm, bm), bm), :], o_ref, hsem)

    def finish_rows():
        for t in range(bm // NORM_ROWS):
            rows = slice(t * NORM_ROWS, (t + 1) * NORM_ROWS)
            o_ref[rows, :] = o_ref[rows, :] + _rms(acc_ref[rows, :]) * g_ref[...]
            acc_ref[rows, :] = jnp.zeros((NORM_ROWS, acc_ref.shape[1]), F32)

    @pl.when((i == 0) & (j == 0))
    def _():
        acc_ref[...] = jnp.zeros_like(acc_ref)

    @pl.when(j == 1)
    def _():
        h_copy(i).start()

    def ff_tile(with_side, finish_prev):
        fsub = wg_ref.shape[1] // FFN_SPLIT
        nsub = acc_ref.shape[1] // FFN_SPLIT
        if finish_prev:
            h_copy(i - 1).wait()
            finish_rows()
            if with_side:
                os_ref[...] = jnp.zeros_like(os_ref)

        def through(f, acc_r, s):
            ff = slice(s * fsub, (s + 1) * fsub)
            a = (_silu(_dot(f, wg_ref[:, ff])) * _dot(f, wu_ref[:, ff])).astype(BF16)
            for n in range(FFN_SPLIT):
                nn = slice(n * nsub, (n + 1) * nsub)
                acc_r[:, nn] += _dot(a, wd_ref[ff, nn])

        for s in range(FFN_SPLIT):
            through(f_ref[...], acc_ref, s)
            if with_side:
                through(fs_ref[...], os_ref, s)

    first = (j == 0) & (i > 0)
    rest = jnp.logical_not(first)
    not_last = jnp.logical_not(on_last)
    pl.when(first & on_last)(functools.partial(ff_tile, True, True))
    pl.when(first & not_last)(functools.partial(ff_tile, False, True))
    pl.when(rest & on_last)(functools.partial(ff_tile, True, False))
    pl.when(rest & not_last)(functools.partial(ff_tile, False, False))

    @pl.when(on_last & (j == nj - 1))
    def _():
        h_copy(i).wait()
        finish_rows()
        os_ref[...] = hs_ref[...] + _rms(os_ref[...]) * g_ref[...]


def _ffn(f, fs, wg, wu, wd, h, hs, g, *, bm, bf):
    m = f.shape[0]
    ms = fs.shape[0]
    assert m // bm > 1 and D_FF // bf > 1
    const = lambda i, j: (0, 0)
    once = pl.Buffered(1)
    return pl.pallas_call(
        _ffn_kernel,
        out_shape=(jax.ShapeDtypeStruct((m, D_MODEL), F32), jax.ShapeDtypeStruct((ms, D_MODEL), F32)),
        grid=(m // bm, D_FF // bf),
        in_specs=[pl.BlockSpec((bm, D_MODEL), lambda i, j: (i, 0)),
                  pl.BlockSpec((ms, D_MODEL), const, pipeline_mode=once),
                  pl.BlockSpec((D_MODEL, bf), lambda i, j: (0, j)),
                  pl.BlockSpec((D_MODEL, bf), lambda i, j: (0, j)),
                  pl.BlockSpec((bf, D_MODEL), lambda i, j: (j, 0)),
                  pl.BlockSpec(memory_space=pl.ANY),
                  pl.BlockSpec((ms, D_MODEL), const, pipeline_mode=once),
                  pl.BlockSpec((1, D_MODEL), const)],
        out_specs=(pl.BlockSpec((bm, D_MODEL), lambda i, j: (jnp.where((j == 0) & (i > 0), i - 1, i), 0)),
                   pl.BlockSpec((ms, D_MODEL), const)),
        scratch_shapes=[pltpu.VMEM((bm, D_MODEL), F32), pltpu.SemaphoreType.DMA],
        compiler_params=pltpu.CompilerParams(
            dimension_semantics=("arbitrary", "arbitrary"), vmem_limit_bytes=VMEM_LIMIT_FFN),
        name="ffn",
    )(f, fs, wg, wu, wd, h, hs, g)


def kernel(x_prompt, x_sample, state_conv, state_ssm, state_ret, meta_tokens, pre_mix_g, post_mix_g,
           pre_ffn_g, post_ffn_g, w_in, conv_w, conv_b, dt_bias, a_log, d_skip, ssm_norm_g, ret_norm_g,
           w_out, w_gate, w_up, w_down):
    bp, seq = x_prompt.shape[:2]
    bs = x_sample.shape[0]
    assert w_in.shape[0] == 1 and x_sample.shape[1] == 1 and seq % CHUNK == 0 and bs == CHUNK

    w_in_t = jnp.swapaxes(w_in[0], 0, 1)
    pad16 = lambda v: jnp.pad(v, ((0, 0), (0, LANES - SSM_HEADS)))
    inv_freq = (ROPE_BASE ** (-jnp.arange(RET_HEADDIM // 2, dtype=F32) / (RET_HEADDIM // 2)))[None, :]
    params = (conv_w[0], conv_b, pad16(dt_bias), pad16(a_log),
              jnp.repeat(d_skip, SSM_HEADDIM, axis=1), ssm_norm_g, ret_norm_g, inv_freq)

    xp = x_prompt.reshape(bp * seq, D_MODEL)
    xs_rows = x_sample.reshape(bs, D_MODEL)
    x_small = jnp.concatenate(
        [xs_rows, meta_tokens.astype(F32), jnp.zeros((CHUNK - N_META, D_MODEL), F32)], axis=0)
    proj_p, dtr_p, proj_s, dtr_s = _inproj(xp, x_small, pre_mix_g, w_in_t, bm=INPROJ_ROWS, xr=INPROJ_NORM_ROWS,
                                           bn=INPROJ_COLS)

    _, m_conv, m_ssm, m_ret = _mixer_seq(
        proj_s.reshape(1, 2 * CHUNK, PROJ_MAIN), dtr_s.reshape(1, 2 * CHUNK, LANES), None, params,
        nchunks=1, chunk_offset=1, valid=N_META, pos_base=0, name="mixer_meta")[:4]

    streams = 2 if bp % 2 == 0 else 1
    nsteps = (bp // streams) * (seq // CHUNK)
    mix_p, p_conv, p_ssm, p_ret, w_out_b, w_gate_b, w_up_b, w_down_b = _mixer_seq(
        proj_p.reshape(bp, seq, PROJ_MAIN), dtr_p.reshape(bp, seq, LANES), (m_conv, m_ssm, m_ret),
        params,
        nchunks=seq // CHUNK, chunk_offset=0, valid=CHUNK, pos_base=N_META, name="mixer_prompt",
        cast=((w_out[0], nsteps), (w_gate[0], nsteps), (w_up[0], nsteps), (w_down[0], nsteps // 2)),
        streams=streams)

    conv_t = jnp.transpose(state_conv[0], (1, 0, 2))
    mix_s, s_conv_t, s_ssm, s_ret = _mixer_step(
        proj_s, dtr_s, conv_t, state_ssm[0].reshape(bs, D_SSM, D_STATE),
        state_ret[0].reshape(bs, D_RET, RET_HEADDIM), params, nb=bs)

    h1_p, f_p, h1_s, f_s = _outproj(mix_p.reshape(bp * seq, D_MODEL), mix_s, w_out_b, xp, xs_rows,
                                    post_mix_g, pre_ffn_g, bm=OUTPROJ_ROWS)
    y_p, y_s = _ffn(f_p, f_s, w_gate_b, w_up_b, w_down_b, h1_p, h1_s, post_ffn_g, bm=FFN_ROWS,
                    bf=FFN_COLS)

    return (y_p.reshape(bp, seq, D_MODEL),
            y_s.reshape(bs, 1, D_MODEL),
            p_conv[None],
            p_ssm.reshape(1, bp, SSM_HEADS, SSM_HEADDIM, D_STATE),
            p_ret.reshape(1, bp, RET_HEADS, RET_HEADDIM, RET_HEADDIM),
            jnp.transpose(s_conv_t, (1, 0, 2))[None],
            s_ssm.reshape(1, bs, SSM_HEADS, SSM_HEADDIM, D_STATE),
            s_ret.reshape(1, bs, RET_HEADS, RET_HEADDIM, RET_HEADDIM))
```

```python
import functools

import numpy as np
import jax
import jax.numpy as jnp
from jax import lax
from jax.experimental import pallas as pl
from jax.experimental.pallas import tpu as pltpu

F32 = jnp.float32
BF16 = jnp.bfloat16

D_MODEL = 2048
N_META = 16
CHUNK = 128
D_SSM = 1024
D_RET = 1024
SSM_HEADDIM = 64
SSM_HEADS = 16
SSM_GROUPS = 2
GROUP_DIM = D_SSM // SSM_GROUPS
D_STATE = 128
CONV_W = 4
CONV_DIM = D_SSM + 2 * SSM_GROUPS * D_STATE
RET_HEADS = 4
RET_HEADDIM = 256
ROPE_BASE = 10000.0
D_FF = 5632
EPS = 1e-6
PAST_LEN = 16384
LOG2E = float(np.log2(np.e))

LANES = 128
SUBLANES = 8
STEP_ROWS = SUBLANES
CONV_PAD = SUBLANES

OFF_Z = 0
OFF_XBC = D_SSM
OFF_Q = OFF_XBC + CONV_DIM
OFF_K = OFF_Q + D_RET
OFF_V = OFF_K + D_RET
OFF_G = OFF_V + D_RET
PROJ_MAIN = OFF_G + D_RET

VMEM_LIMIT = 56 * 1024 * 1024
VMEM_LIMIT_FFN = 60 * 1024 * 1024

INPROJ_ROWS = 2048
INPROJ_NORM_ROWS = 1024
INPROJ_COLS = 512
OUTPROJ_ROWS = 512
FFN_ROWS = 1024
FFN_COLS = 512

RET_LOG_GAMMA = [float(np.log1p(-np.float32(2.0) ** np.float32(-5.0 - h)).astype(np.float32))
                 for h in range(RET_HEADS)]


def _silu(x):
    return x / (1.0 + jnp.exp2(x * (-LOG2E)))


def _softplus(x):
    return jnp.maximum(x, 0.0) + jnp.log1p(jnp.exp(-jnp.abs(x)))


def _rms(x):
    return x * lax.rsqrt(jnp.mean(x * x, axis=-1, keepdims=True) + EPS)


def _split3(x):
    hi = x.astype(BF16)
    r = x - hi.astype(F32)
    mid = r.astype(BF16)
    lo = (r - mid.astype(F32)).astype(BF16)
    return hi, mid, lo


def _dot(a, b):
    return jnp.dot(a, b, preferred_element_type=F32)


def _dot_nt(a, b):
    return lax.dot_general(a, b, (((1,), (1,)), ((), ())), preferred_element_type=F32)


def _dot_tn(a, b):
    return lax.dot_general(a, b, (((0,), (0,)), ((), ())), preferred_element_type=F32)


def _exact_right(x, sel):
    hi, mid, lo = x if isinstance(x, tuple) else _split3(x)
    return _dot(hi, sel) + _dot(mid, sel) + _dot(lo, sel)


def _select_right(x, sel):
    hi = x.astype(BF16)
    lo = (x - hi.astype(F32)).astype(BF16)
    return _dot(hi, sel) + _dot(lo, sel)


def _exact_left(sel, x):
    hi, mid, lo = x if isinstance(x, tuple) else _split3(x)
    return _dot(sel, hi) + _dot(sel, mid) + _dot(sel, lo)


def _exact_tn(x, sel):
    hi, mid, lo = x if isinstance(x, tuple) else _split3(x)
    return _dot_tn(hi, sel) + _dot_tn(mid, sel) + _dot_tn(lo, sel)


def _head_expand():
    r = lax.broadcasted_iota(jnp.int32, (LANES, D_SSM), 0)
    c = lax.broadcasted_iota(jnp.int32, (LANES, D_SSM), 1)
    return (c // SSM_HEADDIM == r).astype(BF16)


NORM_ROWS = 256
DT_ROW = D_SSM + CONV_DIM


def _inproj_kernel(x_ref, xs_ref, g_ref, wt_ref, wdt_ref, o_ref, odt_ref, os_ref, odts_ref, u_ref, us_ref,
                   *, npro, nsplit):
    i = pl.program_id(0)
    j = pl.program_id(1)
    on_last = i == pl.num_programs(0) - 1
    xr = x_ref.shape[0]

    @pl.when(j < npro)
    def _():
        wdt = wdt_ref[...].astype(BF16)
        lane = lax.broadcasted_iota(jnp.int32, (NORM_ROWS, LANES), 1)

        def norm_rows(src_ref, src, dst_ref, dt_ref, dst):
            u = (_rms(src_ref[src, :]) * g_ref[...]).astype(BF16)
            dst_ref[dst, :] = u
            dt_ref[dst, :] = jnp.where(lane < SSM_HEADS, _dot_nt(u, wdt), 0.0)

        def body(t, carry):
            src = pl.ds(pl.multiple_of(t * NORM_ROWS, NORM_ROWS), NORM_ROWS)
            dst = pl.ds(pl.multiple_of(j * xr + t * NORM_ROWS, NORM_ROWS), NORM_ROWS)
            norm_rows(x_ref, src, u_ref, odt_ref, dst)
            return carry
        lax.fori_loop(0, xr // NORM_ROWS, body, 0)

        @pl.when(on_last & (j == 0))
        def _():
            for t in range(xs_ref.shape[0] // NORM_ROWS):
                rows = pl.ds(t * NORM_ROWS, NORM_ROWS)
                norm_rows(xs_ref, rows, us_ref, odts_ref, rows)

    def column_tile(with_side):
        sub = wt_ref.shape[0] // nsplit
        for s in range(nsplit):
            cols = slice(s * sub, (s + 1) * sub)
            w = wt_ref[cols, :].astype(BF16)
            o_ref[0, :, cols] = _dot_nt(u_ref[...], w).astype(o_ref.dtype)
            if with_side:
                os_ref[:, cols] = _dot_nt(us_ref[...], w)

    pl.when((j >= npro) & on_last)(functools.partial(column_tile, True))
    pl.when((j >= npro) & jnp.logical_not(on_last))(functools.partial(column_tile, False))


def _inproj(x, xs, g, wt, *, bm, xr, bn):
    m = x.shape[0]
    ms = xs.shape[0]
    nm = m // bm
    npro = bm // xr
    assert DT_ROW % bn == 0 and ms % NORM_ROWS == 0

    def wrow(i, j):
        t = jnp.maximum(j - npro, 0)
        skip = jnp.where(t * bn >= DT_ROW, SSM_HEADS // SUBLANES, 0)
        return ((t * (bn // SUBLANES) + skip) * SUBLANES, 0)

    col = lambda j: jnp.maximum(j - npro, 0)
    const = lambda i, j: (0, 0)
    return pl.pallas_call(
        functools.partial(_inproj_kernel, npro=npro, nsplit=2),
        out_shape=(jax.ShapeDtypeStruct((PROJ_MAIN // bn, m, bn), BF16), jax.ShapeDtypeStruct((m, LANES), F32),
                   jax.ShapeDtypeStruct((ms, PROJ_MAIN), F32), jax.ShapeDtypeStruct((ms, LANES), F32)),
        grid=(nm, npro + PROJ_MAIN // bn),
        in_specs=[pl.BlockSpec((xr, D_MODEL), lambda i, j: (i * npro + jnp.minimum(j, npro - 1), 0)),
                  pl.BlockSpec((ms, D_MODEL), const),
                  pl.BlockSpec((1, D_MODEL), const),
                  pl.BlockSpec((pl.Element(bn), pl.Element(D_MODEL)), wrow),
                  pl.BlockSpec((pl.Element(LANES), pl.Element(D_MODEL)), lambda i, j: (DT_ROW, 0))],
        out_specs=(pl.BlockSpec((1, bm, bn), lambda i, j: (col(j), i, 0)),
                   pl.BlockSpec((bm, LANES), lambda i, j: (i, 0)),
                   pl.BlockSpec((ms, bn), lambda i, j: (0, jnp.where(i == nm - 1, col(j), 0))),
                   pl.BlockSpec((ms, LANES), const)),
        scratch_shapes=[pltpu.VMEM((bm, D_MODEL), BF16), pltpu.VMEM((ms, D_MODEL), BF16)],
        compiler_params=pltpu.CompilerParams(
            dimension_semantics=("arbitrary", "arbitrary"), vmem_limit_bytes=VMEM_LIMIT),
        name="inproj",
    )(x, xs, g, wt, wt)


N_MIXER_IN = 13
N_MIXER_OUT = 4


def _mixer_seq_kernel(*refs, valid, pos_base, ncast):
    ins = refs[:N_MIXER_IN]
    cast_in = refs[N_MIXER_IN:N_MIXER_IN + ncast]
    outs = refs[N_MIXER_IN + ncast:N_MIXER_IN + ncast + N_MIXER_OUT]
    cast_out = refs[N_MIXER_IN + ncast + N_MIXER_OUT:N_MIXER_IN + 2 * ncast + N_MIXER_OUT]
    scratch = refs[N_MIXER_IN + 2 * ncast + N_MIXER_OUT:]
    _mixer_seq_body(*ins, *outs, *scratch, valid=valid, pos_base=pos_base)
    for src, dst in zip(cast_in, cast_out):
        dst[...] = src[...].astype(BF16)


def _mixer_seq_body(proj_ref, dtr_ref, conv0_ref, ssm0_ref, ret0_ref,
                    convw_ref, convb_ref, dtb_ref, alog_ref, dskip_ref, sg_ref, rg_ref, invf_ref,
                    mix_ref, convo_ref, ssmo_ref, reto_ref,
                    cbuf_ref, rdec_ref, cdec_ref, trig_ref, *, valid, pos_base):
    C = CHUNK
    b = pl.program_id(0)
    c = pl.program_id(1)
    rowi = lax.broadcasted_iota(jnp.int32, (C, 1), 0)
    rowf = rowi.astype(F32)
    ri = lax.broadcasted_iota(jnp.int32, (C, C), 0)
    ci = lax.broadcasted_iota(jnp.int32, (C, C), 1)
    causal = ri >= ci

    @pl.when((b == 0) & (c == 0))
    def _():
        diff = (ri - ci).astype(F32)
        for h in range(RET_HEADS):
            rdec_ref[h] = jnp.where(causal, jnp.exp(jnp.maximum(diff, 0.0) * RET_LOG_GAMMA[h]), 0.0)
            rows_l = jnp.broadcast_to(rowf, (C, LANES))
            cdec_ref[h] = jnp.exp((rows_l + 1.0) * RET_LOG_GAMMA[h])
            cdec_ref[RET_HEADS + h] = jnp.exp((valid - 1.0 - rows_l) * RET_LOG_GAMMA[h])
        row_ang = rowf * invf_ref[...]
        trig_ref[0] = jnp.cos(row_ang)
        trig_ref[1] = jnp.sin(row_ang)

    hist = CONV_PAD - (CONV_W - 1)

    @pl.when(c == 0)
    def _():
        cbuf_ref[hist:CONV_PAD, :] = conv0_ref[0]
        ssmo_ref[0] = ssm0_ref[0]
        reto_ref[0] = ret0_ref[0]

    def proj_cols(lo, hi):
        if len(proj_ref.shape) == 3:
            return proj_ref[0, :, lo:hi]
        tw = proj_ref.shape[-1]
        parts = [proj_ref[t, 0, :, max(lo, t * tw) - t * tw:min(hi, (t + 1) * tw) - t * tw]
                 for t in range(lo // tw, (hi - 1) // tw + 1)]
        return parts[0] if len(parts) == 1 else jnp.concatenate(parts, axis=1)

    xbc_raw = proj_cols(OFF_XBC, OFF_Q).astype(F32)
    cbuf_ref[CONV_PAD:CONV_PAD + C, :] = xbc_raw
    acc = convb_ref[...] + xbc_raw * convw_ref[CONV_W - 1:CONV_W, :]
    for i in range(CONV_W - 1):
        acc = acc + cbuf_ref[hist + i:hist + i + C, :] * convw_ref[i:i + 1, :]
    xbc = _silu(acc)
    new_prev = cbuf_ref[hist + valid:CONV_PAD + valid, :]
    cbuf_ref[hist:CONV_PAD, :] = new_prev
    convo_ref[0] = new_prev

    xs = xbc[:, :D_SSM]
    bmat = xbc[:, D_SSM:D_SSM + SSM_GROUPS * D_STATE].astype(BF16)
    cmat = xbc[:, D_SSM + SSM_GROUPS * D_STATE:].astype(BF16)

    dt = _softplus(dtr_ref[0] + dtb_ref[...])
    if valid < C:
        dt = jnp.where(rowi < valid, dt, 0.0)
    la = dt * (-jnp.exp(alog_ref[...]))
    tril = causal.astype(BF16)
    triu = (ri <= ci).astype(BF16)
    eye = (ri == ci).astype(BF16)
    la3 = _split3(la)
    lcum = _exact_left(tril, la3)
    lcum_t = _exact_tn(la3, triu)
    dt_t = _exact_tn(dt, eye)
    expand = _head_expand()
    carry_scale = _select_right(jnp.exp(lcum), expand)
    tail_scale = _select_right(jnp.exp(lcum[C - 1:C, :] - lcum) * dt, expand)
    lcum2 = lcum * LOG2E
    lcum2_t = lcum_t * LOG2E

    cbs = [_dot_nt(cmat[:, g * D_STATE:(g + 1) * D_STATE], bmat[:, g * D_STATE:(g + 1) * D_STATE])
           for g in range(SSM_GROUPS)]
    lane = lax.broadcasted_iota(jnp.int32, (C, LANES), 1)
    left = lane < SSM_HEADDIM
    y_intra = []
    for m in range(SSM_HEADS // 2):
        ws = []
        for h in (2 * m, 2 * m + 1):
            seg2 = lcum2[:, h:h + 1] - lcum2_t[h:h + 1, :]
            decay = jnp.exp2(jnp.where(causal, seg2, -jnp.inf))
            ws.append((cbs[h // (SSM_HEADS // SSM_GROUPS)] * decay * dt_t[h:h + 1, :]).astype(BF16))
        xm = xs[:, m * LANES:(m + 1) * LANES]
        xst = jnp.concatenate([jnp.where(left, xm, 0.0), jnp.where(left, 0.0, xm)], axis=0).astype(BF16)
        y_intra.append(_dot(jnp.concatenate(ws, axis=1), xst))
    y = jnp.concatenate(y_intra, axis=1)

    hstate = ssmo_ref[0]
    hb = hstate.astype(BF16)
    y_inter = jnp.concatenate(
        [_dot_nt(cmat[:, g * D_STATE:(g + 1) * D_STATE], hb[g * GROUP_DIM:(g + 1) * GROUP_DIM, :])
         for g in range(SSM_GROUPS)], axis=1)
    y = y + y_inter * carry_scale + dskip_ref[...] * xs

    xw = (xs * tail_scale).astype(BF16)
    upd = jnp.concatenate(
        [_dot_tn(xw[:, g * GROUP_DIM:(g + 1) * GROUP_DIM], bmat[:, g * D_STATE:(g + 1) * D_STATE])
         for g in range(SSM_GROUPS)], axis=0)
    la_tot = _exact_tn(la3, jnp.ones((C, LANES), BF16))
    er = lax.broadcasted_iota(jnp.int32, (D_SSM, LANES), 0)
    ec = lax.broadcasted_iota(jnp.int32, (D_SSM, LANES), 1)
    expand_t = (er // SSM_HEADDIM == ec).astype(BF16)
    chunk_decay = jnp.exp(_exact_left(expand_t, la_tot))
    ssmo_ref[0] = chunk_decay * hstate + upd

    z = proj_cols(OFF_Z, OFF_XBC).astype(F32)
    y = y * _silu(z)
    y1 = jnp.concatenate([_rms(y[:, g * GROUP_DIM:(g + 1) * GROUP_DIM]) for g in range(SSM_GROUPS)],
                         axis=1) * sg_ref[...]

    ang0 = (pos_base + c * C).astype(F32) * invf_ref[...]
    cos0, sin0 = jnp.cos(ang0), jnp.sin(ang0)
    cos = cos0 * trig_ref[0] - sin0 * trig_ref[1]
    sin = sin0 * trig_ref[0] + cos0 * trig_ref[1]
    kscale = RET_HEADDIM ** -0.5
    cos_k, sin_k = cos * kscale, sin * kscale
    half = RET_HEADDIM // 2
    y2 = []
    for h in range(RET_HEADS):
        lg = RET_LOG_GAMMA[h]
        q1 = proj_cols(OFF_Q + h * RET_HEADDIM, OFF_Q + h * RET_HEADDIM + half).astype(F32)
        q2 = proj_cols(OFF_Q + h * RET_HEADDIM + half, OFF_Q + (h + 1) * RET_HEADDIM).astype(F32)
        k1 = proj_cols(OFF_K + h * RET_HEADDIM, OFF_K + h * RET_HEADDIM + half).astype(F32)
        k2 = proj_cols(OFF_K + h * RET_HEADDIM + half, OFF_K + (h + 1) * RET_HEADDIM).astype(F32)
        vh = proj_cols(OFF_V + h * RET_HEADDIM, OFF_V + (h + 1) * RET_HEADDIM).astype(BF16)
        qr = jnp.concatenate([q1 * cos - q2 * sin, q1 * sin + q2 * cos], axis=1)
        kr = jnp.concatenate([k1 * cos_k - k2 * sin_k, k1 * sin_k + k2 * cos_k], axis=1)
        if valid < C:
            kr = jnp.where(rowi < valid, kr, 0.0)
        qb = qr.astype(BF16)
        scores = _dot_nt(qb, kr.astype(BF16)) * rdec_ref[h]
        s_old = reto_ref[0, h * RET_HEADDIM:(h + 1) * RET_HEADDIM, :]
        carry_dec = cdec_ref[h]
        tail_dec = cdec_ref[RET_HEADS + h]
        yr = (_dot(scores.astype(BF16), vh)
              + _dot(qb, s_old.astype(BF16)) * jnp.concatenate([carry_dec, carry_dec], axis=1))
        kw = (kr * jnp.concatenate([tail_dec, tail_dec], axis=1)).astype(BF16)
        reto_ref[0, h * RET_HEADDIM:(h + 1) * RET_HEADDIM, :] = (
            float(np.exp(np.float32(valid * lg))) * s_old + _dot_tn(kw, vh))
        y2.append(_rms(yr))
    gate = proj_cols(OFF_G, PROJ_MAIN).astype(F32)
    y2 = jnp.concatenate(y2, axis=1) * rg_ref[...] * _silu(gate)

    mix_ref[0, :, :D_SSM] = y1.astype(BF16)
    mix_ref[0, :, D_SSM:] = y2.astype(BF16)


def _mixer_seq(proj, dtr, conv0, ssm0, ret0, params, *, nchunks, chunk_offset, valid, pos_base, name,
               cast=()):
    nb = proj.shape[-3]
    nsteps = nb * nchunks
    row = lambda b, c: (b, c + chunk_offset, 0)
    if proj.ndim == 4:
        proj_spec = pl.BlockSpec((proj.shape[0], 1, CHUNK, proj.shape[3]),
                                 lambda b, c: (0, b, c + chunk_offset, 0))
    else:
        proj_spec = pl.BlockSpec((1, CHUNK, PROJ_MAIN), row)
    const3 = lambda b, c: (0, 0, 0)
    const2 = lambda b, c: (0, 0)
    per_b = lambda b, c: (b, 0, 0)
    pspecs = [pl.BlockSpec(p.shape, const2) for p in params]
    cast_specs = []
    for w, nblk in cast:
        assert nsteps % nblk == 0 and w.shape[0] % nblk == 0
        every = nsteps // nblk
        cast_specs.append(pl.BlockSpec((w.shape[0] // nblk, w.shape[1]),
                                       lambda b, c, every=every: ((b * nchunks + c) // every, 0)))
    kern = functools.partial(_mixer_seq_kernel, valid=valid, pos_base=pos_base, ncast=len(cast))
    return pl.pallas_call(
        kern,
        out_shape=(jax.ShapeDtypeStruct((nb, nchunks * CHUNK, D_MODEL), BF16),
                   jax.ShapeDtypeStruct((nb, CONV_W - 1, CONV_DIM), F32),
                   jax.ShapeDtypeStruct((nb, D_SSM, D_STATE), F32),
                   jax.ShapeDtypeStruct((nb, D_RET, RET_HEADDIM), F32))
        + tuple(jax.ShapeDtypeStruct(w.shape, BF16) for w, _ in cast),
        grid=(nb, nchunks),
        in_specs=[proj_spec,
                  pl.BlockSpec((1, CHUNK, LANES), row),
                  pl.BlockSpec((1, CONV_W - 1, CONV_DIM), const3),
                  pl.BlockSpec((1, D_SSM, D_STATE), const3),
                  pl.BlockSpec((1, D_RET, RET_HEADDIM), const3)] + pspecs + cast_specs,
        out_specs=(pl.BlockSpec((1, CHUNK, D_MODEL), lambda b, c: (b, c, 0)),
                   pl.BlockSpec((1, CONV_W - 1, CONV_DIM), per_b),
                   pl.BlockSpec((1, D_SSM, D_STATE), per_b),
                   pl.BlockSpec((1, D_RET, RET_HEADDIM), per_b)) + tuple(cast_specs),
        scratch_shapes=[pltpu.VMEM((CONV_PAD + CHUNK, CONV_DIM), F32),
                        pltpu.VMEM((RET_HEADS, CHUNK, CHUNK), F32),
                        pltpu.VMEM((2 * RET_HEADS, CHUNK, LANES), F32),
                        pltpu.VMEM((2, CHUNK, RET_HEADDIM // 2), F32)],
        compiler_params=pltpu.CompilerParams(
            dimension_semantics=("arbitrary", "arbitrary"), vmem_limit_bytes=VMEM_LIMIT),
        name=name,
    )(proj, dtr, conv0, ssm0, ret0, *params, *[w for w, _ in cast])


def _mixer_step_kernel(proj_ref, dtr_ref, conv_ref, ssm_ref, ret_ref,
                       convw_ref, convb_ref, dtb_ref, alog_ref, dskip_ref, sg_ref, rg_ref, invf_ref,
                       mix_ref, convo_ref, ssmo_ref, reto_ref, cols_ref):
    R = STEP_ROWS
    xbc_raw = proj_ref[:, OFF_XBC:OFF_Q]
    acc = convb_ref[...] + xbc_raw * convw_ref[CONV_W - 1:CONV_W, :]
    for i in range(CONV_W - 1):
        acc = acc + conv_ref[i] * convw_ref[i:i + 1, :]
    xbc = _silu(acc)
    convo_ref[0] = conv_ref[1]
    convo_ref[1] = conv_ref[2]
    convo_ref[2] = xbc_raw

    xs = xbc[:, :D_SSM]
    bmat = xbc[:, D_SSM:D_SSM + SSM_GROUPS * D_STATE]
    cmat = xbc[:, D_SSM + SSM_GROUPS * D_STATE:]
    dt = _softplus(dtr_ref[...] + dtb_ref[...])
    la = dt * (-jnp.exp(alog_ref[...]))
    expand = _head_expand()
    dt_x = _exact_right(dt, expand)
    decay_x = jnp.exp(_exact_right(la, expand))
    xdt = xs * dt_x

    ang = jnp.float32(PAST_LEN) * invf_ref[...]
    cos = jnp.cos(ang)
    sin = jnp.sin(ang)
    half = RET_HEADDIM // 2
    qs, ks = [], []
    for h in range(RET_HEADS):
        q1 = proj_ref[:, OFF_Q + h * RET_HEADDIM:OFF_Q + h * RET_HEADDIM + half]
        q2 = proj_ref[:, OFF_Q + h * RET_HEADDIM + half:OFF_Q + (h + 1) * RET_HEADDIM]
        k1 = proj_ref[:, OFF_K + h * RET_HEADDIM:OFF_K + h * RET_HEADDIM + half]
        k2 = proj_ref[:, OFF_K + h * RET_HEADDIM + half:OFF_K + (h + 1) * RET_HEADDIM]
        qs += [q1 * cos - q2 * sin, q1 * sin + q2 * cos]
        ks += [(k1 * cos - k2 * sin) * (RET_HEADDIM ** -0.5), (k1 * sin + k2 * cos) * (RET_HEADDIM ** -0.5)]
    qr = jnp.concatenate(qs, axis=1)
    kr = jnp.concatenate(ks, axis=1)
    vv = proj_ref[:, OFF_V:OFF_G]

    allq = jnp.concatenate([decay_x, xdt, kr, qr], axis=1)
    hi = allq.astype(BF16).astype(F32)
    r1 = allq - hi
    mid = r1.astype(BF16).astype(F32)
    lo = (r1 - mid).astype(BF16).astype(F32)
    stack = jnp.concatenate([hi, mid, lo, jnp.zeros_like(hi)], axis=0).astype(BF16)
    krow = lax.broadcasted_iota(jnp.int32, (4 * R, LANES), 0)
    row8 = lax.broadcasted_iota(jnp.int32, (R, 1), 0)
    lane = lax.broadcasted_iota(jnp.int32, (1, LANES), 1)

    y_cols = jnp.zeros((D_SSM, LANES), F32)
    y_ret = jnp.zeros((R, D_RET), F32)
    for r in range(R):
        sel = ((krow % R == r) & (krow < 3 * R)).astype(BF16)
        cols_ref[...] = _dot_tn(stack, sel)
        ycol = []
        for g in range(SSM_GROUPS):
            rows = slice(g * GROUP_DIM, (g + 1) * GROUP_DIM)
            h_old = ssm_ref[r, rows, :]
            h_new = (h_old * cols_ref[g * GROUP_DIM:(g + 1) * GROUP_DIM, :]
                     + cols_ref[D_SSM + g * GROUP_DIM:D_SSM + (g + 1) * GROUP_DIM, :]
                     * bmat[r:r + 1, g * D_STATE:(g + 1) * D_STATE])
            ssmo_ref[r, rows, :] = h_new
            ycol.append(jnp.sum(h_new * cmat[r:r + 1, g * D_STATE:(g + 1) * D_STATE], axis=1, keepdims=True))
        y_cols = jnp.where(lane == r, jnp.concatenate(ycol, axis=0), y_cols)
        yrow = []
        for h in range(RET_HEADS):
            rows = slice(h * RET_HEADDIM, (h + 1) * RET_HEADDIM)
            kcol = cols_ref[2 * D_SSM + h * RET_HEADDIM:2 * D_SSM + (h + 1) * RET_HEADDIM, :]
            qcol = cols_ref[3 * D_SSM + h * RET_HEADDIM:3 * D_SSM + (h + 1) * RET_HEADDIM, :]
            gamma = float(np.exp(np.float32(RET_LOG_GAMMA[h])))
            s_new = (gamma * ret_ref[r, rows, :]
                     + jnp.concatenate([kcol, kcol], axis=1) * vv[r:r + 1, h * RET_HEADDIM:(h + 1) * RET_HEADDIM])
            reto_ref[r, rows, :] = s_new
            yrow.append(jnp.sum(jnp.concatenate([qcol, qcol], axis=1) * s_new, axis=0, keepdims=True))
        y_ret = jnp.where(row8 == r, jnp.concatenate(yrow, axis=1), y_ret)

    y_ssd = y_cols.T[:R, :]
    y = (y_ssd + dskip_ref[...] * xs) * _silu(proj_ref[:, OFF_Z:OFF_XBC])
    y1 = jnp.concatenate([_rms(y[:, g * GROUP_DIM:(g + 1) * GROUP_DIM]) for g in range(SSM_GROUPS)],
                         axis=1) * sg_ref[...]
    y2 = jnp.concatenate([_rms(y_ret[:, h * RET_HEADDIM:(h + 1) * RET_HEADDIM]) for h in range(RET_HEADS)],
                         axis=1) * rg_ref[...] * _silu(proj_ref[:, OFF_G:PROJ_MAIN])
    mix_ref[:, :D_SSM] = y1
    mix_ref[:, D_SSM:] = y2


def _mixer_step(proj, dtr, conv_t, ssm, ret, params, *, nb):
    R = STEP_ROWS
    rows2 = lambda i: (i, 0)
    rows3 = lambda i: (i, 0, 0)
    mid3 = lambda i: (0, i, 0)
    const2 = lambda i: (0, 0)
    pspecs = [pl.BlockSpec(p.shape, const2) for p in params]
    return pl.pallas_call(
        _mixer_step_kernel,
        out_shape=(jax.ShapeDtypeStruct((nb, D_MODEL), F32),
                   jax.ShapeDtypeStruct((CONV_W - 1, nb, CONV_DIM), F32),
                   jax.ShapeDtypeStruct((nb, D_SSM, D_STATE), F32),
                   jax.ShapeDtypeStruct((nb, D_RET, RET_HEADDIM), F32)),
        grid=(nb // R,),
        in_specs=[pl.BlockSpec((R, PROJ_MAIN), rows2),
                  pl.BlockSpec((R, LANES), rows2),
                  pl.BlockSpec((CONV_W - 1, R, CONV_DIM), mid3),
                  pl.BlockSpec((R, D_SSM, D_STATE), rows3),
                  pl.BlockSpec((R, D_RET, RET_HEADDIM), rows3)] + pspecs,
        out_specs=(pl.BlockSpec((R, D_MODEL), rows2),
                   pl.BlockSpec((CONV_W - 1, R, CONV_DIM), mid3),
                   pl.BlockSpec((R, D_SSM, D_STATE), rows3),
                   pl.BlockSpec((R, D_RET, RET_HEADDIM), rows3)),
        scratch_shapes=[pltpu.VMEM((4 * D_SSM, LANES), F32)],
        compiler_params=pltpu.CompilerParams(
            dimension_semantics=("arbitrary",), vmem_limit_bytes=VMEM_LIMIT),
        name="mixer_step",
    )(proj, dtr, conv_t, ssm, ret, *params)


def _outproj_kernel(mix_ref, mixs_ref, w_ref, h_ref, hs_ref, g1_ref, g2_ref, hout_ref, f_ref, houts_ref, fs_ref):
    def rows(mix_r, h_r, hout_r, f_r):
        y = _dot(mix_r[...].astype(BF16), w_ref[...])
        h = h_r[...] + _rms(y) * g1_ref[...]
        hout_r[...] = h
        f_r[...] = (_rms(h) * g2_ref[...]).astype(BF16)

    rows(mix_ref, h_ref, hout_ref, f_ref)

    @pl.when(pl.program_id(0) == pl.num_programs(0) - 1)
    def _():
        rows(mixs_ref, hs_ref, houts_ref, fs_ref)


def _outproj(mix, mixs, w, h, hs, g1, g2, *, bm):
    m = mix.shape[0]
    ms = mixs.shape[0]
    row = lambda i: (i, 0)
    const = lambda i: (0, 0)
    return pl.pallas_call(
        _outproj_kernel,
        out_shape=(jax.ShapeDtypeStruct((m, D_MODEL), F32), jax.ShapeDtypeStruct((m, D_MODEL), BF16),
                   jax.ShapeDtypeStruct((ms, D_MODEL), F32), jax.ShapeDtypeStruct((ms, D_MODEL), BF16)),
        grid=(m // bm,),
        in_specs=[pl.BlockSpec((bm, D_MODEL), row),
                  pl.BlockSpec((ms, D_MODEL), const),
                  pl.BlockSpec((D_MODEL, D_MODEL), const),
                  pl.BlockSpec((bm, D_MODEL), row),
                  pl.BlockSpec((ms, D_MODEL), const),
                  pl.BlockSpec((1, D_MODEL), const),
                  pl.BlockSpec((1, D_MODEL), const)],
        out_specs=(pl.BlockSpec((bm, D_MODEL), row), pl.BlockSpec((bm, D_MODEL), row),
                   pl.BlockSpec((ms, D_MODEL), const), pl.BlockSpec((ms, D_MODEL), const)),
        compiler_params=pltpu.CompilerParams(
            dimension_semantics=("arbitrary",), vmem_limit_bytes=VMEM_LIMIT),
        name="outproj",
    )(mix, mixs, w, h, hs, g1, g2)


FFN_SPLIT = 2


def _ffn_kernel(f_ref, fs_ref, wg_ref, wu_ref, wd_ref, h_hbm, hs_ref, g_ref, o_ref, os_ref, acc_ref, hsem):
    i = pl.program_id(0)
    j = pl.program_id(1)
    nj = pl.num_programs(1)
    on_last = i == pl.num_programs(0) - 1
    bm = o_ref.shape[0]

    def h_copy(tile):
        return pltpu.make_async_copy(h_hbm.at[pl.ds(pl.multiple_of(tile * bm, bm), bm), :], o_ref, hsem)

    def finish_rows():
        for t in range(bm // NORM_ROWS):
            rows = slice(t * NORM_ROWS, (t + 1) * NORM_ROWS)
            o_ref[rows, :] = o_ref[rows, :] + _rms(acc_ref[rows, :]) * g_ref[...]
            acc_ref[rows, :] = jnp.zeros((NORM_ROWS, acc_ref.shape[1]), F32)

    @pl.when((i == 0) & (j == 0))
    def _():
        acc_ref[...] = jnp.zeros_like(acc_ref)

    @pl.when(j == 1)
    def _():
        h_copy(i).start()

    def ff_tile(with_side, finish_prev):
        fsub = wg_ref.shape[1] // FFN_SPLIT
        nsub = acc_ref.shape[1] // FFN_SPLIT
        if finish_prev:
            h_copy(i - 1).wait()
            finish_rows()
            if with_side:
                os_ref[...] = jnp.zeros_like(os_ref)

        def through(f, acc_r, s):
            ff = slice(s * fsub, (s + 1) * fsub)
            a = (_silu(_dot(f, wg_ref[:, ff])) * _dot(f, wu_ref[:, ff])).astype(BF16)
            for n in range(FFN_SPLIT):
                nn = slice(n * nsub, (n + 1) * nsub)
                acc_r[:, nn] += _dot(a, wd_ref[ff, nn])

        for s in range(FFN_SPLIT):
            through(f_ref[...], acc_ref, s)
            if with_side:
                through(fs_ref[...], os_ref, s)

    first = (j == 0) & (i > 0)
    rest = jnp.logical_not(first)
    not_last = jnp.logical_not(on_last)
    pl.when(first & on_last)(functools.partial(ff_tile, True, True))
    pl.when(first & not_last)(functools.partial(ff_tile, False, True))
    pl.when(rest & on_last)(functools.partial(ff_tile, True, False))
    pl.when(rest & not_last)(functools.partial(ff_tile, False, False))

    @pl.when(on_last & (j == nj - 1))
    def _():
        h_copy(i).wait()
        finish_rows()
        os_ref[...] = hs_ref[...] + _rms(os_ref[...]) * g_ref[...]


def _ffn(f, fs, wg, wu, wd, h, hs, g, *, bm, bf):
    m = f.shape[0]
    ms = fs.shape[0]
    assert m // bm > 1 and D_FF // bf > 1
    const = lambda i, j: (0, 0)
    once = pl.Buffered(1)
    return pl.pallas_call(
        _ffn_kernel,
        out_shape=(jax.ShapeDtypeStruct((m, D_MODEL), F32), jax.ShapeDtypeStruct((ms, D_MODEL), F32)),
        grid=(m // bm, D_FF // bf),
        in_specs=[pl.BlockSpec((bm, D_MODEL), lambda i, j: (i, 0)),
                  pl.BlockSpec((ms, D_MODEL), const, pipeline_mode=once),
                  pl.BlockSpec((D_MODEL, bf), lambda i, j: (0, j)),
                  pl.BlockSpec((D_MODEL, bf), lambda i, j: (0, j)),
                  pl.BlockSpec((bf, D_MODEL), lambda i, j: (j, 0)),
                  pl.BlockSpec(memory_space=pl.ANY),
                  pl.BlockSpec((ms, D_MODEL), const, pipeline_mode=once),
                  pl.BlockSpec((1, D_MODEL), const)],
        out_specs=(pl.BlockSpec((bm, D_MODEL), lambda i, j: (jnp.where((j == 0) & (i > 0), i - 1, i), 0)),
                   pl.BlockSpec((ms, D_MODEL), const)),
        scratch_shapes=[pltpu.VMEM((bm, D_MODEL), F32), pltpu.SemaphoreType.DMA],
        compiler_params=pltpu.CompilerParams(
            dimension_semantics=("arbitrary", "arbitrary"), vmem_limit_bytes=VMEM_LIMIT_FFN),
        name="ffn",
    )(f, fs, wg, wu, wd, h, hs, g)


def kernel(x_prompt, x_sample, state_conv, state_ssm, state_ret, meta_tokens, pre_mix_g, post_mix_g,
           pre_ffn_g, post_ffn_g, w_in, conv_w, conv_b, dt_bias, a_log, d_skip, ssm_norm_g, ret_norm_g,
           w_out, w_gate, w_up, w_down):
    bp, seq = x_prompt.shape[:2]
    bs = x_sample.shape[0]
    assert w_in.shape[0] == 1 and x_sample.shape[1] == 1 and seq % CHUNK == 0 and bs == CHUNK

    w_in_t = jnp.swapaxes(w_in[0], 0, 1)
    pad16 = lambda v: jnp.pad(v, ((0, 0), (0, LANES - SSM_HEADS)))
    inv_freq = (ROPE_BASE ** (-jnp.arange(RET_HEADDIM // 2, dtype=F32) / (RET_HEADDIM // 2)))[None, :]
    params = (conv_w[0], conv_b, pad16(dt_bias), pad16(a_log),
              jnp.repeat(d_skip, SSM_HEADDIM, axis=1), ssm_norm_g, ret_norm_g, inv_freq)

    xp = x_prompt.reshape(bp * seq, D_MODEL)
    xs_rows = x_sample.reshape(bs, D_MODEL)
    x_small = jnp.concatenate(
        [xs_rows, meta_tokens.astype(F32), jnp.zeros((CHUNK - N_META, D_MODEL), F32)], axis=0)
    proj_p, dtr_p, proj_s, dtr_s = _inproj(xp, x_small, pre_mix_g, w_in_t, bm=INPROJ_ROWS, xr=INPROJ_NORM_ROWS,
                                           bn=INPROJ_COLS)

    zc = jnp.zeros((1, CONV_W - 1, CONV_DIM), F32)
    zs = jnp.zeros((1, D_SSM, D_STATE), F32)
    zr = jnp.zeros((1, D_RET, RET_HEADDIM), F32)
    _, m_conv, m_ssm, m_ret = _mixer_seq(
        proj_s.reshape(1, 2 * CHUNK, PROJ_MAIN), dtr_s.reshape(1, 2 * CHUNK, LANES), zc, zs, zr, params,
        nchunks=1, chunk_offset=1, valid=N_META, pos_base=0, name="mixer_meta")[:4]

    nsteps = bp * (seq // CHUNK)
    mix_p, p_conv, p_ssm, p_ret, w_out_b, w_gate_b, w_up_b, w_down_b = _mixer_seq(
        proj_p.reshape(-1, bp, seq, INPROJ_COLS), dtr_p.reshape(bp, seq, LANES), m_conv, m_ssm, m_ret, params,
        nchunks=seq // CHUNK, chunk_offset=0, valid=CHUNK, pos_base=N_META, name="mixer_prompt",
        cast=((w_out[0], nsteps), (w_gate[0], nsteps), (w_up[0], nsteps), (w_down[0], nsteps // 2)))

    conv_t = jnp.transpose(state_conv[0], (1, 0, 2))
    mix_s, s_conv_t, s_ssm, s_ret = _mixer_step(
        proj_s, dtr_s, conv_t, state_ssm[0].reshape(bs, D_SSM, D_STATE),
        state_ret[0].reshape(bs, D_RET, RET_HEADDIM), params, nb=bs)

    h1_p, f_p, h1_s, f_s = _outproj(mix_p.reshape(bp * seq, D_MODEL), mix_s, w_out_b, xp, xs_rows,
                                    post_mix_g, pre_ffn_g, bm=OUTPROJ_ROWS)
    y_p, y_s = _ffn(f_p, f_s, w_gate_b, w_up_b, w_down_b, h1_p, h1_s, post_ffn_g, bm=FFN_ROWS,
                    bf=FFN_COLS)

    return (y_p.reshape(bp, seq, D_MODEL),
            y_s.reshape(bs, 1, D_MODEL),
            p_conv[None],
            p_ssm.reshape(1, bp, SSM_HEADS, SSM_HEADDIM, D_STATE),
            p_ret.reshape(1, bp, RET_HEADS, RET_HEADDIM, RET_HEADDIM),
            jnp.transpose(s_conv_t, (1, 0, 2))[None],
            s_ssm.reshape(1, bs, SSM_HEADS, SSM_HEADDIM, D_STATE),
            s_ret.reshape(1, bs, RET_HEADS, RET_HEADDIM, RET_HEADDIM))
```

```python
import functools

import numpy as np
import jax
import jax.numpy as jnp
from jax import lax
from jax.experimental import pallas as pl
from jax.experimental.pallas import tpu as pltpu

F32 = jnp.float32
BF16 = jnp.bfloat16

D_MODEL = 2048
N_META = 16
CHUNK = 128
D_SSM = 1024
D_RET = 1024
SSM_HEADDIM = 64
SSM_HEADS = 16
SSM_GROUPS = 2
GROUP_DIM = D_SSM // SSM_GROUPS
D_STATE = 128
CONV_W = 4
CONV_DIM = D_SSM + 2 * SSM_GROUPS * D_STATE
RET_HEADS = 4
RET_HEADDIM = 256
ROPE_BASE = 10000.0
D_FF = 5632
EPS = 1e-6
PAST_LEN = 16384
LOG2E = float(np.log2(np.e))

LANES = 128
SUBLANES = 8
STEP_ROWS = SUBLANES
CONV_PAD = SUBLANES

OFF_Z = 0
OFF_XBC = D_SSM
OFF_Q = OFF_XBC + CONV_DIM
OFF_K = OFF_Q + D_RET
OFF_V = OFF_K + D_RET
OFF_G = OFF_V + D_RET
PROJ_MAIN = OFF_G + D_RET

VMEM_LIMIT = 56 * 1024 * 1024
VMEM_LIMIT_FFN = 60 * 1024 * 1024

INPROJ_ROWS = 2048
INPROJ_NORM_ROWS = 1024
INPROJ_COLS = 512
OUTPROJ_ROWS = 512
FFN_ROWS = 1024
FFN_COLS = 512

RET_LOG_GAMMA = [float(np.log1p(-np.float32(2.0) ** np.float32(-5.0 - h)).astype(np.float32))
                 for h in range(RET_HEADS)]


def _silu(x):
    return x / (1.0 + jnp.exp2(x * (-LOG2E)))


def _softplus(x):
    return jnp.maximum(x, 0.0) + jnp.log1p(jnp.exp(-jnp.abs(x)))


def _rms(x):
    return x * lax.rsqrt(jnp.mean(x * x, axis=-1, keepdims=True) + EPS)


def _split3(x):
    hi = x.astype(BF16)
    r = x - hi.astype(F32)
    mid = r.astype(BF16)
    lo = (r - mid.astype(F32)).astype(BF16)
    return hi, mid, lo


def _dot(a, b):
    return jnp.dot(a, b, preferred_element_type=F32)


def _dot_nt(a, b):
    return lax.dot_general(a, b, (((1,), (1,)), ((), ())), preferred_element_type=F32)


def _dot_tn(a, b):
    return lax.dot_general(a, b, (((0,), (0,)), ((), ())), preferred_element_type=F32)


def _exact_right(x, sel):
    hi, mid, lo = x if isinstance(x, tuple) else _split3(x)
    return _dot(hi, sel) + _dot(mid, sel) + _dot(lo, sel)


def _select_right(x, sel):
    hi = x.astype(BF16)
    lo = (x - hi.astype(F32)).astype(BF16)
    return _dot(hi, sel) + _dot(lo, sel)


def _exact_left(sel, x):
    hi, mid, lo = x if isinstance(x, tuple) else _split3(x)
    return _dot(sel, hi) + _dot(sel, mid) + _dot(sel, lo)


def _exact_tn(x, sel):
    hi, mid, lo = x if isinstance(x, tuple) else _split3(x)
    return _dot_tn(hi, sel) + _dot_tn(mid, sel) + _dot_tn(lo, sel)


def _head_expand():
    r = lax.broadcasted_iota(jnp.int32, (LANES, D_SSM), 0)
    c = lax.broadcasted_iota(jnp.int32, (LANES, D_SSM), 1)
    return (c // SSM_HEADDIM == r).astype(BF16)


NORM_ROWS = 256
DT_ROW = D_SSM + CONV_DIM


def _inproj_kernel(x_ref, xs_ref, g_ref, wt_ref, wdt_ref, o_ref, odt_ref, os_ref, odts_ref, u_ref, us_ref,
                   *, npro, nsplit):
    i = pl.program_id(0)
    j = pl.program_id(1)
    on_last = i == pl.num_programs(0) - 1
    xr = x_ref.shape[0]

    @pl.when(j < npro)
    def _():
        wdt = wdt_ref[...].astype(BF16)
        lane = lax.broadcasted_iota(jnp.int32, (NORM_ROWS, LANES), 1)

        def norm_rows(src_ref, src, dst_ref, dt_ref, dst):
            u = (_rms(src_ref[src, :]) * g_ref[...]).astype(BF16)
            dst_ref[dst, :] = u
            dt_ref[dst, :] = jnp.where(lane < SSM_HEADS, _dot_nt(u, wdt), 0.0)

        def body(t, carry):
            src = pl.ds(pl.multiple_of(t * NORM_ROWS, NORM_ROWS), NORM_ROWS)
            dst = pl.ds(pl.multiple_of(j * xr + t * NORM_ROWS, NORM_ROWS), NORM_ROWS)
            norm_rows(x_ref, src, u_ref, odt_ref, dst)
            return carry
        lax.fori_loop(0, xr // NORM_ROWS, body, 0)

        @pl.when(on_last & (j == 0))
        def _():
            for t in range(xs_ref.shape[0] // NORM_ROWS):
                rows = pl.ds(t * NORM_ROWS, NORM_ROWS)
                norm_rows(xs_ref, rows, us_ref, odts_ref, rows)

    def column_tile(with_side):
        sub = wt_ref.shape[0] // nsplit
        for s in range(nsplit):
            cols = slice(s * sub, (s + 1) * sub)
            w = wt_ref[cols, :].astype(BF16)
            o_ref[:, cols] = _dot_nt(u_ref[...], w).astype(o_ref.dtype)
            if with_side:
                os_ref[:, cols] = _dot_nt(us_ref[...], w)

    pl.when((j >= npro) & on_last)(functools.partial(column_tile, True))
    pl.when((j >= npro) & jnp.logical_not(on_last))(functools.partial(column_tile, False))


def _inproj(x, xs, g, wt, *, bm, xr, bn):
    m = x.shape[0]
    ms = xs.shape[0]
    nm = m // bm
    npro = bm // xr
    assert DT_ROW % bn == 0 and ms % NORM_ROWS == 0

    def wrow(i, j):
        t = jnp.maximum(j - npro, 0)
        skip = jnp.where(t * bn >= DT_ROW, SSM_HEADS // SUBLANES, 0)
        return ((t * (bn // SUBLANES) + skip) * SUBLANES, 0)

    col = lambda j: jnp.maximum(j - npro, 0)
    const = lambda i, j: (0, 0)
    return pl.pallas_call(
        functools.partial(_inproj_kernel, npro=npro, nsplit=2),
        out_shape=(jax.ShapeDtypeStruct((m, PROJ_MAIN), BF16), jax.ShapeDtypeStruct((m, LANES), F32),
                   jax.ShapeDtypeStruct((ms, PROJ_MAIN), F32), jax.ShapeDtypeStruct((ms, LANES), F32)),
        grid=(nm, npro + PROJ_MAIN // bn),
        in_specs=[pl.BlockSpec((xr, D_MODEL), lambda i, j: (i * npro + jnp.minimum(j, npro - 1), 0)),
                  pl.BlockSpec((ms, D_MODEL), const),
                  pl.BlockSpec((1, D_MODEL), const),
                  pl.BlockSpec((pl.Element(bn), pl.Element(D_MODEL)), wrow),
                  pl.BlockSpec((pl.Element(LANES), pl.Element(D_MODEL)), lambda i, j: (DT_ROW, 0))],
        out_specs=(pl.BlockSpec((bm, bn), lambda i, j: (i, col(j))),
                   pl.BlockSpec((bm, LANES), lambda i, j: (i, 0)),
                   pl.BlockSpec((ms, bn), lambda i, j: (0, jnp.where(i == nm - 1, col(j), 0))),
                   pl.BlockSpec((ms, LANES), const)),
        scratch_shapes=[pltpu.VMEM((bm, D_MODEL), BF16), pltpu.VMEM((ms, D_MODEL), BF16)],
        compiler_params=pltpu.CompilerParams(
            dimension_semantics=("arbitrary", "arbitrary"), vmem_limit_bytes=VMEM_LIMIT),
        name="inproj",
    )(x, xs, g, wt, wt)


N_MIXER_IN = 13
N_MIXER_OUT = 4


def _mixer_seq_kernel(*refs, valid, pos_base, ncast):
    ins = refs[:N_MIXER_IN]
    cast_in = refs[N_MIXER_IN:N_MIXER_IN + ncast]
    outs = refs[N_MIXER_IN + ncast:N_MIXER_IN + ncast + N_MIXER_OUT]
    cast_out = refs[N_MIXER_IN + ncast + N_MIXER_OUT:N_MIXER_IN + 2 * ncast + N_MIXER_OUT]
    scratch = refs[N_MIXER_IN + 2 * ncast + N_MIXER_OUT:]
    _mixer_seq_body(*ins, *outs, *scratch, valid=valid, pos_base=pos_base)
    for src, dst in zip(cast_in, cast_out):
        dst[...] = src[...].astype(BF16)


def _mixer_seq_body(proj_ref, dtr_ref, conv0_ref, ssm0_ref, ret0_ref,
                    convw_ref, convb_ref, dtb_ref, alog_ref, dskip_ref, sg_ref, rg_ref, invf_ref,
                    mix_ref, convo_ref, ssmo_ref, reto_ref,
                    cbuf_ref, rdec_ref, cdec_ref, trig_ref, *, valid, pos_base):
    C = CHUNK
    b = pl.program_id(0)
    c = pl.program_id(1)
    rowi = lax.broadcasted_iota(jnp.int32, (C, 1), 0)
    rowf = rowi.astype(F32)
    ri = lax.broadcasted_iota(jnp.int32, (C, C), 0)
    ci = lax.broadcasted_iota(jnp.int32, (C, C), 1)
    causal = ri >= ci

    @pl.when((b == 0) & (c == 0))
    def _():
        diff = (ri - ci).astype(F32)
        for h in range(RET_HEADS):
            rdec_ref[h] = jnp.where(causal, jnp.exp(jnp.maximum(diff, 0.0) * RET_LOG_GAMMA[h]), 0.0)
            rows_l = jnp.broadcast_to(rowf, (C, LANES))
            cdec_ref[h] = jnp.exp((rows_l + 1.0) * RET_LOG_GAMMA[h])
            cdec_ref[RET_HEADS + h] = jnp.exp((valid - 1.0 - rows_l) * RET_LOG_GAMMA[h])
        row_ang = rowf * invf_ref[...]
        trig_ref[0] = jnp.cos(row_ang)
        trig_ref[1] = jnp.sin(row_ang)

    hist = CONV_PAD - (CONV_W - 1)

    @pl.when(c == 0)
    def _():
        cbuf_ref[hist:CONV_PAD, :] = conv0_ref[0]
        ssmo_ref[0] = ssm0_ref[0]
        reto_ref[0] = ret0_ref[0]

    xbc_raw = proj_ref[0, :, OFF_XBC:OFF_Q].astype(F32)
    cbuf_ref[CONV_PAD:CONV_PAD + C, :] = xbc_raw
    acc = convb_ref[...] + xbc_raw * convw_ref[CONV_W - 1:CONV_W, :]
    for i in range(CONV_W - 1):
        acc = acc + cbuf_ref[hist + i:hist + i + C, :] * convw_ref[i:i + 1, :]
    xbc = _silu(acc)
    new_prev = cbuf_ref[hist + valid:CONV_PAD + valid, :]
    cbuf_ref[hist:CONV_PAD, :] = new_prev
    convo_ref[0] = new_prev

    xs = xbc[:, :D_SSM]
    bmat = xbc[:, D_SSM:D_SSM + SSM_GROUPS * D_STATE].astype(BF16)
    cmat = xbc[:, D_SSM + SSM_GROUPS * D_STATE:].astype(BF16)

    dt = _softplus(dtr_ref[0] + dtb_ref[...])
    if valid < C:
        dt = jnp.where(rowi < valid, dt, 0.0)
    la = dt * (-jnp.exp(alog_ref[...]))
    tril = causal.astype(BF16)
    triu = (ri <= ci).astype(BF16)
    eye = (ri == ci).astype(BF16)
    la3 = _split3(la)
    lcum = _exact_left(tril, la3)
    lcum_t = _exact_tn(la3, triu)
    dt_t = _exact_tn(dt, eye)
    expand = _head_expand()
    carry_scale = _select_right(jnp.exp(lcum), expand)
    tail_scale = _select_right(jnp.exp(lcum[C - 1:C, :] - lcum) * dt, expand)
    lcum2 = lcum * LOG2E
    lcum2_t = lcum_t * LOG2E

    cbs = [_dot_nt(cmat[:, g * D_STATE:(g + 1) * D_STATE], bmat[:, g * D_STATE:(g + 1) * D_STATE])
           for g in range(SSM_GROUPS)]
    lane = lax.broadcasted_iota(jnp.int32, (C, LANES), 1)
    left = lane < SSM_HEADDIM
    y_intra = []
    for m in range(SSM_HEADS // 2):
        ws = []
        for h in (2 * m, 2 * m + 1):
            seg2 = lcum2[:, h:h + 1] - lcum2_t[h:h + 1, :]
            decay = jnp.exp2(jnp.where(causal, seg2, -jnp.inf))
            ws.append((cbs[h // (SSM_HEADS // SSM_GROUPS)] * decay * dt_t[h:h + 1, :]).astype(BF16))
        xm = xs[:, m * LANES:(m + 1) * LANES]
        xst = jnp.concatenate([jnp.where(left, xm, 0.0), jnp.where(left, 0.0, xm)], axis=0).astype(BF16)
        y_intra.append(_dot(jnp.concatenate(ws, axis=1), xst))
    y = jnp.concatenate(y_intra, axis=1)

    hstate = ssmo_ref[0]
    hb = hstate.astype(BF16)
    y_inter = jnp.concatenate(
        [_dot_nt(cmat[:, g * D_STATE:(g + 1) * D_STATE], hb[g * GROUP_DIM:(g + 1) * GROUP_DIM, :])
         for g in range(SSM_GROUPS)], axis=1)
    y = y + y_inter * carry_scale + dskip_ref[...] * xs

    xw = (xs * tail_scale).astype(BF16)
    upd = jnp.concatenate(
        [_dot_tn(xw[:, g * GROUP_DIM:(g + 1) * GROUP_DIM], bmat[:, g * D_STATE:(g + 1) * D_STATE])
         for g in range(SSM_GROUPS)], axis=0)
    la_tot = _exact_tn(la3, jnp.ones((C, LANES), BF16))
    er = lax.broadcasted_iota(jnp.int32, (D_SSM, LANES), 0)
    ec = lax.broadcasted_iota(jnp.int32, (D_SSM, LANES), 1)
    expand_t = (er // SSM_HEADDIM == ec).astype(BF16)
    chunk_decay = jnp.exp(_exact_left(expand_t, la_tot))
    ssmo_ref[0] = chunk_decay * hstate + upd

    z = proj_ref[0, :, OFF_Z:OFF_XBC].astype(F32)
    y = y * _silu(z)
    y1 = jnp.concatenate([_rms(y[:, g * GROUP_DIM:(g + 1) * GROUP_DIM]) for g in range(SSM_GROUPS)],
                         axis=1) * sg_ref[...]

    ang0 = (pos_base + c * C).astype(F32) * invf_ref[...]
    cos0, sin0 = jnp.cos(ang0), jnp.sin(ang0)
    cos = cos0 * trig_ref[0] - sin0 * trig_ref[1]
    sin = sin0 * trig_ref[0] + cos0 * trig_ref[1]
    kscale = RET_HEADDIM ** -0.5
    cos_k, sin_k = cos * kscale, sin * kscale
    half = RET_HEADDIM // 2
    y2 = []
    for h in range(RET_HEADS):
        lg = RET_LOG_GAMMA[h]
        q1 = proj_ref[0, :, OFF_Q + h * RET_HEADDIM:OFF_Q + h * RET_HEADDIM + half].astype(F32)
        q2 = proj_ref[0, :, OFF_Q + h * RET_HEADDIM + half:OFF_Q + (h + 1) * RET_HEADDIM].astype(F32)
        k1 = proj_ref[0, :, OFF_K + h * RET_HEADDIM:OFF_K + h * RET_HEADDIM + half].astype(F32)
        k2 = proj_ref[0, :, OFF_K + h * RET_HEADDIM + half:OFF_K + (h + 1) * RET_HEADDIM].astype(F32)
        vh = proj_ref[0, :, OFF_V + h * RET_HEADDIM:OFF_V + (h + 1) * RET_HEADDIM].astype(BF16)
        qr = jnp.concatenate([q1 * cos - q2 * sin, q1 * sin + q2 * cos], axis=1)
        kr = jnp.concatenate([k1 * cos_k - k2 * sin_k, k1 * sin_k + k2 * cos_k], axis=1)
        if valid < C:
            kr = jnp.where(rowi < valid, kr, 0.0)
        qb = qr.astype(BF16)
        scores = _dot_nt(qb, kr.astype(BF16)) * rdec_ref[h]
        s_old = reto_ref[0, h * RET_HEADDIM:(h + 1) * RET_HEADDIM, :]
        carry_dec = cdec_ref[h]
        tail_dec = cdec_ref[RET_HEADS + h]
        yr = (_dot(scores.astype(BF16), vh)
              + _dot(qb, s_old.astype(BF16)) * jnp.concatenate([carry_dec, carry_dec], axis=1))
        kw = (kr * jnp.concatenate([tail_dec, tail_dec], axis=1)).astype(BF16)
        reto_ref[0, h * RET_HEADDIM:(h + 1) * RET_HEADDIM, :] = (
            float(np.exp(np.float32(valid * lg))) * s_old + _dot_tn(kw, vh))
        y2.append(_rms(yr))
    gate = proj_ref[0, :, OFF_G:PROJ_MAIN].astype(F32)
    y2 = jnp.concatenate(y2, axis=1) * rg_ref[...] * _silu(gate)

    mix_ref[0, :, :D_SSM] = y1.astype(BF16)
    mix_ref[0, :, D_SSM:] = y2.astype(BF16)


def _mixer_seq(proj, dtr, conv0, ssm0, ret0, params, *, nchunks, chunk_offset, valid, pos_base, name,
               cast=()):
    nb = proj.shape[0]
    nsteps = nb * nchunks
    row = lambda b, c: (b, c + chunk_offset, 0)
    const3 = lambda b, c: (0, 0, 0)
    const2 = lambda b, c: (0, 0)
    per_b = lambda b, c: (b, 0, 0)
    pspecs = [pl.BlockSpec(p.shape, const2) for p in params]
    cast_specs = []
    for w, nblk in cast:
        assert nsteps % nblk == 0 and w.shape[0] % nblk == 0
        every = nsteps // nblk
        cast_specs.append(pl.BlockSpec((w.shape[0] // nblk, w.shape[1]),
                                       lambda b, c, every=every: ((b * nchunks + c) // every, 0)))
    kern = functools.partial(_mixer_seq_kernel, valid=valid, pos_base=pos_base, ncast=len(cast))
    return pl.pallas_call(
        kern,
        out_shape=(jax.ShapeDtypeStruct((nb, nchunks * CHUNK, D_MODEL), BF16),
                   jax.ShapeDtypeStruct((nb, CONV_W - 1, CONV_DIM), F32),
                   jax.ShapeDtypeStruct((nb, D_SSM, D_STATE), F32),
                   jax.ShapeDtypeStruct((nb, D_RET, RET_HEADDIM), F32))
        + tuple(jax.ShapeDtypeStruct(w.shape, BF16) for w, _ in cast),
        grid=(nb, nchunks),
        in_specs=[pl.BlockSpec((1, CHUNK, PROJ_MAIN), row),
                  pl.BlockSpec((1, CHUNK, LANES), row),
                  pl.BlockSpec((1, CONV_W - 1, CONV_DIM), const3),
                  pl.BlockSpec((1, D_SSM, D_STATE), const3),
                  pl.BlockSpec((1, D_RET, RET_HEADDIM), const3)] + pspecs + cast_specs,
        out_specs=(pl.BlockSpec((1, CHUNK, D_MODEL), lambda b, c: (b, c, 0)),
                   pl.BlockSpec((1, CONV_W - 1, CONV_DIM), per_b),
                   pl.BlockSpec((1, D_SSM, D_STATE), per_b),
                   pl.BlockSpec((1, D_RET, RET_HEADDIM), per_b)) + tuple(cast_specs),
        scratch_shapes=[pltpu.VMEM((CONV_PAD + CHUNK, CONV_DIM), F32),
                        pltpu.VMEM((RET_HEADS, CHUNK, CHUNK), F32),
                        pltpu.VMEM((2 * RET_HEADS, CHUNK, LANES), F32),
                        pltpu.VMEM((2, CHUNK, RET_HEADDIM // 2), F32)],
        compiler_params=pltpu.CompilerParams(
            dimension_semantics=("arbitrary", "arbitrary"), vmem_limit_bytes=VMEM_LIMIT),
        name=name,
    )(proj, dtr, conv0, ssm0, ret0, *params, *[w for w, _ in cast])


def _mixer_step_kernel(proj_ref, dtr_ref, conv_ref, ssm_ref, ret_ref,
                       convw_ref, convb_ref, dtb_ref, alog_ref, dskip_ref, sg_ref, rg_ref, invf_ref,
                       mix_ref, convo_ref, ssmo_ref, reto_ref, cols_ref):
    R = STEP_ROWS
    xbc_raw = proj_ref[:, OFF_XBC:OFF_Q]
    acc = convb_ref[...] + xbc_raw * convw_ref[CONV_W - 1:CONV_W, :]
    for i in range(CONV_W - 1):
        acc = acc + conv_ref[i] * convw_ref[i:i + 1, :]
    xbc = _silu(acc)
    convo_ref[0] = conv_ref[1]
    convo_ref[1] = conv_ref[2]
    convo_ref[2] = xbc_raw

    xs = xbc[:, :D_SSM]
    bmat = xbc[:, D_SSM:D_SSM + SSM_GROUPS * D_STATE]
    cmat = xbc[:, D_SSM + SSM_GROUPS * D_STATE:]
    dt = _softplus(dtr_ref[...] + dtb_ref[...])
    la = dt * (-jnp.exp(alog_ref[...]))
    expand = _head_expand()
    dt_x = _exact_right(dt, expand)
    decay_x = jnp.exp(_exact_right(la, expand))
    xdt = xs * dt_x

    ang = jnp.float32(PAST_LEN) * invf_ref[...]
    cos = jnp.cos(ang)
    sin = jnp.sin(ang)
    half = RET_HEADDIM // 2
    qs, ks = [], []
    for h in range(RET_HEADS):
        q1 = proj_ref[:, OFF_Q + h * RET_HEADDIM:OFF_Q + h * RET_HEADDIM + half]
        q2 = proj_ref[:, OFF_Q + h * RET_HEADDIM + half:OFF_Q + (h + 1) * RET_HEADDIM]
        k1 = proj_ref[:, OFF_K + h * RET_HEADDIM:OFF_K + h * RET_HEADDIM + half]
        k2 = proj_ref[:, OFF_K + h * RET_HEADDIM + half:OFF_K + (h + 1) * RET_HEADDIM]
        qs += [q1 * cos - q2 * sin, q1 * sin + q2 * cos]
        ks += [(k1 * cos - k2 * sin) * (RET_HEADDIM ** -0.5), (k1 * sin + k2 * cos) * (RET_HEADDIM ** -0.5)]
    qr = jnp.concatenate(qs, axis=1)
    kr = jnp.concatenate(ks, axis=1)
    vv = proj_ref[:, OFF_V:OFF_G]

    allq = jnp.concatenate([decay_x, xdt, kr, qr], axis=1)
    hi = allq.astype(BF16).astype(F32)
    r1 = allq - hi
    mid = r1.astype(BF16).astype(F32)
    lo = (r1 - mid).astype(BF16).astype(F32)
    stack = jnp.concatenate([hi, mid, lo, jnp.zeros_like(hi)], axis=0).astype(BF16)
    krow = lax.broadcasted_iota(jnp.int32, (4 * R, LANES), 0)
    row8 = lax.broadcasted_iota(jnp.int32, (R, 1), 0)
    lane = lax.broadcasted_iota(jnp.int32, (1, LANES), 1)

    y_cols = jnp.zeros((D_SSM, LANES), F32)
    y_ret = jnp.zeros((R, D_RET), F32)
    for r in range(R):
        sel = ((krow % R == r) & (krow < 3 * R)).astype(BF16)
        cols_ref[...] = _dot_tn(stack, sel)
        ycol = []
        for g in range(SSM_GROUPS):
            rows = slice(g * GROUP_DIM, (g + 1) * GROUP_DIM)
            h_old = ssm_ref[r, rows, :]
            h_new = (h_old * cols_ref[g * GROUP_DIM:(g + 1) * GROUP_DIM, :]
                     + cols_ref[D_SSM + g * GROUP_DIM:D_SSM + (g + 1) * GROUP_DIM, :]
                     * bmat[r:r + 1, g * D_STATE:(g + 1) * D_STATE])
            ssmo_ref[r, rows, :] = h_new
            ycol.append(jnp.sum(h_new * cmat[r:r + 1, g * D_STATE:(g + 1) * D_STATE], axis=1, keepdims=True))
        y_cols = jnp.where(lane == r, jnp.concatenate(ycol, axis=0), y_cols)
        yrow = []
        for h in range(RET_HEADS):
            rows = slice(h * RET_HEADDIM, (h + 1) * RET_HEADDIM)
            kcol = cols_ref[2 * D_SSM + h * RET_HEADDIM:2 * D_SSM + (h + 1) * RET_HEADDIM, :]
            qcol = cols_ref[3 * D_SSM + h * RET_HEADDIM:3 * D_SSM + (h + 1) * RET_HEADDIM, :]
            gamma = float(np.exp(np.float32(RET_LOG_GAMMA[h])))
            s_new = (gamma * ret_ref[r, rows, :]
                     + jnp.concatenate([kcol, kcol], axis=1) * vv[r:r + 1, h * RET_HEADDIM:(h + 1) * RET_HEADDIM])
            reto_ref[r, rows, :] = s_new
            yrow.append(jnp.sum(jnp.concatenate([qcol, qcol], axis=1) * s_new, axis=0, keepdims=True))
        y_ret = jnp.where(row8 == r, jnp.concatenate(yrow, axis=1), y_ret)

    y_ssd = y_cols.T[:R, :]
    y = (y_ssd + dskip_ref[...] * xs) * _silu(proj_ref[:, OFF_Z:OFF_XBC])
    y1 = jnp.concatenate([_rms(y[:, g * GROUP_DIM:(g + 1) * GROUP_DIM]) for g in range(SSM_GROUPS)],
                         axis=1) * sg_ref[...]
    y2 = jnp.concatenate([_rms(y_ret[:, h * RET_HEADDIM:(h + 1) * RET_HEADDIM]) for h in range(RET_HEADS)],
                         axis=1) * rg_ref[...] * _silu(proj_ref[:, OFF_G:PROJ_MAIN])
    mix_ref[:, :D_SSM] = y1
    mix_ref[:, D_SSM:] = y2


def _mixer_step(proj, dtr, conv_t, ssm, ret, params, *, nb):
    R = STEP_ROWS
    rows2 = lambda i: (i, 0)
    rows3 = lambda i: (i, 0, 0)
    mid3 = lambda i: (0, i, 0)
    const2 = lambda i: (0, 0)
    pspecs = [pl.BlockSpec(p.shape, const2) for p in params]
    return pl.pallas_call(
        _mixer_step_kernel,
        out_shape=(jax.ShapeDtypeStruct((nb, D_MODEL), F32),
                   jax.ShapeDtypeStruct((CONV_W - 1, nb, CONV_DIM), F32),
                   jax.ShapeDtypeStruct((nb, D_SSM, D_STATE), F32),
                   jax.ShapeDtypeStruct((nb, D_RET, RET_HEADDIM), F32)),
        grid=(nb // R,),
        in_specs=[pl.BlockSpec((R, PROJ_MAIN), rows2),
                  pl.BlockSpec((R, LANES), rows2),
                  pl.BlockSpec((CONV_W - 1, R, CONV_DIM), mid3),
                  pl.BlockSpec((R, D_SSM, D_STATE), rows3),
                  pl.BlockSpec((R, D_RET, RET_HEADDIM), rows3)] + pspecs,
        out_specs=(pl.BlockSpec((R, D_MODEL), rows2),
                   pl.BlockSpec((CONV_W - 1, R, CONV_DIM), mid3),
                   pl.BlockSpec((R, D_SSM, D_STATE), rows3),
                   pl.BlockSpec((R, D_RET, RET_HEADDIM), rows3)),
        scratch_shapes=[pltpu.VMEM((4 * D_SSM, LANES), F32)],
        compiler_params=pltpu.CompilerParams(
            dimension_semantics=("arbitrary",), vmem_limit_bytes=VMEM_LIMIT),
        name="mixer_step",
    )(proj, dtr, conv_t, ssm, ret, *params)


def _outproj_kernel(mix_ref, mixs_ref, w_ref, h_ref, hs_ref, g1_ref, g2_ref, hout_ref, f_ref, houts_ref, fs_ref):
    def rows(mix_r, h_r, hout_r, f_r):
        y = _dot(mix_r[...].astype(BF16), w_ref[...])
        h = h_r[...] + _rms(y) * g1_ref[...]
        hout_r[...] = h
        f_r[...] = (_rms(h) * g2_ref[...]).astype(BF16)

    rows(mix_ref, h_ref, hout_ref, f_ref)

    @pl.when(pl.program_id(0) == pl.num_programs(0) - 1)
    def _():
        rows(mixs_ref, hs_ref, houts_ref, fs_ref)


def _outproj(mix, mixs, w, h, hs, g1, g2, *, bm):
    m = mix.shape[0]
    ms = mixs.shape[0]
    row = lambda i: (i, 0)
    const = lambda i: (0, 0)
    return pl.pallas_call(
        _outproj_kernel,
        out_shape=(jax.ShapeDtypeStruct((m, D_MODEL), F32), jax.ShapeDtypeStruct((m, D_MODEL), BF16),
                   jax.ShapeDtypeStruct((ms, D_MODEL), F32), jax.ShapeDtypeStruct((ms, D_MODEL), BF16)),
        grid=(m // bm,),
        in_specs=[pl.BlockSpec((bm, D_MODEL), row),
                  pl.BlockSpec((ms, D_MODEL), const),
                  pl.BlockSpec((D_MODEL, D_MODEL), const),
                  pl.BlockSpec((bm, D_MODEL), row),
                  pl.BlockSpec((ms, D_MODEL), const),
                  pl.BlockSpec((1, D_MODEL), const),
                  pl.BlockSpec((1, D_MODEL), const)],
        out_specs=(pl.BlockSpec((bm, D_MODEL), row), pl.BlockSpec((bm, D_MODEL), row),
                   pl.BlockSpec((ms, D_MODEL), const), pl.BlockSpec((ms, D_MODEL), const)),
        compiler_params=pltpu.CompilerParams(
            dimension_semantics=("arbitrary",), vmem_limit_bytes=VMEM_LIMIT),
        name="outproj",
    )(mix, mixs, w, h, hs, g1, g2)


FFN_SPLIT = 2


def _ffn_kernel(f_ref, fs_ref, wg_ref, wu_ref, wd_ref, h_hbm, hs_ref, g_ref, o_ref, os_ref, acc_ref, hsem):
    i = pl.program_id(0)
    j = pl.program_id(1)
    nj = pl.num_programs(1)
    on_last = i == pl.num_programs(0) - 1
    bm = o_ref.shape[0]

    def h_copy(tile):
        return pltpu.make_async_copy(h_hbm.at[pl.ds(pl.multiple_of(tile * bm, bm), bm), :], o_ref, hsem)

    def finish_chunk(rows):
        o_ref[rows, :] = o_ref[rows, :] + _rms(acc_ref[rows, :]) * g_ref[...]
        acc_ref[rows, :] = jnp.zeros((NORM_ROWS, acc_ref.shape[1]), F32)

    def finish_rows():
        for t in range(bm // NORM_ROWS):
            finish_chunk(slice(t * NORM_ROWS, (t + 1) * NORM_ROWS))

    def finish_rows_loop():
        def body(t, carry):
            finish_chunk(pl.ds(pl.multiple_of(t * NORM_ROWS, NORM_ROWS), NORM_ROWS))
            return carry
        lax.fori_loop(0, bm // NORM_ROWS, body, 0)

    @pl.when((i == 0) & (j == 0))
    def _():
        acc_ref[...] = jnp.zeros_like(acc_ref)

    @pl.when(j == 1)
    def _():
        h_copy(i).start()

    first = (j == 0) & (i > 0)

    @pl.when(first & on_last)
    def _():
        h_copy(i - 1).wait()
        finish_rows_loop()
        os_ref[...] = jnp.zeros_like(os_ref)

    def ff_tile(with_side, finish_prev):
        fsub = wg_ref.shape[1] // FFN_SPLIT
        nsub = acc_ref.shape[1] // FFN_SPLIT
        if finish_prev:
            h_copy(i - 1).wait()
            finish_rows()

        def through(f, acc_r, s):
            ff = slice(s * fsub, (s + 1) * fsub)
            a = (_silu(_dot(f, wg_ref[:, ff])) * _dot(f, wu_ref[:, ff])).astype(BF16)
            for n in range(FFN_SPLIT):
                nn = slice(n * nsub, (n + 1) * nsub)
                acc_r[:, nn] += _dot(a, wd_ref[ff, nn])

        for s in range(FFN_SPLIT):
            through(f_ref[...], acc_ref, s)
            if with_side:
                through(fs_ref[...], os_ref, s)

    not_last = jnp.logical_not(on_last)
    pl.when(on_last)(functools.partial(ff_tile, True, False))
    pl.when(first & not_last)(functools.partial(ff_tile, False, True))
    pl.when(jnp.logical_not(first) & not_last)(functools.partial(ff_tile, False, False))

    @pl.when(on_last & (j == nj - 1))
    def _():
        h_copy(i).wait()
        finish_rows_loop()
        os_ref[...] = hs_ref[...] + _rms(os_ref[...]) * g_ref[...]


def _ffn(f, fs, wg, wu, wd, h, hs, g, *, bm, bf):
    m = f.shape[0]
    ms = fs.shape[0]
    assert m // bm > 1 and D_FF // bf > 1
    const = lambda i, j: (0, 0)
    once = pl.Buffered(1)
    return pl.pallas_call(
        _ffn_kernel,
        out_shape=(jax.ShapeDtypeStruct((m, D_MODEL), F32), jax.ShapeDtypeStruct((ms, D_MODEL), F32)),
        grid=(m // bm, D_FF // bf),
        in_specs=[pl.BlockSpec((bm, D_MODEL), lambda i, j: (i, 0)),
                  pl.BlockSpec((ms, D_MODEL), const, pipeline_mode=once),
                  pl.BlockSpec((D_MODEL, bf), lambda i, j: (0, j)),
                  pl.BlockSpec((D_MODEL, bf), lambda i, j: (0, j)),
                  pl.BlockSpec((bf, D_MODEL), lambda i, j: (j, 0)),
                  pl.BlockSpec(memory_space=pl.ANY),
                  pl.BlockSpec((ms, D_MODEL), const, pipeline_mode=once),
                  pl.BlockSpec((1, D_MODEL), const)],
        out_specs=(pl.BlockSpec((bm, D_MODEL), lambda i, j: (jnp.where((j == 0) & (i > 0), i - 1, i), 0)),
                   pl.BlockSpec((ms, D_MODEL), const)),
        scratch_shapes=[pltpu.VMEM((bm, D_MODEL), F32), pltpu.SemaphoreType.DMA],
        compiler_params=pltpu.CompilerParams(
            dimension_semantics=("arbitrary", "arbitrary"), vmem_limit_bytes=VMEM_LIMIT_FFN),
        name="ffn",
    )(f, fs, wg, wu, wd, h, hs, g)


def kernel(x_prompt, x_sample, state_conv, state_ssm, state_ret, meta_tokens, pre_mix_g, post_mix_g,
           pre_ffn_g, post_ffn_g, w_in, conv_w, conv_b, dt_bias, a_log, d_skip, ssm_norm_g, ret_norm_g,
           w_out, w_gate, w_up, w_down):
    bp, seq = x_prompt.shape[:2]
    bs = x_sample.shape[0]
    assert w_in.shape[0] == 1 and x_sample.shape[1] == 1 and seq % CHUNK == 0 and bs == CHUNK

    w_in_t = jnp.swapaxes(w_in[0], 0, 1)
    pad16 = lambda v: jnp.pad(v, ((0, 0), (0, LANES - SSM_HEADS)))
    inv_freq = (ROPE_BASE ** (-jnp.arange(RET_HEADDIM // 2, dtype=F32) / (RET_HEADDIM // 2)))[None, :]
    params = (conv_w[0], conv_b, pad16(dt_bias), pad16(a_log),
              jnp.repeat(d_skip, SSM_HEADDIM, axis=1), ssm_norm_g, ret_norm_g, inv_freq)

    xp = x_prompt.reshape(bp * seq, D_MODEL)
    xs_rows = x_sample.reshape(bs, D_MODEL)
    x_small = jnp.concatenate(
        [xs_rows, meta_tokens.astype(F32), jnp.zeros((CHUNK - N_META, D_MODEL), F32)], axis=0)
    proj_p, dtr_p, proj_s, dtr_s = _inproj(xp, x_small, pre_mix_g, w_in_t, bm=INPROJ_ROWS, xr=INPROJ_NORM_ROWS,
                                           bn=INPROJ_COLS)

    zc = jnp.zeros((1, CONV_W - 1, CONV_DIM), F32)
    zs = jnp.zeros((1, D_SSM, D_STATE), F32)
    zr = jnp.zeros((1, D_RET, RET_HEADDIM), F32)
    _, m_conv, m_ssm, m_ret = _mixer_seq(
        proj_s.reshape(1, 2 * CHUNK, PROJ_MAIN), dtr_s.reshape(1, 2 * CHUNK, LANES), zc, zs, zr, params,
        nchunks=1, chunk_offset=1, valid=N_META, pos_base=0, name="mixer_meta")[:4]

    nsteps = bp * (seq // CHUNK)
    mix_p, p_conv, p_ssm, p_ret, w_out_b, w_gate_b, w_up_b, w_down_b = _mixer_seq(
        proj_p.reshape(bp, seq, PROJ_MAIN), dtr_p.reshape(bp, seq, LANES), m_conv, m_ssm, m_ret, params,
        nchunks=seq // CHUNK, chunk_offset=0, valid=CHUNK, pos_base=N_META, name="mixer_prompt",
        cast=((w_out[0], nsteps), (w_gate[0], nsteps), (w_up[0], nsteps), (w_down[0], nsteps // 2)))

    conv_t = jnp.transpose(state_conv[0], (1, 0, 2))
    mix_s, s_conv_t, s_ssm, s_ret = _mixer_step(
        proj_s, dtr_s, conv_t, state_ssm[0].reshape(bs, D_SSM, D_STATE),
        state_ret[0].reshape(bs, D_RET, RET_HEADDIM), params, nb=bs)

    h1_p, f_p, h1_s, f_s = _outproj(mix_p.reshape(bp * seq, D_MODEL), mix_s, w_out_b, xp, xs_rows,
                                    post_mix_g, pre_ffn_g, bm=OUTPROJ_ROWS)
    y_p, y_s = _ffn(f_p, f_s, w_gate_b, w_up_b, w_down_b, h1_p, h1_s, post_ffn_g, bm=FFN_ROWS,
                    bf=FFN_COLS)

    return (y_p.reshape(bp, seq, D_MODEL),
            y_s.reshape(bs, 1, D_MODEL),
            p_conv[None],
            p_ssm.reshape(1, bp, SSM_HEADS, SSM_HEADDIM, D_STATE),
            p_ret.reshape(1, bp, RET_HEADS, RET_HEADDIM, RET_HEADDIM),
            jnp.transpose(s_conv_t, (1, 0, 2))[None],
            s_ssm.reshape(1, bs, SSM_HEADS, SSM_HEADDIM, D_STATE),
            s_ret.reshape(1, bs, RET_HEADS, RET_HEADDIM, RET_HEADDIM))
```

```python
import functools

import numpy as np
import jax
import jax.numpy as jnp
from jax import lax
from jax.experimental import pallas as pl
from jax.experimental.pallas import tpu as pltpu

F32 = jnp.float32
BF16 = jnp.bfloat16

D_MODEL = 2048
N_META = 16
CHUNK = 128
D_SSM = 1024
D_RET = 1024
SSM_HEADDIM = 64
SSM_HEADS = 16
SSM_GROUPS = 2
GROUP_DIM = D_SSM // SSM_GROUPS
D_STATE = 128
CONV_W = 4
CONV_DIM = D_SSM + 2 * SSM_GROUPS * D_STATE
RET_HEADS = 4
RET_HEADDIM = 256
ROPE_BASE = 10000.0
D_FF = 5632
EPS = 1e-6
PAST_LEN = 16384
LOG2E = float(np.log2(np.e))

LANES = 128
SUBLANES = 8
STEP_ROWS = SUBLANES
CONV_PAD = SUBLANES

OFF_Z = 0
OFF_XBC = D_SSM
OFF_Q = OFF_XBC + CONV_DIM
OFF_K = OFF_Q + D_RET
OFF_V = OFF_K + D_RET
OFF_G = OFF_V + D_RET
PROJ_MAIN = OFF_G + D_RET

VMEM_LIMIT = 56 * 1024 * 1024
VMEM_LIMIT_FFN = 60 * 1024 * 1024

INPROJ_ROWS = 4096
INPROJ_NORM_ROWS = 512
INPROJ_COLS = 512
OUTPROJ_ROWS = 512
FFN_ROWS = 1024
FFN_COLS = 512

RET_LOG_GAMMA = [float(np.log1p(-np.float32(2.0) ** np.float32(-5.0 - h)).astype(np.float32))
                 for h in range(RET_HEADS)]


def _silu(x):
    return x / (1.0 + jnp.exp2(x * (-LOG2E)))


def _softplus(x):
    return jnp.maximum(x, 0.0) + jnp.log1p(jnp.exp(-jnp.abs(x)))


def _rms(x):
    return x * lax.rsqrt(jnp.mean(x * x, axis=-1, keepdims=True) + EPS)


def _split3(x):
    hi = x.astype(BF16)
    r = x - hi.astype(F32)
    mid = r.astype(BF16)
    lo = (r - mid.astype(F32)).astype(BF16)
    return hi, mid, lo


def _dot(a, b):
    return jnp.dot(a, b, preferred_element_type=F32)


def _dot_nt(a, b):
    return lax.dot_general(a, b, (((1,), (1,)), ((), ())), preferred_element_type=F32)


def _dot_tn(a, b):
    return lax.dot_general(a, b, (((0,), (0,)), ((), ())), preferred_element_type=F32)


def _exact_right(x, sel):
    hi, mid, lo = x if isinstance(x, tuple) else _split3(x)
    return _dot(hi, sel) + _dot(mid, sel) + _dot(lo, sel)


def _select_right(x, sel):
    hi = x.astype(BF16)
    lo = (x - hi.astype(F32)).astype(BF16)
    return _dot(hi, sel) + _dot(lo, sel)


def _exact_left(sel, x):
    hi, mid, lo = x if isinstance(x, tuple) else _split3(x)
    return _dot(sel, hi) + _dot(sel, mid) + _dot(sel, lo)


def _exact_tn(x, sel):
    hi, mid, lo = x if isinstance(x, tuple) else _split3(x)
    return _dot_tn(hi, sel) + _dot_tn(mid, sel) + _dot_tn(lo, sel)


def _head_expand():
    r = lax.broadcasted_iota(jnp.int32, (LANES, D_SSM), 0)
    c = lax.broadcasted_iota(jnp.int32, (LANES, D_SSM), 1)
    return (c // SSM_HEADDIM == r).astype(BF16)


NORM_ROWS = 256
DT_ROW = D_SSM + CONV_DIM


def _inproj_kernel(x_ref, xs_ref, g_ref, wt_ref, wdt_ref, o_ref, odt_ref, os_ref, odts_ref, u_ref, us_ref,
                   *, npro, nsplit):
    i = pl.program_id(0)
    j = pl.program_id(1)
    on_last = i == pl.num_programs(0) - 1
    xr = x_ref.shape[0]

    @pl.when(j < npro)
    def _():
        wdt = wdt_ref[...].astype(BF16)
        lane = lax.broadcasted_iota(jnp.int32, (NORM_ROWS, LANES), 1)

        def norm_rows(src_ref, src, dst_ref, dt_ref, dst):
            u = (_rms(src_ref[src, :]) * g_ref[...]).astype(BF16)
            dst_ref[dst, :] = u
            dt_ref[dst, :] = jnp.where(lane < SSM_HEADS, _dot_nt(u, wdt), 0.0)

        def body(t, carry):
            src = pl.ds(pl.multiple_of(t * NORM_ROWS, NORM_ROWS), NORM_ROWS)
            dst = pl.ds(pl.multiple_of(j * xr + t * NORM_ROWS, NORM_ROWS), NORM_ROWS)
            norm_rows(x_ref, src, u_ref, odt_ref, dst)
            return carry
        lax.fori_loop(0, xr // NORM_ROWS, body, 0)

        @pl.when(on_last & (j == 0))
        def _():
            for t in range(xs_ref.shape[0] // NORM_ROWS):
                rows = pl.ds(t * NORM_ROWS, NORM_ROWS)
                norm_rows(xs_ref, rows, us_ref, odts_ref, rows)

    def column_tile(with_side):
        sub = wt_ref.shape[0] // nsplit
        for s in range(nsplit):
            cols = slice(s * sub, (s + 1) * sub)
            w = wt_ref[cols, :].astype(BF16)
            o_ref[:, cols] = _dot_nt(u_ref[...], w).astype(o_ref.dtype)
            if with_side:
                os_ref[:, cols] = _dot_nt(us_ref[...], w)

    pl.when((j >= npro) & on_last)(functools.partial(column_tile, True))
    pl.when((j >= npro) & jnp.logical_not(on_last))(functools.partial(column_tile, False))


def _inproj(x, xs, g, wt, *, bm, xr, bn):
    m = x.shape[0]
    ms = xs.shape[0]
    nm = m // bm
    npro = bm // xr
    assert DT_ROW % bn == 0 and ms % NORM_ROWS == 0

    def wrow(i, j):
        t = jnp.maximum(j - npro, 0)
        skip = jnp.where(t * bn >= DT_ROW, SSM_HEADS // SUBLANES, 0)
        return ((t * (bn // SUBLANES) + skip) * SUBLANES, 0)

    col = lambda j: jnp.maximum(j - npro, 0)
    const = lambda i, j: (0, 0)
    return pl.pallas_call(
        functools.partial(_inproj_kernel, npro=npro, nsplit=2),
        out_shape=(jax.ShapeDtypeStruct((m, PROJ_MAIN), BF16), jax.ShapeDtypeStruct((m, LANES), F32),
                   jax.ShapeDtypeStruct((ms, PROJ_MAIN), F32), jax.ShapeDtypeStruct((ms, LANES), F32)),
        grid=(nm, npro + PROJ_MAIN // bn),
        in_specs=[pl.BlockSpec((xr, D_MODEL), lambda i, j: (i * npro + jnp.minimum(j, npro - 1), 0)),
                  pl.BlockSpec((ms, D_MODEL), const),
                  pl.BlockSpec((1, D_MODEL), const),
                  pl.BlockSpec((pl.Element(bn), pl.Element(D_MODEL)), wrow),
                  pl.BlockSpec((pl.Element(LANES), pl.Element(D_MODEL)), lambda i, j: (DT_ROW, 0))],
        out_specs=(pl.BlockSpec((bm, bn), lambda i, j: (i, col(j))),
                   pl.BlockSpec((bm, LANES), lambda i, j: (i, 0)),
                   pl.BlockSpec((ms, bn), lambda i, j: (0, jnp.where(i == nm - 1, col(j), 0))),
                   pl.BlockSpec((ms, LANES), const)),
        scratch_shapes=[pltpu.VMEM((bm, D_MODEL), BF16), pltpu.VMEM((ms, D_MODEL), BF16)],
        compiler_params=pltpu.CompilerParams(
            dimension_semantics=("arbitrary", "arbitrary"), vmem_limit_bytes=VMEM_LIMIT_FFN),
        name="inproj",
    )(x, xs, g, wt, wt)


N_MIXER_IN = 13
N_MIXER_OUT = 4


def _mixer_seq_kernel(*refs, valid, pos_base, ncast):
    ins = refs[:N_MIXER_IN]
    cast_in = refs[N_MIXER_IN:N_MIXER_IN + ncast]
    outs = refs[N_MIXER_IN + ncast:N_MIXER_IN + ncast + N_MIXER_OUT]
    cast_out = refs[N_MIXER_IN + ncast + N_MIXER_OUT:N_MIXER_IN + 2 * ncast + N_MIXER_OUT]
    scratch = refs[N_MIXER_IN + 2 * ncast + N_MIXER_OUT:]
    _mixer_seq_body(*ins, *outs, *scratch, valid=valid, pos_base=pos_base)
    for src, dst in zip(cast_in, cast_out):
        dst[...] = src[...].astype(BF16)


def _mixer_seq_body(proj_ref, dtr_ref, conv0_ref, ssm0_ref, ret0_ref,
                    convw_ref, convb_ref, dtb_ref, alog_ref, dskip_ref, sg_ref, rg_ref, invf_ref,
                    mix_ref, convo_ref, ssmo_ref, reto_ref,
                    cbuf_ref, rdec_ref, cdec_ref, trig_ref, *, valid, pos_base):
    C = CHUNK
    b = pl.program_id(0)
    c = pl.program_id(1)
    rowi = lax.broadcasted_iota(jnp.int32, (C, 1), 0)
    rowf = rowi.astype(F32)
    ri = lax.broadcasted_iota(jnp.int32, (C, C), 0)
    ci = lax.broadcasted_iota(jnp.int32, (C, C), 1)
    causal = ri >= ci

    @pl.when((b == 0) & (c == 0))
    def _():
        diff = (ri - ci).astype(F32)
        for h in range(RET_HEADS):
            rdec_ref[h] = jnp.where(causal, jnp.exp(jnp.maximum(diff, 0.0) * RET_LOG_GAMMA[h]), 0.0)
            rows_l = jnp.broadcast_to(rowf, (C, LANES))
            cdec_ref[h] = jnp.exp((rows_l + 1.0) * RET_LOG_GAMMA[h])
            cdec_ref[RET_HEADS + h] = jnp.exp((valid - 1.0 - rows_l) * RET_LOG_GAMMA[h])
        row_ang = rowf * invf_ref[...]
        trig_ref[0] = jnp.cos(row_ang)
        trig_ref[1] = jnp.sin(row_ang)

    hist = CONV_PAD - (CONV_W - 1)

    @pl.when(c == 0)
    def _():
        cbuf_ref[hist:CONV_PAD, :] = conv0_ref[0]
        ssmo_ref[0] = ssm0_ref[0]
        reto_ref[0] = ret0_ref[0]

    xbc_raw = proj_ref[0, :, OFF_XBC:OFF_Q].astype(F32)
    cbuf_ref[CONV_PAD:CONV_PAD + C, :] = xbc_raw
    acc = convb_ref[...] + xbc_raw * convw_ref[CONV_W - 1:CONV_W, :]
    for i in range(CONV_W - 1):
        acc = acc + cbuf_ref[hist + i:hist + i + C, :] * convw_ref[i:i + 1, :]
    xbc = _silu(acc)
    new_prev = cbuf_ref[hist + valid:CONV_PAD + valid, :]
    cbuf_ref[hist:CONV_PAD, :] = new_prev
    convo_ref[0] = new_prev

    xs = xbc[:, :D_SSM]
    bmat = xbc[:, D_SSM:D_SSM + SSM_GROUPS * D_STATE].astype(BF16)
    cmat = xbc[:, D_SSM + SSM_GROUPS * D_STATE:].astype(BF16)

    dt = _softplus(dtr_ref[0] + dtb_ref[...])
    if valid < C:
        dt = jnp.where(rowi < valid, dt, 0.0)
    la = dt * (-jnp.exp(alog_ref[...]))
    tril = causal.astype(BF16)
    triu = (ri <= ci).astype(BF16)
    eye = (ri == ci).astype(BF16)
    la3 = _split3(la)
    lcum = _exact_left(tril, la3)
    lcum_t = _exact_tn(la3, triu)
    dt_t = _exact_tn(dt, eye)
    expand = _head_expand()
    carry_scale = _select_right(jnp.exp(lcum), expand)
    tail_scale = _select_right(jnp.exp(lcum[C - 1:C, :] - lcum) * dt, expand)
    lcum2 = lcum * LOG2E
    lcum2_t = lcum_t * LOG2E

    cbs = [_dot_nt(cmat[:, g * D_STATE:(g + 1) * D_STATE], bmat[:, g * D_STATE:(g + 1) * D_STATE])
           for g in range(SSM_GROUPS)]
    lane = lax.broadcasted_iota(jnp.int32, (C, LANES), 1)
    left = lane < SSM_HEADDIM
    y_intra = []
    for m in range(SSM_HEADS // 2):
        ws = []
        for h in (2 * m, 2 * m + 1):
            seg2 = lcum2[:, h:h + 1] - lcum2_t[h:h + 1, :]
            decay = jnp.exp2(jnp.where(causal, seg2, -jnp.inf))
            ws.append((cbs[h // (SSM_HEADS // SSM_GROUPS)] * decay * dt_t[h:h + 1, :]).astype(BF16))
        xm = xs[:, m * LANES:(m + 1) * LANES]
        xst = jnp.concatenate([jnp.where(left, xm, 0.0), jnp.where(left, 0.0, xm)], axis=0).astype(BF16)
        y_intra.append(_dot(jnp.concatenate(ws, axis=1), xst))
    y = jnp.concatenate(y_intra, axis=1)

    hstate = ssmo_ref[0]
    hb = hstate.astype(BF16)
    y_inter = jnp.concatenate(
        [_dot_nt(cmat[:, g * D_STATE:(g + 1) * D_STATE], hb[g * GROUP_DIM:(g + 1) * GROUP_DIM, :])
         for g in range(SSM_GROUPS)], axis=1)
    y = y + y_inter * carry_scale + dskip_ref[...] * xs

    xw = (xs * tail_scale).astype(BF16)
    upd = jnp.concatenate(
        [_dot_tn(xw[:, g * GROUP_DIM:(g + 1) * GROUP_DIM], bmat[:, g * D_STATE:(g + 1) * D_STATE])
         for g in range(SSM_GROUPS)], axis=0)
    la_tot = _exact_tn(la3, jnp.ones((C, LANES), BF16))
    er = lax.broadcasted_iota(jnp.int32, (D_SSM, LANES), 0)
    ec = lax.broadcasted_iota(jnp.int32, (D_SSM, LANES), 1)
    expand_t = (er // SSM_HEADDIM == ec).astype(BF16)
    chunk_decay = jnp.exp(_exact_left(expand_t, la_tot))
    ssmo_ref[0] = chunk_decay * hstate + upd

    z = proj_ref[0, :, OFF_Z:OFF_XBC].astype(F32)
    y = y * _silu(z)
    y1 = jnp.concatenate([_rms(y[:, g * GROUP_DIM:(g + 1) * GROUP_DIM]) for g in range(SSM_GROUPS)],
                         axis=1) * sg_ref[...]

    ang0 = (pos_base + c * C).astype(F32) * invf_ref[...]
    cos0, sin0 = jnp.cos(ang0), jnp.sin(ang0)
    cos = cos0 * trig_ref[0] - sin0 * trig_ref[1]
    sin = sin0 * trig_ref[0] + cos0 * trig_ref[1]
    kscale = RET_HEADDIM ** -0.5
    cos_k, sin_k = cos * kscale, sin * kscale
    half = RET_HEADDIM // 2
    y2 = []
    for h in range(RET_HEADS):
        lg = RET_LOG_GAMMA[h]
        q1 = proj_ref[0, :, OFF_Q + h * RET_HEADDIM:OFF_Q + h * RET_HEADDIM + half].astype(F32)
        q2 = proj_ref[0, :, OFF_Q + h * RET_HEADDIM + half:OFF_Q + (h + 1) * RET_HEADDIM].astype(F32)
        k1 = proj_ref[0, :, OFF_K + h * RET_HEADDIM:OFF_K + h * RET_HEADDIM + half].astype(F32)
        k2 = proj_ref[0, :, OFF_K + h * RET_HEADDIM + half:OFF_K + (h + 1) * RET_HEADDIM].astype(F32)
        vh = proj_ref[0, :, OFF_V + h * RET_HEADDIM:OFF_V + (h + 1) * RET_HEADDIM].astype(BF16)
        qr = jnp.concatenate([q1 * cos - q2 * sin, q1 * sin + q2 * cos], axis=1)
        kr = jnp.concatenate([k1 * cos_k - k2 * sin_k, k1 * sin_k + k2 * cos_k], axis=1)
        if valid < C:
            kr = jnp.where(rowi < valid, kr, 0.0)
        qb = qr.astype(BF16)
        scores = _dot_nt(qb, kr.astype(BF16)) * rdec_ref[h]
        s_old = reto_ref[0, h * RET_HEADDIM:(h + 1) * RET_HEADDIM, :]
        carry_dec = cdec_ref[h]
        tail_dec = cdec_ref[RET_HEADS + h]
        yr = (_dot(scores.astype(BF16), vh)
              + _dot(qb, s_old.astype(BF16)) * jnp.concatenate([carry_dec, carry_dec], axis=1))
        kw = (kr * jnp.concatenate([tail_dec, tail_dec], axis=1)).astype(BF16)
        reto_ref[0, h * RET_HEADDIM:(h + 1) * RET_HEADDIM, :] = (
            float(np.exp(np.float32(valid * lg))) * s_old + _dot_tn(kw, vh))
        y2.append(_rms(yr))
    gate = proj_ref[0, :, OFF_G:PROJ_MAIN].astype(F32)
    y2 = jnp.concatenate(y2, axis=1) * rg_ref[...] * _silu(gate)

    mix_ref[0, :, :D_SSM] = y1.astype(BF16)
    mix_ref[0, :, D_SSM:] = y2.astype(BF16)


def _mixer_seq(proj, dtr, conv0, ssm0, ret0, params, *, nchunks, chunk_offset, valid, pos_base, name,
               cast=()):
    nb = proj.shape[0]
    nsteps = nb * nchunks
    row = lambda b, c: (b, c + chunk_offset, 0)
    const3 = lambda b, c: (0, 0, 0)
    const2 = lambda b, c: (0, 0)
    per_b = lambda b, c: (b, 0, 0)
    pspecs = [pl.BlockSpec(p.shape, const2) for p in params]
    cast_specs = []
    for w, nblk in cast:
        assert nsteps % nblk == 0 and w.shape[0] % nblk == 0
        every = nsteps // nblk
        cast_specs.append(pl.BlockSpec((w.shape[0] // nblk, w.shape[1]),
                                       lambda b, c, every=every: ((b * nchunks + c) // every, 0)))
    kern = functools.partial(_mixer_seq_kernel, valid=valid, pos_base=pos_base, ncast=len(cast))
    return pl.pallas_call(
        kern,
        out_shape=(jax.ShapeDtypeStruct((nb, nchunks * CHUNK, D_MODEL), BF16),
                   jax.ShapeDtypeStruct((nb, CONV_W - 1, CONV_DIM), F32),
                   jax.ShapeDtypeStruct((nb, D_SSM, D_STATE), F32),
                   jax.ShapeDtypeStruct((nb, D_RET, RET_HEADDIM), F32))
        + tuple(jax.ShapeDtypeStruct(w.shape, BF16) for w, _ in cast),
        grid=(nb, nchunks),
        in_specs=[pl.BlockSpec((1, CHUNK, PROJ_MAIN), row),
                  pl.BlockSpec((1, CHUNK, LANES), row),
                  pl.BlockSpec((1, CONV_W - 1, CONV_DIM), const3),
                  pl.BlockSpec((1, D_SSM, D_STATE), const3),
                  pl.BlockSpec((1, D_RET, RET_HEADDIM), const3)] + pspecs + cast_specs,
        out_specs=(pl.BlockSpec((1, CHUNK, D_MODEL), lambda b, c: (b, c, 0)),
                   pl.BlockSpec((1, CONV_W - 1, CONV_DIM), per_b),
                   pl.BlockSpec((1, D_SSM, D_STATE), per_b),
                   pl.BlockSpec((1, D_RET, RET_HEADDIM), per_b)) + tuple(cast_specs),
        scratch_shapes=[pltpu.VMEM((CONV_PAD + CHUNK, CONV_DIM), F32),
                        pltpu.VMEM((RET_HEADS, CHUNK, CHUNK), F32),
                        pltpu.VMEM((2 * RET_HEADS, CHUNK, LANES), F32),
                        pltpu.VMEM((2, CHUNK, RET_HEADDIM // 2), F32)],
        compiler_params=pltpu.CompilerParams(
            dimension_semantics=("arbitrary", "arbitrary"), vmem_limit_bytes=VMEM_LIMIT),
        name=name,
    )(proj, dtr, conv0, ssm0, ret0, *params, *[w for w, _ in cast])


def _mixer_step_kernel(proj_ref, dtr_ref, conv_ref, ssm_ref, ret_ref,
                       convw_ref, convb_ref, dtb_ref, alog_ref, dskip_ref, sg_ref, rg_ref, invf_ref,
                       mix_ref, convo_ref, ssmo_ref, reto_ref, cols_ref):
    R = STEP_ROWS
    xbc_raw = proj_ref[:, OFF_XBC:OFF_Q]
    acc = convb_ref[...] + xbc_raw * convw_ref[CONV_W - 1:CONV_W, :]
    for i in range(CONV_W - 1):
        acc = acc + conv_ref[i] * convw_ref[i:i + 1, :]
    xbc = _silu(acc)
    convo_ref[0] = conv_ref[1]
    convo_ref[1] = conv_ref[2]
    convo_ref[2] = xbc_raw

    xs = xbc[:, :D_SSM]
    bmat = xbc[:, D_SSM:D_SSM + SSM_GROUPS * D_STATE]
    cmat = xbc[:, D_SSM + SSM_GROUPS * D_STATE:]
    dt = _softplus(dtr_ref[...] + dtb_ref[...])
    la = dt * (-jnp.exp(alog_ref[...]))
    expand = _head_expand()
    dt_x = _exact_right(dt, expand)
    decay_x = jnp.exp(_exact_right(la, expand))
    xdt = xs * dt_x

    ang = jnp.float32(PAST_LEN) * invf_ref[...]
    cos = jnp.cos(ang)
    sin = jnp.sin(ang)
    half = RET_HEADDIM // 2
    qs, ks = [], []
    for h in range(RET_HEADS):
        q1 = proj_ref[:, OFF_Q + h * RET_HEADDIM:OFF_Q + h * RET_HEADDIM + half]
        q2 = proj_ref[:, OFF_Q + h * RET_HEADDIM + half:OFF_Q + (h + 1) * RET_HEADDIM]
        k1 = proj_ref[:, OFF_K + h * RET_HEADDIM:OFF_K + h * RET_HEADDIM + half]
        k2 = proj_ref[:, OFF_K + h * RET_HEADDIM + half:OFF_K + (h + 1) * RET_HEADDIM]
        qs += [q1 * cos - q2 * sin, q1 * sin + q2 * cos]
        ks += [(k1 * cos - k2 * sin) * (RET_HEADDIM ** -0.5), (k1 * sin + k2 * cos) * (RET_HEADDIM ** -0.5)]
    qr = jnp.concatenate(qs, axis=1)
    kr = jnp.concatenate(ks, axis=1)
    vv = proj_ref[:, OFF_V:OFF_G]

    allq = jnp.concatenate([decay_x, xdt, kr, qr], axis=1)
    hi = allq.astype(BF16).astype(F32)
    r1 = allq - hi
    mid = r1.astype(BF16).astype(F32)
    lo = (r1 - mid).astype(BF16).astype(F32)
    stack = jnp.concatenate([hi, mid, lo, jnp.zeros_like(hi)], axis=0).astype(BF16)
    krow = lax.broadcasted_iota(jnp.int32, (4 * R, LANES), 0)
    row8 = lax.broadcasted_iota(jnp.int32, (R, 1), 0)
    lane = lax.broadcasted_iota(jnp.int32, (1, LANES), 1)

    y_cols = jnp.zeros((D_SSM, LANES), F32)
    y_ret = jnp.zeros((R, D_RET), F32)
    for r in range(R):
        sel = ((krow % R == r) & (krow < 3 * R)).astype(BF16)
        cols_ref[...] = _dot_tn(stack, sel)
        ycol = []
        for g in range(SSM_GROUPS):
            rows = slice(g * GROUP_DIM, (g + 1) * GROUP_DIM)
            h_old = ssm_ref[r, rows, :]
            h_new = (h_old * cols_ref[g * GROUP_DIM:(g + 1) * GROUP_DIM, :]
                     + cols_ref[D_SSM + g * GROUP_DIM:D_SSM + (g + 1) * GROUP_DIM, :]
                     * bmat[r:r + 1, g * D_STATE:(g + 1) * D_STATE])
            ssmo_ref[r, rows, :] = h_new
            ycol.append(jnp.sum(h_new * cmat[r:r + 1, g * D_STATE:(g + 1) * D_STATE], axis=1, keepdims=True))
        y_cols = jnp.where(lane == r, jnp.concatenate(ycol, axis=0), y_cols)
        yrow = []
        for h in range(RET_HEADS):
            rows = slice(h * RET_HEADDIM, (h + 1) * RET_HEADDIM)
            kcol = cols_ref[2 * D_SSM + h * RET_HEADDIM:2 * D_SSM + (h + 1) * RET_HEADDIM, :]
            qcol = cols_ref[3 * D_SSM + h * RET_HEADDIM:3 * D_SSM + (h + 1) * RET_HEADDIM, :]
            gamma = float(np.exp(np.float32(RET_LOG_GAMMA[h])))
            s_new = (gamma * ret_ref[r, rows, :]
                     + jnp.concatenate([kcol, kcol], axis=1) * vv[r:r + 1, h * RET_HEADDIM:(h + 1) * RET_HEADDIM])
            reto_ref[r, rows, :] = s_new
            yrow.append(jnp.sum(jnp.concatenate([qcol, qcol], axis=1) * s_new, axis=0, keepdims=True))
        y_ret = jnp.where(row8 == r, jnp.concatenate(yrow, axis=1), y_ret)

    y_ssd = y_cols.T[:R, :]
    y = (y_ssd + dskip_ref[...] * xs) * _silu(proj_ref[:, OFF_Z:OFF_XBC])
    y1 = jnp.concatenate([_rms(y[:, g * GROUP_DIM:(g + 1) * GROUP_DIM]) for g in range(SSM_GROUPS)],
                         axis=1) * sg_ref[...]
    y2 = jnp.concatenate([_rms(y_ret[:, h * RET_HEADDIM:(h + 1) * RET_HEADDIM]) for h in range(RET_HEADS)],
                         axis=1) * rg_ref[...] * _silu(proj_ref[:, OFF_G:PROJ_MAIN])
    mix_ref[:, :D_SSM] = y1
    mix_ref[:, D_SSM:] = y2


def _mixer_step(proj, dtr, conv_t, ssm, ret, params, *, nb):
    R = STEP_ROWS
    rows2 = lambda i: (i, 0)
    rows3 = lambda i: (i, 0, 0)
    mid3 = lambda i: (0, i, 0)
    const2 = lambda i: (0, 0)
    pspecs = [pl.BlockSpec(p.shape, const2) for p in params]
    return pl.pallas_call(
        _mixer_step_kernel,
        out_shape=(jax.ShapeDtypeStruct((nb, D_MODEL), F32),
                   jax.ShapeDtypeStruct((CONV_W - 1, nb, CONV_DIM), F32),
                   jax.ShapeDtypeStruct((nb, D_SSM, D_STATE), F32),
                   jax.ShapeDtypeStruct((nb, D_RET, RET_HEADDIM), F32)),
        grid=(nb // R,),
        in_specs=[pl.BlockSpec((R, PROJ_MAIN), rows2),
                  pl.BlockSpec((R, LANES), rows2),
                  pl.BlockSpec((CONV_W - 1, R, CONV_DIM), mid3),
                  pl.BlockSpec((R, D_SSM, D_STATE), rows3),
                  pl.BlockSpec((R, D_RET, RET_HEADDIM), rows3)] + pspecs,
        out_specs=(pl.BlockSpec((R, D_MODEL), rows2),
                   pl.BlockSpec((CONV_W - 1, R, CONV_DIM), mid3),
                   pl.BlockSpec((R, D_SSM, D_STATE), rows3),
                   pl.BlockSpec((R, D_RET, RET_HEADDIM), rows3)),
        scratch_shapes=[pltpu.VMEM((4 * D_SSM, LANES), F32)],
        compiler_params=pltpu.CompilerParams(
            dimension_semantics=("arbitrary",), vmem_limit_bytes=VMEM_LIMIT),
        name="mixer_step",
    )(proj, dtr, conv_t, ssm, ret, *params)


def _outproj_kernel(mix_ref, mixs_ref, w_ref, h_ref, hs_ref, g1_ref, g2_ref, hout_ref, f_ref, houts_ref, fs_ref):
    def rows(mix_r, h_r, hout_r, f_r):
        y = _dot(mix_r[...].astype(BF16), w_ref[...])
        h = h_r[...] + _rms(y) * g1_ref[...]
        hout_r[...] = h
        f_r[...] = (_rms(h) * g2_ref[...]).astype(BF16)

    rows(mix_ref, h_ref, hout_ref, f_ref)

    @pl.when(pl.program_id(0) == pl.num_programs(0) - 1)
    def _():
        rows(mixs_ref, hs_ref, houts_ref, fs_ref)


def _outproj(mix, mixs, w, h, hs, g1, g2, *, bm):
    m = mix.shape[0]
    ms = mixs.shape[0]
    row = lambda i: (i, 0)
    const = lambda i: (0, 0)
    return pl.pallas_call(
        _outproj_kernel,
        out_shape=(jax.ShapeDtypeStruct((m, D_MODEL), F32), jax.ShapeDtypeStruct((m, D_MODEL), BF16),
                   jax.ShapeDtypeStruct((ms, D_MODEL), F32), jax.ShapeDtypeStruct((ms, D_MODEL), BF16)),
        grid=(m // bm,),
        in_specs=[pl.BlockSpec((bm, D_MODEL), row),
                  pl.BlockSpec((ms, D_MODEL), const),
                  pl.BlockSpec((D_MODEL, D_MODEL), const),
                  pl.BlockSpec((bm, D_MODEL), row),
                  pl.BlockSpec((ms, D_MODEL), const),
                  pl.BlockSpec((1, D_MODEL), const),
                  pl.BlockSpec((1, D_MODEL), const)],
        out_specs=(pl.BlockSpec((bm, D_MODEL), row), pl.BlockSpec((bm, D_MODEL), row),
                   pl.BlockSpec((ms, D_MODEL), const), pl.BlockSpec((ms, D_MODEL), const)),
        compiler_params=pltpu.CompilerParams(
            dimension_semantics=("arbitrary",), vmem_limit_bytes=VMEM_LIMIT),
        name="outproj",
    )(mix, mixs, w, h, hs, g1, g2)


FFN_SPLIT = 2


def _ffn_kernel(f_ref, fs_ref, wg_ref, wu_ref, wd_ref, h_hbm, hs_ref, g_ref, o_ref, os_ref, acc_ref, hsem):
    i = pl.program_id(0)
    j = pl.program_id(1)
    nj = pl.num_programs(1)
    on_last = i == pl.num_programs(0) - 1
    bm = o_ref.shape[0]

    def h_copy(tile):
        return pltpu.make_async_copy(h_hbm.at[pl.ds(pl.multiple_of(tile * bm, bm), bm), :], o_ref, hsem)

    def finish_rows():
        for t in range(bm // NORM_ROWS):
            rows = slice(t * NORM_ROWS, (t + 1) * NORM_ROWS)
            o_ref[rows, :] = o_ref[rows, :] + _rms(acc_ref[rows, :]) * g_ref[...]
            acc_ref[rows, :] = jnp.zeros((NORM_ROWS, acc_ref.shape[1]), F32)

    @pl.when((i == 0) & (j == 0))
    def _():
        acc_ref[...] = jnp.zeros_like(acc_ref)

    @pl.when(j == 1)
    def _():
        h_copy(i).start()

    def ff_tile(with_side, finish_prev):
        fsub = wg_ref.shape[1] // FFN_SPLIT
        nsub = acc_ref.shape[1] // FFN_SPLIT
        if finish_prev:
            h_copy(i - 1).wait()
            finish_rows()
            if with_side:
                os_ref[...] = jnp.zeros_like(os_ref)

        def through(f, acc_r, s):
            ff = slice(s * fsub, (s + 1) * fsub)
            a = (_silu(_dot(f, wg_ref[:, ff])) * _dot(f, wu_ref[:, ff])).astype(BF16)
            for n in range(FFN_SPLIT):
                nn = slice(n * nsub, (n + 1) * nsub)
                acc_r[:, nn] += _dot(a, wd_ref[ff, nn])

        for s in range(FFN_SPLIT):
            through(f_ref[...], acc_ref, s)
            if with_side:
                through(fs_ref[...], os_ref, s)

    first = (j == 0) & (i > 0)
    rest = jnp.logical_not(first)
    not_last = jnp.logical_not(on_last)
    pl.when(first & on_last)(functools.partial(ff_tile, True, True))
    pl.when(first & not_last)(functools.partial(ff_tile, False, True))
    pl.when(rest & on_last)(functools.partial(ff_tile, True, False))
    pl.when(rest & not_last)(functools.partial(ff_tile, False, False))

    @pl.when(on_last & (j == nj - 1))
    def _():
        h_copy(i).wait()
        finish_rows()
        os_ref[...] = hs_ref[...] + _rms(os_ref[...]) * g_ref[...]


def _ffn(f, fs, wg, wu, wd, h, hs, g, *, bm, bf):
    m = f.shape[0]
    ms = fs.shape[0]
    assert m // bm > 1 and D_FF // bf > 1
    const = lambda i, j: (0, 0)
    once = pl.Buffered(1)
    return pl.pallas_call(
        _ffn_kernel,
        out_shape=(jax.ShapeDtypeStruct((m, D_MODEL), F32), jax.ShapeDtypeStruct((ms, D_MODEL), F32)),
        grid=(m // bm, D_FF // bf),
        in_specs=[pl.BlockSpec((bm, D_MODEL), lambda i, j: (i, 0)),
                  pl.BlockSpec((ms, D_MODEL), const, pipeline_mode=once),
                  pl.BlockSpec((D_MODEL, bf), lambda i, j: (0, j)),
                  pl.BlockSpec((D_MODEL, bf), lambda i, j: (0, j)),
                  pl.BlockSpec((bf, D_MODEL), lambda i, j: (j, 0)),
                  pl.BlockSpec(memory_space=pl.ANY),
                  pl.BlockSpec((ms, D_MODEL), const, pipeline_mode=once),
                  pl.BlockSpec((1, D_MODEL), const)],
        out_specs=(pl.BlockSpec((bm, D_MODEL), lambda i, j: (jnp.where((j == 0) & (i > 0), i - 1, i), 0)),
                   pl.BlockSpec((ms, D_MODEL), const)),
        scratch_shapes=[pltpu.VMEM((bm, D_MODEL), F32), pltpu.SemaphoreType.DMA],
        compiler_params=pltpu.CompilerParams(
            dimension_semantics=("arbitrary", "arbitrary"), vmem_limit_bytes=VMEM_LIMIT_FFN),
        name="ffn",
    )(f, fs, wg, wu, wd, h, hs, g)


def kernel(x_prompt, x_sample, state_conv, state_ssm, state_ret, meta_tokens, pre_mix_g, post_mix_g,
           pre_ffn_g, post_ffn_g, w_in, conv_w, conv_b, dt_bias, a_log, d_skip, ssm_norm_g, ret_norm_g,
           w_out, w_gate, w_up, w_down):
    bp, seq = x_prompt.shape[:2]
    bs = x_sample.shape[0]
    assert w_in.shape[0] == 1 and x_sample.shape[1] == 1 and seq % CHUNK == 0 and bs == CHUNK

    w_in_t = jnp.swapaxes(w_in[0], 0, 1)
    pad16 = lambda v: jnp.pad(v, ((0, 0), (0, LANES - SSM_HEADS)))
    inv_freq = (ROPE_BASE ** (-jnp.arange(RET_HEADDIM // 2, dtype=F32) / (RET_HEADDIM // 2)))[None, :]
    params = (conv_w[0], conv_b, pad16(dt_bias), pad16(a_log),
              jnp.repeat(d_skip, SSM_HEADDIM, axis=1), ssm_norm_g, ret_norm_g, inv_freq)

    xp = x_prompt.reshape(bp * seq, D_MODEL)
    xs_rows = x_sample.reshape(bs, D_MODEL)
    x_small = jnp.concatenate(
        [xs_rows, meta_tokens.astype(F32), jnp.zeros((CHUNK - N_META, D_MODEL), F32)], axis=0)
    proj_p, dtr_p, proj_s, dtr_s = _inproj(xp, x_small, pre_mix_g, w_in_t, bm=INPROJ_ROWS, xr=INPROJ_NORM_ROWS,
                                           bn=INPROJ_COLS)

    zc = jnp.zeros((1, CONV_W - 1, CONV_DIM), F32)
    zs = jnp.zeros((1, D_SSM, D_STATE), F32)
    zr = jnp.zeros((1, D_RET, RET_HEADDIM), F32)
    _, m_conv, m_ssm, m_ret = _mixer_seq(
        proj_s.reshape(1, 2 * CHUNK, PROJ_MAIN), dtr_s.reshape(1, 2 * CHUNK, LANES), zc, zs, zr, params,
        nchunks=1, chunk_offset=1, valid=N_META, pos_base=0, name="mixer_meta")[:4]

    nsteps = bp * (seq // CHUNK)
    mix_p, p_conv, p_ssm, p_ret, w_out_b, w_gate_b, w_up_b, w_down_b = _mixer_seq(
        proj_p.reshape(bp, seq, PROJ_MAIN), dtr_p.reshape(bp, seq, LANES), m_conv, m_ssm, m_ret, params,
        nchunks=seq // CHUNK, chunk_offset=0, valid=CHUNK, pos_base=N_META, name="mixer_prompt",
        cast=((w_out[0], nsteps), (w_gate[0], nsteps), (w_up[0], nsteps), (w_down[0], nsteps // 2)))

    conv_t = jnp.transpose(state_conv[0], (1, 0, 2))
    mix_s, s_conv_t, s_ssm, s_ret = _mixer_step(
        proj_s, dtr_s, conv_t, state_ssm[0].reshape(bs, D_SSM, D_STATE),
        state_ret[0].reshape(bs, D_RET, RET_HEADDIM), params, nb=bs)

    h1_p, f_p, h1_s, f_s = _outproj(mix_p.reshape(bp * seq, D_MODEL), mix_s, w_out_b, xp, xs_rows,
                                    post_mix_g, pre_ffn_g, bm=OUTPROJ_ROWS)
    y_p, y_s = _ffn(f_p, f_s, w_gate_b, w_up_b, w_down_b, h1_p, h1_s, post_ffn_g, bm=FFN_ROWS,
                    bf=FFN_COLS)

    return (y_p.reshape(bp, seq, D_MODEL),
            y_s.reshape(bs, 1, D_MODEL),
            p_conv[None],
            p_ssm.reshape(1, bp, SSM_HEADS, SSM_HEADDIM, D_STATE),
            p_ret.reshape(1, bp, RET_HEADS, RET_HEADDIM, RET_HEADDIM),
            jnp.transpose(s_conv_t, (1, 0, 2))[None],
            s_ssm.reshape(1, bs, SSM_HEADS, SSM_HEADDIM, D_STATE),
            s_ret.reshape(1, bs, RET_HEADS, RET_HEADDIM, RET_HEADDIM))
```
